```python
import jax, jax.numpy as jnp
from jax import lax
import numpy as np

D_MODEL = 2048
BATCH = 8
SEQ = 8192
DEPTH = 4

GRID_W = 64
GLA_HEADS = 4
GLA_DK = D_MODEL // 4
GLA_DV = D_MODEL // 2
GLA_HK = GLA_DK // GLA_HEADS
GLA_HV = GLA_DV // GLA_HEADS
GLA_GATE_RANK = 16
GLA_TAU = 16.0
GLA_CHUNK = 64
NA_HD = 64
NA_WIDTH = D_MODEL // 4
NA_HEADS = NA_WIDTH // NA_HD
NA_ROWS = 8
NA_COLS = 16
NA_QB_W = 16
NA_KB_W = NA_QB_W + NA_COLS
MEM_TOKENS = 256
MEM_HEADS = 4
MEM_WIDTH = D_MODEL // 4
MEM_HD = MEM_WIDTH // MEM_HEADS
MIX_WIDTH = GLA_DV + NA_WIDTH + MEM_WIDTH
IN_SPLITS = (GLA_DK, GLA_DK, GLA_DV, GLA_DV, GLA_GATE_RANK, GLA_GATE_RANK,
             NA_WIDTH, NA_WIDTH, NA_WIDTH, MEM_WIDTH)
IN_COLS = sum(IN_SPLITS)
D_FF = ((8 * D_MODEL // 3 + 255) // 256) * 256
RMS_EPS = 1e-6

kernel_name = "hybrid_gla_natten_mem_encoder"


def rms_norm(x, g):
    xf = x.astype(jnp.float32)
    y = xf * lax.rsqrt(jnp.mean(xf * xf, axis=-1, keepdims=True) + RMS_EPS)
    return (y * g.astype(jnp.float32)).astype(x.dtype)


def head_rms_norm(t, g, n_heads):
    B, S, W = t.shape
    y = rms_norm(t.reshape(B, S, n_heads, W // n_heads), g.reshape(n_heads, W // n_heads))
    return y.reshape(B, S, W)


def split_heads(t, n_heads):
    B, S, W = t.shape
    return t.reshape(B, S, n_heads, W // n_heads).transpose(0, 2, 1, 3)


def merge_heads(t):
    B, H, S, d = t.shape
    return t.transpose(0, 2, 1, 3).reshape(B, S, H * d)


def gla_chunked(q, k, v, log_a, strict):
    B, H, S, dk = q.shape
    dv = v.shape[-1]
    C = GLA_CHUNK
    N = S // C
    qc = q.astype(jnp.float32).reshape(B, H, N, C, dk)
    kc = k.astype(jnp.float32).reshape(B, H, N, C, dk)
    vc = v.astype(jnp.float32).reshape(B, H, N, C, dv)
    cum = jnp.cumsum(log_a.astype(jnp.float32).reshape(B, H, N, C, dk), axis=3)
    last = cum[:, :, :, -1:, :]
    q_e = qc * jnp.exp(cum)
    k_e = kc * jnp.exp(-cum)
    k_end = kc * jnp.exp(last - cum)
    scores = jnp.einsum('bhnqd,bhnkd->bhnqk', q_e, k_e)
    mask = np.tril(np.ones((C, C), dtype=bool), k=-1 if strict else 0)
    scores = jnp.where(mask, scores, 0.0)
    o_intra = jnp.einsum('bhnqk,bhnke->bhnqe', scores, vc)
    kv = jnp.einsum('bhnkd,bhnke->nbhde', k_end, vc)
    decay = jnp.exp(last[:, :, :, 0, :]).transpose(2, 0, 1, 3)

    def step(state, inp):
        dec, kv_n = inp
        return dec[..., None] * state + kv_n, state

    _, s_prev = lax.scan(step, jnp.zeros((B, H, dk, dv), jnp.float32), (decay, kv))
    o_inter = jnp.einsum('bhnqd,nbhde->bhnqe', q_e, s_prev)
    return (o_intra + o_inter).reshape(B, H, S, dv)


def gla_group(q_in, k_in, v_in, r_in, gf_in, gb_in, wg2_f, bg_f, wg2_b, bg_b, out_norm):
    q = split_heads(q_in, GLA_HEADS) * (GLA_HK ** -0.5)
    k = split_heads(k_in, GLA_HEADS)
    v = split_heads(v_in, GLA_HEADS)
    z_f = gf_in.astype(jnp.float32) @ wg2_f.astype(jnp.float32) + bg_f.astype(jnp.float32)
    z_b = gb_in.astype(jnp.float32) @ wg2_b.astype(jnp.float32) + bg_b.astype(jnp.float32)
    la_f = split_heads(jax.nn.log_sigmoid(z_f) / GLA_TAU, GLA_HEADS)
    la_b = split_heads(jax.nn.log_sigmoid(z_b) / GLA_TAU, GLA_HEADS)
    o_f = gla_chunked(q, k, v, la_f, strict=False)
    flip = lambda t: jnp.flip(t, axis=2)
    o_b = flip(gla_chunked(flip(q), flip(k), flip(v), flip(la_b), strict=True))
    o = merge_heads(o_f + o_b).astype(q_in.dtype)
    o = head_rms_norm(o, out_norm, GLA_HEADS)
    return o * jax.nn.silu(r_in)


def natten2d(q, k, v, rpb):
    B, H, S, dh = q.shape
    rows = S // GRID_W
    kr = min(NA_ROWS, rows)
    nj = GRID_W // NA_QB_W
    r = np.arange(rows)
    rs = np.clip(r - kr // 2, 0, rows - kr)
    c0 = np.arange(nj) * NA_QB_W
    kc0 = np.clip(c0 - NA_COLS // 2, 0, GRID_W - NA_KB_W)
    key_rows = rs[:, None] + np.arange(kr)[None, :]
    key_cols = kc0[:, None] + np.arange(NA_KB_W)[None, :]
    nk = kr * NA_KB_W
    idx = (key_rows[:, None, :, None] * GRID_W + key_cols[None, :, None, :]).reshape(-1)
    k_blk = jnp.take(k, idx, axis=2).reshape(B, H, rows, nj, nk, dh)
    v_blk = jnp.take(v, idx, axis=2).reshape(B, H, rows, nj, nk, dh)
    q_blk = q.reshape(B, H, rows, nj, NA_QB_W, dh)
    qcol = c0[:, None] + np.arange(NA_QB_W)[None, :]
    cs = np.clip(qcol - NA_COLS // 2, 0, GRID_W - NA_COLS)
    in_win = (key_cols[:, None, :] >= cs[:, :, None]) & (key_cols[:, None, :] < cs[:, :, None] + NA_COLS)
    mask = np.broadcast_to(in_win[:, :, None, :], (nj, NA_QB_W, kr, NA_KB_W)).reshape(nj, NA_QB_W, nk)
    dr = key_rows - r[:, None] + (NA_ROWS - 1)
    dc = np.clip(key_cols[:, None, :] - qcol[:, :, None], -(NA_COLS - 1), NA_COLS - 1) + (NA_COLS - 1)
    bias = rpb[:, dr[:, None, None, :, None], dc[None, :, :, None, :]].reshape(H, rows, nj, NA_QB_W, nk)
    s = jnp.einsum('bhrjqd,bhrjkd->bhrjqk', q_blk, k_blk).astype(jnp.float32) * (dh ** -0.5)
    s = s + bias[None].astype(jnp.float32)
    s = jnp.where(mask, s, -jnp.inf)
    p = jax.nn.softmax(s, axis=-1).astype(v.dtype)
    o = jnp.einsum('bhrjqk,bhrjkd->bhrjqd', p, v_blk)
    return o.reshape(B, H, S, dh)


def natten_group(q_in, k_in, v_in, q_norm, k_norm, rpb, out_norm):
    q = rms_norm(split_heads(q_in, NA_HEADS), q_norm)
    k = rms_norm(split_heads(k_in, NA_HEADS), k_norm)
    v = split_heads(v_in, NA_HEADS)
    o = merge_heads(natten2d(q, k, v, rpb))
    return head_rms_norm(o, out_norm, NA_HEADS)


def memory_group(q_in, mem, mem_norm, mem_wkv, q_norm, k_norm, out_norm):
    q = rms_norm(split_heads(q_in, MEM_HEADS), q_norm)
    kv = rms_norm(mem, mem_norm) @ mem_wkv
    k_m, v_m = jnp.split(kv, 2, axis=-1)
    k_m = rms_norm(split_heads(k_m, MEM_HEADS), k_norm)
    v_m = split_heads(v_m, MEM_HEADS)
    s = jnp.einsum('bhqd,bhkd->bhqk', q, k_m).astype(jnp.float32) * (MEM_HD ** -0.5)
    p = jax.nn.softmax(s, axis=-1).astype(v_m.dtype)
    o = merge_heads(jnp.einsum('bhqk,bhkd->bhqd', p, v_m))
    return head_rms_norm(o, out_norm, MEM_HEADS)


def _fwd_setup_inputs(seed: int = 0) -> dict:
    key = jax.random.key(seed)
    ks = jax.random.split(key, 22)
    f32 = jnp.float32

    def nrm(k, shape, scale):
        return jax.random.normal(k, shape, f32) * scale

    def gain(k, shape):
        return 1.0 + 0.05 * jax.random.normal(k, shape, f32)

    out_scale = (2 * DEPTH) ** -0.5
    return {
        "x": nrm(ks[0], (BATCH, SEQ, D_MODEL), 1.0),
        "mem": nrm(ks[1], (BATCH, MEM_TOKENS, D_MODEL), 1.0),
        "attn_norm": gain(ks[2], (DEPTH, D_MODEL)),
        "w_in": nrm(ks[3], (DEPTH, D_MODEL, IN_COLS), D_MODEL ** -0.5),
        "gla_wg2_f": nrm(ks[4], (DEPTH, GLA_GATE_RANK, GLA_DK), GLA_GATE_RANK ** -0.5),
        "gla_bg_f": nrm(ks[5], (DEPTH, GLA_DK), 0.1),
        "gla_wg2_b": nrm(ks[6], (DEPTH, GLA_GATE_RANK, GLA_DK), GLA_GATE_RANK ** -0.5),
        "gla_bg_b": nrm(ks[7], (DEPTH, GLA_DK), 0.1),
        "gla_out_norm": gain(ks[8], (DEPTH, GLA_DV)),
        "na_q_norm": gain(ks[9], (DEPTH, NA_HD)),
        "na_k_norm": gain(ks[10], (DEPTH, NA_HD)),
        "na_rpb": nrm(ks[11], (DEPTH, NA_HEADS, 2 * NA_ROWS - 1, 2 * NA_COLS - 1), 0.1),
        "na_out_norm": gain(ks[12], (DEPTH, NA_WIDTH)),
        "mem_norm": gain(ks[13], (DEPTH, D_MODEL)),
        "mem_wkv": nrm(ks[14], (DEPTH, D_MODEL, 2 * MEM_WIDTH), D_MODEL ** -0.5),
        "mem_q_norm": gain(ks[15], (DEPTH, MEM_HD)),
        "mem_k_norm": gain(ks[16], (DEPTH, MEM_HD)),
        "mem_out_norm": gain(ks[17], (DEPTH, MEM_WIDTH)),
        "w_out": nrm(ks[18], (DEPTH, MIX_WIDTH, D_MODEL), MIX_WIDTH ** -0.5 * out_scale),
        "ffn_norm": gain(ks[19], (DEPTH, D_MODEL)),
        "ffn_w13": nrm(ks[20], (DEPTH, D_MODEL, 2 * D_FF), D_MODEL ** -0.5),
        "ffn_w2": nrm(ks[21], (DEPTH, D_FF, D_MODEL), D_FF ** -0.5 * out_scale),
    }


def _fwd_reference(x, mem, attn_norm, w_in, gla_wg2_f, gla_bg_f, gla_wg2_b, gla_bg_b, gla_out_norm,
              na_q_norm, na_k_norm, na_rpb, na_out_norm, mem_norm, mem_wkv, mem_q_norm, mem_k_norm,
              mem_out_norm, w_out, ffn_norm, ffn_w13, ffn_w2):
    split_points = np.cumsum(IN_SPLITS)[:-1].tolist()
    for l in range(DEPTH):
        xn = rms_norm(x, attn_norm[l])
        proj = xn @ w_in[l]
        (g_q, g_k, g_v, g_r, g_f, g_b, n_q, n_k, n_v, m_q) = jnp.split(proj, split_points, axis=-1)
        y_gla = gla_group(g_q, g_k, g_v, g_r, g_f, g_b, gla_wg2_f[l], gla_bg_f[l],
                          gla_wg2_b[l], gla_bg_b[l], gla_out_norm[l])
        y_na = natten_group(n_q, n_k, n_v, na_q_norm[l], na_k_norm[l], na_rpb[l], na_out_norm[l])
        y_mem = memory_group(m_q, mem, mem_norm[l], mem_wkv[l], mem_q_norm[l], mem_k_norm[l],
                             mem_out_norm[l])
        x = x + jnp.concatenate([y_gla, y_na, y_mem], axis=-1) @ w_out[l]
        h = rms_norm(x, ffn_norm[l])
        gate, up = jnp.split(h @ ffn_w13[l], 2, axis=-1)
        x = x + (jax.nn.silu(gate) * up) @ ffn_w2[l]
    return x


import jax as _jax
import jax.numpy as _jnp

TWIN_FORMAT = 'train_step'
FWD_PARAMS = ['x', 'mem', 'attn_norm', 'w_in', 'gla_wg2_f', 'gla_bg_f', 'gla_wg2_b', 'gla_bg_b', 'gla_out_norm', 'na_q_norm', 'na_k_norm', 'na_rpb', 'na_out_norm', 'mem_norm', 'mem_wkv', 'mem_q_norm', 'mem_k_norm', 'mem_out_norm', 'w_out', 'ffn_norm', 'ffn_w13', 'ffn_w2']
TWIN_WEIGHTS = ['attn_norm', 'w_in', 'gla_wg2_f', 'gla_bg_f', 'gla_wg2_b', 'gla_bg_b', 'gla_out_norm', 'na_q_norm', 'na_k_norm', 'na_rpb', 'na_out_norm', 'mem_norm', 'mem_wkv', 'mem_q_norm', 'mem_k_norm', 'mem_out_norm', 'w_out', 'ffn_norm', 'ffn_w13', 'ffn_w2']
TWIN_DIFF_INPUT = 'x'
TWIN_INPUTS = ['x', 'mem', 'attn_norm', 'w_in', 'gla_wg2_f', 'gla_bg_f', 'gla_wg2_b', 'gla_bg_b', 'gla_out_norm', 'na_q_norm', 'na_k_norm', 'na_rpb', 'na_out_norm', 'mem_norm', 'mem_wkv', 'mem_q_norm', 'mem_k_norm', 'mem_out_norm', 'w_out', 'ffn_norm', 'ffn_w13', 'ffn_w2', 'loss_target', 'm_attn_norm', 'm_w_in', 'm_gla_wg2_f', 'm_gla_bg_f', 'm_gla_wg2_b', 'm_gla_bg_b', 'm_gla_out_norm', 'm_na_q_norm', 'm_na_k_norm', 'm_na_rpb', 'm_na_out_norm', 'm_mem_norm', 'm_mem_wkv', 'm_mem_q_norm', 'm_mem_k_norm', 'm_mem_out_norm', 'm_w_out', 'm_ffn_norm', 'm_ffn_w13', 'm_ffn_w2', 'v_attn_norm', 'v_w_in', 'v_gla_wg2_f', 'v_gla_bg_f', 'v_gla_wg2_b', 'v_gla_bg_b', 'v_gla_out_norm', 'v_na_q_norm', 'v_na_k_norm', 'v_na_rpb', 'v_na_out_norm', 'v_mem_norm', 'v_mem_wkv', 'v_mem_q_norm', 'v_mem_k_norm', 'v_mem_out_norm', 'v_w_out', 'v_ffn_norm', 'v_ffn_w13', 'v_ffn_w2']
TWIN_OUTPUTS = ['loss', 'grad_x', 'grad_attn_norm', 'grad_w_in', 'grad_gla_wg2_f', 'grad_gla_bg_f', 'grad_gla_wg2_b', 'grad_gla_bg_b', 'grad_gla_out_norm', 'grad_na_q_norm', 'grad_na_k_norm', 'grad_na_rpb', 'grad_na_out_norm', 'grad_mem_norm', 'grad_mem_wkv', 'grad_mem_q_norm', 'grad_mem_k_norm', 'grad_mem_out_norm', 'grad_w_out', 'grad_ffn_norm', 'grad_ffn_w13', 'grad_ffn_w2', 'delta_attn_norm', 'delta_w_in', 'delta_gla_wg2_f', 'delta_gla_bg_f', 'delta_gla_wg2_b', 'delta_gla_bg_b', 'delta_gla_out_norm', 'delta_na_q_norm', 'delta_na_k_norm', 'delta_na_rpb', 'delta_na_out_norm', 'delta_mem_norm', 'delta_mem_wkv', 'delta_mem_q_norm', 'delta_mem_k_norm', 'delta_mem_out_norm', 'delta_w_out', 'delta_ffn_norm', 'delta_ffn_w13', 'delta_ffn_w2', 'new_m_attn_norm', 'new_m_w_in', 'new_m_gla_wg2_f', 'new_m_gla_bg_f', 'new_m_gla_wg2_b', 'new_m_gla_bg_b', 'new_m_gla_out_norm', 'new_m_na_q_norm', 'new_m_na_k_norm', 'new_m_na_rpb', 'new_m_na_out_norm', 'new_m_mem_norm', 'new_m_mem_wkv', 'new_m_mem_q_norm', 'new_m_mem_k_norm', 'new_m_mem_out_norm', 'new_m_w_out', 'new_m_ffn_norm', 'new_m_ffn_w13', 'new_m_ffn_w2', 'new_v_attn_norm', 'new_v_w_in', 'new_v_gla_wg2_f', 'new_v_gla_bg_f', 'new_v_gla_wg2_b', 'new_v_gla_bg_b', 'new_v_gla_out_norm', 'new_v_na_q_norm', 'new_v_na_k_norm', 'new_v_na_rpb', 'new_v_na_out_norm', 'new_v_mem_norm', 'new_v_mem_wkv', 'new_v_mem_q_norm', 'new_v_mem_k_norm', 'new_v_mem_out_norm', 'new_v_w_out', 'new_v_ffn_norm', 'new_v_ffn_w13', 'new_v_ffn_w2']
TWIN_LEAF_KINDS = {'loss': 'loss', 'grad_x': 'grad_x', 'grad_attn_norm': 'grad_w', 'grad_w_in': 'grad_w', 'grad_gla_wg2_f': 'grad_w', 'grad_gla_bg_f': 'grad_w', 'grad_gla_wg2_b': 'grad_w', 'grad_gla_bg_b': 'grad_w', 'grad_gla_out_norm': 'grad_w', 'grad_na_q_norm': 'grad_w', 'grad_na_k_norm': 'grad_w', 'grad_na_rpb': 'grad_w', 'grad_na_out_norm': 'grad_w', 'grad_mem_norm': 'grad_w', 'grad_mem_wkv': 'grad_w', 'grad_mem_q_norm': 'grad_w', 'grad_mem_k_norm': 'grad_w', 'grad_mem_out_norm': 'grad_w', 'grad_w_out': 'grad_w', 'grad_ffn_norm': 'grad_w', 'grad_ffn_w13': 'grad_w', 'grad_ffn_w2': 'grad_w', 'delta_attn_norm': 'delta_w', 'delta_w_in': 'delta_w', 'delta_gla_wg2_f': 'delta_w', 'delta_gla_bg_f': 'delta_w', 'delta_gla_wg2_b': 'delta_w', 'delta_gla_bg_b': 'delta_w', 'delta_gla_out_norm': 'delta_w', 'delta_na_q_norm': 'delta_w', 'delta_na_k_norm': 'delta_w', 'delta_na_rpb': 'delta_w', 'delta_na_out_norm': 'delta_w', 'delta_mem_norm': 'delta_w', 'delta_mem_wkv': 'delta_w', 'delta_mem_q_norm': 'delta_w', 'delta_mem_k_norm': 'delta_w', 'delta_mem_out_norm': 'delta_w', 'delta_w_out': 'delta_w', 'delta_ffn_norm': 'delta_w', 'delta_ffn_w13': 'delta_w', 'delta_ffn_w2': 'delta_w', 'new_m_attn_norm': 'new_m', 'new_m_w_in': 'new_m', 'new_m_gla_wg2_f': 'new_m', 'new_m_gla_bg_f': 'new_m', 'new_m_gla_wg2_b': 'new_m', 'new_m_gla_bg_b': 'new_m', 'new_m_gla_out_norm': 'new_m', 'new_m_na_q_norm': 'new_m', 'new_m_na_k_norm': 'new_m', 'new_m_na_rpb': 'new_m', 'new_m_na_out_norm': 'new_m', 'new_m_mem_norm': 'new_m', 'new_m_mem_wkv': 'new_m', 'new_m_mem_q_norm': 'new_m', 'new_m_mem_k_norm': 'new_m', 'new_m_mem_out_norm': 'new_m', 'new_m_w_out': 'new_m', 'new_m_ffn_norm': 'new_m', 'new_m_ffn_w13': 'new_m', 'new_m_ffn_w2': 'new_m', 'new_v_attn_norm': 'new_v', 'new_v_w_in': 'new_v', 'new_v_gla_wg2_f': 'new_v', 'new_v_gla_bg_f': 'new_v', 'new_v_gla_wg2_b': 'new_v', 'new_v_gla_bg_b': 'new_v', 'new_v_gla_out_norm': 'new_v', 'new_v_na_q_norm': 'new_v', 'new_v_na_k_norm': 'new_v', 'new_v_na_rpb': 'new_v', 'new_v_na_out_norm': 'new_v', 'new_v_mem_norm': 'new_v', 'new_v_mem_wkv': 'new_v', 'new_v_mem_q_norm': 'new_v', 'new_v_mem_k_norm': 'new_v', 'new_v_mem_out_norm': 'new_v', 'new_v_w_out': 'new_v', 'new_v_ffn_norm': 'new_v', 'new_v_ffn_w13': 'new_v', 'new_v_ffn_w2': 'new_v'}


def _forward(args):
    return _fwd_reference(*[args[k] for k in FWD_PARAMS])


def _output_shape():
    def fwd():
        inp = _fwd_setup_inputs(0)
        return _fwd_reference(*[inp[k] for k in FWD_PARAMS])
    out = _jax.eval_shape(fwd)
    return out.shape, out.dtype

N_MICROBATCH = 1
ADAM_LR = 0.001
ADAM_B1 = 0.9
ADAM_B2 = 0.999
ADAM_EPS = 1e-08
ADAM_WD = 0.01
ADAM_STEP = 10
PER_EXAMPLE_BATCH_AXIS = {'x': 0, 'mem': 0, 'loss_target': 0}
SHARED_INPUTS = []
_WEIGHT_DTYPES = {'attn_norm': _jnp.float32, 'w_in': _jnp.float32, 'gla_wg2_f': _jnp.float32, 'gla_bg_f': _jnp.float32, 'gla_wg2_b': _jnp.float32, 'gla_bg_b': _jnp.float32, 'gla_out_norm': _jnp.float32, 'na_q_norm': _jnp.float32, 'na_k_norm': _jnp.float32, 'na_rpb': _jnp.float32, 'na_out_norm': _jnp.float32, 'mem_norm': _jnp.float32, 'mem_wkv': _jnp.float32, 'mem_q_norm': _jnp.float32, 'mem_k_norm': _jnp.float32, 'mem_out_norm': _jnp.float32, 'w_out': _jnp.float32, 'ffn_norm': _jnp.float32, 'ffn_w13': _jnp.float32, 'ffn_w2': _jnp.float32}
MOMENT_SCALE = {'attn_norm': 8.800692e-01, 'w_in': 2.504843e-01, 'gla_wg2_f': 7.267757e-03, 'gla_bg_f': 2.426527e-02, 'gla_wg2_b': 6.949223e-03, 'gla_bg_b': 2.425794e-02, 'gla_out_norm': 1.417527e+00, 'na_q_norm': 2.032587e-01, 'na_k_norm': 2.035471e-01, 'na_rpb': 2.360049e-02, 'na_out_norm': 4.242240e+00, 'mem_norm': 5.565649e-01, 'mem_wkv': 7.894968e-01, 'mem_q_norm': 2.928087e-01, 'mem_k_norm': 2.885574e-01, 'mem_out_norm': 4.259116e+00, 'w_out': 2.126191e+00, 'ffn_norm': 3.092338e+00, 'ffn_w13': 4.304885e-02, 'ffn_w2': 2.180872e-01}


def _to_microbatches(a, axis):
    t = _jnp.moveaxis(a, axis, 0)
    t = t.reshape((N_MICROBATCH, t.shape[0] // N_MICROBATCH) + t.shape[1:])
    return _jnp.moveaxis(t, 1, axis + 1)


def setup_inputs(seed: int = 0) -> dict:
    inp = _fwd_setup_inputs(seed)
    key = _jax.random.fold_in(_jax.random.key(seed), 7919)
    shape, _ = _output_shape()
    out = dict(inp)
    out["loss_target"] = _jax.random.normal(_jax.random.fold_in(key, 0), shape, _jnp.float32)
    for i, name in enumerate(TWIN_WEIGHTS):
        w = inp[name].astype(_jnp.float32)
        if MOMENT_SCALE is None:
            s = _jnp.sqrt(_jnp.mean(_jnp.square(w)) + 1e-30)
        else:
            s = MOMENT_SCALE[name]
        km, kv = _jax.random.split(_jax.random.fold_in(key, i + 1))
        out[name] = w
        out["m_" + name] = s * _jax.random.normal(km, w.shape, _jnp.float32)
        out["v_" + name] = (s * s) * _jax.random.uniform(kv, w.shape, _jnp.float32, 0.5, 1.5)
    if N_MICROBATCH > 1:
        for name, axis in PER_EXAMPLE_BATCH_AXIS.items():
            out[name] = _to_microbatches(out[name], axis)
    return {'x': out['x'], 'mem': out['mem'], 'attn_norm': out['attn_norm'], 'w_in': out['w_in'], 'gla_wg2_f': out['gla_wg2_f'], 'gla_bg_f': out['gla_bg_f'], 'gla_wg2_b': out['gla_wg2_b'], 'gla_bg_b': out['gla_bg_b'], 'gla_out_norm': out['gla_out_norm'], 'na_q_norm': out['na_q_norm'], 'na_k_norm': out['na_k_norm'], 'na_rpb': out['na_rpb'], 'na_out_norm': out['na_out_norm'], 'mem_norm': out['mem_norm'], 'mem_wkv': out['mem_wkv'], 'mem_q_norm': out['mem_q_norm'], 'mem_k_norm': out['mem_k_norm'], 'mem_out_norm': out['mem_out_norm'], 'w_out': out['w_out'], 'ffn_norm': out['ffn_norm'], 'ffn_w13': out['ffn_w13'], 'ffn_w2': out['ffn_w2'], 'loss_target': out['loss_target'], 'm_attn_norm': out['m_attn_norm'], 'm_w_in': out['m_w_in'], 'm_gla_wg2_f': out['m_gla_wg2_f'], 'm_gla_bg_f': out['m_gla_bg_f'], 'm_gla_wg2_b': out['m_gla_wg2_b'], 'm_gla_bg_b': out['m_gla_bg_b'], 'm_gla_out_norm': out['m_gla_out_norm'], 'm_na_q_norm': out['m_na_q_norm'], 'm_na_k_norm': out['m_na_k_norm'], 'm_na_rpb': out['m_na_rpb'], 'm_na_out_norm': out['m_na_out_norm'], 'm_mem_norm': out['m_mem_norm'], 'm_mem_wkv': out['m_mem_wkv'], 'm_mem_q_norm': out['m_mem_q_norm'], 'm_mem_k_norm': out['m_mem_k_norm'], 'm_mem_out_norm': out['m_mem_out_norm'], 'm_w_out': out['m_w_out'], 'm_ffn_norm': out['m_ffn_norm'], 'm_ffn_w13': out['m_ffn_w13'], 'm_ffn_w2': out['m_ffn_w2'], 'v_attn_norm': out['v_attn_norm'], 'v_w_in': out['v_w_in'], 'v_gla_wg2_f': out['v_gla_wg2_f'], 'v_gla_bg_f': out['v_gla_bg_f'], 'v_gla_wg2_b': out['v_gla_wg2_b'], 'v_gla_bg_b': out['v_gla_bg_b'], 'v_gla_out_norm': out['v_gla_out_norm'], 'v_na_q_norm': out['v_na_q_norm'], 'v_na_k_norm': out['v_na_k_norm'], 'v_na_rpb': out['v_na_rpb'], 'v_na_out_norm': out['v_na_out_norm'], 'v_mem_norm': out['v_mem_norm'], 'v_mem_wkv': out['v_mem_wkv'], 'v_mem_q_norm': out['v_mem_q_norm'], 'v_mem_k_norm': out['v_mem_k_norm'], 'v_mem_out_norm': out['v_mem_out_norm'], 'v_w_out': out['v_w_out'], 'v_ffn_norm': out['v_ffn_norm'], 'v_ffn_w13': out['v_ffn_w13'], 'v_ffn_w2': out['v_ffn_w2']}


def _loss(weights, diff, rest, loss_target):
    with _jax.named_scope("forward"):
        args = {**rest, TWIN_DIFF_INPUT: diff, **{k: w.astype(_WEIGHT_DTYPES[k]) for k, w in weights.items()}}
        y = _forward(args)
    with _jax.named_scope("loss_head"):
        err = _jnp.square(y.astype(_jnp.float32) - loss_target)
        return 0.5 * _jnp.sum(_jnp.mean(err, axis=-1)) if err.ndim else 0.5 * err


def _adamw(w, g, m, v):
    m = ADAM_B1 * m + (1.0 - ADAM_B1) * g
    v = ADAM_B2 * v + (1.0 - ADAM_B2) * _jnp.square(g)
    m_hat = m / (1.0 - ADAM_B1 ** ADAM_STEP)
    v_hat = v / (1.0 - ADAM_B2 ** ADAM_STEP)
    delta = -ADAM_LR * (m_hat / (_jnp.sqrt(v_hat) + ADAM_EPS) + ADAM_WD * w)
    return delta, m, v


def reference(x, mem, attn_norm, w_in, gla_wg2_f, gla_bg_f, gla_wg2_b, gla_bg_b, gla_out_norm, na_q_norm, na_k_norm, na_rpb, na_out_norm, mem_norm, mem_wkv, mem_q_norm, mem_k_norm, mem_out_norm, w_out, ffn_norm, ffn_w13, ffn_w2, loss_target, m_attn_norm, m_w_in, m_gla_wg2_f, m_gla_bg_f, m_gla_wg2_b, m_gla_bg_b, m_gla_out_norm, m_na_q_norm, m_na_k_norm, m_na_rpb, m_na_out_norm, m_mem_norm, m_mem_wkv, m_mem_q_norm, m_mem_k_norm, m_mem_out_norm, m_w_out, m_ffn_norm, m_ffn_w13, m_ffn_w2, v_attn_norm, v_w_in, v_gla_wg2_f, v_gla_bg_f, v_gla_wg2_b, v_gla_bg_b, v_gla_out_norm, v_na_q_norm, v_na_k_norm, v_na_rpb, v_na_out_norm, v_mem_norm, v_mem_wkv, v_mem_q_norm, v_mem_k_norm, v_mem_out_norm, v_w_out, v_ffn_norm, v_ffn_w13, v_ffn_w2):
    given = dict(x=x, mem=mem, attn_norm=attn_norm, w_in=w_in, gla_wg2_f=gla_wg2_f, gla_bg_f=gla_bg_f, gla_wg2_b=gla_wg2_b, gla_bg_b=gla_bg_b, gla_out_norm=gla_out_norm, na_q_norm=na_q_norm, na_k_norm=na_k_norm, na_rpb=na_rpb, na_out_norm=na_out_norm, mem_norm=mem_norm, mem_wkv=mem_wkv, mem_q_norm=mem_q_norm, mem_k_norm=mem_k_norm, mem_out_norm=mem_out_norm, w_out=w_out, ffn_norm=ffn_norm, ffn_w13=ffn_w13, ffn_w2=ffn_w2, loss_target=loss_target, m_attn_norm=m_attn_norm, m_w_in=m_w_in, m_gla_wg2_f=m_gla_wg2_f, m_gla_bg_f=m_gla_bg_f, m_gla_wg2_b=m_gla_wg2_b, m_gla_bg_b=m_gla_bg_b, m_gla_out_norm=m_gla_out_norm, m_na_q_norm=m_na_q_norm, m_na_k_norm=m_na_k_norm, m_na_rpb=m_na_rpb, m_na_out_norm=m_na_out_norm, m_mem_norm=m_mem_norm, m_mem_wkv=m_mem_wkv, m_mem_q_norm=m_mem_q_norm, m_mem_k_norm=m_mem_k_norm, m_mem_out_norm=m_mem_out_norm, m_w_out=m_w_out, m_ffn_norm=m_ffn_norm, m_ffn_w13=m_ffn_w13, m_ffn_w2=m_ffn_w2, v_attn_norm=v_attn_norm, v_w_in=v_w_in, v_gla_wg2_f=v_gla_wg2_f, v_gla_bg_f=v_gla_bg_f, v_gla_wg2_b=v_gla_wg2_b, v_gla_bg_b=v_gla_bg_b, v_gla_out_norm=v_gla_out_norm, v_na_q_norm=v_na_q_norm, v_na_k_norm=v_na_k_norm, v_na_rpb=v_na_rpb, v_na_out_norm=v_na_out_norm, v_mem_norm=v_mem_norm, v_mem_wkv=v_mem_wkv, v_mem_q_norm=v_mem_q_norm, v_mem_k_norm=v_mem_k_norm, v_mem_out_norm=v_mem_out_norm, v_w_out=v_w_out, v_ffn_norm=v_ffn_norm, v_ffn_w13=v_ffn_w13, v_ffn_w2=v_ffn_w2)
    weights = {n: given[n] for n in TWIN_WEIGHTS}
    shared = {n: given[n] for n in SHARED_INPUTS}
    per_example = {n: given[n] for n in ['x', 'mem']}
    grad_fn = _jax.value_and_grad(_loss, argnums=(0, 1))

    def one_microbatch(ex, loss_target):
        ex = dict(ex)
        diff = ex.pop(TWIN_DIFF_INPUT)
        return grad_fn(weights, diff, {**shared, **ex}, loss_target)

    if N_MICROBATCH == 1:
        loss, (grad_w, grad_x) = one_microbatch(per_example, given["loss_target"])
    else:
        def body(carry, xs):
            loss_sum, grad_sum = carry
            l_k, (gw_k, gx_k) = one_microbatch(xs[0], xs[1])
            with _jax.named_scope("update"):
                return (loss_sum + l_k, _jax.tree.map(_jnp.add, grad_sum, gw_k)), gx_k

        init = (_jnp.zeros((), _jnp.float32), _jax.tree.map(_jnp.zeros_like, weights))
        (loss, grad_w), grad_x = _jax.lax.scan(body, init, (per_example, given["loss_target"]))
    with _jax.named_scope("update"):
        delta_w, new_m, new_v = {}, {}, {}
        for n in TWIN_WEIGHTS:
            delta_w[n], new_m[n], new_v[n] = _adamw(weights[n], grad_w[n], given["m_" + n], given["v_" + n])
    return (loss, grad_x, *[grad_w[n] for n in TWIN_WEIGHTS], *[delta_w[n] for n in TWIN_WEIGHTS],
            *[new_m[n] for n in TWIN_WEIGHTS], *[new_v[n] for n in TWIN_WEIGHTS])
```

```python
import functools

import numpy as np
import jax
import jax.numpy as jnp
from jax import lax
from jax.experimental import pallas as pl
from jax.experimental.pallas import tpu as pltpu

_F32 = jnp.float32
_BF = jnp.bfloat16
_MESH = pl.DeviceIdType.MESH

_VMEM_LIMIT_BYTES = 56 * 1024 * 1024
_LANES = 128

RMS_EPS = 1e-6
GLA_HEADS = 4
GLA_GATE_RANK = 16
GLA_TAU = 16.0
GLA_CHUNK = 64
GRID_W = 64
NA_HD = 64
NA_ROWS = 8
NA_COLS = 16
MEM_HEADS = 4
ADAM_LR = 0.001
ADAM_B1 = 0.9
ADAM_B2 = 0.999
ADAM_EPS = 1e-08
ADAM_WD = 0.01
ADAM_STEP = 10

WEIGHTS = ['attn_norm', 'w_in', 'gla_wg2_f', 'gla_bg_f', 'gla_wg2_b', 'gla_bg_b', 'gla_out_norm', 'na_q_norm',
           'na_k_norm', 'na_rpb', 'na_out_norm', 'mem_norm', 'mem_wkv', 'mem_q_norm', 'mem_k_norm', 'mem_out_norm',
           'w_out', 'ffn_norm', 'ffn_w13', 'ffn_w2']
BIG = ['w_in', 'mem_wkv', 'w_out', 'ffn_w13', 'ffn_w2']
SMALL_SHARDED = ['gla_wg2_f', 'gla_wg2_b']


def _pick(n, cands):
    for c in cands:
        if n % c == 0:
            return c
    return n


def _row_block(rows, row_bytes, cap_bytes):
    best = 8
    for rb in range(8, rows + 1, 8):
        if rows % rb == 0 and rb * row_bytes <= cap_bytes:
            best = rb
    return best


def _params(sem=None):
    return pltpu.CompilerParams(dimension_semantics=sem, vmem_limit_bytes=_VMEM_LIMIT_BYTES)


def _dg(a, b, ca, cb):
    return lax.dot_general(a.astype(_BF), b.astype(_BF), (((ca,), (cb,)), ((), ())), preferred_element_type=_F32)


def _split(a):
    hi = a.astype(_BF)
    return hi, (a - hi.astype(_F32)).astype(_BF)


def _dg_hl(a, b, ca, cb):
    hi, lo = _split(a)
    return _dg(hi, b, ca, cb) + _dg(lo, b, ca, cb)


def _dg_lh(a, b, ca, cb):
    hi, lo = _split(b)
    return _dg(a, hi, ca, cb) + _dg(a, lo, ca, cb)


def _sigmoid(x):
    return 1.0 / (1.0 + jnp.exp(-x))


def _log_sigmoid(z):
    return jnp.minimum(z, 0.0) - jnp.log(1.0 + jnp.exp(-jnp.abs(z)))


def _block_diag(width, hd):
    i = np.arange(width) // hd
    return jnp.asarray((i[:, None] == i[None, :]).astype(np.float32), dtype=_BF)


def _mm(a, b, mode, name, out_dtype=_F32, res=None):
    if mode == 'nn':
        (M, K), N = a.shape, b.shape[1]
    elif mode == 'nt':
        (M, K), N = a.shape, b.shape[0]
    else:
        (K, M), N = a.shape, b.shape[1]
    tm = _pick(M, (1024, 512, 256, 128))
    tn = _pick(N, (1024, 512, 256, 128))
    tk = _pick(K, (512, 256, 128))
    nk = K // tk
    if mode == 'nn':
        a_spec = pl.BlockSpec((tm, tk), lambda i, j, k: (i, k))
        b_spec = pl.BlockSpec((tk, tn), lambda i, j, k: (k, j))
        ca, cb = 1, 0
    elif mode == 'nt':
        a_spec = pl.BlockSpec((tm, tk), lambda i, j, k: (i, k))
        b_spec = pl.BlockSpec((tn, tk), lambda i, j, k: (j, k))
        ca, cb = 1, 1
    else:
        a_spec = pl.BlockSpec((tk, tm), lambda i, j, k: (k, i))
        b_spec = pl.BlockSpec((tk, tn), lambda i, j, k: (k, j))
        ca, cb = 0, 0
    o_spec = pl.BlockSpec((tm, tn), lambda i, j, k: (i, j))
    has_res = res is not None

    def body(*refs):
        if has_res:
            a_ref, b_ref, r_ref, o_ref, acc = refs
        else:
            a_ref, b_ref, o_ref, acc = refs
        k = pl.program_id(2)

        @pl.when(k == 0)
        def _():
            acc[...] = jnp.zeros_like(acc)

        acc[...] += _dg(a_ref[...], b_ref[...], ca, cb)

        @pl.when(k == nk - 1)
        def _():
            out = acc[...]
            if has_res:
                out = out + r_ref[...]
            o_ref[...] = out.astype(o_ref.dtype)

    in_specs = [a_spec, b_spec] + ([o_spec] if has_res else [])
    args = (a, b) + ((res,) if has_res else ())
    return pl.pallas_call(
        body, name=name, out_shape=jax.ShapeDtypeStruct((M, N), out_dtype),
        grid=(M // tm, N // tn, nk), in_specs=in_specs, out_specs=o_spec,
        scratch_shapes=[pltpu.VMEM((tm, tn), _F32)],
        compiler_params=_params(("parallel", "parallel", "arbitrary")))(*args)


def _rmsnorm_fwd(x, g, name):
    S, D = x.shape
    ts = _pick(S, (256,))

    def body(x_ref, g_ref, o_ref, r_ref):
        xv = x_ref[...]
        r = lax.rsqrt(jnp.mean(xv * xv, axis=-1, keepdims=True) + RMS_EPS)
        o_ref[...] = (xv * r * g_ref[...]).astype(o_ref.dtype)
        r_ref[...] = r

    return pl.pallas_call(
        body, name=name,
        out_shape=(jax.ShapeDtypeStruct((S, D), _BF), jax.ShapeDtypeStruct((S, 1), _F32)),
        grid=(S // ts,),
        in_specs=[pl.BlockSpec((ts, D), lambda i: (i, 0)), pl.BlockSpec((1, D), lambda i: (0, 0))],
        out_specs=(pl.BlockSpec((ts, D), lambda i: (i, 0)), pl.BlockSpec((ts, 1), lambda i: (i, 0))),
        compiler_params=_params(("parallel",)))(x, g)


def _rmsnorm_bwd(x, r, g, dy, dres, name):
    S, D = x.shape
    ts = _pick(S, (256,))
    has_res = dres is not None

    def body(*refs):
        if has_res:
            x_ref, r_ref, g_ref, dy_ref, dr_ref, dx_ref, dxb_ref, dg_ref = refs
        else:
            x_ref, r_ref, g_ref, dy_ref, dx_ref, dxb_ref, dg_ref = refs
        rv = r_ref[...]
        n = x_ref[...] * rv
        dyv = dy_ref[...]
        dn = dyv * g_ref[...]
        c = jnp.mean(dn * n, axis=-1, keepdims=True)
        dx = rv * (dn - n * c)
        if has_res:
            dx = dx + dr_ref[...]
        dx_ref[...] = dx
        dxb_ref[...] = dx.astype(dxb_ref.dtype)

        @pl.when(pl.program_id(0) == 0)
        def _():
            dg_ref[...] = jnp.zeros_like(dg_ref)

        dg_ref[...] += jnp.sum(dyv * n, axis=0, keepdims=True)

    row = pl.BlockSpec((ts, D), lambda i: (i, 0))
    vec = pl.BlockSpec((1, D), lambda i: (0, 0))
    in_specs = [row, pl.BlockSpec((ts, 1), lambda i: (i, 0)), vec, row] + ([row] if has_res else [])
    args = (x, r, g, dy) + ((dres,) if has_res else ())
    return pl.pallas_call(
        body, name=name,
        out_shape=(jax.ShapeDtypeStruct((S, D), _F32), jax.ShapeDtypeStruct((S, D), _BF),
                   jax.ShapeDtypeStruct((1, D), _F32)),
        grid=(S // ts,), in_specs=in_specs, out_specs=(row, row, vec),
        compiler_params=_params(("arbitrary",)))(*args)


def _headnorm_fwd(t, cb, W, hd, g, name):
    S = t.shape[0]
    ts = _pick(S, (512, 256))
    bd = _block_diag(W, hd)

    def body(x_ref, g_ref, bd_ref, o_ref):
        xv = x_ref[...].astype(_F32)
        ms = _dg_hl(xv * xv, bd_ref[...], 1, 0) * (1.0 / hd)
        o_ref[...] = (xv * lax.rsqrt(ms + RMS_EPS) * g_ref[...]).astype(o_ref.dtype)

    return pl.pallas_call(
        body, name=name, out_shape=jax.ShapeDtypeStruct((S, W), _BF), grid=(S // ts,),
        in_specs=[pl.BlockSpec((ts, W), lambda i: (i, cb)), pl.BlockSpec((1, W), lambda i: (0, 0)),
                  pl.BlockSpec((W, W), lambda i: (0, 0))],
        out_specs=pl.BlockSpec((ts, W), lambda i: (i, 0)),
        compiler_params=_params(("parallel",)))(t, g, bd)


def _headnorm_bwd(t, cb, W, hd, g, dy, dcb, name, out_dtype=_F32):
    S = t.shape[0]
    ts = _pick(S, (512, 256))
    bd = _block_diag(W, hd)

    def body(x_ref, g_ref, bd_ref, dy_ref, dx_ref, dg_ref):
        xv = x_ref[...].astype(_F32)
        bdv = bd_ref[...]
        ms = _dg_hl(xv * xv, bdv, 1, 0) * (1.0 / hd)
        rv = lax.rsqrt(ms + RMS_EPS)
        n = xv * rv
        dyv = dy_ref[...].astype(_F32)
        dn = dyv * g_ref[...]
        c = _dg_hl(dn * n, bdv, 1, 0) * (1.0 / hd)
        dx_ref[...] = (rv * (dn - n * c)).astype(dx_ref.dtype)

        @pl.when(pl.program_id(0) == 0)
        def _():
            dg_ref[...] = jnp.zeros_like(dg_ref)

        dg_ref[...] += jnp.sum(dyv * n, axis=0, keepdims=True)

    return pl.pallas_call(
        body, name=name,
        out_shape=(jax.ShapeDtypeStruct((S, W), out_dtype), jax.ShapeDtypeStruct((1, W), _F32)),
        grid=(S // ts,),
        in_specs=[pl.BlockSpec((ts, W), lambda i: (i, cb)), pl.BlockSpec((1, W), lambda i: (0, 0)),
                  pl.BlockSpec((W, W), lambda i: (0, 0)), pl.BlockSpec((ts, W), lambda i: (i, dcb))],
        out_specs=(pl.BlockSpec((ts, W), lambda i: (i, 0)), pl.BlockSpec((1, W), lambda i: (0, 0))),
        compiler_params=_params(("arbitrary",)))(t, g, bd, dy)


def _gla_post_fwd(o_f, o_b, proj, r_cb, g, name):
    S, W = o_f.shape
    hd = W // GLA_HEADS
    ts = _pick(S, (256,))
    bd = _block_diag(W, hd)

    def body(of_ref, ob_ref, r_ref, g_ref, bd_ref, y_ref):
        o = of_ref[...] + ob_ref[...]
        ms = _dg_hl(o * o, bd_ref[...], 1, 0) * (1.0 / hd)
        u = o * lax.rsqrt(ms + RMS_EPS) * g_ref[...]
        rr = r_ref[...]
        y_ref[...] = (u * (rr * _sigmoid(rr))).astype(y_ref.dtype)

    row = pl.BlockSpec((ts, W), lambda i: (i, 0))
    return pl.pallas_call(
        body, name=name, out_shape=jax.ShapeDtypeStruct((S, W), _BF), grid=(S // ts,),
        in_specs=[row, row, pl.BlockSpec((ts, W), lambda i: (i, r_cb)), pl.BlockSpec((1, W), lambda i: (0, 0)),
                  pl.BlockSpec((W, W), lambda i: (0, 0))],
        out_specs=row, compiler_params=_params(("parallel",)))(o_f, o_b, proj, g, bd)


def _gla_post_bwd(o_f, o_b, proj, r_cb, g, dy, name):
    S, W = o_f.shape
    hd = W // GLA_HEADS
    ts = _pick(S, (256,))
    bd = _block_diag(W, hd)

    def body(of_ref, ob_ref, r_ref, g_ref, bd_ref, dy_ref, do_ref, dr_ref, dg_ref):
        o = of_ref[...] + ob_ref[...]
        bdv = bd_ref[...]
        ms = _dg_hl(o * o, bdv, 1, 0) * (1.0 / hd)
        rv = lax.rsqrt(ms + RMS_EPS)
        n = o * rv
        gv = g_ref[...]
        rr = r_ref[...]
        sg = _sigmoid(rr)
        dyv = dy_ref[...]
        dr_ref[...] = (dyv * (n * gv) * (sg * (1.0 + rr * (1.0 - sg)))).astype(dr_ref.dtype)
        du = dyv * (rr * sg)
        dn = du * gv
        c = _dg_hl(dn * n, bdv, 1, 0) * (1.0 / hd)
        do_ref[...] = rv * (dn - n * c)

        @pl.when(pl.program_id(0) == 0)
        def _():
            dg_ref[...] = jnp.zeros_like(dg_ref)

        dg_ref[...] += jnp.sum(du * n, axis=0, keepdims=True)

    row = pl.BlockSpec((ts, W), lambda i: (i, 0))
    vec = pl.BlockSpec((1, W), lambda i: (0, 0))
    return pl.pallas_call(
        body, name=name,
        out_shape=(jax.ShapeDtypeStruct((S, W), _F32), jax.ShapeDtypeStruct((S, W), _BF),
                   jax.ShapeDtypeStruct((1, W), _F32)),
        grid=(S // ts,),
        in_specs=[row, row, pl.BlockSpec((ts, W), lambda i: (i, r_cb)), vec, pl.BlockSpec((W, W), lambda i: (0, 0)),
                  row],
        out_specs=(row, row, vec), compiler_params=_params(("arbitrary",)))(o_f, o_b, proj, g, bd, dy)


def _swiglu_fwd(gu, name):
    S, F2 = gu.shape
    F = F2 // 2
    ts = _pick(S, (512,))
    tc = _pick(F, (512, 256, 128))
    nj = F // tc

    def body(g_ref, u_ref, o_ref):
        gv = g_ref[...]
        o_ref[...] = (gv * _sigmoid(gv) * u_ref[...]).astype(o_ref.dtype)

    return pl.pallas_call(
        body, name=name, out_shape=jax.ShapeDtypeStruct((S, F), _BF), grid=(S // ts, nj),
        in_specs=[pl.BlockSpec((ts, tc), lambda i, j: (i, j)), pl.BlockSpec((ts, tc), lambda i, j: (i, j + nj))],
        out_specs=pl.BlockSpec((ts, tc), lambda i, j: (i, j)),
        compiler_params=_params(("parallel", "parallel")))(gu, gu)


def _swiglu_bwd(gu, da, name):
    S, F2 = gu.shape
    F = F2 // 2
    ts = _pick(S, (512,))
    tc = _pick(F, (512, 256, 128))
    nj = F // tc

    def body(g_ref, u_ref, da_ref, o_ref):
        gv = g_ref[...]
        sg = _sigmoid(gv)
        dav = da_ref[...]
        is_gate = pl.program_id(1) < nj
        dgate = dav * u_ref[...] * (sg * (1.0 + gv * (1.0 - sg)))
        dup = dav * (gv * sg)
        o_ref[...] = jnp.where(is_gate, dgate, dup).astype(o_ref.dtype)

    return pl.pallas_call(
        body, name=name, out_shape=jax.ShapeDtypeStruct((S, F2), _BF), grid=(S // ts, 2 * nj),
        in_specs=[pl.BlockSpec((ts, tc), lambda i, j: (i, j % nj)),
                  pl.BlockSpec((ts, tc), lambda i, j: (i, j % nj + nj)),
                  pl.BlockSpec((ts, tc), lambda i, j: (i, j % nj))],
        out_specs=pl.BlockSpec((ts, tc), lambda i, j: (i, j)),
        compiler_params=_params(("parallel", "parallel")))(gu, gu, da)


def _loss_head(y, tgt, name):
    S, D = y.shape
    ts = _pick(S, (256,))

    def body(y_ref, t_ref, l_ref, d_ref, db_ref):
        err = y_ref[...] - t_ref[...]
        d = err * (1.0 / D)
        d_ref[...] = d
        db_ref[...] = d.astype(db_ref.dtype)

        @pl.when(pl.program_id(0) == 0)
        def _():
            l_ref[...] = jnp.zeros_like(l_ref)

        l_ref[...] += 0.5 * jnp.sum(jnp.mean(err * err, axis=-1, keepdims=True))

    row = pl.BlockSpec((ts, D), lambda i: (i, 0))
    return pl.pallas_call(
        body, name=name,
        out_shape=(jax.ShapeDtypeStruct((8, _LANES), _F32), jax.ShapeDtypeStruct((S, D), _F32),
                   jax.ShapeDtypeStruct((S, D), _BF)),
        grid=(S // ts,), in_specs=[row, row],
        out_specs=(pl.BlockSpec((8, _LANES), lambda i: (0, 0)), row, row),
        compiler_params=_params(("arbitrary",)))(y, tgt)


def _adamw(w, g, m, v, name):
    shape = w.shape
    if w.ndim == 3 and shape[1] * shape[2] > 256 * 1024:
        rb = _row_block(shape[1], shape[2] * 4, 1536 * 1024)
        grid = (shape[0], shape[1] // rb)
        spec = pl.BlockSpec((1, rb, shape[2]), lambda l, i: (l, i, 0))
        sem = ("parallel", "parallel")
    else:
        grid = ()
        spec = pl.BlockSpec(memory_space=pltpu.VMEM)
        sem = None

    def body(w_ref, g_ref, m_ref, v_ref, d_ref, nm_ref, nv_ref):
        gv = g_ref[...]
        mn = ADAM_B1 * m_ref[...] + (1.0 - ADAM_B1) * gv
        vn = ADAM_B2 * v_ref[...] + (1.0 - ADAM_B2) * (gv * gv)
        m_hat = mn / (1.0 - ADAM_B1 ** ADAM_STEP)
        v_hat = vn / (1.0 - ADAM_B2 ** ADAM_STEP)
        d_ref[...] = -ADAM_LR * (m_hat / (jnp.sqrt(v_hat) + ADAM_EPS) + ADAM_WD * w_ref[...])
        nm_ref[...] = mn
        nv_ref[...] = vn

    out = jax.ShapeDtypeStruct(shape, _F32)
    return pl.pallas_call(
        body, name=name, out_shape=(out, out, out), grid=grid, in_specs=[spec] * 4, out_specs=(spec,) * 3,
        compiler_params=_params(sem))(w, g, m, v)


def _add_pair(t, la, cidx, name):
    _, _, r, c = t.shape
    rb = _pick(r, (256, 352, 176, 88, 8))

    def body(c_ref, t_ref, l_ref, o_ref):
        o_ref[...] = t_ref[...] + l_ref[...]

    return pl.pallas_call(
        body, name=name, out_shape=jax.ShapeDtypeStruct((4, r, c), _F32),
        grid_spec=pltpu.PrefetchScalarGridSpec(
            num_scalar_prefetch=1, grid=(4, r // rb),
            in_specs=[pl.BlockSpec((1, None, rb, c), lambda j, i, cr: (j, cr[0], i, 0)),
                      pl.BlockSpec((1, rb, c), lambda j, i, cr: (j, i, 0))],
            out_specs=pl.BlockSpec((1, rb, c), lambda j, i, cr: (j, i, 0))),
        compiler_params=_params(("parallel", "parallel")))(cidx, t, la)


def _add_four(lb, name):
    _, r, c = lb.shape
    rb = _pick(r, (256, 352, 176, 88, 8))

    def body(l_ref, o_ref):
        o_ref[...] = ((l_ref[0] + l_ref[1]) + l_ref[2]) + l_ref[3]

    return pl.pallas_call(
        body, name=name, out_shape=jax.ShapeDtypeStruct((r, c), _F32), grid=(r // rb,),
        in_specs=[pl.BlockSpec((4, rb, c), lambda i: (0, i, 0))],
        out_specs=pl.BlockSpec((rb, c), lambda i: (i, 0)),
        compiler_params=_params(("parallel",)))(lb)


def _gla_chunk(q, k, gfb, wg, bg, tri, rev, scale):
    C = q.shape[0]
    z = _dg(gfb, wg, 1, 0) + bg
    la = _log_sigmoid(z) * (1.0 / GLA_TAU)
    if rev:
        cum = _dg_lh(tri, la, 0, 0)
    else:
        cum = _dg_lh(tri, la, 1, 0)
    tot = jnp.sum(la, axis=0, keepdims=True)
    e_a = jnp.exp(cum)
    e_na = jnp.exp(-cum)
    e_la = jnp.exp(tot - cum)
    qe = q * scale * e_a
    ke = k * e_na
    kend = k * e_la
    row = lax.broadcasted_iota(jnp.int32, (C, C), 0)
    col = lax.broadcasted_iota(jnp.int32, (C, C), 1)
    keep = (col > row) if rev else (col <= row)
    p = jnp.where(keep, _dg(qe, ke, 1, 1), 0.0)
    return dict(z=z, tot=tot, e_a=e_a, e_na=e_na, e_la=e_la, qe=qe, ke=ke, kend=kend, keep=keep, p=p)


def _gla_fwd(proj, gfb, wg, bg, rev, name):
    S = proj.shape[0]
    H = GLA_HEADS
    dk = wg.shape[1] // (2 * H)
    dv = 2 * dk
    C = GLA_CHUNK
    cb_n = _pick(S // C, (8, 4, 2, 1))
    tb = cb_n * C
    nb = S // tb
    scale = float(dk) ** -0.5
    tri = jnp.asarray(np.tril(np.ones((C, C), np.float32)), dtype=_BF)
    wcol = H if rev else 0

    def bmap(b):
        return nb - 1 - b if rev else b

    def body(q_ref, k_ref, v_ref, g_ref, wg_ref, bg_ref, tri_ref, o_ref, st_ref, state):
        h = pl.program_id(1)

        @pl.when(pl.program_id(0) == 0)
        def _():
            state[h] = jnp.zeros((dv, dk), _F32)

        order = range(cb_n - 1, -1, -1) if rev else range(cb_n)
        for ci in order:
            sl = pl.ds(ci * C, C)
            t = _gla_chunk(q_ref[sl, :], k_ref[sl, :], g_ref[sl, :], wg_ref[...], bg_ref[...], tri_ref[...], rev, scale)
            vv = v_ref[sl, :]
            st = state[h]
            o_ref[sl, :] = _dg(t['p'], vv, 1, 0) + _dg(t['qe'], st, 1, 1)
            st_ref[0, ci] = st
            state[h] = st * jnp.exp(t['tot']) + _dg(vv, t['kend'], 0, 0)

    nq = (H * dk) // dk
    return pl.pallas_call(
        body, name=name,
        out_shape=(jax.ShapeDtypeStruct((S, H * dv), _F32), jax.ShapeDtypeStruct((H, S // C, dv, dk), _F32)),
        grid=(nb, H),
        in_specs=[pl.BlockSpec((tb, dk), lambda b, h: (bmap(b), h)),
                  pl.BlockSpec((tb, dk), lambda b, h: (bmap(b), nq + h)),
                  pl.BlockSpec((tb, dv), lambda b, h: (bmap(b), (2 * H * dk) // dv + h)),
                  pl.BlockSpec((tb, _LANES), lambda b, h: (bmap(b), 0)),
                  pl.BlockSpec((_LANES, dk), lambda b, h: (0, wcol + h)),
                  pl.BlockSpec((1, dk), lambda b, h: (0, wcol + h)),
                  pl.BlockSpec((C, C), lambda b, h: (0, 0))],
        out_specs=(pl.BlockSpec((tb, dv), lambda b, h: (bmap(b), h)),
                   pl.BlockSpec((1, cb_n, dv, dk), lambda b, h: (h, bmap(b), 0, 0))),
        scratch_shapes=[pltpu.VMEM((H, dv, dk), _F32)],
        compiler_params=_params(("arbitrary", "arbitrary")))(proj, proj, proj, gfb, wg, bg, tri)


def _gla_bwd(proj, gfb, wg, bg, st, do, rev, prev, name):
    S = proj.shape[0]
    H = GLA_HEADS
    dk = wg.shape[1] // (2 * H)
    dv = 2 * dk
    C = GLA_CHUNK
    cb_n = _pick(S // C, (4, 2, 1))
    tb = cb_n * C
    nb = S // tb
    scale = float(dk) ** -0.5
    tri = jnp.asarray(np.tril(np.ones((C, C), np.float32)), dtype=_BF)
    wcol = H if rev else 0
    has_prev = prev is not None

    def bmap(b):
        return b if rev else nb - 1 - b

    def body(*refs):
        if has_prev:
            (q_ref, k_ref, v_ref, g_ref, wg_ref, bg_ref, tri_ref, st_ref, do_ref, pq_ref, pk_ref, pv_ref,
             dq_ref, dk_ref, dv_ref, dz_ref, dstate) = refs
        else:
            (q_ref, k_ref, v_ref, g_ref, wg_ref, bg_ref, tri_ref, st_ref, do_ref,
             dq_ref, dk_ref, dv_ref, dz_ref, dstate) = refs
        h = pl.program_id(1)

        @pl.when(pl.program_id(0) == 0)
        def _():
            dstate[h] = jnp.zeros((dv, dk), _F32)

        order = range(cb_n) if rev else range(cb_n - 1, -1, -1)
        for ci in order:
            sl = pl.ds(ci * C, C)
            t = _gla_chunk(q_ref[sl, :], k_ref[sl, :], g_ref[sl, :], wg_ref[...], bg_ref[...], tri_ref[...], rev, scale)
            vv = v_ref[sl, :]
            dov = do_ref[sl, :]
            stp = st_ref[0, ci]
            dst = dstate[h]
            e_l = jnp.exp(t['tot'])
            dp = jnp.where(t['keep'], _dg(dov, vv, 1, 1), 0.0)
            dvv = _dg(t['p'], dov, 0, 0) + _dg(t['kend'], dst, 1, 1)
            dqe = _dg(dp, t['ke'], 1, 0) + _dg(dov, stp, 1, 0)
            dke = _dg(dp, t['qe'], 0, 0)
            dkend = _dg(vv, dst, 1, 0)
            dstate[h] = dst * e_l + _dg(dov, t['qe'], 0, 0)
            dtot = (e_l * jnp.sum(dst * stp, axis=0, keepdims=True)
                    + jnp.sum(dkend * t['kend'], axis=0, keepdims=True))
            dqv = dqe * t['e_a'] * scale
            dkv = dke * t['e_na'] + dkend * t['e_la']
            d_a = dqe * t['qe'] - dke * t['ke'] - dkend * t['kend']
            if rev:
                dla = _dg_lh(tri_ref[...], d_a, 1, 0) + dtot
            else:
                dla = _dg_lh(tri_ref[...], d_a, 0, 0) + dtot
            dz_ref[sl, :] = dla * (1.0 / GLA_TAU) * _sigmoid(-t['z'])
            if has_prev:
                dqv = dqv + pq_ref[sl, :]
                dkv = dkv + pk_ref[sl, :]
                dvv = dvv + pv_ref[sl, :]
            dq_ref[sl, :] = dqv
            dk_ref[sl, :] = dkv
            dv_ref[sl, :] = dvv

    nq = (H * dk) // dk
    kblk = pl.BlockSpec((tb, dk), lambda b, h: (bmap(b), h))
    vblk = pl.BlockSpec((tb, dv), lambda b, h: (bmap(b), h))
    in_specs = [kblk,
                pl.BlockSpec((tb, dk), lambda b, h: (bmap(b), nq + h)),
                pl.BlockSpec((tb, dv), lambda b, h: (bmap(b), (2 * H * dk) // dv + h)),
                pl.BlockSpec((tb, _LANES), lambda b, h: (bmap(b), 0)),
                pl.BlockSpec((_LANES, dk), lambda b, h: (0, wcol + h)),
                pl.BlockSpec((1, dk), lambda b, h: (0, wcol + h)),
                pl.BlockSpec((C, C), lambda b, h: (0, 0)),
                pl.BlockSpec((1, cb_n, dv, dk), lambda b, h: (h, bmap(b), 0, 0)),
                vblk]
    args = [proj, proj, proj, gfb, wg, bg, tri, st, do]
    if has_prev:
        in_specs += [kblk, kblk, vblk]
        args += list(prev)
    return pl.pallas_call(
        body, name=name,
        out_shape=(jax.ShapeDtypeStruct((S, H * dk), _F32), jax.ShapeDtypeStruct((S, H * dk), _F32),
                   jax.ShapeDtypeStruct((S, H * dv), _F32), jax.ShapeDtypeStruct((S, H * dk), _F32)),
        grid=(nb, H), in_specs=in_specs, out_specs=(kblk, kblk, vblk, kblk),
        scratch_shapes=[pltpu.VMEM((H, dv, dk), _F32)],
        compiler_params=_params(("arbitrary", "arbitrary")))(*args)


def _na_geometry(S):
    rows = S // GRID_W
    assert rows >= NA_ROWS
    return rows


def _na_row_start(r, rows):
    return jnp.clip(r - NA_ROWS // 2, 0, rows - NA_ROWS)


def _na_bias_tables(rpb):
    H = rpb.shape[0]
    c = np.arange(GRID_W)
    cs = np.clip(c - NA_COLS // 2, 0, GRID_W - NA_COLS)
    kc = np.arange(GRID_W)
    win = (kc[None, :] >= cs[:, None]) & (kc[None, :] < cs[:, None] + NA_COLS)
    idx = np.clip(kc[None, :] - c[:, None], -(NA_COLS - 1), NA_COLS - 1) + (NA_COLS - 1)
    t = jnp.where(jnp.asarray(win)[None, None], rpb[:, :, idx], -jnp.inf)
    offs = [jnp.transpose(t[:, o:o + NA_ROWS], (0, 2, 1, 3)).reshape(H, GRID_W, NA_ROWS * GRID_W)
            for o in range(NA_ROWS)]
    return jnp.stack(offs, axis=0)


def _na_bias_grad(dtab):
    H = dtab.shape[1]
    c = np.arange(GRID_W)
    cs = np.clip(c - NA_COLS // 2, 0, GRID_W - NA_COLS)
    kc = np.arange(GRID_W)
    win = (kc[None, :] >= cs[:, None]) & (kc[None, :] < cs[:, None] + NA_COLS)
    idx = np.clip(kc[None, :] - c[:, None], -(NA_COLS - 1), NA_COLS - 1) + (NA_COLS - 1)
    onehot = (idx[:, :, None] == np.arange(2 * NA_COLS - 1)[None, None, :]) & win[:, :, None]
    d5 = dtab.reshape(NA_ROWS, H, GRID_W, NA_ROWS, GRID_W)
    dt = jnp.zeros((H, 2 * NA_ROWS - 1, GRID_W, GRID_W), _F32)
    for o in range(NA_ROWS):
        dt = dt.at[:, o:o + NA_ROWS].add(jnp.transpose(d5[o], (0, 2, 1, 3)))
    return jnp.einsum('hicd,cdj->hij', dt, jnp.asarray(onehot, _F32), precision=lax.Precision.HIGHEST)


def _na_off(r, rows):
    return (NA_ROWS - 1) - (r - _na_row_start(r, rows))


def _na_fwd(qn, kn, vb, tab, name):
    S, W = qn.shape
    rows = _na_geometry(S)
    npair = W // _LANES
    nk = NA_ROWS * GRID_W
    sc = float(NA_HD) ** -0.5

    def body(q_ref, k_ref, v_ref, b_ref, o_ref):
        r = pl.program_id(1)
        k0 = pl.multiple_of(_na_row_start(r, rows) * GRID_W, GRID_W)
        kw = k_ref[pl.ds(k0, nk), :]
        vw = v_ref[pl.ds(k0, nk), :]
        qv = q_ref[...]
        lane = lax.broadcasted_iota(jnp.int32, (GRID_W, _LANES), 1)
        outs = []
        for hh in range(2):
            mine = (lane >= hh * NA_HD) & (lane < (hh + 1) * NA_HD)
            s = _dg(jnp.where(mine, qv, jnp.zeros_like(qv)), kw, 1, 1) * sc + b_ref[0, hh]
            e = jnp.exp(s - jnp.max(s, axis=-1, keepdims=True))
            p = e / jnp.sum(e, axis=-1, keepdims=True)
            outs.append(_dg(p, vw, 1, 0))
        o_ref[...] = jnp.where(lane < NA_HD, outs[0], outs[1])

    return pl.pallas_call(
        body, name=name, out_shape=jax.ShapeDtypeStruct((S, W), _F32), grid=(npair, rows),
        in_specs=[pl.BlockSpec((GRID_W, _LANES), lambda p, r: (r, p)),
                  pl.BlockSpec((S, _LANES), lambda p, r: (0, p)),
                  pl.BlockSpec((S, _LANES), lambda p, r: (0, p)),
                  pl.BlockSpec((1, 2, GRID_W, nk), lambda p, r: (_na_off(r, rows), p, 0, 0))],
        out_specs=pl.BlockSpec((GRID_W, _LANES), lambda p, r: (r, p)),
        compiler_params=_params(("parallel", "arbitrary")))(qn, kn, vb, tab)


def _na_bwd(qn, kn, vb, tab, do, name):
    S, W = qn.shape
    rows = _na_geometry(S)
    npair = W // _LANES
    nk = NA_ROWS * GRID_W
    sc = float(NA_HD) ** -0.5
    H = W // NA_HD

    def body(q_ref, k_ref, v_ref, b_ref, do_ref, dq_ref, dk_ref, dv_ref, db_ref):
        r = pl.program_id(1)

        @pl.when(r == 0)
        def _():
            dk_ref[...] = jnp.zeros_like(dk_ref)
            dv_ref[...] = jnp.zeros_like(dv_ref)

        first_of_off = (r <= NA_ROWS // 2) | (r > rows - NA_ROWS // 2)

        @pl.when(first_of_off)
        def _():
            db_ref[...] = jnp.zeros_like(db_ref)

        k0 = pl.multiple_of(_na_row_start(r, rows) * GRID_W, GRID_W)
        kw = k_ref[pl.ds(k0, nk), :]
        vw = v_ref[pl.ds(k0, nk), :]
        qv = q_ref[...]
        dov = do_ref[...]
        lane = lax.broadcasted_iota(jnp.int32, (GRID_W, _LANES), 1)
        dqs = []
        dkw = jnp.zeros((nk, _LANES), _F32)
        dvw = jnp.zeros((nk, _LANES), _F32)
        for hh in range(2):
            mine = (lane >= hh * NA_HD) & (lane < (hh + 1) * NA_HD)
            qm = jnp.where(mine, qv, jnp.zeros_like(qv))
            dom = jnp.where(mine, dov, 0.0)
            s = _dg(qm, kw, 1, 1) * sc + b_ref[0, hh]
            e = jnp.exp(s - jnp.max(s, axis=-1, keepdims=True))
            p = e / jnp.sum(e, axis=-1, keepdims=True)
            dp = _dg(dom, vw, 1, 1)
            ds = p * (dp - jnp.sum(p * dp, axis=-1, keepdims=True))
            db_ref[0, hh] += ds
            dqs.append(_dg(ds, kw, 1, 0) * sc)
            dkw = dkw + _dg(ds, qm, 0, 0) * sc
            dvw = dvw + _dg(p, dom, 0, 0)
        dq_ref[...] = jnp.where(lane < NA_HD, dqs[0], dqs[1])
        dk_ref[pl.ds(k0, nk), :] += dkw
        dv_ref[pl.ds(k0, nk), :] += dvw

    blk = pl.BlockSpec((GRID_W, _LANES), lambda p, r: (r, p))
    full = pl.BlockSpec((S, _LANES), lambda p, r: (0, p))
    tspec = pl.BlockSpec((1, 2, GRID_W, nk), lambda p, r: (_na_off(r, rows), p, 0, 0))
    return pl.pallas_call(
        body, name=name,
        out_shape=(jax.ShapeDtypeStruct((S, W), _F32), jax.ShapeDtypeStruct((S, W), _F32),
                   jax.ShapeDtypeStruct((S, W), _F32), jax.ShapeDtypeStruct((NA_ROWS, H, GRID_W, nk), _F32)),
        grid=(npair, rows), in_specs=[blk, full, full, tspec, blk],
        out_specs=(blk, full, full, tspec),
        compiler_params=_params(("arbitrary", "arbitrary")))(qn, kn, vb, tab, do)


def _mem_fwd(qn, km, vm, name):
    S, W = qn.shape
    hd = W // MEM_HEADS
    tq = _pick(S, (512, 256))
    sc = float(hd) ** -0.5

    def body(q_ref, k_ref, v_ref, o_ref):
        for h in range(MEM_HEADS):
            cs = slice(h * hd, (h + 1) * hd)
            s = _dg(q_ref[:, cs], k_ref[:, cs], 1, 1) * sc
            e = jnp.exp(s - jnp.max(s, axis=-1, keepdims=True))
            p = e / jnp.sum(e, axis=-1, keepdims=True)
            o_ref[:, cs] = _dg(p, v_ref[:, cs], 1, 0)

    full = pl.BlockSpec(km.shape, lambda i: (0, 0))
    return pl.pallas_call(
        body, name=name, out_shape=jax.ShapeDtypeStruct((S, W), _F32), grid=(S // tq,),
        in_specs=[pl.BlockSpec((tq, W), lambda i: (i, 0)), full, full],
        out_specs=pl.BlockSpec((tq, W), lambda i: (i, 0)),
        compiler_params=_params(("parallel",)))(qn, km, vm)


def _mem_bwd(qn, km, vm, do, name):
    S, W = qn.shape
    hd = W // MEM_HEADS
    tq = _pick(S, (512, 256))
    sc = float(hd) ** -0.5

    def body(q_ref, k_ref, v_ref, do_ref, dq_ref, dk_ref, dv_ref):
        @pl.when(pl.program_id(0) == 0)
        def _():
            dk_ref[...] = jnp.zeros_like(dk_ref)
            dv_ref[...] = jnp.zeros_like(dv_ref)

        for h in range(MEM_HEADS):
            cs = slice(h * hd, (h + 1) * hd)
            qh = q_ref[:, cs]
            kh = k_ref[:, cs]
            doh = do_ref[:, cs]
            s = _dg(qh, kh, 1, 1) * sc
            e = jnp.exp(s - jnp.max(s, axis=-1, keepdims=True))
            p = e / jnp.sum(e, axis=-1, keepdims=True)
            dp = _dg(doh, v_ref[:, cs], 1, 1)
            ds = p * (dp - jnp.sum(p * dp, axis=-1, keepdims=True))
            dq_ref[:, cs] = _dg(ds, kh, 1, 0) * sc
            dk_ref[:, cs] += _dg(ds, qh, 0, 0) * sc
            dv_ref[:, cs] += _dg(p, doh, 0, 0)

    full = pl.BlockSpec(km.shape, lambda i: (0, 0))
    row = pl.BlockSpec((tq, W), lambda i: (i, 0))
    return pl.pallas_call(
        body, name=name,
        out_shape=(jax.ShapeDtypeStruct((S, W), _F32), jax.ShapeDtypeStruct(km.shape, _F32),
                   jax.ShapeDtypeStruct(km.shape, _F32)),
        grid=(S // tq,), in_specs=[row, full, full, row], out_specs=(row, full, full),
        compiler_params=_params(("arbitrary",)))(qn, km, vm, do)


_ANY = pl.BlockSpec(memory_space=pl.ANY)


def _place():
    x, y, c = lax.axis_index("x"), lax.axis_index("y"), lax.axis_index("c")
    chips = [(1 - x, y), (x, 1 - y), (1 - x, 1 - y)]
    return x, y, c, chips


def _gather_weights(shards, name):
    n = len(shards)

    def body(*refs):
        w = refs[:n]
        g = refs[n:2 * n]
        send, recv, local = refs[2 * n:]
        x, y, c, chips = _place()
        me = 2 * x + y
        sib = (x, y, 1 - c)

        def piece(t, chip, half):
            return g[t].at[chip, half]

        def rcopy(t, k, chip, half, to, src=None):
            dst = piece(t, chip, half)
            return pltpu.make_async_remote_copy(
                src_ref=dst if src is None else src, dst_ref=dst, send_sem=send.at[t, k], recv_sem=recv.at[t, k],
                device_id=to, device_id_type=_MESH)

        mine = [pltpu.make_async_copy(w[t], g[t].at[me], local.at[t]) for t in range(n)]
        for cp in mine:
            cp.start()
        first = [rcopy(t, j, me, c, (cx, cy, c), src=w[t].at[c]) for t in range(n) for j, (cx, cy) in enumerate(chips)]
        for cp in first:
            cp.start()
        passed = []
        for t in range(n):
            for j, (cx, cy) in enumerate(chips):
                rcopy(t, j, 2 * cx + cy, c, (x, y, c)).wait_recv()
                cp = rcopy(t, 3 + j, 2 * cx + cy, c, sib)
                cp.start()
                passed.append(cp)
        for t in range(n):
            for j, (cx, cy) in enumerate(chips):
                rcopy(t, 3 + j, 2 * cx + cy, 1 - c, (x, y, c)).wait_recv()
        for cp in first + passed:
            cp.wait_send()
        for cp in mine:
            cp.wait()

    return pl.pallas_call(
        body, name=name,
        out_shape=tuple(jax.ShapeDtypeStruct((4,) + s.shape, s.dtype) for s in shards),
        in_specs=[_ANY] * n, out_specs=tuple([_ANY] * n),
        scratch_shapes=[pltpu.SemaphoreType.DMA((n, 6)), pltpu.SemaphoreType.DMA((n, 6)),
                        pltpu.SemaphoreType.DMA((n,))],
        compiler_params=_params())(*shards)


def _swap_halves(ts, name):
    n = len(ts)

    def body(*refs):
        t_in = refs[:n]
        land = refs[n:2 * n]
        send, recv = refs[2 * n:]
        x, y, c, _ = _place()
        cps = [pltpu.make_async_remote_copy(
            src_ref=t_in[t].at[:, 1 - c], dst_ref=land[t], send_sem=send.at[t], recv_sem=recv.at[t],
            device_id=(x, y, 1 - c), device_id_type=_MESH) for t in range(n)]
        for cp in cps:
            cp.start()
        for cp in cps:
            cp.wait()

    return pl.pallas_call(
        body, name=name,
        out_shape=tuple(jax.ShapeDtypeStruct((4,) + t.shape[2:], t.dtype) for t in ts),
        in_specs=[_ANY] * n, out_specs=tuple([_ANY] * n),
        scratch_shapes=[pltpu.SemaphoreType.DMA((n,)), pltpu.SemaphoreType.DMA((n,))],
        compiler_params=_params())(*ts)


def _scatter_chips(ps, name):
    n = len(ps)

    def body(*refs):
        p = refs[:n]
        land = refs[n:2 * n]
        send, recv, local = refs[2 * n:]
        x, y, c, chips = _place()
        me = 2 * x + y
        mine = [pltpu.make_async_copy(p[t].at[me], land[t].at[me], local.at[t]) for t in range(n)]
        for cp in mine:
            cp.start()
        cps = []
        for t in range(n):
            for j, (cx, cy) in enumerate(chips):
                cps.append(pltpu.make_async_remote_copy(
                    src_ref=p[t].at[2 * cx + cy], dst_ref=land[t].at[me], send_sem=send.at[t, j],
                    recv_sem=recv.at[t, j], device_id=(cx, cy, c), device_id_type=_MESH))
        for cp in cps:
            cp.start()
        for t in range(n):
            for j, (cx, cy) in enumerate(chips):
                pltpu.make_async_remote_copy(
                    src_ref=p[t].at[me], dst_ref=land[t].at[2 * cx + cy], send_sem=send.at[t, j],
                    recv_sem=recv.at[t, j], device_id=(cx, cy, c), device_id_type=_MESH).wait_recv()
        for cp in cps:
            cp.wait_send()
        for cp in mine:
            cp.wait()

    return pl.pallas_call(
        body, name=name,
        out_shape=tuple(jax.ShapeDtypeStruct(t.shape, t.dtype) for t in ps),
        in_specs=[_ANY] * n, out_specs=tuple([_ANY] * n),
        scratch_shapes=[pltpu.SemaphoreType.DMA((n, 3)), pltpu.SemaphoreType.DMA((n, 3)),
                        pltpu.SemaphoreType.DMA((n,))],
        compiler_params=_params())(*ps)


def _join_halves(rs, name):
    n = len(rs)

    def body(*refs):
        r_in = refs[:n]
        out = refs[n:2 * n]
        send, recv, local = refs[2 * n:]
        x, y, c, _ = _place()
        mine = [pltpu.make_async_copy(r_in[t], out[t].at[c], local.at[t]) for t in range(n)]
        for cp in mine:
            cp.start()
        cps = [pltpu.make_async_remote_copy(
            src_ref=r_in[t], dst_ref=out[t].at[c], send_sem=send.at[t], recv_sem=recv.at[t],
            device_id=(x, y, 1 - c), device_id_type=_MESH) for t in range(n)]
        for cp in cps:
            cp.start()
        for t in range(n):
            pltpu.make_async_remote_copy(
                src_ref=r_in[t], dst_ref=out[t].at[1 - c], send_sem=send.at[t], recv_sem=recv.at[t],
                device_id=(x, y, 1 - c), device_id_type=_MESH).wait_recv()
        for cp in cps:
            cp.wait_send()
        for cp in mine:
            cp.wait()

    return pl.pallas_call(
        body, name=name,
        out_shape=tuple(jax.ShapeDtypeStruct((2,) + t.shape, t.dtype) for t in rs),
        in_specs=[_ANY] * n, out_specs=tuple([_ANY] * n),
        scratch_shapes=[pltpu.SemaphoreType.DMA((n,)), pltpu.SemaphoreType.DMA((n,)),
                        pltpu.SemaphoreType.DMA((n,))],
        compiler_params=_params())(*rs)


def _all_devices(v, reduce, name):
    rows = v.shape[0]

    def body(v_ref, o_ref, *rest):
        if reduce:
            all_ref, send, recv = rest
        else:
            send, recv = rest
            all_ref = o_ref
        x, y, c, _ = _place()
        me = 4 * x + 2 * y + c
        all_ref[me] = v_ref[...]
        cps = []
        for k in range(1, 8):
            fx, fy, fc = (k >> 2) & 1, (k >> 1) & 1, k & 1
            to = (x ^ fx, y ^ fy, c ^ fc)
            cps.append(pltpu.make_async_remote_copy(
                src_ref=v_ref, dst_ref=all_ref.at[me], send_sem=send.at[k - 1], recv_sem=recv.at[k - 1],
                device_id=to, device_id_type=_MESH))
        for cp in cps:
            cp.start()
        for k in range(1, 8):
            fx, fy, fc = (k >> 2) & 1, (k >> 1) & 1, k & 1
            frm = 4 * (x ^ fx) + 2 * (y ^ fy) + (c ^ fc)
            pltpu.make_async_remote_copy(
                src_ref=v_ref, dst_ref=all_ref.at[frm], send_sem=send.at[k - 1], recv_sem=recv.at[k - 1],
                device_id=(x, y, c), device_id_type=_MESH).wait_recv()
        for cp in cps:
            cp.wait_send()
        if reduce:
            acc = all_ref[0]
            for d in range(1, 8):
                acc = acc + all_ref[d]
            o_ref[...] = acc

    vm = pl.BlockSpec(memory_space=pltpu.VMEM)
    if reduce:
        out_shape = jax.ShapeDtypeStruct((rows, _LANES), _F32)
        scratch = [pltpu.VMEM((8, rows, _LANES), _F32)]
    else:
        out_shape = jax.ShapeDtypeStruct((8, rows, _LANES), _F32)
        scratch = []
    return pl.pallas_call(
        body, name=name, out_shape=out_shape, in_specs=[vm], out_specs=vm,
        scratch_shapes=scratch + [pltpu.SemaphoreType.DMA((7,)), pltpu.SemaphoreType.DMA((7,))],
        compiler_params=_params())(v)


def _gather_layer_weights(shards, name):
    halves = [shards[k].reshape((2, shards[k].shape[0] // 2) + shards[k].shape[1:]) for k in BIG]
    got = _gather_weights(halves, name)
    full = {}
    for k, g in zip(BIG, got):
        _, _, r, c = g.shape
        if k in ('w_in', 'ffn_w13'):
            full[k] = jnp.transpose(g, (1, 2, 0, 3)).reshape(2 * r, 4 * c)
        else:
            full[k] = g.reshape(8 * r, c)
    return full


def _reduce_layer_grads(grads, cidx, name):
    ts = []
    for k in BIG:
        g = grads[k]
        if k in ('w_in', 'ffn_w13'):
            r, c = g.shape[0] // 2, g.shape[1] // 4
            ts.append(jnp.transpose(g.reshape(2, r, 4, c), (2, 0, 1, 3)))
        else:
            r, c = g.shape[0] // 8, g.shape[1]
            ts.append(g.reshape(4, 2, r, c))
    landed = _swap_halves(ts, name + "_swap")
    partial = [_add_pair(t, la, cidx, f"{name}_add2_{k}") for k, t, la in zip(BIG, ts, landed)]
    slots = _scatter_chips(partial, name + "_scatter")
    reduced = [_add_four(s, f"{name}_add4_{k}") for k, s in zip(BIG, slots)]
    joined = _join_halves(reduced, name + "_join")
    return {k: j.reshape((2 * j.shape[1], j.shape[2])) for k, j in zip(BIG, joined)}


def _pack_rows(arrs):
    parts, spans = [], []
    off = 0
    for a in arrs:
        n = int(np.prod(a.shape))
        pad = (-n) % (8 * _LANES)
        parts.append(jnp.pad(a.reshape(-1), (0, pad)))
        spans.append((off, n, a.shape))
        off += n + pad
    return jnp.concatenate(parts).reshape(-1, _LANES), spans


def _unpack_rows(packed, spans):
    flat = packed.reshape(-1)
    return [flat[o:o + n].reshape(shape) for o, n, shape in spans]


def _in_split(d):
    dk, dv, w = d // 4, d // 2, d // 4
    names = [('g_q', dk), ('g_k', dk), ('g_v', dv), ('g_r', dv), ('g_f', GLA_GATE_RANK), ('g_b', GLA_GATE_RANK),
             ('n_q', w), ('n_k', w), ('n_v', w), ('m_q', w)]
    out, off = {}, 0
    for nme, wd in names:
        out[nme] = (off, wd)
        off += wd
    return out


def _permute_w_in(w):
    sp = _in_split(w.shape[0])
    f0 = sp['g_f'][0]
    n0 = sp['n_q'][0]
    main = jnp.concatenate([w[:, :f0], w[:, n0:]], axis=1)
    gate = jnp.pad(w[:, f0:n0], ((0, 0), (0, _LANES - (n0 - f0))))
    return main, gate


def _unpermute_w_in(main, gate, d):
    sp = _in_split(d)
    f0 = sp['g_f'][0]
    n0 = sp['n_q'][0]
    return jnp.concatenate([main[:, :f0], gate[:, :n0 - f0], main[:, f0:]], axis=1)


def _gate_weight(wg2_f, wg2_b):
    r, dk = wg2_f.shape
    top = jnp.concatenate([wg2_f, jnp.zeros_like(wg2_f)], axis=1)
    mid = jnp.concatenate([jnp.zeros_like(wg2_b), wg2_b], axis=1)
    return jnp.concatenate([top, mid, jnp.zeros((_LANES - 2 * r, 2 * dk), wg2_f.dtype)], axis=0)


def _layer_fwd(l, x, mem, w, full, tab):
    S, D = x.shape
    dk, dv, nw = D // 4, D // 2, D // 4
    sv = {}
    w_main, w_gate = _permute_w_in(full['w_in'])
    sv['w_main'], sv['w_gate'] = w_main, w_gate
    xn, r1 = _rmsnorm_fwd(x, w['attn_norm'][None], f"rms1_fwd")
    proj = _mm(xn, w_main, 'nn', "mm_proj")
    gfb = _mm(xn, w_gate, 'nn', "mm_gate")
    sv.update(x=x, xn=xn, r1=r1, proj=proj, gfb=gfb)
    wg = _gate_weight(w['gla_wg2_f'], w['gla_wg2_b']).astype(_BF)
    bg = jnp.concatenate([w['gla_bg_f'], w['gla_bg_b']])[None]
    o_f, st_f = _gla_fwd(proj, gfb, wg, bg, False, "gla_fwd_f")
    o_b, st_b = _gla_fwd(proj, gfb, wg, bg, True, "gla_fwd_b")
    y_gla = _gla_post_fwd(o_f, o_b, proj, (2 * dk) // dv + 1, w['gla_out_norm'][None], "gla_post_fwd")
    sv.update(wg=wg, bg=bg, o_f=o_f, o_b=o_b, st_f=st_f, st_b=st_b)
    c0 = (2 * dk + 2 * dv) // nw
    qn = _headnorm_fwd(proj, c0, nw, NA_HD, jnp.tile(w['na_q_norm'], nw // NA_HD)[None], "na_qnorm_fwd")
    kn = _headnorm_fwd(proj, c0 + 1, nw, NA_HD, jnp.tile(w['na_k_norm'], nw // NA_HD)[None], "na_knorm_fwd")
    vb = proj[:, (c0 + 2) * nw:(c0 + 3) * nw].astype(_BF)
    o_na = _na_fwd(qn, kn, vb, tab, "na_fwd")
    y_na = _headnorm_fwd(o_na, 0, nw, NA_HD, w['na_out_norm'][None], "na_onorm_fwd")
    sv.update(qn=qn, kn=kn, vb=vb, o_na=o_na)
    mhd = nw // MEM_HEADS
    mqn = _headnorm_fwd(proj, c0 + 3, nw, mhd, jnp.tile(w['mem_q_norm'], MEM_HEADS)[None], "mem_qnorm_fwd")
    memn, rm = _rmsnorm_fwd(mem, w['mem_norm'][None], "mem_rms_fwd")
    kv = _mm(memn, full['mem_wkv'], 'nn', "mm_memkv")
    km = _headnorm_fwd(kv, 0, nw, mhd, jnp.tile(w['mem_k_norm'], MEM_HEADS)[None], "mem_knorm_fwd")
    vm = kv[:, nw:].astype(_BF)
    o_mem = _mem_fwd(mqn, km, vm, "mem_fwd")
    y_mem = _headnorm_fwd(o_mem, 0, nw, mhd, w['mem_out_norm'][None], "mem_onorm_fwd")
    sv.update(mqn=mqn, memn=memn, rm=rm, kv=kv, km=km, vm=vm, o_mem=o_mem)
    y = jnp.concatenate([y_gla, y_na, y_mem], axis=1)
    x1 = _mm(y, full['w_out'], 'nn', "mm_out", res=x)
    h, r2 = _rmsnorm_fwd(x1, w['ffn_norm'][None], "rms2_fwd")
    gu = _mm(h, full['ffn_w13'], 'nn', "mm_w13")
    a = _swiglu_fwd(gu, "swiglu_fwd")
    x2 = _mm(a, full['ffn_w2'], 'nn', "mm_w2", res=x1)
    sv.update(y=y, x1=x1, h=h, r2=r2, gu=gu, a=a)
    return x2, sv


def _layer_bwd(l, dx2, dx2b, mem, w, full, tab, sv):
    S, D = dx2.shape
    dk, dv, nw = D // 4, D // 2, D // 4
    gb, gs = {}, {}
    da = _mm(dx2b, full['ffn_w2'], 'nt', "mm_da")
    gb['ffn_w2'] = _mm(sv['a'], dx2b, 'tn', "mm_dw2")
    dgu = _swiglu_bwd(sv['gu'], da, "swiglu_bwd")
    dh = _mm(dgu, full['ffn_w13'], 'nt', "mm_dh")
    gb['ffn_w13'] = _mm(sv['h'], dgu, 'tn', "mm_dw13")
    dx1, dx1b, g = _rmsnorm_bwd(sv['x1'], sv['r2'], w['ffn_norm'][None], dh, dx2, "rms2_bwd")
    gs['ffn_norm'] = g[0]
    dy = _mm(dx1b, full['w_out'], 'nt', "mm_dy")
    gb['w_out'] = _mm(sv['y'], dx1b, 'tn', "mm_dwout")
    c0 = (2 * dk + 2 * dv) // nw
    mhd = nw // MEM_HEADS
    do_mem, g = _headnorm_bwd(sv['o_mem'], 0, nw, mhd, w['mem_out_norm'][None], dy, (dv + nw) // nw, "mem_onorm_bwd")
    gs['mem_out_norm'] = g[0]
    dmqn, dkm, dvm = _mem_bwd(sv['mqn'], sv['km'], sv['vm'], do_mem, "mem_bwd")
    dmq, g = _headnorm_bwd(sv['proj'], c0 + 3, nw, mhd, jnp.tile(w['mem_q_norm'], MEM_HEADS)[None], dmqn, 0,
                           "mem_qnorm_bwd", _BF)
    gs['mem_q_norm'] = g[0].reshape(MEM_HEADS, mhd).sum(0)
    dkvk, g = _headnorm_bwd(sv['kv'], 0, nw, mhd, jnp.tile(w['mem_k_norm'], MEM_HEADS)[None], dkm, 0,
                            "mem_knorm_bwd")
    gs['mem_k_norm'] = g[0].reshape(MEM_HEADS, mhd).sum(0)
    dkv = jnp.concatenate([dkvk, dvm], axis=1).astype(_BF)
    gb['mem_wkv'] = _mm(sv['memn'], dkv, 'tn', "mm_dwkv")
    dmemn = _mm(dkv, full['mem_wkv'], 'nt', "mm_dmemn")
    _, _, g = _rmsnorm_bwd(mem, sv['rm'], w['mem_norm'][None], dmemn, None, "mem_rms_bwd")
    gs['mem_norm'] = g[0]
    do_na, g = _headnorm_bwd(sv['o_na'], 0, nw, NA_HD, w['na_out_norm'][None], dy, dv // nw, "na_onorm_bwd")
    gs['na_out_norm'] = g[0]
    dqn, dkn, dnv, dtab = _na_bwd(sv['qn'], sv['kn'], sv['vb'], tab, do_na, "na_bwd")
    gs['na_rpb'] = _na_bias_grad(dtab)
    dnq, g = _headnorm_bwd(sv['proj'], c0, nw, NA_HD, jnp.tile(w['na_q_norm'], nw // NA_HD)[None], dqn, 0,
                           "na_qnorm_bwd", _BF)
    gs['na_q_norm'] = g[0].reshape(nw // NA_HD, NA_HD).sum(0)
    dnk, g = _headnorm_bwd(sv['proj'], c0 + 1, nw, NA_HD, jnp.tile(w['na_k_norm'], nw // NA_HD)[None], dkn, 0,
                           "na_knorm_bwd", _BF)
    gs['na_k_norm'] = g[0].reshape(nw // NA_HD, NA_HD).sum(0)
    do_gla, dgr, g = _gla_post_bwd(sv['o_f'], sv['o_b'], sv['proj'], (2 * dk) // dv + 1, w['gla_out_norm'][None], dy,
                                   "gla_post_bwd")
    gs['gla_out_norm'] = g[0]
    dq1, dk1, dv1, dz_f = _gla_bwd(sv['proj'], sv['gfb'], sv['wg'], sv['bg'], sv['st_f'], do_gla, False, None,
                                   "gla_bwd_f")
    dgq, dgk, dgv, dz_b = _gla_bwd(sv['proj'], sv['gfb'], sv['wg'], sv['bg'], sv['st_b'], do_gla, True,
                                   (dq1, dk1, dv1), "gla_bwd_b")
    dz = jnp.concatenate([dz_f, dz_b], axis=1).astype(_BF)
    ones_lane = (jnp.arange(_LANES) == 2 * GLA_GATE_RANK)[None]
    gfb_aug = jnp.where(ones_lane, 1.0, sv['gfb']).astype(_BF)
    dwg = _mm(gfb_aug, dz, 'tn', "mm_dwg")
    r16 = GLA_GATE_RANK
    gs['gla_wg2_f'] = dwg[:r16, :dk]
    gs['gla_wg2_b'] = dwg[r16:2 * r16, dk:]
    gs['gla_bg_f'] = dwg[2 * r16, :dk]
    gs['gla_bg_b'] = dwg[2 * r16, dk:]
    dgfb = _mm(dz, sv['wg'], 'nt', "mm_dgfb", out_dtype=_BF)
    dproj = jnp.concatenate([dgq.astype(_BF), dgk.astype(_BF), dgv.astype(_BF), dgr, dnq, dnk, dnv.astype(_BF), dmq],
                            axis=1)
    t = _mm(dgfb, sv['w_gate'], 'nt', "mm_dxn_gate")
    dxn = _mm(dproj, sv['w_main'], 'nt', "mm_dxn", res=t)
    dw_main = _mm(sv['xn'], dproj, 'tn', "mm_dwmain")
    dw_gate = _mm(sv['xn'], dgfb, 'tn', "mm_dwgate")
    gb['w_in'] = _unpermute_w_in(dw_main, dw_gate, D)
    dx, dxb, g = _rmsnorm_bwd(sv['x'], sv['r1'], w['attn_norm'][None], dxn, dx1, "rms1_bwd")
    gs['attn_norm'] = g[0]
    return dx, dxb, gb, gs


def kernel(x, mem, attn_norm, w_in, gla_wg2_f, gla_bg_f, gla_wg2_b, gla_bg_b, gla_out_norm, na_q_norm, na_k_norm, na_rpb, na_out_norm, mem_norm, mem_wkv, mem_q_norm, mem_k_norm, mem_out_norm, w_out, ffn_norm, ffn_w13, ffn_w2, loss_target, m_attn_norm, m_w_in, m_gla_wg2_f, m_gla_bg_f, m_gla_wg2_b, m_gla_bg_b, m_gla_out_norm, m_na_q_norm, m_na_k_norm, m_na_rpb, m_na_out_norm, m_mem_norm, m_mem_wkv, m_mem_q_norm, m_mem_k_norm, m_mem_out_norm, m_w_out, m_ffn_norm, m_ffn_w13, m_ffn_w2, v_attn_norm, v_w_in, v_gla_wg2_f, v_gla_bg_f, v_gla_wg2_b, v_gla_bg_b, v_gla_out_norm, v_na_q_norm, v_na_k_norm, v_na_rpb, v_na_out_norm, v_mem_norm, v_mem_wkv, v_mem_q_norm, v_mem_k_norm, v_mem_out_norm, v_w_out, v_ffn_norm, v_ffn_w13, v_ffn_w2):
    args = locals()
    W = {k: args[k] for k in WEIGHTS}
    M = {k: args['m_' + k] for k in WEIGHTS}
    V = {k: args['v_' + k] for k in WEIGHTS}
    depth = attn_norm.shape[0]
    xs, mems, tgt = x[0], mem[0], loss_target[0]
    cidx = lax.axis_index("c").astype(jnp.int32).reshape(1)

    gate_rows, gate_spans = _pack_rows([W[k] for k in SMALL_SHARDED])
    gate_all = _all_devices(gate_rows, False, "gate_weights_gather")
    gate_full = {}
    for i, k in enumerate(SMALL_SHARDED):
        per_chip = [_unpack_rows(gate_all[2 * j], gate_spans)[i] for j in range(4)]
        gate_full[k] = jnp.concatenate(per_chip, axis=-1)

    small = [k for k in WEIGHTS if k not in BIG]

    def layer_small(l):
        d = {k: W[k][l] for k in small if k not in SMALL_SHARDED}
        d.update({k: gate_full[k][l] for k in SMALL_SHARDED})
        return d

    saved, fulls, tabs = [], [], []
    cur = xs
    for l in range(depth):
        full = _gather_layer_weights({k: W[k][l].astype(_BF) for k in BIG}, "gather_weights")
        tab = _na_bias_tables(W['na_rpb'][l])
        cur, sv = _layer_fwd(l, cur, mems, layer_small(l), full, tab)
        saved.append(sv)
        fulls.append(full)
        tabs.append(tab)
    loss_tile, dy, dyb = _loss_head(cur, tgt, "loss_head")
    loss = lax.psum(loss_tile[0, 0], ("x", "y", "c"))

    big_grads = [None] * depth
    small_grads = [None] * depth
    for l in range(depth - 1, -1, -1):
        dy, dyb, gb, gs = _layer_bwd(l, dy, dyb, mems, layer_small(l), fulls[l], tabs[l], saved[l])
        big_grads[l] = _reduce_layer_grads(gb, cidx, "reduce_grads")
        small_grads[l] = gs
    grad_x = dy[None]

    small_stack = [jnp.stack([small_grads[l][k] for l in range(depth)]) for k in small]
    packed, spans = _pack_rows(small_stack)
    summed = _unpack_rows(_all_devices(packed, True, "small_grads_sum"), spans)
    G = dict(zip(small, summed))
    chip = 2 * lax.axis_index("x") + lax.axis_index("y")
    for k in SMALL_SHARDED:
        wdt = W[k].shape[-1]
        G[k] = lax.dynamic_slice_in_dim(G[k], chip * wdt, wdt, axis=2)
    for k in BIG:
        G[k] = jnp.stack([big_grads[l][k] for l in range(depth)])

    delta, new_m, new_v = {}, {}, {}
    for k in WEIGHTS:
        delta[k], new_m[k], new_v[k] = _adamw(W[k], G[k], M[k], V[k], "adamw_" + k)
    return (loss, grad_x, *[G[k] for k in WEIGHTS], *[delta[k] for k in WEIGHTS], *[new_m[k] for k in WEIGHTS],
            *[new_v[k] for k in WEIGHTS])
```

```python
import functools

import numpy as np
import jax
import jax.numpy as jnp
from jax import lax
from jax.experimental import pallas as pl
from jax.experimental.pallas import tpu as pltpu

_F32 = jnp.float32
_BF = jnp.bfloat16
_MESH = pl.DeviceIdType.MESH

_VMEM_LIMIT_BYTES = 56 * 1024 * 1024
_LANES = 128

RMS_EPS = 1e-6
GLA_HEADS = 4
GLA_GATE_RANK = 16
GLA_TAU = 16.0
GLA_CHUNK = 64
GRID_W = 64
NA_HD = 64
NA_ROWS = 8
NA_COLS = 16
MEM_HEADS = 4
ADAM_LR = 0.001
ADAM_B1 = 0.9
ADAM_B2 = 0.999
ADAM_EPS = 1e-08
ADAM_WD = 0.01
ADAM_STEP = 10

WEIGHTS = ['attn_norm', 'w_in', 'gla_wg2_f', 'gla_bg_f', 'gla_wg2_b', 'gla_bg_b', 'gla_out_norm', 'na_q_norm',
           'na_k_norm', 'na_rpb', 'na_out_norm', 'mem_norm', 'mem_wkv', 'mem_q_norm', 'mem_k_norm', 'mem_out_norm',
           'w_out', 'ffn_norm', 'ffn_w13', 'ffn_w2']
BIG = ['w_in', 'mem_wkv', 'w_out', 'ffn_w13', 'ffn_w2']
SMALL_SHARDED = ['gla_wg2_f', 'gla_wg2_b']


def _pick(n, cands):
    for c in cands:
        if n % c == 0:
            return c
    return n


def _row_block(rows, row_bytes, cap_bytes):
    best = 8
    for rb in range(8, rows + 1, 8):
        if rows % rb == 0 and rb * row_bytes <= cap_bytes:
            best = rb
    return best


def _params(sem=None):
    return pltpu.CompilerParams(dimension_semantics=sem, vmem_limit_bytes=_VMEM_LIMIT_BYTES)


def _dg(a, b, ca, cb):
    return lax.dot_general(a.astype(_BF), b.astype(_BF), (((ca,), (cb,)), ((), ())), preferred_element_type=_F32)


def _split(a):
    hi = a.astype(_BF)
    return hi, (a - hi.astype(_F32)).astype(_BF)


def _dg_hl(a, b, ca, cb):
    hi, lo = _split(a)
    return _dg(hi, b, ca, cb) + _dg(lo, b, ca, cb)


def _dg_lh(a, b, ca, cb):
    hi, lo = _split(b)
    return _dg(a, hi, ca, cb) + _dg(a, lo, ca, cb)


def _sigmoid(x):
    return 1.0 / (1.0 + jnp.exp(-x))


def _log_sigmoid(z):
    return jnp.minimum(z, 0.0) - jnp.log(1.0 + jnp.exp(-jnp.abs(z)))


def _block_diag(width, hd):
    i = np.arange(width) // hd
    return jnp.asarray((i[:, None] == i[None, :]).astype(np.float32), dtype=_BF)


def _mm(a, b, mode, name, out_dtype=_F32, res=None):
    if mode == 'nn':
        (M, K), N = a.shape, b.shape[1]
    elif mode == 'nt':
        (M, K), N = a.shape, b.shape[0]
    else:
        (K, M), N = a.shape, b.shape[1]
    ca, cb = {'nn': (1, 0), 'nt': (1, 1), 'tn': (0, 0)}[mode]
    has_res = res is not None
    args = (a, b) + ((res,) if has_res else ())
    out_shape = jax.ShapeDtypeStruct((M, N), out_dtype)
    if K <= _K_RESIDENT:
        tm = _pick(M, (1024, 512, 256, 128))
        tn = _pick(N, (1024, 512, 256, 128))
        a_spec = (pl.BlockSpec((K, tm), lambda i, j: (0, i)) if mode == 'tn'
                  else pl.BlockSpec((tm, K), lambda i, j: (i, 0)))
        b_spec = (pl.BlockSpec((tn, K), lambda i, j: (j, 0)) if mode == 'nt'
                  else pl.BlockSpec((K, tn), lambda i, j: (0, j)))
        o_spec = pl.BlockSpec((tm, tn), lambda i, j: (i, j))

        def body1(*refs):
            a_ref, b_ref = refs[:2]
            o_ref = refs[-1]
            out = _dg(a_ref[...], b_ref[...], ca, cb)
            if has_res:
                out = out + refs[2][...]
            o_ref[...] = out.astype(o_ref.dtype)

        return pl.pallas_call(
            body1, name=name, out_shape=out_shape, grid=(M // tm, N // tn),
            in_specs=[a_spec, b_spec] + ([o_spec] if has_res else []), out_specs=o_spec,
            compiler_params=_params(("parallel", "parallel")))(*args)

    tk = _pick(K, (512, 256, 128))
    nk = K // tk
    tm, tn = _loop_tiles(M, N, tk, has_res, out_dtype)
    if mode == 'tn':
        a_spec = pl.BlockSpec((tk, tm), lambda i, j, k: (k, i))
    else:
        a_spec = pl.BlockSpec((tm, tk), lambda i, j, k: (i, k))
    if mode == 'nt':
        b_spec = pl.BlockSpec((tn, tk), lambda i, j, k: (j, k))
    else:
        b_spec = pl.BlockSpec((tk, tn), lambda i, j, k: (k, j))
    o_spec = pl.BlockSpec((tm, tn), lambda i, j, k: (i, j))
    in_out = out_dtype == _F32

    def body(*refs):
        a_ref, b_ref = refs[:2]
        o_ref = refs[-1] if in_out else refs[-2]
        acc = o_ref if in_out else refs[-1]
        k = pl.program_id(2)
        part = _dg(a_ref[...], b_ref[...], ca, cb)

        @pl.when(k == 0)
        def _():
            acc[...] = part + refs[2][...] if has_res else part

        @pl.when(k > 0)
        def _():
            acc[...] += part

        if not in_out:
            @pl.when(k == nk - 1)
            def _():
                o_ref[...] = acc[...].astype(o_ref.dtype)

    return pl.pallas_call(
        body, name=name, out_shape=out_shape, grid=(M // tm, N // tn, nk),
        in_specs=[a_spec, b_spec] + ([o_spec] if has_res else []), out_specs=o_spec,
        scratch_shapes=[] if in_out else [pltpu.VMEM((tm, tn), _F32)],
        compiler_params=_params(("parallel", "parallel", "arbitrary")))(*args)


_K_RESIDENT = 2048
_LOOP_TILE_BYTES = 28 * 1024 * 1024


def _loop_tiles(M, N, tk, has_res, out_dtype):
    best = None
    for tm in (2048, 1408, 1024, 512, 256, 128):
        for tn in (2048, 1024, 512, 256, 128):
            if M % tm or N % tn:
                continue
            obytes = 4 if out_dtype == _F32 else 2
            need = 2 * 2 * tk * (tm + tn) + tm * tn * (2 * obytes + (8 if has_res else 0) + (0 if obytes == 4 else 4))
            if need <= _LOOP_TILE_BYTES and (best is None or tm * tn > best[0] * best[1]):
                best = (tm, tn)
    return best


_SWIGLU_TILE = 512


def _interleave_gate_up(w, inverse=False):
    lead, f2 = w.shape[:-1], w.shape[-1]
    nt = f2 // (2 * _SWIGLU_TILE)
    shape = lead + ((nt, 2, _SWIGLU_TILE) if inverse else (2, nt, _SWIGLU_TILE))
    return jnp.swapaxes(w.reshape(shape), -3, -2).reshape(lead + (f2,))


def _mm_swiglu_fwd(h, w13i, name):
    S, K = h.shape
    F2 = w13i.shape[1]
    tf = _SWIGLU_TILE
    tm = _pick(S, (1024, 512, 256, 128))

    def body(h_ref, w_ref, a_ref, gu_ref):
        r = _dg(h_ref[...], w_ref[...], 1, 0)
        gv, uv = r[:, :tf], r[:, tf:]
        a_ref[...] = (gv * _sigmoid(gv) * uv).astype(a_ref.dtype)
        gu_ref[...] = r.astype(gu_ref.dtype)

    return pl.pallas_call(
        body, name=name,
        out_shape=(jax.ShapeDtypeStruct((S, F2 // 2), _BF), jax.ShapeDtypeStruct((S, F2), _BF)),
        grid=(S // tm, F2 // (2 * tf)),
        in_specs=[pl.BlockSpec((tm, K), lambda i, j: (i, 0)), pl.BlockSpec((K, 2 * tf), lambda i, j: (0, j))],
        out_specs=(pl.BlockSpec((tm, tf), lambda i, j: (i, j)), pl.BlockSpec((tm, 2 * tf), lambda i, j: (i, j))),
        compiler_params=_params(("parallel", "parallel")))(h, w13i)


def _mm_swiglu_bwd(dxb, w2, gu, name):
    S, D = dxb.shape
    F = w2.shape[0]
    tf = _SWIGLU_TILE
    tm = _pick(S, (1024, 512, 256, 128))

    def body(d_ref, w_ref, gu_ref, o_ref):
        da = _dg(d_ref[...], w_ref[...], 1, 1)
        gv = gu_ref[:, :tf].astype(_F32)
        uv = gu_ref[:, tf:].astype(_F32)
        sg = _sigmoid(gv)
        o_ref[:, :tf] = (da * uv * (sg * (1.0 + gv * (1.0 - sg)))).astype(o_ref.dtype)
        o_ref[:, tf:] = (da * (gv * sg)).astype(o_ref.dtype)

    return pl.pallas_call(
        body, name=name, out_shape=jax.ShapeDtypeStruct((S, 2 * F), _BF), grid=(S // tm, F // tf),
        in_specs=[pl.BlockSpec((tm, D), lambda i, j: (i, 0)), pl.BlockSpec((tf, D), lambda i, j: (j, 0)),
                  pl.BlockSpec((tm, 2 * tf), lambda i, j: (i, j))],
        out_specs=pl.BlockSpec((tm, 2 * tf), lambda i, j: (i, j)),
        compiler_params=_params(("parallel", "parallel")))(dxb, w2, gu)


def _rmsnorm_fwd(x, g, name):
    S, D = x.shape
    ts = _pick(S, (256,))

    def body(x_ref, g_ref, o_ref, r_ref):
        xv = x_ref[...]
        r = lax.rsqrt(jnp.mean(xv * xv, axis=-1, keepdims=True) + RMS_EPS)
        o_ref[...] = (xv * r * g_ref[...]).astype(o_ref.dtype)
        r_ref[...] = r

    return pl.pallas_call(
        body, name=name,
        out_shape=(jax.ShapeDtypeStruct((S, D), _BF), jax.ShapeDtypeStruct((S, 1), _F32)),
        grid=(S // ts,),
        in_specs=[pl.BlockSpec((ts, D), lambda i: (i, 0)), pl.BlockSpec((1, D), lambda i: (0, 0))],
        out_specs=(pl.BlockSpec((ts, D), lambda i: (i, 0)), pl.BlockSpec((ts, 1), lambda i: (i, 0))),
        compiler_params=_params(("parallel",)))(x, g)


def _rmsnorm_bwd(x, r, g, dy, dres, name):
    S, D = x.shape
    ts = _pick(S, (256,))
    has_res = dres is not None

    def body(*refs):
        if has_res:
            x_ref, r_ref, g_ref, dy_ref, dr_ref, dx_ref, dxb_ref, dg_ref = refs
        else:
            x_ref, r_ref, g_ref, dy_ref, dx_ref, dxb_ref, dg_ref = refs
        rv = r_ref[...]
        n = x_ref[...] * rv
        dyv = dy_ref[...]
        dn = dyv * g_ref[...]
        c = jnp.mean(dn * n, axis=-1, keepdims=True)
        dx = rv * (dn - n * c)
        if has_res:
            dx = dx + dr_ref[...]
        dx_ref[...] = dx
        dxb_ref[...] = dx.astype(dxb_ref.dtype)

        @pl.when(pl.program_id(0) == 0)
        def _():
            dg_ref[...] = jnp.zeros_like(dg_ref)

        dg_ref[...] += jnp.sum(dyv * n, axis=0, keepdims=True)

    row = pl.BlockSpec((ts, D), lambda i: (i, 0))
    vec = pl.BlockSpec((1, D), lambda i: (0, 0))
    in_specs = [row, pl.BlockSpec((ts, 1), lambda i: (i, 0)), vec, row] + ([row] if has_res else [])
    args = (x, r, g, dy) + ((dres,) if has_res else ())
    return pl.pallas_call(
        body, name=name,
        out_shape=(jax.ShapeDtypeStruct((S, D), _F32), jax.ShapeDtypeStruct((S, D), _BF),
                   jax.ShapeDtypeStruct((1, D), _F32)),
        grid=(S // ts,), in_specs=in_specs, out_specs=(row, row, vec),
        compiler_params=_params(("arbitrary",)))(*args)


def _headnorm_fwd(t, cb, W, hd, g, name):
    S = t.shape[0]
    ts = _pick(S, (512, 256))
    bd = _block_diag(W, hd)

    def body(x_ref, g_ref, bd_ref, o_ref):
        xv = x_ref[...].astype(_F32)
        ms = _dg_hl(xv * xv, bd_ref[...], 1, 0) * (1.0 / hd)
        o_ref[...] = (xv * lax.rsqrt(ms + RMS_EPS) * g_ref[...]).astype(o_ref.dtype)

    return pl.pallas_call(
        body, name=name, out_shape=jax.ShapeDtypeStruct((S, W), _BF), grid=(S // ts,),
        in_specs=[pl.BlockSpec((ts, W), lambda i: (i, cb)), pl.BlockSpec((1, W), lambda i: (0, 0)),
                  pl.BlockSpec((W, W), lambda i: (0, 0))],
        out_specs=pl.BlockSpec((ts, W), lambda i: (i, 0)),
        compiler_params=_params(("parallel",)))(t, g, bd)


def _headnorm_bwd(t, cb, W, hd, g, dy, dcb, name, out_dtype=_F32):
    S = t.shape[0]
    ts = _pick(S, (512, 256))
    bd = _block_diag(W, hd)

    def body(x_ref, g_ref, bd_ref, dy_ref, dx_ref, dg_ref):
        xv = x_ref[...].astype(_F32)
        bdv = bd_ref[...]
        ms = _dg_hl(xv * xv, bdv, 1, 0) * (1.0 / hd)
        rv = lax.rsqrt(ms + RMS_EPS)
        n = xv * rv
        dyv = dy_ref[...].astype(_F32)
        dn = dyv * g_ref[...]
        c = _dg_hl(dn * n, bdv, 1, 0) * (1.0 / hd)
        dx_ref[...] = (rv * (dn - n * c)).astype(dx_ref.dtype)

        @pl.when(pl.program_id(0) == 0)
        def _():
            dg_ref[...] = jnp.zeros_like(dg_ref)

        dg_ref[...] += jnp.sum(dyv * n, axis=0, keepdims=True)

    return pl.pallas_call(
        body, name=name,
        out_shape=(jax.ShapeDtypeStruct((S, W), out_dtype), jax.ShapeDtypeStruct((1, W), _F32)),
        grid=(S // ts,),
        in_specs=[pl.BlockSpec((ts, W), lambda i: (i, cb)), pl.BlockSpec((1, W), lambda i: (0, 0)),
                  pl.BlockSpec((W, W), lambda i: (0, 0)), pl.BlockSpec((ts, W), lambda i: (i, dcb))],
        out_specs=(pl.BlockSpec((ts, W), lambda i: (i, 0)), pl.BlockSpec((1, W), lambda i: (0, 0))),
        compiler_params=_params(("arbitrary",)))(t, g, bd, dy)


def _gla_post_fwd(o_f, o_b, proj, r_cb, g, name):
    S, W = o_f.shape
    hd = W // GLA_HEADS
    ts = _pick(S, (256,))
    bd = _block_diag(W, hd)

    def body(of_ref, ob_ref, r_ref, g_ref, bd_ref, y_ref):
        o = of_ref[...] + ob_ref[...]
        ms = _dg_hl(o * o, bd_ref[...], 1, 0) * (1.0 / hd)
        u = o * lax.rsqrt(ms + RMS_EPS) * g_ref[...]
        rr = r_ref[...]
        y_ref[...] = (u * (rr * _sigmoid(rr))).astype(y_ref.dtype)

    row = pl.BlockSpec((ts, W), lambda i: (i, 0))
    return pl.pallas_call(
        body, name=name, out_shape=jax.ShapeDtypeStruct((S, W), _BF), grid=(S // ts,),
        in_specs=[row, row, pl.BlockSpec((ts, W), lambda i: (i, r_cb)), pl.BlockSpec((1, W), lambda i: (0, 0)),
                  pl.BlockSpec((W, W), lambda i: (0, 0))],
        out_specs=row, compiler_params=_params(("parallel",)))(o_f, o_b, proj, g, bd)


def _gla_post_bwd(o_f, o_b, proj, r_cb, g, dy, name):
    S, W = o_f.shape
    hd = W // GLA_HEADS
    ts = _pick(S, (256,))
    bd = _block_diag(W, hd)

    def body(of_ref, ob_ref, r_ref, g_ref, bd_ref, dy_ref, do_ref, dr_ref, dg_ref):
        o = of_ref[...] + ob_ref[...]
        bdv = bd_ref[...]
        ms = _dg_hl(o * o, bdv, 1, 0) * (1.0 / hd)
        rv = lax.rsqrt(ms + RMS_EPS)
        n = o * rv
        gv = g_ref[...]
        rr = r_ref[...]
        sg = _sigmoid(rr)
        dyv = dy_ref[...]
        dr_ref[...] = (dyv * (n * gv) * (sg * (1.0 + rr * (1.0 - sg)))).astype(dr_ref.dtype)
        du = dyv * (rr * sg)
        dn = du * gv
        c = _dg_hl(dn * n, bdv, 1, 0) * (1.0 / hd)
        do_ref[...] = rv * (dn - n * c)

        @pl.when(pl.program_id(0) == 0)
        def _():
            dg_ref[...] = jnp.zeros_like(dg_ref)

        dg_ref[...] += jnp.sum(du * n, axis=0, keepdims=True)

    row = pl.BlockSpec((ts, W), lambda i: (i, 0))
    vec = pl.BlockSpec((1, W), lambda i: (0, 0))
    return pl.pallas_call(
        body, name=name,
        out_shape=(jax.ShapeDtypeStruct((S, W), _F32), jax.ShapeDtypeStruct((S, W), _BF),
                   jax.ShapeDtypeStruct((1, W), _F32)),
        grid=(S // ts,),
        in_specs=[row, row, pl.BlockSpec((ts, W), lambda i: (i, r_cb)), vec, pl.BlockSpec((W, W), lambda i: (0, 0)),
                  row],
        out_specs=(row, row, vec), compiler_params=_params(("arbitrary",)))(o_f, o_b, proj, g, bd, dy)


def _loss_head(y, tgt, name):
    S, D = y.shape
    ts = _pick(S, (256,))

    def body(y_ref, t_ref, l_ref, d_ref, db_ref):
        err = y_ref[...] - t_ref[...]
        d = err * (1.0 / D)
        d_ref[...] = d
        db_ref[...] = d.astype(db_ref.dtype)

        @pl.when(pl.program_id(0) == 0)
        def _():
            l_ref[...] = jnp.zeros_like(l_ref)

        l_ref[...] += 0.5 * jnp.sum(jnp.mean(err * err, axis=-1, keepdims=True))

    row = pl.BlockSpec((ts, D), lambda i: (i, 0))
    return pl.pallas_call(
        body, name=name,
        out_shape=(jax.ShapeDtypeStruct((8, _LANES), _F32), jax.ShapeDtypeStruct((S, D), _F32),
                   jax.ShapeDtypeStruct((S, D), _BF)),
        grid=(S // ts,), in_specs=[row, row],
        out_specs=(pl.BlockSpec((8, _LANES), lambda i: (0, 0)), row, row),
        compiler_params=_params(("arbitrary",)))(y, tgt)


def _adamw(w, g, m, v, name):
    shape = w.shape
    if w.ndim == 3 and shape[1] * shape[2] > 256 * 1024:
        rb = _row_block(shape[1], shape[2] * 4, 1536 * 1024)
        grid = (shape[0], shape[1] // rb)
        spec = pl.BlockSpec((1, rb, shape[2]), lambda l, i: (l, i, 0))
        sem = ("parallel", "parallel")
    else:
        grid = ()
        spec = pl.BlockSpec(memory_space=pltpu.VMEM)
        sem = None

    def body(w_ref, g_ref, m_ref, v_ref, d_ref, nm_ref, nv_ref):
        gv = g_ref[...]
        mn = ADAM_B1 * m_ref[...] + (1.0 - ADAM_B1) * gv
        vn = ADAM_B2 * v_ref[...] + (1.0 - ADAM_B2) * (gv * gv)
        m_hat = mn / (1.0 - ADAM_B1 ** ADAM_STEP)
        v_hat = vn / (1.0 - ADAM_B2 ** ADAM_STEP)
        d_ref[...] = -ADAM_LR * (m_hat / (jnp.sqrt(v_hat) + ADAM_EPS) + ADAM_WD * w_ref[...])
        nm_ref[...] = mn
        nv_ref[...] = vn

    out = jax.ShapeDtypeStruct(shape, _F32)
    return pl.pallas_call(
        body, name=name, out_shape=(out, out, out), grid=grid, in_specs=[spec] * 4, out_specs=(spec,) * 3,
        compiler_params=_params(sem))(w, g, m, v)


def _add_pair(t, la, cidx, name):
    _, _, r, c = t.shape
    rb = _pick(r, (256, 352, 176, 88, 8))

    def body(c_ref, t_ref, l_ref, o_ref):
        o_ref[...] = (t_ref[...] + l_ref[...]).astype(o_ref.dtype)

    return pl.pallas_call(
        body, name=name, out_shape=jax.ShapeDtypeStruct((4, r, c), _BF),
        grid_spec=pltpu.PrefetchScalarGridSpec(
            num_scalar_prefetch=1, grid=(4, r // rb),
            in_specs=[pl.BlockSpec((1, None, rb, c), lambda j, i, cr: (j, cr[0], i, 0)),
                      pl.BlockSpec((1, rb, c), lambda j, i, cr: (j, i, 0))],
            out_specs=pl.BlockSpec((1, rb, c), lambda j, i, cr: (j, i, 0))),
        compiler_params=_params(("parallel", "parallel")))(cidx, t, la)


def _add_four(p, land, chip, name):
    _, r, c = p.shape
    rb = _pick(r, (256, 352, 176, 88, 16))

    def body(c_ref, p_ref, l1_ref, l2_ref, l3_ref, o_ref):
        o_ref[...] = ((p_ref[...].astype(_F32) + l1_ref[...].astype(_F32)) + l2_ref[...].astype(_F32)
                      ) + l3_ref[...].astype(_F32)

    def slot(flip):
        return pl.BlockSpec((None, rb, c), lambda i, cr: (jnp.bitwise_xor(cr[0], flip), i, 0))

    return pl.pallas_call(
        body, name=name, out_shape=jax.ShapeDtypeStruct((r, c), _F32),
        grid_spec=pltpu.PrefetchScalarGridSpec(
            num_scalar_prefetch=1, grid=(r // rb,),
            in_specs=[slot(0), slot(1), slot(2), slot(3)],
            out_specs=pl.BlockSpec((rb, c), lambda i, cr: (i, 0))),
        compiler_params=_params(("parallel",)))(chip, p, land, land, land)


def _gla_chunk(q, k, gfb, wg, bg, tri, rev, scale):
    C = q.shape[0]
    z = _dg(gfb, wg, 1, 0) + bg
    la = _log_sigmoid(z) * (1.0 / GLA_TAU)
    if rev:
        cum = _dg_lh(tri, la, 0, 0)
    else:
        cum = _dg_lh(tri, la, 1, 0)
    tot = jnp.sum(la, axis=0, keepdims=True)
    e_a = jnp.exp(cum)
    e_na = jnp.exp(-cum)
    e_la = jnp.exp(tot - cum)
    qe = q * scale * e_a
    ke = k * e_na
    kend = k * e_la
    row = lax.broadcasted_iota(jnp.int32, (C, C), 0)
    col = lax.broadcasted_iota(jnp.int32, (C, C), 1)
    keep = (col > row) if rev else (col <= row)
    p = jnp.where(keep, _dg(qe, ke, 1, 1), 0.0)
    return dict(z=z, tot=tot, e_a=e_a, e_na=e_na, e_la=e_la, qe=qe, ke=ke, kend=kend, keep=keep, p=p)


def _gla_fwd(proj, gfb, wg, bg, rev, name):
    S = proj.shape[0]
    H = GLA_HEADS
    dk = wg.shape[1] // (2 * H)
    dv = 2 * dk
    C = GLA_CHUNK
    cb_n = _pick(S // C, (8, 4, 2, 1))
    tb = cb_n * C
    nb = S // tb
    scale = float(dk) ** -0.5
    tri = jnp.asarray(np.tril(np.ones((C, C), np.float32)), dtype=_BF)
    wcol = H if rev else 0

    def bmap(b):
        return nb - 1 - b if rev else b

    def body(q_ref, k_ref, v_ref, g_ref, wg_ref, bg_ref, tri_ref, o_ref, st_ref, state):
        h = pl.program_id(1)

        @pl.when(pl.program_id(0) == 0)
        def _():
            state[h] = jnp.zeros((dv, dk), _F32)

        order = range(cb_n - 1, -1, -1) if rev else range(cb_n)
        for ci in order:
            sl = pl.ds(ci * C, C)
            t = _gla_chunk(q_ref[sl, :], k_ref[sl, :], g_ref[sl, :], wg_ref[...], bg_ref[...], tri_ref[...], rev, scale)
            vv = v_ref[sl, :]
            st = state[h]
            o_ref[sl, :] = _dg(t['p'], vv, 1, 0) + _dg(t['qe'], st, 1, 1)
            st_ref[0, ci] = st
            state[h] = st * jnp.exp(t['tot']) + _dg(vv, t['kend'], 0, 0)

    nq = (H * dk) // dk
    return pl.pallas_call(
        body, name=name,
        out_shape=(jax.ShapeDtypeStruct((S, H * dv), _F32), jax.ShapeDtypeStruct((H, S // C, dv, dk), _F32)),
        grid=(nb, H),
        in_specs=[pl.BlockSpec((tb, dk), lambda b, h: (bmap(b), h)),
                  pl.BlockSpec((tb, dk), lambda b, h: (bmap(b), nq + h)),
                  pl.BlockSpec((tb, dv), lambda b, h: (bmap(b), (2 * H * dk) // dv + h)),
                  pl.BlockSpec((tb, _LANES), lambda b, h: (bmap(b), 0)),
                  pl.BlockSpec((_LANES, dk), lambda b, h: (0, wcol + h)),
                  pl.BlockSpec((1, dk), lambda b, h: (0, wcol + h)),
                  pl.BlockSpec((C, C), lambda b, h: (0, 0))],
        out_specs=(pl.BlockSpec((tb, dv), lambda b, h: (bmap(b), h)),
                   pl.BlockSpec((1, cb_n, dv, dk), lambda b, h: (h, bmap(b), 0, 0))),
        scratch_shapes=[pltpu.VMEM((H, dv, dk), _F32)],
        compiler_params=_params(("arbitrary", "arbitrary")))(proj, proj, proj, gfb, wg, bg, tri)


def _gla_bwd(proj, gfb, wg, bg, st, do, rev, prev, name):
    S = proj.shape[0]
    H = GLA_HEADS
    dk = wg.shape[1] // (2 * H)
    dv = 2 * dk
    C = GLA_CHUNK
    cb_n = _pick(S // C, (4, 2, 1))
    tb = cb_n * C
    nb = S // tb
    scale = float(dk) ** -0.5
    tri = jnp.asarray(np.tril(np.ones((C, C), np.float32)), dtype=_BF)
    wcol = H if rev else 0
    has_prev = prev is not None

    def bmap(b):
        return b if rev else nb - 1 - b

    def body(*refs):
        if has_prev:
            (q_ref, k_ref, v_ref, g_ref, wg_ref, bg_ref, tri_ref, st_ref, do_ref, pq_ref, pk_ref, pv_ref,
             dq_ref, dk_ref, dv_ref, dz_ref, dstate) = refs
        else:
            (q_ref, k_ref, v_ref, g_ref, wg_ref, bg_ref, tri_ref, st_ref, do_ref,
             dq_ref, dk_ref, dv_ref, dz_ref, dstate) = refs
        h = pl.program_id(1)

        @pl.when(pl.program_id(0) == 0)
        def _():
            dstate[h] = jnp.zeros((dv, dk), _F32)

        order = range(cb_n) if rev else range(cb_n - 1, -1, -1)
        for ci in order:
            sl = pl.ds(ci * C, C)
            t = _gla_chunk(q_ref[sl, :], k_ref[sl, :], g_ref[sl, :], wg_ref[...], bg_ref[...], tri_ref[...], rev, scale)
            vv = v_ref[sl, :]
            dov = do_ref[sl, :]
            stp = st_ref[0, ci]
            dst = dstate[h]
            e_l = jnp.exp(t['tot'])
            dp = jnp.where(t['keep'], _dg(dov, vv, 1, 1), 0.0)
            dvv = _dg(t['p'], dov, 0, 0) + _dg(t['kend'], dst, 1, 1)
            dqe = _dg(dp, t['ke'], 1, 0) + _dg(dov, stp, 1, 0)
            dke = _dg(dp, t['qe'], 0, 0)
            dkend = _dg(vv, dst, 1, 0)
            dstate[h] = dst * e_l + _dg(dov, t['qe'], 0, 0)
            dtot = (e_l * jnp.sum(dst * stp, axis=0, keepdims=True)
                    + jnp.sum(dkend * t['kend'], axis=0, keepdims=True))
            dqv = dqe * t['e_a'] * scale
            dkv = dke * t['e_na'] + dkend * t['e_la']
            d_a = dqe * t['qe'] - dke * t['ke'] - dkend * t['kend']
            if rev:
                dla = _dg_lh(tri_ref[...], d_a, 1, 0) + dtot
            else:
                dla = _dg_lh(tri_ref[...], d_a, 0, 0) + dtot
            dz_ref[sl, :] = dla * (1.0 / GLA_TAU) * _sigmoid(-t['z'])
            if has_prev:
                dqv = dqv + pq_ref[sl, :]
                dkv = dkv + pk_ref[sl, :]
                dvv = dvv + pv_ref[sl, :]
            dq_ref[sl, :] = dqv
            dk_ref[sl, :] = dkv
            dv_ref[sl, :] = dvv

    nq = (H * dk) // dk
    kblk = pl.BlockSpec((tb, dk), lambda b, h: (bmap(b), h))
    vblk = pl.BlockSpec((tb, dv), lambda b, h: (bmap(b), h))
    in_specs = [kblk,
                pl.BlockSpec((tb, dk), lambda b, h: (bmap(b), nq + h)),
                pl.BlockSpec((tb, dv), lambda b, h: (bmap(b), (2 * H * dk) // dv + h)),
                pl.BlockSpec((tb, _LANES), lambda b, h: (bmap(b), 0)),
                pl.BlockSpec((_LANES, dk), lambda b, h: (0, wcol + h)),
                pl.BlockSpec((1, dk), lambda b, h: (0, wcol + h)),
                pl.BlockSpec((C, C), lambda b, h: (0, 0)),
                pl.BlockSpec((1, cb_n, dv, dk), lambda b, h: (h, bmap(b), 0, 0)),
                vblk]
    args = [proj, proj, proj, gfb, wg, bg, tri, st, do]
    if has_prev:
        in_specs += [kblk, kblk, vblk]
        args += list(prev)
    return pl.pallas_call(
        body, name=name,
        out_shape=(jax.ShapeDtypeStruct((S, H * dk), _F32), jax.ShapeDtypeStruct((S, H * dk), _F32),
                   jax.ShapeDtypeStruct((S, H * dv), _F32), jax.ShapeDtypeStruct((S, H * dk), _F32)),
        grid=(nb, H), in_specs=in_specs, out_specs=(kblk, kblk, vblk, kblk),
        scratch_shapes=[pltpu.VMEM((H, dv, dk), _F32)],
        compiler_params=_params(("arbitrary", "arbitrary")))(*args)


NA_GROUP = 4
NA_WIN_ROWS = NA_GROUP + NA_ROWS


def _na_geometry(S):
    rows = S // GRID_W
    assert rows % NA_GROUP == 0 and rows >= NA_WIN_ROWS + NA_GROUP
    return rows, rows // NA_GROUP


def _na_win_start(g, rows):
    return jnp.clip(NA_GROUP * g - NA_ROWS // 2, 0, rows - NA_WIN_ROWS)


def _na_class(g, groups):
    return jnp.where(g == 0, 0, jnp.where(g == groups - 1, 2, 1))


def _na_onehots(rows):
    groups = rows // NA_GROUP
    by_row = np.zeros((3, NA_GROUP, NA_WIN_ROWS, 2 * NA_ROWS - 1), np.float32)
    for cls, g in enumerate((0, 1, groups - 1)):
        ws = int(np.clip(NA_GROUP * g - NA_ROWS // 2, 0, rows - NA_WIN_ROWS))
        for qr in range(NA_GROUP):
            r = NA_GROUP * g + qr
            rs = int(np.clip(r - NA_ROWS // 2, 0, rows - NA_ROWS))
            for kr in range(NA_WIN_ROWS):
                if rs <= ws + kr < rs + NA_ROWS:
                    by_row[cls, qr, kr, ws + kr - r + NA_ROWS - 1] = 1.0
    c = np.arange(GRID_W)
    cs = np.clip(c - NA_COLS // 2, 0, GRID_W - NA_COLS)
    kc = np.arange(GRID_W)
    win = (kc[None, :] >= cs[:, None]) & (kc[None, :] < cs[:, None] + NA_COLS)
    idx = np.clip(kc[None, :] - c[:, None], -(NA_COLS - 1), NA_COLS - 1) + (NA_COLS - 1)
    by_col = ((idx[:, :, None] == np.arange(2 * NA_COLS - 1)[None, None, :]) & win[:, :, None]).astype(np.float32)
    return by_row, by_col


def _na_bias_tables(rpb, rows):
    by_row, by_col = _na_onehots(rows)
    H = rpb.shape[0]
    hp = lax.Precision.HIGHEST
    e1 = jnp.einsum('hij,ckj->hick', rpb, by_col, precision=hp)
    t = jnp.einsum('hick,zqri->zhqcrk', e1, by_row, precision=hp)
    valid = (by_row.sum(-1) > 0)[:, None, :, None, :, None] & (by_col.sum(-1) > 0)[None, None, None, :, None, :]
    t = jnp.where(valid, t, -jnp.inf)
    return t.reshape(3, H, NA_GROUP * GRID_W, NA_WIN_ROWS * GRID_W)


def _na_bias_grad(dtab, rows):
    by_row, by_col = _na_onehots(rows)
    H = dtab.shape[1]
    hp = lax.Precision.HIGHEST
    d6 = dtab.reshape(3, H, NA_GROUP, GRID_W, NA_WIN_ROWS, GRID_W)
    de1 = jnp.einsum('zhqcrk,zqri->hick', d6, by_row, precision=hp)
    return jnp.einsum('hick,ckj->hij', de1, by_col, precision=hp)


def _na_fwd(qn, kn, vb, tab, name):
    S, W = qn.shape
    rows, groups = _na_geometry(S)
    npair = W // _LANES
    nq = NA_GROUP * GRID_W
    nk = NA_WIN_ROWS * GRID_W
    sc = float(NA_HD) ** -0.5

    def body(q_ref, k_ref, v_ref, b_ref, o_ref):
        g = pl.program_id(1)
        k0 = pl.multiple_of(_na_win_start(g, rows) * GRID_W, GRID_W)
        kw = k_ref[pl.ds(k0, nk), :]
        vw = v_ref[pl.ds(k0, nk), :]
        qv = q_ref[...]
        lane = lax.broadcasted_iota(jnp.int32, (nq, _LANES), 1)
        outs = []
        for hh in range(2):
            mine = (lane >= hh * NA_HD) & (lane < (hh + 1) * NA_HD)
            s = _dg(jnp.where(mine, qv, jnp.zeros_like(qv)), kw, 1, 1) * sc + b_ref[0, hh]
            e = jnp.exp(s - jnp.max(s, axis=-1, keepdims=True))
            p = e / jnp.sum(e, axis=-1, keepdims=True)
            outs.append(_dg(p, vw, 1, 0))
        o_ref[...] = jnp.where(lane < NA_HD, outs[0], outs[1])

    return pl.pallas_call(
        body, name=name, out_shape=jax.ShapeDtypeStruct((S, W), _F32), grid=(npair, groups),
        in_specs=[pl.BlockSpec((nq, _LANES), lambda p, g: (g, p)),
                  pl.BlockSpec((S, _LANES), lambda p, g: (0, p)),
                  pl.BlockSpec((S, _LANES), lambda p, g: (0, p)),
                  pl.BlockSpec((1, 2, nq, nk), lambda p, g: (_na_class(g, groups), p, 0, 0))],
        out_specs=pl.BlockSpec((nq, _LANES), lambda p, g: (g, p)),
        compiler_params=_params(("parallel", "arbitrary")))(qn, kn, vb, tab)


def _na_bwd(qn, kn, vb, tab, do, name):
    S, W = qn.shape
    rows, groups = _na_geometry(S)
    npair = W // _LANES
    nq = NA_GROUP * GRID_W
    nk = NA_WIN_ROWS * GRID_W
    sc = float(NA_HD) ** -0.5

    def body(q_ref, k_ref, v_ref, b_ref, do_ref, dq_ref, dk_ref, dv_ref, db_ref):
        g = pl.program_id(1)

        @pl.when(g == 0)
        def _():
            dk_ref[...] = jnp.zeros_like(dk_ref)
            dv_ref[...] = jnp.zeros_like(dv_ref)

        @pl.when((g <= 1) | (g == groups - 1))
        def _():
            db_ref[...] = jnp.zeros_like(db_ref)

        k0 = pl.multiple_of(_na_win_start(g, rows) * GRID_W, GRID_W)
        kw = k_ref[pl.ds(k0, nk), :]
        vw = v_ref[pl.ds(k0, nk), :]
        qv = q_ref[...]
        dov = do_ref[...]
        lane = lax.broadcasted_iota(jnp.int32, (nq, _LANES), 1)
        dqs = []
        dkw = jnp.zeros((nk, _LANES), _F32)
        dvw = jnp.zeros((nk, _LANES), _F32)
        for hh in range(2):
            mine = (lane >= hh * NA_HD) & (lane < (hh + 1) * NA_HD)
            qm = jnp.where(mine, qv, jnp.zeros_like(qv))
            dom = jnp.where(mine, dov, 0.0)
            s = _dg(qm, kw, 1, 1) * sc + b_ref[0, hh]
            e = jnp.exp(s - jnp.max(s, axis=-1, keepdims=True))
            p = e / jnp.sum(e, axis=-1, keepdims=True)
            dp = _dg(dom, vw, 1, 1)
            ds = p * (dp - jnp.sum(p * dp, axis=-1, keepdims=True))
            db_ref[0, hh] += ds
            dqs.append(_dg(ds, kw, 1, 0) * sc)
            dkw = dkw + _dg(ds, qm, 0, 0) * sc
            dvw = dvw + _dg(p, dom, 0, 0)
        dq_ref[...] = jnp.where(lane < NA_HD, dqs[0], dqs[1])
        dk_ref[pl.ds(k0, nk), :] += dkw
        dv_ref[pl.ds(k0, nk), :] += dvw

    blk = pl.BlockSpec((nq, _LANES), lambda p, g: (g, p))
    full = pl.BlockSpec((S, _LANES), lambda p, g: (0, p))
    tspec = pl.BlockSpec((1, 2, nq, nk), lambda p, g: (_na_class(g, groups), p, 0, 0))
    return pl.pallas_call(
        body, name=name,
        out_shape=(jax.ShapeDtypeStruct((S, W), _F32), jax.ShapeDtypeStruct((S, W), _F32),
                   jax.ShapeDtypeStruct((S, W), _F32), jax.ShapeDtypeStruct(tab.shape, _F32)),
        grid=(npair, groups), in_specs=[blk, full, full, tspec, blk],
        out_specs=(blk, full, full, tspec),
        compiler_params=_params(("arbitrary", "arbitrary")))(qn, kn, vb, tab, do)


def _mem_fwd(qn, km, vm, name):
    S, W = qn.shape
    hd = W // MEM_HEADS
    tq = _pick(S, (512, 256))
    sc = float(hd) ** -0.5

    def body(q_ref, k_ref, v_ref, o_ref):
        for h in range(MEM_HEADS):
            cs = slice(h * hd, (h + 1) * hd)
            s = _dg(q_ref[:, cs], k_ref[:, cs], 1, 1) * sc
            e = jnp.exp(s - jnp.max(s, axis=-1, keepdims=True))
            p = e / jnp.sum(e, axis=-1, keepdims=True)
            o_ref[:, cs] = _dg(p, v_ref[:, cs], 1, 0)

    full = pl.BlockSpec(km.shape, lambda i: (0, 0))
    return pl.pallas_call(
        body, name=name, out_shape=jax.ShapeDtypeStruct((S, W), _F32), grid=(S // tq,),
        in_specs=[pl.BlockSpec((tq, W), lambda i: (i, 0)), full, full],
        out_specs=pl.BlockSpec((tq, W), lambda i: (i, 0)),
        compiler_params=_params(("parallel",)))(qn, km, vm)


def _mem_bwd(qn, km, vm, do, name):
    S, W = qn.shape
    hd = W // MEM_HEADS
    tq = _pick(S, (512, 256))
    sc = float(hd) ** -0.5

    def body(q_ref, k_ref, v_ref, do_ref, dq_ref, dk_ref, dv_ref):
        @pl.when(pl.program_id(0) == 0)
        def _():
            dk_ref[...] = jnp.zeros_like(dk_ref)
            dv_ref[...] = jnp.zeros_like(dv_ref)

        for h in range(MEM_HEADS):
            cs = slice(h * hd, (h + 1) * hd)
            qh = q_ref[:, cs]
            kh = k_ref[:, cs]
            doh = do_ref[:, cs]
            s = _dg(qh, kh, 1, 1) * sc
            e = jnp.exp(s - jnp.max(s, axis=-1, keepdims=True))
            p = e / jnp.sum(e, axis=-1, keepdims=True)
            dp = _dg(doh, v_ref[:, cs], 1, 1)
            ds = p * (dp - jnp.sum(p * dp, axis=-1, keepdims=True))
            dq_ref[:, cs] = _dg(ds, kh, 1, 0) * sc
            dk_ref[:, cs] += _dg(ds, qh, 0, 0) * sc
            dv_ref[:, cs] += _dg(p, doh, 0, 0)

    full = pl.BlockSpec(km.shape, lambda i: (0, 0))
    row = pl.BlockSpec((tq, W), lambda i: (i, 0))
    return pl.pallas_call(
        body, name=name,
        out_shape=(jax.ShapeDtypeStruct((S, W), _F32), jax.ShapeDtypeStruct(km.shape, _F32),
                   jax.ShapeDtypeStruct(km.shape, _F32)),
        grid=(S // tq,), in_specs=[row, full, full, row], out_specs=(row, full, full),
        compiler_params=_params(("arbitrary",)))(qn, km, vm, do)


_ANY = pl.BlockSpec(memory_space=pl.ANY)


def _place():
    x, y, c = lax.axis_index("x"), lax.axis_index("y"), lax.axis_index("c")
    chips = [(1 - x, y), (x, 1 - y), (1 - x, 1 - y)]
    return x, y, c, chips


def _gather_weights(shards, name):
    n = len(shards)

    def body(*refs):
        w = refs[:n]
        g = refs[n:2 * n]
        send, recv = refs[2 * n:]
        x, y, c, chips = _place()
        me = 2 * x + y
        sib = (x, y, 1 - c)

        def piece(t, chip, half):
            return g[t].at[chip, half]

        def rcopy(t, k, chip, half, to, src=None):
            dst = piece(t, chip, half)
            return pltpu.make_async_remote_copy(
                src_ref=dst if src is None else src, dst_ref=dst, send_sem=send.at[t, k], recv_sem=recv.at[t, k],
                device_id=to, device_id_type=_MESH)

        first = [rcopy(t, j, me, c, (cx, cy, c), src=w[t].at[c]) for t in range(n) for j, (cx, cy) in enumerate(chips)]
        for cp in first:
            cp.start()
        passed = []
        for t in range(n):
            for j, (cx, cy) in enumerate(chips):
                rcopy(t, j, 2 * cx + cy, c, (x, y, c)).wait_recv()
                cp = rcopy(t, 3 + j, 2 * cx + cy, c, sib)
                cp.start()
                passed.append(cp)
        for t in range(n):
            for j, (cx, cy) in enumerate(chips):
                rcopy(t, 3 + j, 2 * cx + cy, 1 - c, (x, y, c)).wait_recv()
        for cp in first + passed:
            cp.wait_send()

    return pl.pallas_call(
        body, name=name,
        out_shape=tuple(jax.ShapeDtypeStruct((4,) + s.shape, s.dtype) for s in shards),
        in_specs=[_ANY] * n, out_specs=tuple([_ANY] * n),
        scratch_shapes=[pltpu.SemaphoreType.DMA((n, 6)), pltpu.SemaphoreType.DMA((n, 6))],
        compiler_params=_params())(*shards)


def _swap_halves(ts, name):
    n = len(ts)

    def body(*refs):
        t_in = refs[:n]
        land = refs[n:2 * n]
        send, recv = refs[2 * n:]
        x, y, c, _ = _place()
        cps = [pltpu.make_async_remote_copy(
            src_ref=t_in[t].at[:, 1 - c], dst_ref=land[t], send_sem=send.at[t], recv_sem=recv.at[t],
            device_id=(x, y, 1 - c), device_id_type=_MESH) for t in range(n)]
        for cp in cps:
            cp.start()
        for cp in cps:
            cp.wait()

    return pl.pallas_call(
        body, name=name,
        out_shape=tuple(jax.ShapeDtypeStruct((4,) + t.shape[2:], t.dtype) for t in ts),
        in_specs=[_ANY] * n, out_specs=tuple([_ANY] * n),
        scratch_shapes=[pltpu.SemaphoreType.DMA((n,)), pltpu.SemaphoreType.DMA((n,))],
        compiler_params=_params())(*ts)


def _scatter_chips(ps, name):
    n = len(ps)

    def body(*refs):
        p = refs[:n]
        land = refs[n:2 * n]
        send, recv = refs[2 * n:]
        x, y, c, chips = _place()
        me = 2 * x + y
        cps = []
        for t in range(n):
            for j, (cx, cy) in enumerate(chips):
                cps.append(pltpu.make_async_remote_copy(
                    src_ref=p[t].at[2 * cx + cy], dst_ref=land[t].at[me], send_sem=send.at[t, j],
                    recv_sem=recv.at[t, j], device_id=(cx, cy, c), device_id_type=_MESH))
        for cp in cps:
            cp.start()
        for t in range(n):
            for j, (cx, cy) in enumerate(chips):
                pltpu.make_async_remote_copy(
                    src_ref=p[t].at[me], dst_ref=land[t].at[2 * cx + cy], send_sem=send.at[t, j],
                    recv_sem=recv.at[t, j], device_id=(cx, cy, c), device_id_type=_MESH).wait_recv()
        for cp in cps:
            cp.wait_send()

    return pl.pallas_call(
        body, name=name,
        out_shape=tuple(jax.ShapeDtypeStruct(t.shape, t.dtype) for t in ps),
        in_specs=[_ANY] * n, out_specs=tuple([_ANY] * n),
        scratch_shapes=[pltpu.SemaphoreType.DMA((n, 3)), pltpu.SemaphoreType.DMA((n, 3))],
        compiler_params=_params())(*ps)


def _swap_reduced(rs, name):
    n = len(rs)

    def body(*refs):
        r_in = refs[:n]
        out = refs[n:2 * n]
        send, recv = refs[2 * n:]
        x, y, c, _ = _place()
        cps = [pltpu.make_async_remote_copy(
            src_ref=r_in[t], dst_ref=out[t], send_sem=send.at[t], recv_sem=recv.at[t],
            device_id=(x, y, 1 - c), device_id_type=_MESH) for t in range(n)]
        for cp in cps:
            cp.start()
        for cp in cps:
            cp.wait()

    return pl.pallas_call(
        body, name=name,
        out_shape=tuple(jax.ShapeDtypeStruct(t.shape, t.dtype) for t in rs),
        in_specs=[_ANY] * n, out_specs=tuple([_ANY] * n),
        scratch_shapes=[pltpu.SemaphoreType.DMA((n,)), pltpu.SemaphoreType.DMA((n,))],
        compiler_params=_params())(*rs)


def _all_devices(v, reduce, name):
    rows = v.shape[0]

    def body(v_ref, o_ref, *rest):
        if reduce:
            all_ref, send, recv = rest
        else:
            send, recv = rest
            all_ref = o_ref
        x, y, c, _ = _place()
        me = 4 * x + 2 * y + c
        all_ref[me] = v_ref[...]
        cps = []
        for k in range(1, 8):
            fx, fy, fc = (k >> 2) & 1, (k >> 1) & 1, k & 1
            to = (x ^ fx, y ^ fy, c ^ fc)
            cps.append(pltpu.make_async_remote_copy(
                src_ref=v_ref, dst_ref=all_ref.at[me], send_sem=send.at[k - 1], recv_sem=recv.at[k - 1],
                device_id=to, device_id_type=_MESH))
        for cp in cps:
            cp.start()
        for k in range(1, 8):
            fx, fy, fc = (k >> 2) & 1, (k >> 1) & 1, k & 1
            frm = 4 * (x ^ fx) + 2 * (y ^ fy) + (c ^ fc)
            pltpu.make_async_remote_copy(
                src_ref=v_ref, dst_ref=all_ref.at[frm], send_sem=send.at[k - 1], recv_sem=recv.at[k - 1],
                device_id=(x, y, c), device_id_type=_MESH).wait_recv()
        for cp in cps:
            cp.wait_send()
        if reduce:
            acc = all_ref[0]
            for d in range(1, 8):
                acc = acc + all_ref[d]
            o_ref[...] = acc

    vm = pl.BlockSpec(memory_space=pltpu.VMEM)
    if reduce:
        out_shape = jax.ShapeDtypeStruct((rows, _LANES), _F32)
        scratch = [pltpu.VMEM((8, rows, _LANES), _F32)]
    else:
        out_shape = jax.ShapeDtypeStruct((8, rows, _LANES), _F32)
        scratch = []
    return pl.pallas_call(
        body, name=name, out_shape=out_shape, in_specs=[vm], out_specs=vm,
        scratch_shapes=scratch + [pltpu.SemaphoreType.DMA((7,)), pltpu.SemaphoreType.DMA((7,))],
        compiler_params=_params())(v)


def _gather_layer_weights(shards, name):
    halves = [shards[k].reshape((2, shards[k].shape[0] // 2) + shards[k].shape[1:]) for k in BIG]
    got = _gather_weights(halves, name)
    chip = 2 * lax.axis_index("x") + lax.axis_index("y")
    full = {}
    for k, g, own in zip(BIG, got, halves):
        g = lax.dynamic_update_slice(g, own[None], (chip, 0, 0, 0))
        _, _, r, c = g.shape
        if k in ('w_in', 'ffn_w13'):
            full[k] = jnp.transpose(g, (1, 2, 0, 3)).reshape(2 * r, 4 * c)
        else:
            full[k] = g.reshape(8 * r, c)
    return full


def _reduce_layer_grads(grads, cidx, name):
    ts = []
    for k in BIG:
        g = grads[k]
        if k in ('w_in', 'ffn_w13'):
            r, c = g.shape[0] // 2, g.shape[1] // 4
            ts.append(jnp.transpose(g.reshape(2, r, 4, c), (2, 0, 1, 3)))
        else:
            r, c = g.shape[0] // 8, g.shape[1]
            ts.append(g.reshape(4, 2, r, c))
    chip = (2 * lax.axis_index("x") + lax.axis_index("y")).astype(jnp.int32).reshape(1)
    landed = _swap_halves(ts, name + "_swap")
    partial = [_add_pair(t, la, cidx, f"{name}_add2_{k}") for k, t, la in zip(BIG, ts, landed)]
    slots = _scatter_chips(partial, name + "_scatter")
    reduced = [_add_four(p, s, chip, f"{name}_add4_{k}") for k, p, s in zip(BIG, partial, slots)]
    theirs = _swap_reduced(reduced, name + "_join")
    out = {}
    for k, mine, other in zip(BIG, reduced, theirs):
        both = jnp.stack([mine, other])
        both = jnp.where(cidx[0] == 0, both, both[::-1])
        out[k] = both.reshape((2 * mine.shape[0], mine.shape[1]))
    return out


def _pack_rows(arrs):
    parts, spans = [], []
    off = 0
    for a in arrs:
        n = int(np.prod(a.shape))
        pad = (-n) % (8 * _LANES)
        parts.append(jnp.pad(a.reshape(-1), (0, pad)))
        spans.append((off, n, a.shape))
        off += n + pad
    return jnp.concatenate(parts).reshape(-1, _LANES), spans


def _unpack_rows(packed, spans):
    flat = packed.reshape(-1)
    return [flat[o:o + n].reshape(shape) for o, n, shape in spans]


def _in_split(d):
    dk, dv, w = d // 4, d // 2, d // 4
    names = [('g_q', dk), ('g_k', dk), ('g_v', dv), ('g_r', dv), ('g_f', GLA_GATE_RANK), ('g_b', GLA_GATE_RANK),
             ('n_q', w), ('n_k', w), ('n_v', w), ('m_q', w)]
    out, off = {}, 0
    for nme, wd in names:
        out[nme] = (off, wd)
        off += wd
    return out


def _permute_w_in(w):
    sp = _in_split(w.shape[0])
    f0 = sp['g_f'][0]
    n0 = sp['n_q'][0]
    main = jnp.concatenate([w[:, :f0], w[:, n0:]], axis=1)
    gate = jnp.pad(w[:, f0:n0], ((0, 0), (0, _LANES - (n0 - f0))))
    return main, gate


def _unpermute_w_in(main, gate, d):
    sp = _in_split(d)
    f0 = sp['g_f'][0]
    n0 = sp['n_q'][0]
    return jnp.concatenate([main[:, :f0], gate[:, :n0 - f0], main[:, f0:]], axis=1)


def _gate_weight(wg2_f, wg2_b):
    r, dk = wg2_f.shape
    top = jnp.concatenate([wg2_f, jnp.zeros_like(wg2_f)], axis=1)
    mid = jnp.concatenate([jnp.zeros_like(wg2_b), wg2_b], axis=1)
    return jnp.concatenate([top, mid, jnp.zeros((_LANES - 2 * r, 2 * dk), wg2_f.dtype)], axis=0)


def _layer_fwd(l, x, mem, w, full, tab):
    S, D = x.shape
    dk, dv, nw = D // 4, D // 2, D // 4
    sv = {}
    w_main, w_gate = _permute_w_in(full['w_in'])
    sv['w_main'], sv['w_gate'] = w_main, w_gate
    xn, r1 = _rmsnorm_fwd(x, w['attn_norm'][None], f"rms1_fwd")
    proj = _mm(xn, w_main, 'nn', "mm_proj")
    gfb = _mm(xn, w_gate, 'nn', "mm_gate")
    sv.update(x=x, xn=xn, r1=r1, proj=proj, gfb=gfb)
    wg = _gate_weight(w['gla_wg2_f'], w['gla_wg2_b']).astype(_BF)
    bg = jnp.concatenate([w['gla_bg_f'], w['gla_bg_b']])[None]
    o_f, st_f = _gla_fwd(proj, gfb, wg, bg, False, "gla_fwd_f")
    o_b, st_b = _gla_fwd(proj, gfb, wg, bg, True, "gla_fwd_b")
    y_gla = _gla_post_fwd(o_f, o_b, proj, (2 * dk) // dv + 1, w['gla_out_norm'][None], "gla_post_fwd")
    sv.update(wg=wg, bg=bg, o_f=o_f, o_b=o_b, st_f=st_f, st_b=st_b)
    c0 = (2 * dk + 2 * dv) // nw
    qn = _headnorm_fwd(proj, c0, nw, NA_HD, jnp.tile(w['na_q_norm'], nw // NA_HD)[None], "na_qnorm_fwd")
    kn = _headnorm_fwd(proj, c0 + 1, nw, NA_HD, jnp.tile(w['na_k_norm'], nw // NA_HD)[None], "na_knorm_fwd")
    vb = proj[:, (c0 + 2) * nw:(c0 + 3) * nw].astype(_BF)
    o_na = _na_fwd(qn, kn, vb, tab, "na_fwd")
    y_na = _headnorm_fwd(o_na, 0, nw, NA_HD, w['na_out_norm'][None], "na_onorm_fwd")
    sv.update(qn=qn, kn=kn, vb=vb, o_na=o_na)
    mhd = nw // MEM_HEADS
    mqn = _headnorm_fwd(proj, c0 + 3, nw, mhd, jnp.tile(w['mem_q_norm'], MEM_HEADS)[None], "mem_qnorm_fwd")
    memn, rm = _rmsnorm_fwd(mem, w['mem_norm'][None], "mem_rms_fwd")
    kv = _mm(memn, full['mem_wkv'], 'nn', "mm_memkv")
    km = _headnorm_fwd(kv, 0, nw, mhd, jnp.tile(w['mem_k_norm'], MEM_HEADS)[None], "mem_knorm_fwd")
    vm = kv[:, nw:].astype(_BF)
    o_mem = _mem_fwd(mqn, km, vm, "mem_fwd")
    y_mem = _headnorm_fwd(o_mem, 0, nw, mhd, w['mem_out_norm'][None], "mem_onorm_fwd")
    sv.update(mqn=mqn, memn=memn, rm=rm, kv=kv, km=km, vm=vm, o_mem=o_mem)
    y = jnp.concatenate([y_gla, y_na, y_mem], axis=1)
    x1 = _mm(y, full['w_out'], 'nn', "mm_out", res=x)
    h, r2 = _rmsnorm_fwd(x1, w['ffn_norm'][None], "rms2_fwd")
    w13i = _interleave_gate_up(full['ffn_w13'])
    a, gu = _mm_swiglu_fwd(h, w13i, "mm_w13_swiglu")
    x2 = _mm(a, full['ffn_w2'], 'nn', "mm_w2", res=x1)
    sv.update(y=y, x1=x1, h=h, r2=r2, gu=gu, a=a, w13i=w13i)
    return x2, sv


def _layer_bwd(l, dx2, dx2b, mem, w, full, tab, sv):
    S, D = dx2.shape
    dk, dv, nw = D // 4, D // 2, D // 4
    gb, gs = {}, {}
    dgu = _mm_swiglu_bwd(dx2b, full['ffn_w2'], sv['gu'], "mm_da_swiglu")
    gb['ffn_w2'] = _mm(sv['a'], dx2b, 'tn', "mm_dw2")
    dh = _mm(dgu, sv['w13i'], 'nt', "mm_dh")
    gb['ffn_w13'] = _interleave_gate_up(_mm(sv['h'], dgu, 'tn', "mm_dw13"), inverse=True)
    dx1, dx1b, g = _rmsnorm_bwd(sv['x1'], sv['r2'], w['ffn_norm'][None], dh, dx2, "rms2_bwd")
    gs['ffn_norm'] = g[0]
    dy = _mm(dx1b, full['w_out'], 'nt', "mm_dy")
    gb['w_out'] = _mm(sv['y'], dx1b, 'tn', "mm_dwout")
    c0 = (2 * dk + 2 * dv) // nw
    mhd = nw // MEM_HEADS
    do_mem, g = _headnorm_bwd(sv['o_mem'], 0, nw, mhd, w['mem_out_norm'][None], dy, (dv + nw) // nw, "mem_onorm_bwd")
    gs['mem_out_norm'] = g[0]
    dmqn, dkm, dvm = _mem_bwd(sv['mqn'], sv['km'], sv['vm'], do_mem, "mem_bwd")
    dmq, g = _headnorm_bwd(sv['proj'], c0 + 3, nw, mhd, jnp.tile(w['mem_q_norm'], MEM_HEADS)[None], dmqn, 0,
                           "mem_qnorm_bwd", _BF)
    gs['mem_q_norm'] = g[0].reshape(MEM_HEADS, mhd).sum(0)
    dkvk, g = _headnorm_bwd(sv['kv'], 0, nw, mhd, jnp.tile(w['mem_k_norm'], MEM_HEADS)[None], dkm, 0,
                            "mem_knorm_bwd")
    gs['mem_k_norm'] = g[0].reshape(MEM_HEADS, mhd).sum(0)
    dkv = jnp.concatenate([dkvk, dvm], axis=1).astype(_BF)
    gb['mem_wkv'] = _mm(sv['memn'], dkv, 'tn', "mm_dwkv")
    dmemn = _mm(dkv, full['mem_wkv'], 'nt', "mm_dmemn")
    _, _, g = _rmsnorm_bwd(mem, sv['rm'], w['mem_norm'][None], dmemn, None, "mem_rms_bwd")
    gs['mem_norm'] = g[0]
    do_na, g = _headnorm_bwd(sv['o_na'], 0, nw, NA_HD, w['na_out_norm'][None], dy, dv // nw, "na_onorm_bwd")
    gs['na_out_norm'] = g[0]
    dqn, dkn, dnv, dtab = _na_bwd(sv['qn'], sv['kn'], sv['vb'], tab, do_na, "na_bwd")
    gs['na_rpb'] = _na_bias_grad(dtab, S // GRID_W)
    dnq, g = _headnorm_bwd(sv['proj'], c0, nw, NA_HD, jnp.tile(w['na_q_norm'], nw // NA_HD)[None], dqn, 0,
                           "na_qnorm_bwd", _BF)
    gs['na_q_norm'] = g[0].reshape(nw // NA_HD, NA_HD).sum(0)
    dnk, g = _headnorm_bwd(sv['proj'], c0 + 1, nw, NA_HD, jnp.tile(w['na_k_norm'], nw // NA_HD)[None], dkn, 0,
                           "na_knorm_bwd", _BF)
    gs['na_k_norm'] = g[0].reshape(nw // NA_HD, NA_HD).sum(0)
    do_gla, dgr, g = _gla_post_bwd(sv['o_f'], sv['o_b'], sv['proj'], (2 * dk) // dv + 1, w['gla_out_norm'][None], dy,
                                   "gla_post_bwd")
    gs['gla_out_norm'] = g[0]
    dq1, dk1, dv1, dz_f = _gla_bwd(sv['proj'], sv['gfb'], sv['wg'], sv['bg'], sv['st_f'], do_gla, False, None,
                                   "gla_bwd_f")
    dgq, dgk, dgv, dz_b = _gla_bwd(sv['proj'], sv['gfb'], sv['wg'], sv['bg'], sv['st_b'], do_gla, True,
                                   (dq1, dk1, dv1), "gla_bwd_b")
    dz = jnp.concatenate([dz_f, dz_b], axis=1).astype(_BF)
    ones_lane = (jnp.arange(_LANES) == 2 * GLA_GATE_RANK)[None]
    gfb_aug = jnp.where(ones_lane, 1.0, sv['gfb']).astype(_BF)
    dwg = _mm(gfb_aug, dz, 'tn', "mm_dwg")
    r16 = GLA_GATE_RANK
    gs['gla_wg2_f'] = dwg[:r16, :dk]
    gs['gla_wg2_b'] = dwg[r16:2 * r16, dk:]
    gs['gla_bg_f'] = dwg[2 * r16, :dk]
    gs['gla_bg_b'] = dwg[2 * r16, dk:]
    dgfb = _mm(dz, sv['wg'], 'nt', "mm_dgfb", out_dtype=_BF)
    dproj = jnp.concatenate([dgq.astype(_BF), dgk.astype(_BF), dgv.astype(_BF), dgr, dnq, dnk, dnv.astype(_BF), dmq],
                            axis=1)
    t = _mm(dgfb, sv['w_gate'], 'nt', "mm_dxn_gate")
    dxn = _mm(dproj, sv['w_main'], 'nt', "mm_dxn", res=t)
    dw_main = _mm(sv['xn'], dproj, 'tn', "mm_dwmain")
    dw_gate = _mm(sv['xn'], dgfb, 'tn', "mm_dwgate")
    gb['w_in'] = _unpermute_w_in(dw_main, dw_gate, D)
    dx, dxb, g = _rmsnorm_bwd(sv['x'], sv['r1'], w['attn_norm'][None], dxn, dx1, "rms1_bwd")
    gs['attn_norm'] = g[0]
    return dx, dxb, gb, gs


def kernel(x, mem, attn_norm, w_in, gla_wg2_f, gla_bg_f, gla_wg2_b, gla_bg_b, gla_out_norm, na_q_norm, na_k_norm, na_rpb, na_out_norm, mem_norm, mem_wkv, mem_q_norm, mem_k_norm, mem_out_norm, w_out, ffn_norm, ffn_w13, ffn_w2, loss_target, m_attn_norm, m_w_in, m_gla_wg2_f, m_gla_bg_f, m_gla_wg2_b, m_gla_bg_b, m_gla_out_norm, m_na_q_norm, m_na_k_norm, m_na_rpb, m_na_out_norm, m_mem_norm, m_mem_wkv, m_mem_q_norm, m_mem_k_norm, m_mem_out_norm, m_w_out, m_ffn_norm, m_ffn_w13, m_ffn_w2, v_attn_norm, v_w_in, v_gla_wg2_f, v_gla_bg_f, v_gla_wg2_b, v_gla_bg_b, v_gla_out_norm, v_na_q_norm, v_na_k_norm, v_na_rpb, v_na_out_norm, v_mem_norm, v_mem_wkv, v_mem_q_norm, v_mem_k_norm, v_mem_out_norm, v_w_out, v_ffn_norm, v_ffn_w13, v_ffn_w2):
    args = locals()
    W = {k: args[k] for k in WEIGHTS}
    M = {k: args['m_' + k] for k in WEIGHTS}
    V = {k: args['v_' + k] for k in WEIGHTS}
    depth = attn_norm.shape[0]
    xs, mems, tgt = x[0], mem[0], loss_target[0]
    cidx = lax.axis_index("c").astype(jnp.int32).reshape(1)

    gate_rows, gate_spans = _pack_rows([W[k] for k in SMALL_SHARDED])
    gate_all = _all_devices(gate_rows, False, "gate_weights_gather")
    gate_full = {}
    for i, k in enumerate(SMALL_SHARDED):
        per_chip = [_unpack_rows(gate_all[2 * j], gate_spans)[i] for j in range(4)]
        gate_full[k] = jnp.concatenate(per_chip, axis=-1)

    small = [k for k in WEIGHTS if k not in BIG]

    def layer_small(l):
        d = {k: W[k][l] for k in small if k not in SMALL_SHARDED}
        d.update({k: gate_full[k][l] for k in SMALL_SHARDED})
        return d

    saved, fulls, tabs = [], [], []
    cur = xs
    for l in range(depth):
        full = _gather_layer_weights({k: W[k][l].astype(_BF) for k in BIG}, "gather_weights")
        tab = _na_bias_tables(W['na_rpb'][l], xs.shape[0] // GRID_W)
        cur, sv = _layer_fwd(l, cur, mems, layer_small(l), full, tab)
        saved.append(sv)
        fulls.append(full)
        tabs.append(tab)
    loss_tile, dy, dyb = _loss_head(cur, tgt, "loss_head")
    loss = lax.psum(loss_tile[0, 0], ("x", "y", "c"))

    big_grads = [None] * depth
    small_grads = [None] * depth
    for l in range(depth - 1, -1, -1):
        dy, dyb, gb, gs = _layer_bwd(l, dy, dyb, mems, layer_small(l), fulls[l], tabs[l], saved[l])
        big_grads[l] = _reduce_layer_grads(gb, cidx, "reduce_grads")
        small_grads[l] = gs
    grad_x = dy[None]

    small_stack = [jnp.stack([small_grads[l][k] for l in range(depth)]) for k in small]
    packed, spans = _pack_rows(small_stack)
    summed = _unpack_rows(_all_devices(packed, True, "small_grads_sum"), spans)
    G = dict(zip(small, summed))
    chip = 2 * lax.axis_index("x") + lax.axis_index("y")
    for k in SMALL_SHARDED:
        wdt = W[k].shape[-1]
        G[k] = lax.dynamic_slice_in_dim(G[k], chip * wdt, wdt, axis=2)
    for k in BIG:
        G[k] = jnp.stack([big_grads[l][k] for l in range(depth)])

    delta, new_m, new_v = {}, {}, {}
    for k in WEIGHTS:
        delta[k], new_m[k], new_v[k] = _adamw(W[k], G[k], M[k], V[k], "adamw_" + k)
    return (loss, grad_x, *[G[k] for k in WEIGHTS], *[delta[k] for k in WEIGHTS], *[new_m[k] for k in WEIGHTS],
            *[new_v[k] for k in WEIGHTS])
```

```python
import functools

import numpy as np
import jax
import jax.numpy as jnp
from jax import lax
from jax.experimental import pallas as pl
from jax.experimental.pallas import tpu as pltpu

_F32 = jnp.float32
_BF = jnp.bfloat16
_MESH = pl.DeviceIdType.MESH

_VMEM_LIMIT_BYTES = 56 * 1024 * 1024
_LANES = 128

RMS_EPS = 1e-6
GLA_HEADS = 4
GLA_GATE_RANK = 16
GLA_TAU = 16.0
GLA_CHUNK = 64
GRID_W = 64
NA_HD = 64
NA_ROWS = 8
NA_COLS = 16
MEM_HEADS = 4
ADAM_LR = 0.001
ADAM_B1 = 0.9
ADAM_B2 = 0.999
ADAM_EPS = 1e-08
ADAM_WD = 0.01
ADAM_STEP = 10

WEIGHTS = ['attn_norm', 'w_in', 'gla_wg2_f', 'gla_bg_f', 'gla_wg2_b', 'gla_bg_b', 'gla_out_norm', 'na_q_norm',
           'na_k_norm', 'na_rpb', 'na_out_norm', 'mem_norm', 'mem_wkv', 'mem_q_norm', 'mem_k_norm', 'mem_out_norm',
           'w_out', 'ffn_norm', 'ffn_w13', 'ffn_w2']
BIG = ['w_in', 'mem_wkv', 'w_out', 'ffn_w13', 'ffn_w2']
SMALL_SHARDED = ['gla_wg2_f', 'gla_wg2_b']


def _pick(n, cands):
    for c in cands:
        if n % c == 0:
            return c
    return n


def _row_block(rows, row_bytes, cap_bytes):
    best = 8
    for rb in range(8, rows + 1, 8):
        if rows % rb == 0 and rb * row_bytes <= cap_bytes:
            best = rb
    return best


def _params(sem=None):
    return pltpu.CompilerParams(dimension_semantics=sem, vmem_limit_bytes=_VMEM_LIMIT_BYTES)


def _dg(a, b, ca, cb):
    return lax.dot_general(a.astype(_BF), b.astype(_BF), (((ca,), (cb,)), ((), ())), preferred_element_type=_F32)


def _split(a):
    hi = a.astype(_BF)
    return hi, (a - hi.astype(_F32)).astype(_BF)


def _dg_hl(a, b, ca, cb):
    hi, lo = _split(a)
    return _dg(hi, b, ca, cb) + _dg(lo, b, ca, cb)


def _dg_lh(a, b, ca, cb):
    hi, lo = _split(b)
    return _dg(a, hi, ca, cb) + _dg(a, lo, ca, cb)


def _sigmoid(x):
    return 1.0 / (1.0 + jnp.exp(-x))


def _log_sigmoid(z):
    return jnp.minimum(z, 0.0) - jnp.log(1.0 + jnp.exp(-jnp.abs(z)))


def _block_diag(width, hd):
    i = np.arange(width) // hd
    return jnp.asarray((i[:, None] == i[None, :]).astype(np.float32), dtype=_BF)


def _mm(a, b, mode, name, out_dtype=_F32, res=None):
    if mode == 'nn':
        (M, K), N = a.shape, b.shape[1]
    elif mode == 'nt':
        (M, K), N = a.shape, b.shape[0]
    else:
        (K, M), N = a.shape, b.shape[1]
    ca, cb = {'nn': (1, 0), 'nt': (1, 1), 'tn': (0, 0)}[mode]
    has_res = res is not None
    args = (a, b) + ((res,) if has_res else ())
    out_shape = jax.ShapeDtypeStruct((M, N), out_dtype)
    if K <= _K_RESIDENT:
        tm = _pick(M, (1024, 512, 256, 128))
        tn = _pick(N, (1024, 512, 256, 128))
        a_spec = (pl.BlockSpec((K, tm), lambda i, j: (0, i)) if mode == 'tn'
                  else pl.BlockSpec((tm, K), lambda i, j: (i, 0)))
        b_spec = (pl.BlockSpec((tn, K), lambda i, j: (j, 0)) if mode == 'nt'
                  else pl.BlockSpec((K, tn), lambda i, j: (0, j)))
        o_spec = pl.BlockSpec((tm, tn), lambda i, j: (i, j))

        def body1(*refs):
            a_ref, b_ref = refs[:2]
            o_ref = refs[-1]
            out = _dg(a_ref[...], b_ref[...], ca, cb)
            if has_res:
                out = out + refs[2][...]
            o_ref[...] = out.astype(o_ref.dtype)

        return pl.pallas_call(
            body1, name=name, out_shape=out_shape, grid=(M // tm, N // tn),
            in_specs=[a_spec, b_spec] + ([o_spec] if has_res else []), out_specs=o_spec,
            compiler_params=_params(("parallel", "parallel")))(*args)

    tk = _pick(K, (1024, 1408, 512, 256, 128))
    nk = K // tk
    tm, tn = _loop_tiles(M, N, tk, has_res, out_dtype)
    if mode == 'tn':
        a_spec = pl.BlockSpec((tk, tm), lambda i, j, k: (k, i))
    else:
        a_spec = pl.BlockSpec((tm, tk), lambda i, j, k: (i, k))
    if mode == 'nt':
        b_spec = pl.BlockSpec((tn, tk), lambda i, j, k: (j, k))
    else:
        b_spec = pl.BlockSpec((tk, tn), lambda i, j, k: (k, j))
    o_spec = pl.BlockSpec((tm, tn), lambda i, j, k: (i, j))
    in_out = out_dtype == _F32

    def body(*refs):
        a_ref, b_ref = refs[:2]
        o_ref = refs[-1] if in_out else refs[-2]
        acc = o_ref if in_out else refs[-1]
        k = pl.program_id(2)
        part = _dg(a_ref[...], b_ref[...], ca, cb)

        @pl.when(k == 0)
        def _():
            acc[...] = part + refs[2][...] if has_res else part

        @pl.when(k > 0)
        def _():
            acc[...] += part

        if not in_out:
            @pl.when(k == nk - 1)
            def _():
                o_ref[...] = acc[...].astype(o_ref.dtype)

    return pl.pallas_call(
        body, name=name, out_shape=out_shape, grid=(M // tm, N // tn, nk),
        in_specs=[a_spec, b_spec] + ([o_spec] if has_res else []), out_specs=o_spec,
        scratch_shapes=[] if in_out else [pltpu.VMEM((tm, tn), _F32)],
        compiler_params=_params(("parallel", "parallel", "arbitrary")))(*args)


_K_RESIDENT = 2048
_LOOP_TILE_BYTES = 28 * 1024 * 1024


def _loop_tiles(M, N, tk, has_res, out_dtype):
    best = None
    for tm in (2048, 1408, 1024, 512, 256, 128):
        for tn in (2048, 1024, 512, 256, 128):
            if M % tm or N % tn:
                continue
            obytes = 4 if out_dtype == _F32 else 2
            need = 2 * 2 * tk * (tm + tn) + tm * tn * (2 * obytes + (8 if has_res else 0) + (0 if obytes == 4 else 4))
            if need <= _LOOP_TILE_BYTES and (best is None or tm * tn > best[0] * best[1]):
                best = (tm, tn)
    return best


def _cols_to_shards(w):
    r, c = w.shape[0] // 2, w.shape[1] // 4
    return jnp.transpose(w.reshape(2, r, 4, c), (2, 0, 1, 3))


def _shards_to_cols(g):
    _, _, r, c = g.shape
    return jnp.transpose(g, (1, 2, 0, 3)).reshape(2 * r, 4 * c)


_FFN_SUB = 256


def _ffn_up_fwd(h, g13, name):
    S = h.shape[0]
    _, _, R, C = g13.shape
    tm = _pick(S, (512, 256, 128))
    cw, nc = _ffn_cols(C)
    subs = [(s, min(s + _FFN_SUB, cw)) for s in range(0, cw, _FFN_SUB)]

    def body(h_ref, wg_ref, wu_ref, a_ref, g_ref, u_ref):
        h0, h1 = h_ref[:, :R], h_ref[:, R:]
        for lo, hi in subs:
            sl = slice(lo, hi)
            gv = _dg(h0, wg_ref[0, 0, :, sl], 1, 0) + _dg(h1, wg_ref[0, 1, :, sl], 1, 0)
            uv = _dg(h0, wu_ref[0, 0, :, sl], 1, 0) + _dg(h1, wu_ref[0, 1, :, sl], 1, 0)
            a_ref[:, sl] = (gv * _sigmoid(gv) * uv).astype(a_ref.dtype)
            g_ref[:, sl] = gv.astype(g_ref.dtype)
            u_ref[:, sl] = uv.astype(u_ref.dtype)

    out = jax.ShapeDtypeStruct((S, 2 * C), _BF)
    ospec = pl.BlockSpec((tm, cw), lambda j, i: (i, j))
    return pl.pallas_call(
        body, name=name, out_shape=(out, out, out), grid=(2 * nc, S // tm),
        in_specs=[pl.BlockSpec((tm, 2 * R), lambda j, i: (i, 0)),
                  pl.BlockSpec((1, 2, R, cw), lambda j, i: (lax.div(j, nc), 0, 0, lax.rem(j, nc))),
                  pl.BlockSpec((1, 2, R, cw), lambda j, i: (2 + lax.div(j, nc), 0, 0, lax.rem(j, nc)))],
        out_specs=(ospec, ospec, ospec),
        compiler_params=_params(("parallel", "parallel")))(h, g13, g13)


def _ffn_cols(C):
    if C % (2 * _LANES) == 0:
        return C // 2, 2
    return C, 1


def _ffn_up_bwd(dxb, w2, gate, up, name):
    S, D = dxb.shape
    F = w2.shape[0]
    tf = 2 * _FFN_SUB
    tm = _pick(S, (1024, 512, 256, 128))

    def body(d_ref, w_ref, g_ref, u_ref, dg_ref, du_ref):
        dv = d_ref[...]
        for s in range(tf // _FFN_SUB):
            sl = slice(s * _FFN_SUB, (s + 1) * _FFN_SUB)
            da = _dg(dv, w_ref[sl, :], 1, 1)
            gv = g_ref[:, sl].astype(_F32)
            uv = u_ref[:, sl].astype(_F32)
            sg = _sigmoid(gv)
            dg_ref[:, sl] = (da * uv * (sg * (1.0 + gv * (1.0 - sg)))).astype(dg_ref.dtype)
            du_ref[:, sl] = (da * (gv * sg)).astype(du_ref.dtype)

    out = jax.ShapeDtypeStruct((S, F), _BF)
    tile = pl.BlockSpec((tm, tf), lambda i, j: (i, j))
    return pl.pallas_call(
        body, name=name, out_shape=(out, out), grid=(S // tm, F // tf),
        in_specs=[pl.BlockSpec((tm, D), lambda i, j: (i, 0)), pl.BlockSpec((tf, D), lambda i, j: (j, 0)), tile, tile],
        out_specs=(tile, tile), compiler_params=_params(("parallel", "parallel")))(dxb, w2, gate, up)


def _ffn_up_dh(dgate, dup, g13, name):
    S = dgate.shape[0]
    _, _, R, C = g13.shape
    tm = _pick(S, (512, 256, 128))

    def body(dg_ref, du_ref, wg_ref, wu_ref, o_ref):
        j = pl.program_id(1)
        dgv, duv = dg_ref[...], du_ref[...]
        for hh in range(2):
            part = _dg(dgv, wg_ref[0, hh], 1, 1) + _dg(duv, wu_ref[0, hh], 1, 1)
            cols = slice(hh * R, (hh + 1) * R)

            @pl.when(j == 0)
            def _():
                o_ref[:, cols] = part

            @pl.when(j > 0)
            def _():
                o_ref[:, cols] += part

    cw, nc = _ffn_cols(C)
    tile = pl.BlockSpec((tm, cw), lambda i, j: (i, j))
    return pl.pallas_call(
        body, name=name, out_shape=jax.ShapeDtypeStruct((S, 2 * R), _F32), grid=(S // tm, 2 * nc),
        in_specs=[tile, tile, pl.BlockSpec((1, 2, R, cw), lambda i, j: (lax.div(j, nc), 0, 0, lax.rem(j, nc))),
                  pl.BlockSpec((1, 2, R, cw), lambda i, j: (2 + lax.div(j, nc), 0, 0, lax.rem(j, nc)))],
        out_specs=pl.BlockSpec((tm, 2 * R), lambda i, j: (i, 0)),
        compiler_params=_params(("parallel", "arbitrary")))(dgate, dup, g13, g13)


def _ffn_up_dw(hact, dgate, dup, name):
    S, R2 = hact.shape
    R = R2 // 2
    C = dgate.shape[1] // 2
    tk = _pick(S, (512, 256, 128))
    nk = S // tk

    def body(h_ref, dg_ref, du_ref, o_ref):
        chip = pl.program_id(0)
        k = pl.program_id(2)

        def accumulate(d_ref):
            part = _dg(h_ref[...], d_ref[...], 0, 0)

            @pl.when(k == 0)
            def _():
                o_ref[0, 0] = part

            @pl.when(k > 0)
            def _():
                o_ref[0, 0] += part

        @pl.when(chip < 2)
        def _():
            accumulate(dg_ref)

        @pl.when(chip >= 2)
        def _():
            accumulate(du_ref)

    return pl.pallas_call(
        body, name=name, out_shape=jax.ShapeDtypeStruct((4, 2, R, C), _F32), grid=(4, 2, nk),
        in_specs=[pl.BlockSpec((tk, R), lambda c, h, k: (k, h)),
                  pl.BlockSpec((tk, C), lambda c, h, k: (jnp.where(c < 2, k, 0), jnp.minimum(c, 1))),
                  pl.BlockSpec((tk, C), lambda c, h, k: (jnp.where(c >= 2, k, 0), jnp.maximum(c - 2, 0)))],
        out_specs=pl.BlockSpec((1, 1, R, C), lambda c, h, k: (c, h, 0, 0)),
        compiler_params=_params(("parallel", "parallel", "arbitrary")))(hact, dgate, dup)


def _rmsnorm_fwd(x, g, name):
    S, D = x.shape
    ts = _pick(S, (256,))

    def body(x_ref, g_ref, o_ref, r_ref):
        xv = x_ref[...]
        r = lax.rsqrt(jnp.mean(xv * xv, axis=-1, keepdims=True) + RMS_EPS)
        o_ref[...] = (xv * r * g_ref[...]).astype(o_ref.dtype)
        r_ref[...] = r

    return pl.pallas_call(
        body, name=name,
        out_shape=(jax.ShapeDtypeStruct((S, D), _BF), jax.ShapeDtypeStruct((S, 1), _F32)),
        grid=(S // ts,),
        in_specs=[pl.BlockSpec((ts, D), lambda i: (i, 0)), pl.BlockSpec((1, D), lambda i: (0, 0))],
        out_specs=(pl.BlockSpec((ts, D), lambda i: (i, 0)), pl.BlockSpec((ts, 1), lambda i: (i, 0))),
        compiler_params=_params(("parallel",)))(x, g)


def _rmsnorm_bwd(x, r, g, dy, dres, name):
    S, D = x.shape
    ts = _pick(S, (256,))
    has_res = dres is not None

    def body(*refs):
        if has_res:
            x_ref, r_ref, g_ref, dy_ref, dr_ref, dx_ref, dxb_ref, dg_ref = refs
        else:
            x_ref, r_ref, g_ref, dy_ref, dx_ref, dxb_ref, dg_ref = refs
        rv = r_ref[...]
        n = x_ref[...] * rv
        dyv = dy_ref[...]
        dn = dyv * g_ref[...]
        c = jnp.mean(dn * n, axis=-1, keepdims=True)
        dx = rv * (dn - n * c)
        if has_res:
            dx = dx + dr_ref[...]
        dx_ref[...] = dx
        dxb_ref[...] = dx.astype(dxb_ref.dtype)

        @pl.when(pl.program_id(0) == 0)
        def _():
            dg_ref[...] = jnp.zeros_like(dg_ref)

        dg_ref[...] += jnp.sum(dyv * n, axis=0, keepdims=True)

    row = pl.BlockSpec((ts, D), lambda i: (i, 0))
    vec = pl.BlockSpec((1, D), lambda i: (0, 0))
    in_specs = [row, pl.BlockSpec((ts, 1), lambda i: (i, 0)), vec, row] + ([row] if has_res else [])
    args = (x, r, g, dy) + ((dres,) if has_res else ())
    return pl.pallas_call(
        body, name=name,
        out_shape=(jax.ShapeDtypeStruct((S, D), _F32), jax.ShapeDtypeStruct((S, D), _BF),
                   jax.ShapeDtypeStruct((1, D), _F32)),
        grid=(S // ts,), in_specs=in_specs, out_specs=(row, row, vec),
        compiler_params=_params(("arbitrary",)))(*args)


def _headnorm_fwd(t, cb, W, hd, g, name):
    S = t.shape[0]
    ts = _pick(S, (512, 256))
    bd = _block_diag(W, hd)

    def body(x_ref, g_ref, bd_ref, o_ref):
        xv = x_ref[...].astype(_F32)
        ms = _dg_hl(xv * xv, bd_ref[...], 1, 0) * (1.0 / hd)
        o_ref[...] = (xv * lax.rsqrt(ms + RMS_EPS) * g_ref[...]).astype(o_ref.dtype)

    return pl.pallas_call(
        body, name=name, out_shape=jax.ShapeDtypeStruct((S, W), _BF), grid=(S // ts,),
        in_specs=[pl.BlockSpec((ts, W), lambda i: (i, cb)), pl.BlockSpec((1, W), lambda i: (0, 0)),
                  pl.BlockSpec((W, W), lambda i: (0, 0))],
        out_specs=pl.BlockSpec((ts, W), lambda i: (i, 0)),
        compiler_params=_params(("parallel",)))(t, g, bd)


def _headnorm_bwd(t, cb, W, hd, g, dy, dcb, name, out_dtype=_F32):
    S = t.shape[0]
    ts = _pick(S, (512, 256))
    bd = _block_diag(W, hd)

    def body(x_ref, g_ref, bd_ref, dy_ref, dx_ref, dg_ref):
        xv = x_ref[...].astype(_F32)
        bdv = bd_ref[...]
        ms = _dg_hl(xv * xv, bdv, 1, 0) * (1.0 / hd)
        rv = lax.rsqrt(ms + RMS_EPS)
        n = xv * rv
        dyv = dy_ref[...].astype(_F32)
        dn = dyv * g_ref[...]
        c = _dg_hl(dn * n, bdv, 1, 0) * (1.0 / hd)
        dx_ref[...] = (rv * (dn - n * c)).astype(dx_ref.dtype)

        @pl.when(pl.program_id(0) == 0)
        def _():
            dg_ref[...] = jnp.zeros_like(dg_ref)

        dg_ref[...] += jnp.sum(dyv * n, axis=0, keepdims=True)

    return pl.pallas_call(
        body, name=name,
        out_shape=(jax.ShapeDtypeStruct((S, W), out_dtype), jax.ShapeDtypeStruct((1, W), _F32)),
        grid=(S // ts,),
        in_specs=[pl.BlockSpec((ts, W), lambda i: (i, cb)), pl.BlockSpec((1, W), lambda i: (0, 0)),
                  pl.BlockSpec((W, W), lambda i: (0, 0)), pl.BlockSpec((ts, W), lambda i: (i, dcb))],
        out_specs=(pl.BlockSpec((ts, W), lambda i: (i, 0)), pl.BlockSpec((1, W), lambda i: (0, 0))),
        compiler_params=_params(("arbitrary",)))(t, g, bd, dy)


def _gla_post_fwd(o_f, o_b, proj, r_cb, g, name):
    S, W = o_f.shape
    hd = W // GLA_HEADS
    ts = _pick(S, (256,))
    bd = _block_diag(W, hd)

    def body(of_ref, ob_ref, r_ref, g_ref, bd_ref, y_ref):
        o = of_ref[...] + ob_ref[...]
        ms = _dg_hl(o * o, bd_ref[...], 1, 0) * (1.0 / hd)
        u = o * lax.rsqrt(ms + RMS_EPS) * g_ref[...]
        rr = r_ref[...]
        y_ref[...] = (u * (rr * _sigmoid(rr))).astype(y_ref.dtype)

    row = pl.BlockSpec((ts, W), lambda i: (i, 0))
    return pl.pallas_call(
        body, name=name, out_shape=jax.ShapeDtypeStruct((S, W), _BF), grid=(S // ts,),
        in_specs=[row, row, pl.BlockSpec((ts, W), lambda i: (i, r_cb)), pl.BlockSpec((1, W), lambda i: (0, 0)),
                  pl.BlockSpec((W, W), lambda i: (0, 0))],
        out_specs=row, compiler_params=_params(("parallel",)))(o_f, o_b, proj, g, bd)


def _gla_post_bwd(o_f, o_b, proj, r_cb, g, dy, name):
    S, W = o_f.shape
    hd = W // GLA_HEADS
    ts = _pick(S, (256,))
    bd = _block_diag(W, hd)

    def body(of_ref, ob_ref, r_ref, g_ref, bd_ref, dy_ref, do_ref, dr_ref, dg_ref):
        o = of_ref[...] + ob_ref[...]
        bdv = bd_ref[...]
        ms = _dg_hl(o * o, bdv, 1, 0) * (1.0 / hd)
        rv = lax.rsqrt(ms + RMS_EPS)
        n = o * rv
        gv = g_ref[...]
        rr = r_ref[...]
        sg = _sigmoid(rr)
        dyv = dy_ref[...]
        dr_ref[...] = (dyv * (n * gv) * (sg * (1.0 + rr * (1.0 - sg)))).astype(dr_ref.dtype)
        du = dyv * (rr * sg)
        dn = du * gv
        c = _dg_hl(dn * n, bdv, 1, 0) * (1.0 / hd)
        do_ref[...] = rv * (dn - n * c)

        @pl.when(pl.program_id(0) == 0)
        def _():
            dg_ref[...] = jnp.zeros_like(dg_ref)

        dg_ref[...] += jnp.sum(du * n, axis=0, keepdims=True)

    row = pl.BlockSpec((ts, W), lambda i: (i, 0))
    vec = pl.BlockSpec((1, W), lambda i: (0, 0))
    return pl.pallas_call(
        body, name=name,
        out_shape=(jax.ShapeDtypeStruct((S, W), _F32), jax.ShapeDtypeStruct((S, W), _BF),
                   jax.ShapeDtypeStruct((1, W), _F32)),
        grid=(S // ts,),
        in_specs=[row, row, pl.BlockSpec((ts, W), lambda i: (i, r_cb)), vec, pl.BlockSpec((W, W), lambda i: (0, 0)),
                  row],
        out_specs=(row, row, vec), compiler_params=_params(("arbitrary",)))(o_f, o_b, proj, g, bd, dy)


def _loss_head(y, tgt, name):
    S, D = y.shape
    ts = _pick(S, (256,))

    def body(y_ref, t_ref, l_ref, d_ref, db_ref):
        err = y_ref[...] - t_ref[...]
        d = err * (1.0 / D)
        d_ref[...] = d
        db_ref[...] = d.astype(db_ref.dtype)

        @pl.when(pl.program_id(0) == 0)
        def _():
            l_ref[...] = jnp.zeros_like(l_ref)

        l_ref[...] += 0.5 * jnp.sum(jnp.mean(err * err, axis=-1, keepdims=True))

    row = pl.BlockSpec((ts, D), lambda i: (i, 0))
    return pl.pallas_call(
        body, name=name,
        out_shape=(jax.ShapeDtypeStruct((8, _LANES), _F32), jax.ShapeDtypeStruct((S, D), _F32),
                   jax.ShapeDtypeStruct((S, D), _BF)),
        grid=(S // ts,), in_specs=[row, row],
        out_specs=(pl.BlockSpec((8, _LANES), lambda i: (0, 0)), row, row),
        compiler_params=_params(("arbitrary",)))(y, tgt)


def _adamw(w, g, m, v, name):
    shape = w.shape
    if w.ndim == 3 and shape[1] * shape[2] > 256 * 1024:
        rb = _row_block(shape[1], shape[2] * 4, 1536 * 1024)
        grid = (shape[0], shape[1] // rb)
        spec = pl.BlockSpec((1, rb, shape[2]), lambda l, i: (l, i, 0))
        sem = ("parallel", "parallel")
    else:
        grid = ()
        spec = pl.BlockSpec(memory_space=pltpu.VMEM)
        sem = None

    def body(w_ref, g_ref, m_ref, v_ref, d_ref, nm_ref, nv_ref):
        gv = g_ref[...]
        mn = ADAM_B1 * m_ref[...] + (1.0 - ADAM_B1) * gv
        vn = ADAM_B2 * v_ref[...] + (1.0 - ADAM_B2) * (gv * gv)
        m_hat = mn / (1.0 - ADAM_B1 ** ADAM_STEP)
        v_hat = vn / (1.0 - ADAM_B2 ** ADAM_STEP)
        d_ref[...] = -ADAM_LR * (m_hat / (jnp.sqrt(v_hat) + ADAM_EPS) + ADAM_WD * w_ref[...])
        nm_ref[...] = mn
        nv_ref[...] = vn

    out = jax.ShapeDtypeStruct(shape, _F32)
    return pl.pallas_call(
        body, name=name, out_shape=(out, out, out), grid=grid, in_specs=[spec] * 4, out_specs=(spec,) * 3,
        compiler_params=_params(sem))(w, g, m, v)


def _add_pair(t, la, cidx, name):
    _, _, r, c = t.shape
    rb = _pick(r, (256, 352, 176, 88, 8))

    def body(c_ref, t_ref, l_ref, o_ref):
        o_ref[...] = (t_ref[...] + l_ref[...]).astype(o_ref.dtype)

    return pl.pallas_call(
        body, name=name, out_shape=jax.ShapeDtypeStruct((4, r, c), _BF),
        grid_spec=pltpu.PrefetchScalarGridSpec(
            num_scalar_prefetch=1, grid=(4, r // rb),
            in_specs=[pl.BlockSpec((1, None, rb, c), lambda j, i, cr: (j, cr[0], i, 0)),
                      pl.BlockSpec((1, rb, c), lambda j, i, cr: (j, i, 0))],
            out_specs=pl.BlockSpec((1, rb, c), lambda j, i, cr: (j, i, 0))),
        compiler_params=_params(("parallel", "parallel")))(cidx, t, la)


def _add_four(p, land, chip, name):
    _, r, c = p.shape
    rb = _pick(r, (256, 352, 176, 88, 16))

    def body(c_ref, p_ref, l1_ref, l2_ref, l3_ref, o_ref):
        o_ref[...] = ((p_ref[...].astype(_F32) + l1_ref[...].astype(_F32)) + l2_ref[...].astype(_F32)
                      ) + l3_ref[...].astype(_F32)

    def slot(flip):
        return pl.BlockSpec((None, rb, c), lambda i, cr: (jnp.bitwise_xor(cr[0], flip), i, 0))

    return pl.pallas_call(
        body, name=name, out_shape=jax.ShapeDtypeStruct((r, c), _F32),
        grid_spec=pltpu.PrefetchScalarGridSpec(
            num_scalar_prefetch=1, grid=(r // rb,),
            in_specs=[slot(0), slot(1), slot(2), slot(3)],
            out_specs=pl.BlockSpec((rb, c), lambda i, cr: (i, 0))),
        compiler_params=_params(("parallel",)))(chip, p, land, land, land)


def _gla_masks(tb, rev):
    i = np.arange(tb)
    same = (i[:, None] // GLA_CHUNK) == (i[None, :] // GLA_CHUNK)
    tri = same & (i[None, :] <= i[:, None])
    keep = (same & ~tri) if rev else tri
    return tuple(jnp.asarray(m.astype(np.float32), dtype=_BF) for m in (tri, same, keep))


def _gla_block(q, k, gfb, wg, bg, tri, same, keepm, rev, scale):
    z = _dg(gfb, wg, 1, 0) + bg
    la = _log_sigmoid(z) * (1.0 / GLA_TAU)
    cum = _dg_lh(tri, la, 0 if rev else 1, 0)
    tot = _dg_lh(same, la, 1, 0)
    e_a = jnp.exp(cum)
    e_na = jnp.exp(-cum)
    e_la = jnp.exp(tot - cum)
    qe = q * scale * e_a
    ke = k * e_na
    kend = k * e_la
    keep = keepm > 0
    p = jnp.where(keep, _dg(qe, ke, 1, 1), 0.0)
    return dict(z=z, tot=tot, e_a=e_a, e_na=e_na, e_la=e_la, qe=qe, ke=ke, kend=kend, keep=keep, p=p)


def _gla_fwd(proj, gfb, wg, bg, rev, name):
    S = proj.shape[0]
    H = GLA_HEADS
    dk = wg.shape[1] // (2 * H)
    dv = 2 * dk
    C = GLA_CHUNK
    cb_n = _pick(S // C, (8, 4, 2, 1))
    tb = cb_n * C
    nb = S // tb
    scale = float(dk) ** -0.5
    masks = _gla_masks(tb, rev)
    wcol = H if rev else 0

    def bmap(b):
        return nb - 1 - b if rev else b

    def body(q_ref, k_ref, v_ref, g_ref, wg_ref, bg_ref, tri_ref, same_ref, keep_ref, o_ref, st_ref, state):
        h = pl.program_id(1)

        @pl.when(pl.program_id(0) == 0)
        def _():
            state[h] = jnp.zeros((dv, dk), _F32)

        t = _gla_block(q_ref[...], k_ref[...], g_ref[...], wg_ref[...], bg_ref[...], tri_ref[...], same_ref[...],
                       keep_ref[...], rev, scale)
        vv = v_ref[...]
        o_ref[...] = _dg(t['p'], vv, 1, 0)
        order = range(cb_n - 1, -1, -1) if rev else range(cb_n)
        for ci in order:
            sl = slice(ci * C, (ci + 1) * C)
            st = state[h]
            o_ref[sl, :] += _dg(t['qe'][sl], st, 1, 1)
            st_ref[0, ci] = st
            state[h] = st * jnp.exp(t['tot'][ci * C:ci * C + 1]) + _dg(vv[sl], t['kend'][sl], 0, 0)

    nq = (H * dk) // dk
    msk = pl.BlockSpec((tb, tb), lambda b, h: (0, 0))
    return pl.pallas_call(
        body, name=name,
        out_shape=(jax.ShapeDtypeStruct((S, H * dv), _F32), jax.ShapeDtypeStruct((H, S // C, dv, dk), _F32)),
        grid=(nb, H),
        in_specs=[pl.BlockSpec((tb, dk), lambda b, h: (bmap(b), h)),
                  pl.BlockSpec((tb, dk), lambda b, h: (bmap(b), nq + h)),
                  pl.BlockSpec((tb, dv), lambda b, h: (bmap(b), (2 * H * dk) // dv + h)),
                  pl.BlockSpec((tb, _LANES), lambda b, h: (bmap(b), 0)),
                  pl.BlockSpec((_LANES, dk), lambda b, h: (0, wcol + h)),
                  pl.BlockSpec((1, dk), lambda b, h: (0, wcol + h)),
                  msk, msk, msk],
        out_specs=(pl.BlockSpec((tb, dv), lambda b, h: (bmap(b), h)),
                   pl.BlockSpec((1, cb_n, dv, dk), lambda b, h: (h, bmap(b), 0, 0))),
        scratch_shapes=[pltpu.VMEM((H, dv, dk), _F32)],
        compiler_params=_params(("arbitrary", "arbitrary")))(proj, proj, proj, gfb, wg, bg, *masks)


def _gla_bwd(proj, gfb, wg, bg, st, do, rev, prev, name):
    S = proj.shape[0]
    H = GLA_HEADS
    dk = wg.shape[1] // (2 * H)
    dv = 2 * dk
    C = GLA_CHUNK
    cb_n = _pick(S // C, (8, 4, 2, 1))
    tb = cb_n * C
    nb = S // tb
    scale = float(dk) ** -0.5
    masks = _gla_masks(tb, rev)
    wcol = H if rev else 0
    has_prev = prev is not None

    def bmap(b):
        return b if rev else nb - 1 - b

    def body(*refs):
        if has_prev:
            (q_ref, k_ref, v_ref, g_ref, wg_ref, bg_ref, tri_ref, same_ref, keep_ref, st_ref, do_ref,
             pq_ref, pk_ref, pv_ref, dq_ref, dk_ref, dv_ref, dz_ref, dstate) = refs
        else:
            (q_ref, k_ref, v_ref, g_ref, wg_ref, bg_ref, tri_ref, same_ref, keep_ref, st_ref, do_ref,
             dq_ref, dk_ref, dv_ref, dz_ref, dstate) = refs
        h = pl.program_id(1)

        @pl.when(pl.program_id(0) == 0)
        def _():
            dstate[h] = jnp.zeros((dv, dk), _F32)

        t = _gla_block(q_ref[...], k_ref[...], g_ref[...], wg_ref[...], bg_ref[...], tri_ref[...], same_ref[...],
                       keep_ref[...], rev, scale)
        vv = v_ref[...]
        dov = do_ref[...]
        order = range(cb_n) if rev else range(cb_n - 1, -1, -1)
        for ci in order:
            sl = slice(ci * C, (ci + 1) * C)
            stp = st_ref[0, ci]
            dst = dstate[h]
            e_l = jnp.exp(t['tot'][ci * C:ci * C + 1])
            kend_c = t['kend'][sl]
            dkend_c = _dg(vv[sl], dst, 1, 0)
            dq_ref[sl, :] = _dg(dov[sl], stp, 1, 0)
            dk_ref[sl, :] = dkend_c
            dv_ref[sl, :] = _dg(kend_c, dst, 1, 1)
            dtot = (e_l * jnp.sum(dst * stp, axis=0, keepdims=True)
                    + jnp.sum(dkend_c * kend_c, axis=0, keepdims=True))
            dz_ref[sl, :] = jnp.broadcast_to(dtot, (C, dk))
            dstate[h] = dst * e_l + _dg(dov[sl], t['qe'][sl], 0, 0)
        dp = jnp.where(t['keep'], _dg(dov, vv, 1, 1), 0.0)
        dqe = _dg(dp, t['ke'], 1, 0) + dq_ref[...]
        dke = _dg(dp, t['qe'], 0, 0)
        dkend = dk_ref[...]
        dvv = _dg(t['p'], dov, 0, 0) + dv_ref[...]
        dqv = dqe * t['e_a'] * scale
        dkv = dke * t['e_na'] + dkend * t['e_la']
        d_a = dqe * t['qe'] - dke * t['ke'] - dkend * t['kend']
        dla = _dg_lh(tri_ref[...], d_a, 1 if rev else 0, 0) + dz_ref[...]
        dz_ref[...] = dla * (1.0 / GLA_TAU) * _sigmoid(-t['z'])
        if has_prev:
            dqv = dqv + pq_ref[...]
            dkv = dkv + pk_ref[...]
            dvv = dvv + pv_ref[...]
        dq_ref[...] = dqv
        dk_ref[...] = dkv
        dv_ref[...] = dvv

    nq = (H * dk) // dk
    msk = pl.BlockSpec((tb, tb), lambda b, h: (0, 0))
    kblk = pl.BlockSpec((tb, dk), lambda b, h: (bmap(b), h))
    vblk = pl.BlockSpec((tb, dv), lambda b, h: (bmap(b), h))
    in_specs = [kblk,
                pl.BlockSpec((tb, dk), lambda b, h: (bmap(b), nq + h)),
                pl.BlockSpec((tb, dv), lambda b, h: (bmap(b), (2 * H * dk) // dv + h)),
                pl.BlockSpec((tb, _LANES), lambda b, h: (bmap(b), 0)),
                pl.BlockSpec((_LANES, dk), lambda b, h: (0, wcol + h)),
                pl.BlockSpec((1, dk), lambda b, h: (0, wcol + h)),
                msk, msk, msk,
                pl.BlockSpec((1, cb_n, dv, dk), lambda b, h: (h, bmap(b), 0, 0)),
                vblk]
    args = [proj, proj, proj, gfb, wg, bg, *masks, st, do]
    if has_prev:
        in_specs += [kblk, kblk, vblk]
        args += list(prev)
    return pl.pallas_call(
        body, name=name,
        out_shape=(jax.ShapeDtypeStruct((S, H * dk), _F32), jax.ShapeDtypeStruct((S, H * dk), _F32),
                   jax.ShapeDtypeStruct((S, H * dv), _F32), jax.ShapeDtypeStruct((S, H * dk), _F32)),
        grid=(nb, H), in_specs=in_specs, out_specs=(kblk, kblk, vblk, kblk),
        scratch_shapes=[pltpu.VMEM((H, dv, dk), _F32)],
        compiler_params=_params(("arbitrary", "arbitrary")))(*args)


NA_GROUP = 4
NA_WIN_ROWS = NA_GROUP + NA_ROWS


def _na_geometry(S):
    rows = S // GRID_W
    assert rows % NA_GROUP == 0 and rows >= NA_WIN_ROWS + NA_GROUP
    return rows, rows // NA_GROUP


def _na_win_start(g, rows):
    return jnp.clip(NA_GROUP * g - NA_ROWS // 2, 0, rows - NA_WIN_ROWS)


def _na_class(g, groups):
    return jnp.where(g == 0, 0, jnp.where(g == groups - 1, 2, 1))


def _na_onehots(rows):
    groups = rows // NA_GROUP
    by_row = np.zeros((3, NA_GROUP, NA_WIN_ROWS, 2 * NA_ROWS - 1), np.float32)
    for cls, g in enumerate((0, 1, groups - 1)):
        ws = int(np.clip(NA_GROUP * g - NA_ROWS // 2, 0, rows - NA_WIN_ROWS))
        for qr in range(NA_GROUP):
            r = NA_GROUP * g + qr
            rs = int(np.clip(r - NA_ROWS // 2, 0, rows - NA_ROWS))
            for kr in range(NA_WIN_ROWS):
                if rs <= ws + kr < rs + NA_ROWS:
                    by_row[cls, qr, kr, ws + kr - r + NA_ROWS - 1] = 1.0
    c = np.arange(GRID_W)
    cs = np.clip(c - NA_COLS // 2, 0, GRID_W - NA_COLS)
    kc = np.arange(GRID_W)
    win = (kc[None, :] >= cs[:, None]) & (kc[None, :] < cs[:, None] + NA_COLS)
    idx = np.clip(kc[None, :] - c[:, None], -(NA_COLS - 1), NA_COLS - 1) + (NA_COLS - 1)
    by_col = ((idx[:, :, None] == np.arange(2 * NA_COLS - 1)[None, None, :]) & win[:, :, None]).astype(np.float32)
    return by_row, by_col


def _na_bias_tables(rpb, rows):
    by_row, by_col = _na_onehots(rows)
    H = rpb.shape[0]
    hp = lax.Precision.HIGHEST
    e1 = jnp.einsum('hij,ckj->hick', rpb, by_col, precision=hp)
    t = jnp.einsum('hick,zqri->zhqcrk', e1, by_row, precision=hp)
    valid = (by_row.sum(-1) > 0)[:, None, :, None, :, None] & (by_col.sum(-1) > 0)[None, None, None, :, None, :]
    t = jnp.where(valid, t, -jnp.inf)
    return t.reshape(3, H, NA_GROUP * GRID_W, NA_WIN_ROWS * GRID_W)


def _na_bias_grad(dtab, rows):
    by_row, by_col = _na_onehots(rows)
    H = dtab.shape[1]
    hp = lax.Precision.HIGHEST
    d6 = dtab.reshape(3, H, NA_GROUP, GRID_W, NA_WIN_ROWS, GRID_W)
    de1 = jnp.einsum('zhqcrk,zqri->hick', d6, by_row, precision=hp)
    return jnp.einsum('hick,ckj->hij', de1, by_col, precision=hp)


def _na_fwd(qn, kn, vb, tab, name):
    S, W = qn.shape
    rows, groups = _na_geometry(S)
    npair = W // _LANES
    nq = NA_GROUP * GRID_W
    nk = NA_WIN_ROWS * GRID_W
    sc = float(NA_HD) ** -0.5

    def body(q_ref, k_ref, v_ref, b_ref, o_ref):
        g = pl.program_id(1)
        k0 = pl.multiple_of(_na_win_start(g, rows) * GRID_W, GRID_W)
        kw = k_ref[pl.ds(k0, nk), :]
        vw = v_ref[pl.ds(k0, nk), :]
        qv = q_ref[...]
        lane = lax.broadcasted_iota(jnp.int32, (nq, _LANES), 1)
        outs = []
        for hh in range(2):
            mine = (lane >= hh * NA_HD) & (lane < (hh + 1) * NA_HD)
            s = _dg(jnp.where(mine, qv, jnp.zeros_like(qv)), kw, 1, 1) * sc + b_ref[0, hh]
            e = jnp.exp(s - jnp.max(s, axis=-1, keepdims=True))
            p = e / jnp.sum(e, axis=-1, keepdims=True)
            outs.append(_dg(p, vw, 1, 0))
        o_ref[...] = jnp.where(lane < NA_HD, outs[0], outs[1])

    return pl.pallas_call(
        body, name=name, out_shape=jax.ShapeDtypeStruct((S, W), _F32), grid=(npair, groups),
        in_specs=[pl.BlockSpec((nq, _LANES), lambda p, g: (g, p)),
                  pl.BlockSpec((S, _LANES), lambda p, g: (0, p)),
                  pl.BlockSpec((S, _LANES), lambda p, g: (0, p)),
                  pl.BlockSpec((1, 2, nq, nk), lambda p, g: (_na_class(g, groups), p, 0, 0))],
        out_specs=pl.BlockSpec((nq, _LANES), lambda p, g: (g, p)),
        compiler_params=_params(("parallel", "arbitrary")))(qn, kn, vb, tab)


def _na_bwd(qn, kn, vb, tab, do, name):
    S, W = qn.shape
    rows, groups = _na_geometry(S)
    npair = W // _LANES
    nq = NA_GROUP * GRID_W
    nk = NA_WIN_ROWS * GRID_W
    sc = float(NA_HD) ** -0.5

    def body(q_ref, k_ref, v_ref, b_ref, do_ref, dq_ref, dk_ref, dv_ref, db_ref):
        g = pl.program_id(1)

        @pl.when(g == 0)
        def _():
            dk_ref[...] = jnp.zeros_like(dk_ref)
            dv_ref[...] = jnp.zeros_like(dv_ref)

        @pl.when((g <= 1) | (g == groups - 1))
        def _():
            db_ref[...] = jnp.zeros_like(db_ref)

        k0 = pl.multiple_of(_na_win_start(g, rows) * GRID_W, GRID_W)
        kw = k_ref[pl.ds(k0, nk), :]
        vw = v_ref[pl.ds(k0, nk), :]
        qv = q_ref[...]
        dov = do_ref[...]
        lane = lax.broadcasted_iota(jnp.int32, (nq, _LANES), 1)
        dqs = []
        dkw = jnp.zeros((nk, _LANES), _F32)
        dvw = jnp.zeros((nk, _LANES), _F32)
        for hh in range(2):
            mine = (lane >= hh * NA_HD) & (lane < (hh + 1) * NA_HD)
            qm = jnp.where(mine, qv, jnp.zeros_like(qv))
            dom = jnp.where(mine, dov, 0.0)
            s = _dg(qm, kw, 1, 1) * sc + b_ref[0, hh]
            e = jnp.exp(s - jnp.max(s, axis=-1, keepdims=True))
            p = e / jnp.sum(e, axis=-1, keepdims=True)
            dp = _dg(dom, vw, 1, 1)
            ds = p * (dp - jnp.sum(p * dp, axis=-1, keepdims=True))
            db_ref[0, hh] += ds
            dqs.append(_dg(ds, kw, 1, 0) * sc)
            dkw = dkw + _dg(ds, qm, 0, 0) * sc
            dvw = dvw + _dg(p, dom, 0, 0)
        dq_ref[...] = jnp.where(lane < NA_HD, dqs[0], dqs[1])
        dk_ref[pl.ds(k0, nk), :] += dkw
        dv_ref[pl.ds(k0, nk), :] += dvw

    blk = pl.BlockSpec((nq, _LANES), lambda p, g: (g, p))
    full = pl.BlockSpec((S, _LANES), lambda p, g: (0, p))
    tspec = pl.BlockSpec((1, 2, nq, nk), lambda p, g: (_na_class(g, groups), p, 0, 0))
    return pl.pallas_call(
        body, name=name,
        out_shape=(jax.ShapeDtypeStruct((S, W), _F32), jax.ShapeDtypeStruct((S, W), _F32),
                   jax.ShapeDtypeStruct((S, W), _F32), jax.ShapeDtypeStruct(tab.shape, _F32)),
        grid=(npair, groups), in_specs=[blk, full, full, tspec, blk],
        out_specs=(blk, full, full, tspec),
        compiler_params=_params(("arbitrary", "arbitrary")))(qn, kn, vb, tab, do)


def _mem_fwd(qn, km, vm, name):
    S, W = qn.shape
    hd = W // MEM_HEADS
    tq = _pick(S, (512, 256))
    sc = float(hd) ** -0.5

    def body(q_ref, k_ref, v_ref, o_ref):
        for h in range(MEM_HEADS):
            cs = slice(h * hd, (h + 1) * hd)
            s = _dg(q_ref[:, cs], k_ref[:, cs], 1, 1) * sc
            e = jnp.exp(s - jnp.max(s, axis=-1, keepdims=True))
            p = e / jnp.sum(e, axis=-1, keepdims=True)
            o_ref[:, cs] = _dg(p, v_ref[:, cs], 1, 0)

    full = pl.BlockSpec(km.shape, lambda i: (0, 0))
    return pl.pallas_call(
        body, name=name, out_shape=jax.ShapeDtypeStruct((S, W), _F32), grid=(S // tq,),
        in_specs=[pl.BlockSpec((tq, W), lambda i: (i, 0)), full, full],
        out_specs=pl.BlockSpec((tq, W), lambda i: (i, 0)),
        compiler_params=_params(("parallel",)))(qn, km, vm)


def _mem_bwd(qn, km, vm, do, name):
    S, W = qn.shape
    hd = W // MEM_HEADS
    tq = _pick(S, (512, 256))
    sc = float(hd) ** -0.5

    def body(q_ref, k_ref, v_ref, do_ref, dq_ref, dk_ref, dv_ref):
        @pl.when(pl.program_id(0) == 0)
        def _():
            dk_ref[...] = jnp.zeros_like(dk_ref)
            dv_ref[...] = jnp.zeros_like(dv_ref)

        for h in range(MEM_HEADS):
            cs = slice(h * hd, (h + 1) * hd)
            qh = q_ref[:, cs]
            kh = k_ref[:, cs]
            doh = do_ref[:, cs]
            s = _dg(qh, kh, 1, 1) * sc
            e = jnp.exp(s - jnp.max(s, axis=-1, keepdims=True))
            p = e / jnp.sum(e, axis=-1, keepdims=True)
            dp = _dg(doh, v_ref[:, cs], 1, 1)
            ds = p * (dp - jnp.sum(p * dp, axis=-1, keepdims=True))
            dq_ref[:, cs] = _dg(ds, kh, 1, 0) * sc
            dk_ref[:, cs] += _dg(ds, qh, 0, 0) * sc
            dv_ref[:, cs] += _dg(p, doh, 0, 0)

    full = pl.BlockSpec(km.shape, lambda i: (0, 0))
    row = pl.BlockSpec((tq, W), lambda i: (i, 0))
    return pl.pallas_call(
        body, name=name,
        out_shape=(jax.ShapeDtypeStruct((S, W), _F32), jax.ShapeDtypeStruct(km.shape, _F32),
                   jax.ShapeDtypeStruct(km.shape, _F32)),
        grid=(S // tq,), in_specs=[row, full, full, row], out_specs=(row, full, full),
        compiler_params=_params(("arbitrary",)))(qn, km, vm, do)


_ANY = pl.BlockSpec(memory_space=pl.ANY)


def _place():
    x, y, c = lax.axis_index("x"), lax.axis_index("y"), lax.axis_index("c")
    chips = [(1 - x, y), (x, 1 - y), (1 - x, 1 - y)]
    return x, y, c, chips


def _gather_weights(shards, name):
    n = len(shards)

    def body(*refs):
        w = refs[:n]
        g = refs[n:2 * n]
        send, recv = refs[2 * n:]
        x, y, c, chips = _place()
        me = 2 * x + y
        sib = (x, y, 1 - c)

        def piece(t, chip, half):
            return g[t].at[chip, half]

        def rcopy(t, k, chip, half, to, src=None):
            dst = piece(t, chip, half)
            return pltpu.make_async_remote_copy(
                src_ref=dst if src is None else src, dst_ref=dst, send_sem=send.at[t, k], recv_sem=recv.at[t, k],
                device_id=to, device_id_type=_MESH)

        first = [rcopy(t, j, me, c, (cx, cy, c), src=w[t].at[c]) for t in range(n) for j, (cx, cy) in enumerate(chips)]
        for cp in first:
            cp.start()
        passed = []
        for t in range(n):
            for j, (cx, cy) in enumerate(chips):
                rcopy(t, j, 2 * cx + cy, c, (x, y, c)).wait_recv()
                cp = rcopy(t, 3 + j, 2 * cx + cy, c, sib)
                cp.start()
                passed.append(cp)
        for t in range(n):
            for j, (cx, cy) in enumerate(chips):
                rcopy(t, 3 + j, 2 * cx + cy, 1 - c, (x, y, c)).wait_recv()
        for cp in first + passed:
            cp.wait_send()

    return pl.pallas_call(
        body, name=name,
        out_shape=tuple(jax.ShapeDtypeStruct((4,) + s.shape, s.dtype) for s in shards),
        in_specs=[_ANY] * n, out_specs=tuple([_ANY] * n),
        scratch_shapes=[pltpu.SemaphoreType.DMA((n, 6)), pltpu.SemaphoreType.DMA((n, 6))],
        compiler_params=_params())(*shards)


def _swap_halves(ts, name):
    n = len(ts)

    def body(*refs):
        t_in = refs[:n]
        land = refs[n:2 * n]
        send, recv = refs[2 * n:]
        x, y, c, _ = _place()
        cps = [pltpu.make_async_remote_copy(
            src_ref=t_in[t].at[:, 1 - c], dst_ref=land[t], send_sem=send.at[t], recv_sem=recv.at[t],
            device_id=(x, y, 1 - c), device_id_type=_MESH) for t in range(n)]
        for cp in cps:
            cp.start()
        for cp in cps:
            cp.wait()

    return pl.pallas_call(
        body, name=name,
        out_shape=tuple(jax.ShapeDtypeStruct((4,) + t.shape[2:], t.dtype) for t in ts),
        in_specs=[_ANY] * n, out_specs=tuple([_ANY] * n),
        scratch_shapes=[pltpu.SemaphoreType.DMA((n,)), pltpu.SemaphoreType.DMA((n,))],
        compiler_params=_params())(*ts)


def _scatter_chips(ps, name):
    n = len(ps)

    def body(*refs):
        p = refs[:n]
        land = refs[n:2 * n]
        send, recv = refs[2 * n:]
        x, y, c, chips = _place()
        me = 2 * x + y
        cps = []
        for t in range(n):
            for j, (cx, cy) in enumerate(chips):
                cps.append(pltpu.make_async_remote_copy(
                    src_ref=p[t].at[2 * cx + cy], dst_ref=land[t].at[me], send_sem=send.at[t, j],
                    recv_sem=recv.at[t, j], device_id=(cx, cy, c), device_id_type=_MESH))
        for cp in cps:
            cp.start()
        for t in range(n):
            for j, (cx, cy) in enumerate(chips):
                pltpu.make_async_remote_copy(
                    src_ref=p[t].at[me], dst_ref=land[t].at[2 * cx + cy], send_sem=send.at[t, j],
                    recv_sem=recv.at[t, j], device_id=(cx, cy, c), device_id_type=_MESH).wait_recv()
        for cp in cps:
            cp.wait_send()

    return pl.pallas_call(
        body, name=name,
        out_shape=tuple(jax.ShapeDtypeStruct(t.shape, t.dtype) for t in ps),
        in_specs=[_ANY] * n, out_specs=tuple([_ANY] * n),
        scratch_shapes=[pltpu.SemaphoreType.DMA((n, 3)), pltpu.SemaphoreType.DMA((n, 3))],
        compiler_params=_params())(*ps)


def _swap_reduced(rs, name):
    n = len(rs)

    def body(*refs):
        r_in = refs[:n]
        out = refs[n:2 * n]
        send, recv = refs[2 * n:]
        x, y, c, _ = _place()
        cps = [pltpu.make_async_remote_copy(
            src_ref=r_in[t], dst_ref=out[t], send_sem=send.at[t], recv_sem=recv.at[t],
            device_id=(x, y, 1 - c), device_id_type=_MESH) for t in range(n)]
        for cp in cps:
            cp.start()
        for cp in cps:
            cp.wait()

    return pl.pallas_call(
        body, name=name,
        out_shape=tuple(jax.ShapeDtypeStruct(t.shape, t.dtype) for t in rs),
        in_specs=[_ANY] * n, out_specs=tuple([_ANY] * n),
        scratch_shapes=[pltpu.SemaphoreType.DMA((n,)), pltpu.SemaphoreType.DMA((n,))],
        compiler_params=_params())(*rs)


def _all_devices(v, reduce, name):
    rows = v.shape[0]

    def body(v_ref, o_ref, *rest):
        if reduce:
            all_ref, send, recv = rest
        else:
            send, recv = rest
            all_ref = o_ref
        x, y, c, _ = _place()
        me = 4 * x + 2 * y + c
        all_ref[me] = v_ref[...]
        cps = []
        for k in range(1, 8):
            fx, fy, fc = (k >> 2) & 1, (k >> 1) & 1, k & 1
            to = (x ^ fx, y ^ fy, c ^ fc)
            cps.append(pltpu.make_async_remote_copy(
                src_ref=v_ref, dst_ref=all_ref.at[me], send_sem=send.at[k - 1], recv_sem=recv.at[k - 1],
                device_id=to, device_id_type=_MESH))
        for cp in cps:
            cp.start()
        for k in range(1, 8):
            fx, fy, fc = (k >> 2) & 1, (k >> 1) & 1, k & 1
            frm = 4 * (x ^ fx) + 2 * (y ^ fy) + (c ^ fc)
            pltpu.make_async_remote_copy(
                src_ref=v_ref, dst_ref=all_ref.at[frm], send_sem=send.at[k - 1], recv_sem=recv.at[k - 1],
                device_id=(x, y, c), device_id_type=_MESH).wait_recv()
        for cp in cps:
            cp.wait_send()
        if reduce:
            acc = all_ref[0]
            for d in range(1, 8):
                acc = acc + all_ref[d]
            o_ref[...] = acc

    vm = pl.BlockSpec(memory_space=pltpu.VMEM)
    if reduce:
        out_shape = jax.ShapeDtypeStruct((rows, _LANES), _F32)
        scratch = [pltpu.VMEM((8, rows, _LANES), _F32)]
    else:
        out_shape = jax.ShapeDtypeStruct((8, rows, _LANES), _F32)
        scratch = []
    return pl.pallas_call(
        body, name=name, out_shape=out_shape, in_specs=[vm], out_specs=vm,
        scratch_shapes=scratch + [pltpu.SemaphoreType.DMA((7,)), pltpu.SemaphoreType.DMA((7,))],
        compiler_params=_params())(v)


def _gather_layer_weights(shards, name):
    halves = [shards[k].reshape((2, shards[k].shape[0] // 2) + shards[k].shape[1:]) for k in BIG]
    got = _gather_weights(halves, name)
    chip = 2 * lax.axis_index("x") + lax.axis_index("y")
    full = {}
    for k, g, own in zip(BIG, got, halves):
        g = lax.dynamic_update_slice(g, own[None], (chip, 0, 0, 0))
        _, _, r, c = g.shape
        if k == 'ffn_w13':
            full[k] = g
        elif k == 'w_in':
            full[k] = _shards_to_cols(g)
        else:
            full[k] = g.reshape(8 * r, c)
    return full


def _reduce_layer_grads(grads, cidx, name):
    ts = []
    for k in BIG:
        g = grads[k]
        if k == 'ffn_w13':
            ts.append(g)
        elif k == 'w_in':
            ts.append(_cols_to_shards(g))
        else:
            r, c = g.shape[0] // 8, g.shape[1]
            ts.append(g.reshape(4, 2, r, c))
    chip = (2 * lax.axis_index("x") + lax.axis_index("y")).astype(jnp.int32).reshape(1)
    landed = _swap_halves(ts, name + "_swap")
    partial = [_add_pair(t, la, cidx, f"{name}_add2_{k}") for k, t, la in zip(BIG, ts, landed)]
    slots = _scatter_chips(partial, name + "_scatter")
    reduced = [_add_four(p, s, chip, f"{name}_add4_{k}") for k, p, s in zip(BIG, partial, slots)]
    theirs = _swap_reduced(reduced, name + "_join")
    out = {}
    for k, mine, other in zip(BIG, reduced, theirs):
        both = jnp.stack([mine, other])
        both = jnp.where(cidx[0] == 0, both, both[::-1])
        out[k] = both.reshape((2 * mine.shape[0], mine.shape[1]))
    return out


def _pack_rows(arrs):
    parts, spans = [], []
    off = 0
    for a in arrs:
        n = int(np.prod(a.shape))
        pad = (-n) % (8 * _LANES)
        parts.append(jnp.pad(a.reshape(-1), (0, pad)))
        spans.append((off, n, a.shape))
        off += n + pad
    return jnp.concatenate(parts).reshape(-1, _LANES), spans


def _unpack_rows(packed, spans):
    flat = packed.reshape(-1)
    return [flat[o:o + n].reshape(shape) for o, n, shape in spans]


def _in_split(d):
    dk, dv, w = d // 4, d // 2, d // 4
    names = [('g_q', dk), ('g_k', dk), ('g_v', dv), ('g_r', dv), ('g_f', GLA_GATE_RANK), ('g_b', GLA_GATE_RANK),
             ('n_q', w), ('n_k', w), ('n_v', w), ('m_q', w)]
    out, off = {}, 0
    for nme, wd in names:
        out[nme] = (off, wd)
        off += wd
    return out


def _permute_w_in(w):
    sp = _in_split(w.shape[0])
    f0 = sp['g_f'][0]
    n0 = sp['n_q'][0]
    main = jnp.concatenate([w[:, :f0], w[:, n0:]], axis=1)
    gate = jnp.pad(w[:, f0:n0], ((0, 0), (0, _LANES - (n0 - f0))))
    return main, gate


def _unpermute_w_in(main, gate, d):
    sp = _in_split(d)
    f0 = sp['g_f'][0]
    n0 = sp['n_q'][0]
    return jnp.concatenate([main[:, :f0], gate[:, :n0 - f0], main[:, f0:]], axis=1)


def _gate_weight(wg2_f, wg2_b):
    r, dk = wg2_f.shape
    top = jnp.concatenate([wg2_f, jnp.zeros_like(wg2_f)], axis=1)
    mid = jnp.concatenate([jnp.zeros_like(wg2_b), wg2_b], axis=1)
    return jnp.concatenate([top, mid, jnp.zeros((_LANES - 2 * r, 2 * dk), wg2_f.dtype)], axis=0)


def _layer_fwd(l, x, mem, w, full, tab):
    S, D = x.shape
    dk, dv, nw = D // 4, D // 2, D // 4
    sv = {}
    w_main, w_gate = _permute_w_in(full['w_in'])
    sv['w_main'], sv['w_gate'] = w_main, w_gate
    xn, r1 = _rmsnorm_fwd(x, w['attn_norm'][None], f"rms1_fwd")
    proj = _mm(xn, w_main, 'nn', "mm_proj")
    gfb = _mm(xn, w_gate, 'nn', "mm_gate")
    sv.update(x=x, xn=xn, r1=r1, proj=proj, gfb=gfb)
    wg = _gate_weight(w['gla_wg2_f'], w['gla_wg2_b']).astype(_BF)
    bg = jnp.concatenate([w['gla_bg_f'], w['gla_bg_b']])[None]
    o_f, st_f = _gla_fwd(proj, gfb, wg, bg, False, "gla_fwd_f")
    o_b, st_b = _gla_fwd(proj, gfb, wg, bg, True, "gla_fwd_b")
    y_gla = _gla_post_fwd(o_f, o_b, proj, (2 * dk) // dv + 1, w['gla_out_norm'][None], "gla_post_fwd")
    sv.update(wg=wg, bg=bg, o_f=o_f, o_b=o_b, st_f=st_f, st_b=st_b)
    c0 = (2 * dk + 2 * dv) // nw
    qn = _headnorm_fwd(proj, c0, nw, NA_HD, jnp.tile(w['na_q_norm'], nw // NA_HD)[None], "na_qnorm_fwd")
    kn = _headnorm_fwd(proj, c0 + 1, nw, NA_HD, jnp.tile(w['na_k_norm'], nw // NA_HD)[None], "na_knorm_fwd")
    vb = proj[:, (c0 + 2) * nw:(c0 + 3) * nw].astype(_BF)
    o_na = _na_fwd(qn, kn, vb, tab, "na_fwd")
    y_na = _headnorm_fwd(o_na, 0, nw, NA_HD, w['na_out_norm'][None], "na_onorm_fwd")
    sv.update(qn=qn, kn=kn, vb=vb, o_na=o_na)
    mhd = nw // MEM_HEADS
    mqn = _headnorm_fwd(proj, c0 + 3, nw, mhd, jnp.tile(w['mem_q_norm'], MEM_HEADS)[None], "mem_qnorm_fwd")
    memn, rm = _rmsnorm_fwd(mem, w['mem_norm'][None], "mem_rms_fwd")
    kv = _mm(memn, full['mem_wkv'], 'nn', "mm_memkv")
    km = _headnorm_fwd(kv, 0, nw, mhd, jnp.tile(w['mem_k_norm'], MEM_HEADS)[None], "mem_knorm_fwd")
    vm = kv[:, nw:].astype(_BF)
    o_mem = _mem_fwd(mqn, km, vm, "mem_fwd")
    y_mem = _headnorm_fwd(o_mem, 0, nw, mhd, w['mem_out_norm'][None], "mem_onorm_fwd")
    sv.update(mqn=mqn, memn=memn, rm=rm, kv=kv, km=km, vm=vm, o_mem=o_mem)
    y = jnp.concatenate([y_gla, y_na, y_mem], axis=1)
    x1 = _mm(y, full['w_out'], 'nn', "mm_out", res=x)
    h, r2 = _rmsnorm_fwd(x1, w['ffn_norm'][None], "rms2_fwd")
    a, gate, up = _ffn_up_fwd(h, full['ffn_w13'], "ffn_up_fwd")
    x2 = _mm(a, full['ffn_w2'], 'nn', "mm_w2", res=x1)
    sv.update(y=y, x1=x1, h=h, r2=r2, gate=gate, up=up, a=a)
    return x2, sv


def _layer_bwd(l, dx2, dx2b, mem, w, full, tab, sv):
    S, D = dx2.shape
    dk, dv, nw = D // 4, D // 2, D // 4
    gb, gs = {}, {}
    dgate, dup = _ffn_up_bwd(dx2b, full['ffn_w2'], sv['gate'], sv['up'], "ffn_up_bwd")
    gb['ffn_w2'] = _mm(sv['a'], dx2b, 'tn', "mm_dw2")
    dh = _ffn_up_dh(dgate, dup, full['ffn_w13'], "ffn_up_dh")
    gb['ffn_w13'] = _ffn_up_dw(sv['h'], dgate, dup, "ffn_up_dw")
    dx1, dx1b, g = _rmsnorm_bwd(sv['x1'], sv['r2'], w['ffn_norm'][None], dh, dx2, "rms2_bwd")
    gs['ffn_norm'] = g[0]
    dy = _mm(dx1b, full['w_out'], 'nt', "mm_dy")
    gb['w_out'] = _mm(sv['y'], dx1b, 'tn', "mm_dwout")
    c0 = (2 * dk + 2 * dv) // nw
    mhd = nw // MEM_HEADS
    do_mem, g = _headnorm_bwd(sv['o_mem'], 0, nw, mhd, w['mem_out_norm'][None], dy, (dv + nw) // nw, "mem_onorm_bwd")
    gs['mem_out_norm'] = g[0]
    dmqn, dkm, dvm = _mem_bwd(sv['mqn'], sv['km'], sv['vm'], do_mem, "mem_bwd")
    dmq, g = _headnorm_bwd(sv['proj'], c0 + 3, nw, mhd, jnp.tile(w['mem_q_norm'], MEM_HEADS)[None], dmqn, 0,
                           "mem_qnorm_bwd", _BF)
    gs['mem_q_norm'] = g[0].reshape(MEM_HEADS, mhd).sum(0)
    dkvk, g = _headnorm_bwd(sv['kv'], 0, nw, mhd, jnp.tile(w['mem_k_norm'], MEM_HEADS)[None], dkm, 0,
                            "mem_knorm_bwd")
    gs['mem_k_norm'] = g[0].reshape(MEM_HEADS, mhd).sum(0)
    dkv = jnp.concatenate([dkvk, dvm], axis=1).astype(_BF)
    gb['mem_wkv'] = _mm(sv['memn'], dkv, 'tn', "mm_dwkv")
    dmemn = _mm(dkv, full['mem_wkv'], 'nt', "mm_dmemn")
    _, _, g = _rmsnorm_bwd(mem, sv['rm'], w['mem_norm'][None], dmemn, None, "mem_rms_bwd")
    gs['mem_norm'] = g[0]
    do_na, g = _headnorm_bwd(sv['o_na'], 0, nw, NA_HD, w['na_out_norm'][None], dy, dv // nw, "na_onorm_bwd")
    gs['na_out_norm'] = g[0]
    dqn, dkn, dnv, dtab = _na_bwd(sv['qn'], sv['kn'], sv['vb'], tab, do_na, "na_bwd")
    gs['na_rpb'] = _na_bias_grad(dtab, S // GRID_W)
    dnq, g = _headnorm_bwd(sv['proj'], c0, nw, NA_HD, jnp.tile(w['na_q_norm'], nw // NA_HD)[None], dqn, 0,
                           "na_qnorm_bwd", _BF)
    gs['na_q_norm'] = g[0].reshape(nw // NA_HD, NA_HD).sum(0)
    dnk, g = _headnorm_bwd(sv['proj'], c0 + 1, nw, NA_HD, jnp.tile(w['na_k_norm'], nw // NA_HD)[None], dkn, 0,
                           "na_knorm_bwd", _BF)
    gs['na_k_norm'] = g[0].reshape(nw // NA_HD, NA_HD).sum(0)
    do_gla, dgr, g = _gla_post_bwd(sv['o_f'], sv['o_b'], sv['proj'], (2 * dk) // dv + 1, w['gla_out_norm'][None], dy,
                                   "gla_post_bwd")
    gs['gla_out_norm'] = g[0]
    dq1, dk1, dv1, dz_f = _gla_bwd(sv['proj'], sv['gfb'], sv['wg'], sv['bg'], sv['st_f'], do_gla, False, None,
                                   "gla_bwd_f")
    dgq, dgk, dgv, dz_b = _gla_bwd(sv['proj'], sv['gfb'], sv['wg'], sv['bg'], sv['st_b'], do_gla, True,
                                   (dq1, dk1, dv1), "gla_bwd_b")
    dz = jnp.concatenate([dz_f, dz_b], axis=1).astype(_BF)
    ones_lane = (jnp.arange(_LANES) == 2 * GLA_GATE_RANK)[None]
    gfb_aug = jnp.where(ones_lane, 1.0, sv['gfb']).astype(_BF)
    dwg = _mm(gfb_aug, dz, 'tn', "mm_dwg")
    r16 = GLA_GATE_RANK
    gs['gla_wg2_f'] = dwg[:r16, :dk]
    gs['gla_wg2_b'] = dwg[r16:2 * r16, dk:]
    gs['gla_bg_f'] = dwg[2 * r16, :dk]
    gs['gla_bg_b'] = dwg[2 * r16, dk:]
    dgfb = _mm(dz, sv['wg'], 'nt', "mm_dgfb", out_dtype=_BF)
    dproj = jnp.concatenate([dgq.astype(_BF), dgk.astype(_BF), dgv.astype(_BF), dgr, dnq, dnk, dnv.astype(_BF), dmq],
                            axis=1)
    t = _mm(dgfb, sv['w_gate'], 'nt', "mm_dxn_gate")
    dxn = _mm(dproj, sv['w_main'], 'nt', "mm_dxn", res=t)
    dw_main = _mm(sv['xn'], dproj, 'tn', "mm_dwmain")
    dw_gate = _mm(sv['xn'], dgfb, 'tn', "mm_dwgate")
    gb['w_in'] = _unpermute_w_in(dw_main, dw_gate, D)
    dx, dxb, g = _rmsnorm_bwd(sv['x'], sv['r1'], w['attn_norm'][None], dxn, dx1, "rms1_bwd")
    gs['attn_norm'] = g[0]
    return dx, dxb, gb, gs


def kernel(x, mem, attn_norm, w_in, gla_wg2_f, gla_bg_f, gla_wg2_b, gla_bg_b, gla_out_norm, na_q_norm, na_k_norm, na_rpb, na_out_norm, mem_norm, mem_wkv, mem_q_norm, mem_k_norm, mem_out_norm, w_out, ffn_norm, ffn_w13, ffn_w2, loss_target, m_attn_norm, m_w_in, m_gla_wg2_f, m_gla_bg_f, m_gla_wg2_b, m_gla_bg_b, m_gla_out_norm, m_na_q_norm, m_na_k_norm, m_na_rpb, m_na_out_norm, m_mem_norm, m_mem_wkv, m_mem_q_norm, m_mem_k_norm, m_mem_out_norm, m_w_out, m_ffn_norm, m_ffn_w13, m_ffn_w2, v_attn_norm, v_w_in, v_gla_wg2_f, v_gla_bg_f, v_gla_wg2_b, v_gla_bg_b, v_gla_out_norm, v_na_q_norm, v_na_k_norm, v_na_rpb, v_na_out_norm, v_mem_norm, v_mem_wkv, v_mem_q_norm, v_mem_k_norm, v_mem_out_norm, v_w_out, v_ffn_norm, v_ffn_w13, v_ffn_w2):
    args = locals()
    W = {k: args[k] for k in WEIGHTS}
    M = {k: args['m_' + k] for k in WEIGHTS}
    V = {k: args['v_' + k] for k in WEIGHTS}
    depth = attn_norm.shape[0]
    xs, mems, tgt = x[0], mem[0], loss_target[0]
    cidx = lax.axis_index("c").astype(jnp.int32).reshape(1)

    gate_rows, gate_spans = _pack_rows([W[k] for k in SMALL_SHARDED])
    gate_all = _all_devices(gate_rows, False, "gate_weights_gather")
    gate_full = {}
    for i, k in enumerate(SMALL_SHARDED):
        per_chip = [_unpack_rows(gate_all[2 * j], gate_spans)[i] for j in range(4)]
        gate_full[k] = jnp.concatenate(per_chip, axis=-1)

    small = [k for k in WEIGHTS if k not in BIG]

    def layer_small(l):
        d = {k: W[k][l] for k in small if k not in SMALL_SHARDED}
        d.update({k: gate_full[k][l] for k in SMALL_SHARDED})
        return d

    saved, fulls, tabs = [], [], []
    cur = xs
    for l in range(depth):
        full = _gather_layer_weights({k: W[k][l].astype(_BF) for k in BIG}, "gather_weights")
        tab = _na_bias_tables(W['na_rpb'][l], xs.shape[0] // GRID_W)
        cur, sv = _layer_fwd(l, cur, mems, layer_small(l), full, tab)
        saved.append(sv)
        fulls.append(full)
        tabs.append(tab)
    loss_tile, dy, dyb = _loss_head(cur, tgt, "loss_head")
    loss = lax.psum(loss_tile[0, 0], ("x", "y", "c"))

    big_grads = [None] * depth
    small_grads = [None] * depth
    for l in range(depth - 1, -1, -1):
        dy, dyb, gb, gs = _layer_bwd(l, dy, dyb, mems, layer_small(l), fulls[l], tabs[l], saved[l])
        big_grads[l] = _reduce_layer_grads(gb, cidx, "reduce_grads")
        small_grads[l] = gs
    grad_x = dy[None]

    small_stack = [jnp.stack([small_grads[l][k] for l in range(depth)]) for k in small]
    packed, spans = _pack_rows(small_stack)
    summed = _unpack_rows(_all_devices(packed, True, "small_grads_sum"), spans)
    G = dict(zip(small, summed))
    chip = 2 * lax.axis_index("x") + lax.axis_index("y")
    for k in SMALL_SHARDED:
        wdt = W[k].shape[-1]
        G[k] = lax.dynamic_slice_in_dim(G[k], chip * wdt, wdt, axis=2)
    for k in BIG:
        G[k] = jnp.stack([big_grads[l][k] for l in range(depth)])

    delta, new_m, new_v = {}, {}, {}
    for k in WEIGHTS:
        delta[k], new_m[k], new_v[k] = _adamw(W[k], G[k], M[k], V[k], "adamw_" + k)
    return (loss, grad_x, *[G[k] for k in WEIGHTS], *[delta[k] for k in WEIGHTS], *[new_m[k] for k in WEIGHTS],
            *[new_v[k] for k in WEIGHTS])
```

```python
import functools

import numpy as np
import jax
import jax.numpy as jnp
from jax import lax
from jax.experimental import pallas as pl
from jax.experimental.pallas import tpu as pltpu

_F32 = jnp.float32
_BF = jnp.bfloat16
_MESH = pl.DeviceIdType.MESH

_VMEM_LIMIT_BYTES = 56 * 1024 * 1024
_LANES = 128

RMS_EPS = 1e-6
GLA_HEADS = 4
GLA_GATE_RANK = 16
GLA_TAU = 16.0
GLA_CHUNK = 64
GRID_W = 64
NA_HD = 64
NA_ROWS = 8
NA_COLS = 16
MEM_HEADS = 4
ADAM_LR = 0.001
ADAM_B1 = 0.9
ADAM_B2 = 0.999
ADAM_EPS = 1e-08
ADAM_WD = 0.01
ADAM_STEP = 10

WEIGHTS = ['attn_norm', 'w_in', 'gla_wg2_f', 'gla_bg_f', 'gla_wg2_b', 'gla_bg_b', 'gla_out_norm', 'na_q_norm',
           'na_k_norm', 'na_rpb', 'na_out_norm', 'mem_norm', 'mem_wkv', 'mem_q_norm', 'mem_k_norm', 'mem_out_norm',
           'w_out', 'ffn_norm', 'ffn_w13', 'ffn_w2']
BIG = ['w_in', 'mem_wkv', 'w_out', 'ffn_w13', 'ffn_w2']
SMALL_SHARDED = ['gla_wg2_f', 'gla_wg2_b']


def _pick(n, cands):
    for c in cands:
        if n % c == 0:
            return c
    return n


def _row_block(rows, row_bytes, cap_bytes):
    best = 8
    for rb in range(8, rows + 1, 8):
        if rows % rb == 0 and rb * row_bytes <= cap_bytes:
            best = rb
    return best


def _params(sem=None):
    return pltpu.CompilerParams(dimension_semantics=sem, vmem_limit_bytes=_VMEM_LIMIT_BYTES)


def _dg(a, b, ca, cb):
    return lax.dot_general(a.astype(_BF), b.astype(_BF), (((ca,), (cb,)), ((), ())), preferred_element_type=_F32)


def _split(a):
    hi = a.astype(_BF)
    return hi, (a - hi.astype(_F32)).astype(_BF)


def _dg_hl(a, b, ca, cb):
    hi, lo = _split(a)
    return _dg(hi, b, ca, cb) + _dg(lo, b, ca, cb)


def _dg_lh(a, b, ca, cb):
    hi, lo = _split(b)
    return _dg(a, hi, ca, cb) + _dg(a, lo, ca, cb)


def _sigmoid(x):
    return 1.0 / (1.0 + jnp.exp(-x))


def _log_sigmoid(z):
    return jnp.minimum(z, 0.0) - jnp.log(1.0 + jnp.exp(-jnp.abs(z)))


def _block_diag(width, hd):
    i = np.arange(width) // hd
    return jnp.asarray((i[:, None] == i[None, :]).astype(np.float32), dtype=_BF)


class _Rider:
    def __init__(self, ins, outs, sems, start, finish, aliases=()):
        self.ins, self.outs, self.sems = list(ins), list(outs), list(sems)
        self.start, self.finish, self.aliases = start, finish, tuple(aliases)


def _call(core, name, grid, in_specs, out_specs, out_shape, scratch, sem, args, rider=None):
    out_specs, out_shape = tuple(out_specs), tuple(out_shape)
    if rider is None:
        return pl.pallas_call(core, name=name, out_shape=out_shape, grid=grid, in_specs=list(in_specs),
                              out_specs=out_specs, scratch_shapes=list(scratch), compiler_params=_params(sem))(*args)
    ni, no, ns = len(in_specs), len(out_specs), len(scratch)
    ri, ro = len(rider.ins), len(rider.outs)

    def body(*refs):
        ins, rins = refs[:ni], refs[ni:ni + ri]
        outs, routs = refs[ni + ri:ni + ri + no], refs[ni + ri + no:ni + ri + no + ro]
        scr, rsem = refs[ni + ri + no + ro:ni + ri + no + ro + ns], refs[ni + ri + no + ro + ns:]
        first = pl.program_id(0) == 0
        last = pl.program_id(0) == grid[0] - 1
        for ax in range(1, len(grid)):
            first = first & (pl.program_id(ax) == 0)
            last = last & (pl.program_id(ax) == grid[ax] - 1)

        @pl.when(first)
        def _():
            rider.start(rins, routs, rsem)

        core(*ins, *outs, *scr)

        @pl.when(last)
        def _():
            rider.finish(rins, routs, rsem)

    res = pl.pallas_call(
        body, name=name, out_shape=out_shape + tuple(rider.outs), grid=grid,
        in_specs=list(in_specs) + [_ANY] * ri, out_specs=out_specs + (_ANY,) * ro,
        scratch_shapes=list(scratch) + rider.sems,
        input_output_aliases={ni + i: no + o for i, o in rider.aliases},
        compiler_params=_params(("arbitrary",) * len(grid)))(*args, *rider.ins)
    return tuple(res[:no]), tuple(res[no:])


_ANY = pl.BlockSpec(memory_space=pl.ANY)


def _mm(a, b, mode, name, out_dtype=_F32, res=None, rider=None):
    def finish(r):
        return r[0] if rider is None else (r[0][0], r[1])

    if mode == 'nn':
        (M, K), N = a.shape, b.shape[1]
    elif mode == 'nt':
        (M, K), N = a.shape, b.shape[0]
    else:
        (K, M), N = a.shape, b.shape[1]
    ca, cb = {'nn': (1, 0), 'nt': (1, 1), 'tn': (0, 0)}[mode]
    has_res = res is not None
    args = (a, b) + ((res,) if has_res else ())
    out_shape = jax.ShapeDtypeStruct((M, N), out_dtype)
    if K <= _K_RESIDENT:
        tm = _pick(M, (1024, 512, 256, 128))
        tn = _pick(N, (1024, 512, 256, 128))
        a_spec = (pl.BlockSpec((K, tm), lambda i, j: (0, i)) if mode == 'tn'
                  else pl.BlockSpec((tm, K), lambda i, j: (i, 0)))
        b_spec = (pl.BlockSpec((tn, K), lambda i, j: (j, 0)) if mode == 'nt'
                  else pl.BlockSpec((K, tn), lambda i, j: (0, j)))
        o_spec = pl.BlockSpec((tm, tn), lambda i, j: (i, j))

        def body1(*refs):
            a_ref, b_ref = refs[:2]
            o_ref = refs[-1]
            out = _dg(a_ref[...], b_ref[...], ca, cb)
            if has_res:
                out = out + refs[2][...]
            o_ref[...] = out.astype(o_ref.dtype)

        return finish(_call(body1, name, (M // tm, N // tn), [a_spec, b_spec] + ([o_spec] if has_res else []),
                            [o_spec], [out_shape], [], ("parallel", "parallel"), args, rider))

    tk = _pick(K, (1024, 1408, 512, 256, 128))
    nk = K // tk
    tm, tn = _loop_tiles(M, N, tk, has_res, out_dtype)
    if mode == 'tn':
        a_spec = pl.BlockSpec((tk, tm), lambda i, j, k: (k, i))
    else:
        a_spec = pl.BlockSpec((tm, tk), lambda i, j, k: (i, k))
    if mode == 'nt':
        b_spec = pl.BlockSpec((tn, tk), lambda i, j, k: (j, k))
    else:
        b_spec = pl.BlockSpec((tk, tn), lambda i, j, k: (k, j))
    o_spec = pl.BlockSpec((tm, tn), lambda i, j, k: (i, j))
    in_out = out_dtype == _F32

    def body(*refs):
        a_ref, b_ref = refs[:2]
        o_ref = refs[-1] if in_out else refs[-2]
        acc = o_ref if in_out else refs[-1]
        k = pl.program_id(2)
        part = _dg(a_ref[...], b_ref[...], ca, cb)

        @pl.when(k == 0)
        def _():
            acc[...] = part + refs[2][...] if has_res else part

        @pl.when(k > 0)
        def _():
            acc[...] += part

        if not in_out:
            @pl.when(k == nk - 1)
            def _():
                o_ref[...] = acc[...].astype(o_ref.dtype)

    return finish(_call(body, name, (M // tm, N // tn, nk), [a_spec, b_spec] + ([o_spec] if has_res else []),
                        [o_spec], [out_shape], [] if in_out else [pltpu.VMEM((tm, tn), _F32)],
                        ("parallel", "parallel", "arbitrary"), args, rider))


_K_RESIDENT = 2048
_LOOP_TILE_BYTES = 28 * 1024 * 1024


def _loop_tiles(M, N, tk, has_res, out_dtype):
    best = None
    for tm in (2048, 1408, 1024, 512, 256, 128):
        for tn in (2048, 1024, 512, 256, 128):
            if M % tm or N % tn:
                continue
            obytes = 4 if out_dtype == _F32 else 2
            need = 2 * 2 * tk * (tm + tn) + tm * tn * (2 * obytes + (8 if has_res else 0) + (0 if obytes == 4 else 4))
            if need <= _LOOP_TILE_BYTES and (best is None or tm * tn > best[0] * best[1]):
                best = (tm, tn)
    return best


def _cols_to_shards(w):
    r, c = w.shape[0] // 2, w.shape[1] // 4
    return jnp.transpose(w.reshape(2, r, 4, c), (2, 0, 1, 3))


def _shards_to_cols(g):
    _, _, r, c = g.shape
    return jnp.transpose(g, (1, 2, 0, 3)).reshape(2 * r, 4 * c)


_FFN_SUB = 256


def _ffn_up_fwd(h, g13, name, rider=None):
    S = h.shape[0]
    _, _, R, C = g13.shape
    tm = _pick(S, (512, 256, 128))
    cw, nc = _ffn_cols(C)
    subs = [(s, min(s + _FFN_SUB, cw)) for s in range(0, cw, _FFN_SUB)]

    def body(h_ref, wg_ref, wu_ref, a_ref, g_ref, u_ref):
        h0, h1 = h_ref[:, :R], h_ref[:, R:]
        for lo, hi in subs:
            sl = slice(lo, hi)
            gv = _dg(h0, wg_ref[0, 0, :, sl], 1, 0) + _dg(h1, wg_ref[0, 1, :, sl], 1, 0)
            uv = _dg(h0, wu_ref[0, 0, :, sl], 1, 0) + _dg(h1, wu_ref[0, 1, :, sl], 1, 0)
            a_ref[:, sl] = (gv * _sigmoid(gv) * uv).astype(a_ref.dtype)
            g_ref[:, sl] = gv.astype(g_ref.dtype)
            u_ref[:, sl] = uv.astype(u_ref.dtype)

    out = jax.ShapeDtypeStruct((S, 2 * C), _BF)
    ospec = pl.BlockSpec((tm, cw), lambda j, i: (i, j))
    return _call(
        body, name, (2 * nc, S // tm),
        [pl.BlockSpec((tm, 2 * R), lambda j, i: (i, 0)),
         pl.BlockSpec((1, 2, R, cw), lambda j, i: (lax.div(j, nc), 0, 0, lax.rem(j, nc))),
         pl.BlockSpec((1, 2, R, cw), lambda j, i: (2 + lax.div(j, nc), 0, 0, lax.rem(j, nc)))],
        (ospec, ospec, ospec), (out, out, out), [], ("parallel", "parallel"), (h, g13, g13), rider)


def _ffn_cols(C):
    if C % (2 * _LANES) == 0:
        return C // 2, 2
    return C, 1


def _ffn_up_bwd(dxb, w2, gate, up, name, rider=None):
    S, D = dxb.shape
    F = w2.shape[0]
    tf = 2 * _FFN_SUB
    tm = _pick(S, (1024, 512, 256, 128))

    def body(d_ref, w_ref, g_ref, u_ref, dg_ref, du_ref):
        dv = d_ref[...]
        for s in range(tf // _FFN_SUB):
            sl = slice(s * _FFN_SUB, (s + 1) * _FFN_SUB)
            da = _dg(dv, w_ref[sl, :], 1, 1)
            gv = g_ref[:, sl].astype(_F32)
            uv = u_ref[:, sl].astype(_F32)
            sg = _sigmoid(gv)
            dg_ref[:, sl] = (da * uv * (sg * (1.0 + gv * (1.0 - sg)))).astype(dg_ref.dtype)
            du_ref[:, sl] = (da * (gv * sg)).astype(du_ref.dtype)

    out = jax.ShapeDtypeStruct((S, F), _BF)
    tile = pl.BlockSpec((tm, tf), lambda i, j: (i, j))
    return _call(
        body, name, (S // tm, F // tf),
        [pl.BlockSpec((tm, D), lambda i, j: (i, 0)), pl.BlockSpec((tf, D), lambda i, j: (j, 0)), tile, tile],
        (tile, tile), (out, out), [], ("parallel", "parallel"), (dxb, w2, gate, up), rider)


def _ffn_up_dh(dgate, dup, g13, name):
    S = dgate.shape[0]
    _, _, R, C = g13.shape
    tm = _pick(S, (512, 256, 128))

    def body(dg_ref, du_ref, wg_ref, wu_ref, o_ref):
        j = pl.program_id(1)
        dgv, duv = dg_ref[...], du_ref[...]
        for hh in range(2):
            part = _dg(dgv, wg_ref[0, hh], 1, 1) + _dg(duv, wu_ref[0, hh], 1, 1)
            cols = slice(hh * R, (hh + 1) * R)

            @pl.when(j == 0)
            def _():
                o_ref[:, cols] = part

            @pl.when(j > 0)
            def _():
                o_ref[:, cols] += part

    cw, nc = _ffn_cols(C)
    tile = pl.BlockSpec((tm, cw), lambda i, j: (i, j))
    return pl.pallas_call(
        body, name=name, out_shape=jax.ShapeDtypeStruct((S, 2 * R), _F32), grid=(S // tm, 2 * nc),
        in_specs=[tile, tile, pl.BlockSpec((1, 2, R, cw), lambda i, j: (lax.div(j, nc), 0, 0, lax.rem(j, nc))),
                  pl.BlockSpec((1, 2, R, cw), lambda i, j: (2 + lax.div(j, nc), 0, 0, lax.rem(j, nc)))],
        out_specs=pl.BlockSpec((tm, 2 * R), lambda i, j: (i, 0)),
        compiler_params=_params(("parallel", "arbitrary")))(dgate, dup, g13, g13)


def _ffn_up_dw(hact, dgate, dup, name, rider=None):
    S, R2 = hact.shape
    R = R2 // 2
    C = dgate.shape[1] // 2
    tk = _pick(S, (512, 256, 128))
    nk = S // tk

    def body(h_ref, dg_ref, du_ref, o_ref):
        chip = pl.program_id(0)
        k = pl.program_id(2)

        def accumulate(d_ref):
            part = _dg(h_ref[...], d_ref[...], 0, 0)

            @pl.when(k == 0)
            def _():
                o_ref[0, 0] = part

            @pl.when(k > 0)
            def _():
                o_ref[0, 0] += part

        @pl.when(chip < 2)
        def _():
            accumulate(dg_ref)

        @pl.when(chip >= 2)
        def _():
            accumulate(du_ref)

    r = _call(
        body, name, (4, 2, nk),
        [pl.BlockSpec((tk, R), lambda c, h, k: (k, h)),
         pl.BlockSpec((tk, C), lambda c, h, k: (jnp.where(c < 2, k, 0), jnp.minimum(c, 1))),
         pl.BlockSpec((tk, C), lambda c, h, k: (jnp.where(c >= 2, k, 0), jnp.maximum(c - 2, 0)))],
        [pl.BlockSpec((1, 1, R, C), lambda c, h, k: (c, h, 0, 0))], [jax.ShapeDtypeStruct((4, 2, R, C), _F32)], [],
        ("parallel", "parallel", "arbitrary"), (hact, dgate, dup), rider)
    return r[0] if rider is None else (r[0][0], r[1])


def _rmsnorm_fwd(x, g, name):
    S, D = x.shape
    ts = _pick(S, (256,))

    def body(x_ref, g_ref, o_ref, r_ref):
        xv = x_ref[...]
        r = lax.rsqrt(jnp.mean(xv * xv, axis=-1, keepdims=True) + RMS_EPS)
        o_ref[...] = (xv * r * g_ref[...]).astype(o_ref.dtype)
        r_ref[...] = r

    return pl.pallas_call(
        body, name=name,
        out_shape=(jax.ShapeDtypeStruct((S, D), _BF), jax.ShapeDtypeStruct((S, 1), _F32)),
        grid=(S // ts,),
        in_specs=[pl.BlockSpec((ts, D), lambda i: (i, 0)), pl.BlockSpec((1, D), lambda i: (0, 0))],
        out_specs=(pl.BlockSpec((ts, D), lambda i: (i, 0)), pl.BlockSpec((ts, 1), lambda i: (i, 0))),
        compiler_params=_params(("parallel",)))(x, g)


def _rmsnorm_bwd(x, r, g, dy, dres, name):
    S, D = x.shape
    ts = _pick(S, (256,))
    has_res = dres is not None

    def body(*refs):
        if has_res:
            x_ref, r_ref, g_ref, dy_ref, dr_ref, dx_ref, dxb_ref, dg_ref = refs
        else:
            x_ref, r_ref, g_ref, dy_ref, dx_ref, dxb_ref, dg_ref = refs
        rv = r_ref[...]
        n = x_ref[...] * rv
        dyv = dy_ref[...]
        dn = dyv * g_ref[...]
        c = jnp.mean(dn * n, axis=-1, keepdims=True)
        dx = rv * (dn - n * c)
        if has_res:
            dx = dx + dr_ref[...]
        dx_ref[...] = dx
        dxb_ref[...] = dx.astype(dxb_ref.dtype)

        @pl.when(pl.program_id(0) == 0)
        def _():
            dg_ref[...] = jnp.zeros_like(dg_ref)

        dg_ref[...] += jnp.sum(dyv * n, axis=0, keepdims=True)

    row = pl.BlockSpec((ts, D), lambda i: (i, 0))
    vec = pl.BlockSpec((1, D), lambda i: (0, 0))
    in_specs = [row, pl.BlockSpec((ts, 1), lambda i: (i, 0)), vec, row] + ([row] if has_res else [])
    args = (x, r, g, dy) + ((dres,) if has_res else ())
    return pl.pallas_call(
        body, name=name,
        out_shape=(jax.ShapeDtypeStruct((S, D), _F32), jax.ShapeDtypeStruct((S, D), _BF),
                   jax.ShapeDtypeStruct((1, D), _F32)),
        grid=(S // ts,), in_specs=in_specs, out_specs=(row, row, vec),
        compiler_params=_params(("arbitrary",)))(*args)


def _headnorm_fwd(t, cb, W, hd, g, name):
    S = t.shape[0]
    ts = _pick(S, (512, 256))
    bd = _block_diag(W, hd)

    def body(x_ref, g_ref, bd_ref, o_ref):
        xv = x_ref[...].astype(_F32)
        ms = _dg_hl(xv * xv, bd_ref[...], 1, 0) * (1.0 / hd)
        o_ref[...] = (xv * lax.rsqrt(ms + RMS_EPS) * g_ref[...]).astype(o_ref.dtype)

    return pl.pallas_call(
        body, name=name, out_shape=jax.ShapeDtypeStruct((S, W), _BF), grid=(S // ts,),
        in_specs=[pl.BlockSpec((ts, W), lambda i: (i, cb)), pl.BlockSpec((1, W), lambda i: (0, 0)),
                  pl.BlockSpec((W, W), lambda i: (0, 0))],
        out_specs=pl.BlockSpec((ts, W), lambda i: (i, 0)),
        compiler_params=_params(("parallel",)))(t, g, bd)


def _headnorm_bwd(t, cb, W, hd, g, dy, dcb, name, out_dtype=_F32):
    S = t.shape[0]
    ts = _pick(S, (512, 256))
    bd = _block_diag(W, hd)

    def body(x_ref, g_ref, bd_ref, dy_ref, dx_ref, dg_ref):
        xv = x_ref[...].astype(_F32)
        bdv = bd_ref[...]
        ms = _dg_hl(xv * xv, bdv, 1, 0) * (1.0 / hd)
        rv = lax.rsqrt(ms + RMS_EPS)
        n = xv * rv
        dyv = dy_ref[...].astype(_F32)
        dn = dyv * g_ref[...]
        c = _dg_hl(dn * n, bdv, 1, 0) * (1.0 / hd)
        dx_ref[...] = (rv * (dn - n * c)).astype(dx_ref.dtype)

        @pl.when(pl.program_id(0) == 0)
        def _():
            dg_ref[...] = jnp.zeros_like(dg_ref)

        dg_ref[...] += jnp.sum(dyv * n, axis=0, keepdims=True)

    return pl.pallas_call(
        body, name=name,
        out_shape=(jax.ShapeDtypeStruct((S, W), out_dtype), jax.ShapeDtypeStruct((1, W), _F32)),
        grid=(S // ts,),
        in_specs=[pl.BlockSpec((ts, W), lambda i: (i, cb)), pl.BlockSpec((1, W), lambda i: (0, 0)),
                  pl.BlockSpec((W, W), lambda i: (0, 0)), pl.BlockSpec((ts, W), lambda i: (i, dcb))],
        out_specs=(pl.BlockSpec((ts, W), lambda i: (i, 0)), pl.BlockSpec((1, W), lambda i: (0, 0))),
        compiler_params=_params(("arbitrary",)))(t, g, bd, dy)


def _gla_post_fwd(o_f, o_b, proj, r_cb, g, name):
    S, W = o_f.shape
    hd = W // GLA_HEADS
    ts = _pick(S, (256,))
    bd = _block_diag(W, hd)

    def body(of_ref, ob_ref, r_ref, g_ref, bd_ref, y_ref):
        o = of_ref[...] + ob_ref[...]
        ms = _dg_hl(o * o, bd_ref[...], 1, 0) * (1.0 / hd)
        u = o * lax.rsqrt(ms + RMS_EPS) * g_ref[...]
        rr = r_ref[...]
        y_ref[...] = (u * (rr * _sigmoid(rr))).astype(y_ref.dtype)

    row = pl.BlockSpec((ts, W), lambda i: (i, 0))
    return pl.pallas_call(
        body, name=name, out_shape=jax.ShapeDtypeStruct((S, W), _BF), grid=(S // ts,),
        in_specs=[row, row, pl.BlockSpec((ts, W), lambda i: (i, r_cb)), pl.BlockSpec((1, W), lambda i: (0, 0)),
                  pl.BlockSpec((W, W), lambda i: (0, 0))],
        out_specs=row, compiler_params=_params(("parallel",)))(o_f, o_b, proj, g, bd)


def _gla_post_bwd(o_f, o_b, proj, r_cb, g, dy, name):
    S, W = o_f.shape
    hd = W // GLA_HEADS
    ts = _pick(S, (256,))
    bd = _block_diag(W, hd)

    def body(of_ref, ob_ref, r_ref, g_ref, bd_ref, dy_ref, do_ref, dr_ref, dg_ref):
        o = of_ref[...] + ob_ref[...]
        bdv = bd_ref[...]
        ms = _dg_hl(o * o, bdv, 1, 0) * (1.0 / hd)
        rv = lax.rsqrt(ms + RMS_EPS)
        n = o * rv
        gv = g_ref[...]
        rr = r_ref[...]
        sg = _sigmoid(rr)
        dyv = dy_ref[...]
        dr_ref[...] = (dyv * (n * gv) * (sg * (1.0 + rr * (1.0 - sg)))).astype(dr_ref.dtype)
        du = dyv * (rr * sg)
        dn = du * gv
        c = _dg_hl(dn * n, bdv, 1, 0) * (1.0 / hd)
        do_ref[...] = rv * (dn - n * c)

        @pl.when(pl.program_id(0) == 0)
        def _():
            dg_ref[...] = jnp.zeros_like(dg_ref)

        dg_ref[...] += jnp.sum(du * n, axis=0, keepdims=True)

    row = pl.BlockSpec((ts, W), lambda i: (i, 0))
    vec = pl.BlockSpec((1, W), lambda i: (0, 0))
    return pl.pallas_call(
        body, name=name,
        out_shape=(jax.ShapeDtypeStruct((S, W), _F32), jax.ShapeDtypeStruct((S, W), _BF),
                   jax.ShapeDtypeStruct((1, W), _F32)),
        grid=(S // ts,),
        in_specs=[row, row, pl.BlockSpec((ts, W), lambda i: (i, r_cb)), vec, pl.BlockSpec((W, W), lambda i: (0, 0)),
                  row],
        out_specs=(row, row, vec), compiler_params=_params(("arbitrary",)))(o_f, o_b, proj, g, bd, dy)


def _loss_head(y, tgt, name):
    S, D = y.shape
    ts = _pick(S, (256,))

    def body(y_ref, t_ref, l_ref, d_ref, db_ref):
        err = y_ref[...] - t_ref[...]
        d = err * (1.0 / D)
        d_ref[...] = d
        db_ref[...] = d.astype(db_ref.dtype)

        @pl.when(pl.program_id(0) == 0)
        def _():
            l_ref[...] = jnp.zeros_like(l_ref)

        l_ref[...] += 0.5 * jnp.sum(jnp.mean(err * err, axis=-1, keepdims=True))

    row = pl.BlockSpec((ts, D), lambda i: (i, 0))
    return pl.pallas_call(
        body, name=name,
        out_shape=(jax.ShapeDtypeStruct((8, _LANES), _F32), jax.ShapeDtypeStruct((S, D), _F32),
                   jax.ShapeDtypeStruct((S, D), _BF)),
        grid=(S // ts,), in_specs=[row, row],
        out_specs=(pl.BlockSpec((8, _LANES), lambda i: (0, 0)), row, row),
        compiler_params=_params(("arbitrary",)))(y, tgt)


def _adamw(w, g, m, v, name):
    shape = w.shape
    if w.ndim == 3 and shape[1] * shape[2] > 256 * 1024:
        rb = _row_block(shape[1], shape[2] * 4, 1536 * 1024)
        grid = (shape[0], shape[1] // rb)
        spec = pl.BlockSpec((1, rb, shape[2]), lambda l, i: (l, i, 0))
        sem = ("parallel", "parallel")
    else:
        grid = ()
        spec = pl.BlockSpec(memory_space=pltpu.VMEM)
        sem = None

    def body(w_ref, g_ref, m_ref, v_ref, d_ref, nm_ref, nv_ref):
        gv = g_ref[...]
        mn = ADAM_B1 * m_ref[...] + (1.0 - ADAM_B1) * gv
        vn = ADAM_B2 * v_ref[...] + (1.0 - ADAM_B2) * (gv * gv)
        m_hat = mn / (1.0 - ADAM_B1 ** ADAM_STEP)
        v_hat = vn / (1.0 - ADAM_B2 ** ADAM_STEP)
        d_ref[...] = -ADAM_LR * (m_hat / (jnp.sqrt(v_hat) + ADAM_EPS) + ADAM_WD * w_ref[...])
        nm_ref[...] = mn
        nv_ref[...] = vn

    out = jax.ShapeDtypeStruct(shape, _F32)
    return pl.pallas_call(
        body, name=name, out_shape=(out, out, out), grid=grid, in_specs=[spec] * 4, out_specs=(spec,) * 3,
        compiler_params=_params(sem))(w, g, m, v)


def _add_pair(t, la, cidx, name):
    _, _, r, c = t.shape
    rb = _pick(r, (256, 352, 176, 88, 8))

    def body(c_ref, t_ref, l_ref, o_ref):
        o_ref[...] = (t_ref[...] + l_ref[...]).astype(o_ref.dtype)

    return pl.pallas_call(
        body, name=name, out_shape=jax.ShapeDtypeStruct((4, r, c), _BF),
        grid_spec=pltpu.PrefetchScalarGridSpec(
            num_scalar_prefetch=1, grid=(4, r // rb),
            in_specs=[pl.BlockSpec((1, None, rb, c), lambda j, i, cr: (j, cr[0], i, 0)),
                      pl.BlockSpec((1, rb, c), lambda j, i, cr: (j, i, 0))],
            out_specs=pl.BlockSpec((1, rb, c), lambda j, i, cr: (j, i, 0))),
        compiler_params=_params(("parallel", "parallel")))(cidx, t, la)


def _add_four(p, land, chip, name):
    _, r, c = p.shape
    rb = _pick(r, (256, 352, 176, 88, 16))

    def body(c_ref, p_ref, l1_ref, l2_ref, l3_ref, o_ref):
        o_ref[...] = ((p_ref[...].astype(_F32) + l1_ref[...].astype(_F32)) + l2_ref[...].astype(_F32)
                      ) + l3_ref[...].astype(_F32)

    def slot(flip):
        return pl.BlockSpec((None, rb, c), lambda i, cr: (jnp.bitwise_xor(cr[0], flip), i, 0))

    return pl.pallas_call(
        body, name=name, out_shape=jax.ShapeDtypeStruct((r, c), _F32),
        grid_spec=pltpu.PrefetchScalarGridSpec(
            num_scalar_prefetch=1, grid=(r // rb,),
            in_specs=[slot(0), slot(1), slot(2), slot(3)],
            out_specs=pl.BlockSpec((rb, c), lambda i, cr: (i, 0))),
        compiler_params=_params(("parallel",)))(chip, p, land, land, land)


def _gla_masks(tb, rev):
    i = np.arange(tb)
    same = (i[:, None] // GLA_CHUNK) == (i[None, :] // GLA_CHUNK)
    tri = same & (i[None, :] <= i[:, None])
    keep = (same & ~tri) if rev else tri
    return tuple(jnp.asarray(m.astype(np.float32), dtype=_BF) for m in (tri, same, keep))


def _gla_block(q, k, gfb, wg, bg, tri, same, keepm, rev, scale):
    z = _dg(gfb, wg, 1, 0) + bg
    la = _log_sigmoid(z) * (1.0 / GLA_TAU)
    cum = _dg_lh(tri, la, 0 if rev else 1, 0)
    tot = _dg_lh(same, la, 1, 0)
    e_a = jnp.exp(cum)
    e_na = jnp.exp(-cum)
    e_la = jnp.exp(tot - cum)
    qe = q * scale * e_a
    ke = k * e_na
    kend = k * e_la
    keep = keepm > 0
    p = jnp.where(keep, _dg(qe, ke, 1, 1), 0.0)
    return dict(z=z, tot=tot, e_a=e_a, e_na=e_na, e_la=e_la, qe=qe, ke=ke, kend=kend, keep=keep, p=p)


def _gla_fwd(proj, gfb, wg, bg, rev, name):
    S = proj.shape[0]
    H = GLA_HEADS
    dk = wg.shape[1] // (2 * H)
    dv = 2 * dk
    C = GLA_CHUNK
    cb_n = _pick(S // C, (8, 4, 2, 1))
    tb = cb_n * C
    nb = S // tb
    scale = float(dk) ** -0.5
    masks = _gla_masks(tb, rev)
    wcol = H if rev else 0

    def bmap(b):
        return nb - 1 - b if rev else b

    def body(q_ref, k_ref, v_ref, g_ref, wg_ref, bg_ref, tri_ref, same_ref, keep_ref, o_ref, st_ref, state):
        h = pl.program_id(1)

        @pl.when(pl.program_id(0) == 0)
        def _():
            state[h] = jnp.zeros((dv, dk), _F32)

        t = _gla_block(q_ref[...], k_ref[...], g_ref[...], wg_ref[...], bg_ref[...], tri_ref[...], same_ref[...],
                       keep_ref[...], rev, scale)
        vv = v_ref[...]
        o_ref[...] = _dg(t['p'], vv, 1, 0)
        order = range(cb_n - 1, -1, -1) if rev else range(cb_n)
        for ci in order:
            sl = slice(ci * C, (ci + 1) * C)
            st = state[h]
            o_ref[sl, :] += _dg(t['qe'][sl], st, 1, 1)
            st_ref[0, ci] = st
            state[h] = st * jnp.exp(t['tot'][ci * C:ci * C + 1]) + _dg(vv[sl], t['kend'][sl], 0, 0)

    nq = (H * dk) // dk
    msk = pl.BlockSpec((tb, tb), lambda b, h: (0, 0))
    return pl.pallas_call(
        body, name=name,
        out_shape=(jax.ShapeDtypeStruct((S, H * dv), _F32), jax.ShapeDtypeStruct((H, S // C, dv, dk), _F32)),
        grid=(nb, H),
        in_specs=[pl.BlockSpec((tb, dk), lambda b, h: (bmap(b), h)),
                  pl.BlockSpec((tb, dk), lambda b, h: (bmap(b), nq + h)),
                  pl.BlockSpec((tb, dv), lambda b, h: (bmap(b), (2 * H * dk) // dv + h)),
                  pl.BlockSpec((tb, _LANES), lambda b, h: (bmap(b), 0)),
                  pl.BlockSpec((_LANES, dk), lambda b, h: (0, wcol + h)),
                  pl.BlockSpec((1, dk), lambda b, h: (0, wcol + h)),
                  msk, msk, msk],
        out_specs=(pl.BlockSpec((tb, dv), lambda b, h: (bmap(b), h)),
                   pl.BlockSpec((1, cb_n, dv, dk), lambda b, h: (h, bmap(b), 0, 0))),
        scratch_shapes=[pltpu.VMEM((H, dv, dk), _F32)],
        compiler_params=_params(("arbitrary", "arbitrary")))(proj, proj, proj, gfb, wg, bg, *masks)


def _gla_bwd(proj, gfb, wg, bg, st, do, rev, prev, name):
    S = proj.shape[0]
    H = GLA_HEADS
    dk = wg.shape[1] // (2 * H)
    dv = 2 * dk
    C = GLA_CHUNK
    cb_n = _pick(S // C, (8, 4, 2, 1))
    tb = cb_n * C
    nb = S // tb
    scale = float(dk) ** -0.5
    masks = _gla_masks(tb, rev)
    wcol = H if rev else 0
    has_prev = prev is not None

    def bmap(b):
        return b if rev else nb - 1 - b

    def body(*refs):
        if has_prev:
            (q_ref, k_ref, v_ref, g_ref, wg_ref, bg_ref, tri_ref, same_ref, keep_ref, st_ref, do_ref,
             pq_ref, pk_ref, pv_ref, dq_ref, dk_ref, dv_ref, dz_ref, dstate) = refs
        else:
            (q_ref, k_ref, v_ref, g_ref, wg_ref, bg_ref, tri_ref, same_ref, keep_ref, st_ref, do_ref,
             dq_ref, dk_ref, dv_ref, dz_ref, dstate) = refs
        h = pl.program_id(1)

        @pl.when(pl.program_id(0) == 0)
        def _():
            dstate[h] = jnp.zeros((dv, dk), _F32)

        t = _gla_block(q_ref[...], k_ref[...], g_ref[...], wg_ref[...], bg_ref[...], tri_ref[...], same_ref[...],
                       keep_ref[...], rev, scale)
        vv = v_ref[...]
        dov = do_ref[...]
        order = range(cb_n) if rev else range(cb_n - 1, -1, -1)
        for ci in order:
            sl = slice(ci * C, (ci + 1) * C)
            stp = st_ref[0, ci]
            dst = dstate[h]
            e_l = jnp.exp(t['tot'][ci * C:ci * C + 1])
            kend_c = t['kend'][sl]
            dkend_c = _dg(vv[sl], dst, 1, 0)
            dq_ref[sl, :] = _dg(dov[sl], stp, 1, 0)
            dk_ref[sl, :] = dkend_c
            dv_ref[sl, :] = _dg(kend_c, dst, 1, 1)
            dtot = (e_l * jnp.sum(dst * stp, axis=0, keepdims=True)
                    + jnp.sum(dkend_c * kend_c, axis=0, keepdims=True))
            dz_ref[sl, :] = jnp.broadcast_to(dtot, (C, dk))
            dstate[h] = dst * e_l + _dg(dov[sl], t['qe'][sl], 0, 0)
        dp = jnp.where(t['keep'], _dg(dov, vv, 1, 1), 0.0)
        dqe = _dg(dp, t['ke'], 1, 0) + dq_ref[...]
        dke = _dg(dp, t['qe'], 0, 0)
        dkend = dk_ref[...]
        dvv = _dg(t['p'], dov, 0, 0) + dv_ref[...]
        dqv = dqe * t['e_a'] * scale
        dkv = dke * t['e_na'] + dkend * t['e_la']
        d_a = dqe * t['qe'] - dke * t['ke'] - dkend * t['kend']
        dla = _dg_lh(tri_ref[...], d_a, 1 if rev else 0, 0) + dz_ref[...]
        dz_ref[...] = dla * (1.0 / GLA_TAU) * _sigmoid(-t['z'])
        if has_prev:
            dqv = dqv + pq_ref[...]
            dkv = dkv + pk_ref[...]
            dvv = dvv + pv_ref[...]
        dq_ref[...] = dqv
        dk_ref[...] = dkv
        dv_ref[...] = dvv

    nq = (H * dk) // dk
    msk = pl.BlockSpec((tb, tb), lambda b, h: (0, 0))
    kblk = pl.BlockSpec((tb, dk), lambda b, h: (bmap(b), h))
    vblk = pl.BlockSpec((tb, dv), lambda b, h: (bmap(b), h))
    in_specs = [kblk,
                pl.BlockSpec((tb, dk), lambda b, h: (bmap(b), nq + h)),
                pl.BlockSpec((tb, dv), lambda b, h: (bmap(b), (2 * H * dk) // dv + h)),
                pl.BlockSpec((tb, _LANES), lambda b, h: (bmap(b), 0)),
                pl.BlockSpec((_LANES, dk), lambda b, h: (0, wcol + h)),
                pl.BlockSpec((1, dk), lambda b, h: (0, wcol + h)),
                msk, msk, msk,
                pl.BlockSpec((1, cb_n, dv, dk), lambda b, h: (h, bmap(b), 0, 0)),
                vblk]
    args = [proj, proj, proj, gfb, wg, bg, *masks, st, do]
    if has_prev:
        in_specs += [kblk, kblk, vblk]
        args += list(prev)
    return pl.pallas_call(
        body, name=name,
        out_shape=(jax.ShapeDtypeStruct((S, H * dk), _F32), jax.ShapeDtypeStruct((S, H * dk), _F32),
                   jax.ShapeDtypeStruct((S, H * dv), _F32), jax.ShapeDtypeStruct((S, H * dk), _F32)),
        grid=(nb, H), in_specs=in_specs, out_specs=(kblk, kblk, vblk, kblk),
        scratch_shapes=[pltpu.VMEM((H, dv, dk), _F32)],
        compiler_params=_params(("arbitrary", "arbitrary")))(*args)


NA_GROUP = 4
NA_WIN_ROWS = NA_GROUP + NA_ROWS


def _na_geometry(S):
    rows = S // GRID_W
    assert rows % NA_GROUP == 0 and rows >= NA_WIN_ROWS + NA_GROUP
    return rows, rows // NA_GROUP


def _na_win_start(g, rows):
    return jnp.clip(NA_GROUP * g - NA_ROWS // 2, 0, rows - NA_WIN_ROWS)


def _na_class(g, groups):
    return jnp.where(g == 0, 0, jnp.where(g == groups - 1, 2, 1))


def _na_onehots(rows):
    groups = rows // NA_GROUP
    by_row = np.zeros((3, NA_GROUP, NA_WIN_ROWS, 2 * NA_ROWS - 1), np.float32)
    for cls, g in enumerate((0, 1, groups - 1)):
        ws = int(np.clip(NA_GROUP * g - NA_ROWS // 2, 0, rows - NA_WIN_ROWS))
        for qr in range(NA_GROUP):
            r = NA_GROUP * g + qr
            rs = int(np.clip(r - NA_ROWS // 2, 0, rows - NA_ROWS))
            for kr in range(NA_WIN_ROWS):
                if rs <= ws + kr < rs + NA_ROWS:
                    by_row[cls, qr, kr, ws + kr - r + NA_ROWS - 1] = 1.0
    c = np.arange(GRID_W)
    cs = np.clip(c - NA_COLS // 2, 0, GRID_W - NA_COLS)
    kc = np.arange(GRID_W)
    win = (kc[None, :] >= cs[:, None]) & (kc[None, :] < cs[:, None] + NA_COLS)
    idx = np.clip(kc[None, :] - c[:, None], -(NA_COLS - 1), NA_COLS - 1) + (NA_COLS - 1)
    by_col = ((idx[:, :, None] == np.arange(2 * NA_COLS - 1)[None, None, :]) & win[:, :, None]).astype(np.float32)
    return by_row, by_col


def _na_bias_tables(rpb, rows):
    by_row, by_col = _na_onehots(rows)
    H = rpb.shape[0]
    hp = lax.Precision.HIGHEST
    e1 = jnp.einsum('hij,ckj->hick', rpb, by_col, precision=hp)
    t = jnp.einsum('hick,zqri->zhqcrk', e1, by_row, precision=hp)
    valid = (by_row.sum(-1) > 0)[:, None, :, None, :, None] & (by_col.sum(-1) > 0)[None, None, None, :, None, :]
    t = jnp.where(valid, t, -jnp.inf)
    return t.reshape(3, H, NA_GROUP * GRID_W, NA_WIN_ROWS * GRID_W)


def _na_bias_grad(dtab, rows):
    by_row, by_col = _na_onehots(rows)
    H = dtab.shape[1]
    hp = lax.Precision.HIGHEST
    d6 = dtab.reshape(3, H, NA_GROUP, GRID_W, NA_WIN_ROWS, GRID_W)
    de1 = jnp.einsum('zhqcrk,zqri->hick', d6, by_row, precision=hp)
    return jnp.einsum('hick,ckj->hij', de1, by_col, precision=hp)


def _na_fwd(qn, kn, vb, tab, name):
    S, W = qn.shape
    rows, groups = _na_geometry(S)
    npair = W // _LANES
    nq = NA_GROUP * GRID_W
    nk = NA_WIN_ROWS * GRID_W
    sc = float(NA_HD) ** -0.5

    def body(q_ref, k_ref, v_ref, b_ref, o_ref):
        g = pl.program_id(1)
        k0 = pl.multiple_of(_na_win_start(g, rows) * GRID_W, GRID_W)
        kw = k_ref[pl.ds(k0, nk), :]
        vw = v_ref[pl.ds(k0, nk), :]
        qv = q_ref[...]
        lane = lax.broadcasted_iota(jnp.int32, (nq, _LANES), 1)
        outs = []
        for hh in range(2):
            mine = (lane >= hh * NA_HD) & (lane < (hh + 1) * NA_HD)
            s = _dg(jnp.where(mine, qv, jnp.zeros_like(qv)), kw, 1, 1) * sc + b_ref[0, hh]
            e = jnp.exp(s - jnp.max(s, axis=-1, keepdims=True))
            p = e / jnp.sum(e, axis=-1, keepdims=True)
            outs.append(_dg(p, vw, 1, 0))
        o_ref[...] = jnp.where(lane < NA_HD, outs[0], outs[1])

    return pl.pallas_call(
        body, name=name, out_shape=jax.ShapeDtypeStruct((S, W), _F32), grid=(npair, groups),
        in_specs=[pl.BlockSpec((nq, _LANES), lambda p, g: (g, p)),
                  pl.BlockSpec((S, _LANES), lambda p, g: (0, p)),
                  pl.BlockSpec((S, _LANES), lambda p, g: (0, p)),
                  pl.BlockSpec((1, 2, nq, nk), lambda p, g: (_na_class(g, groups), p, 0, 0))],
        out_specs=pl.BlockSpec((nq, _LANES), lambda p, g: (g, p)),
        compiler_params=_params(("parallel", "arbitrary")))(qn, kn, vb, tab)


def _na_bwd(qn, kn, vb, tab, do, name):
    S, W = qn.shape
    rows, groups = _na_geometry(S)
    npair = W // _LANES
    nq = NA_GROUP * GRID_W
    nk = NA_WIN_ROWS * GRID_W
    sc = float(NA_HD) ** -0.5

    def body(q_ref, k_ref, v_ref, b_ref, do_ref, dq_ref, dk_ref, dv_ref, db_ref):
        g = pl.program_id(1)

        @pl.when(g == 0)
        def _():
            dk_ref[...] = jnp.zeros_like(dk_ref)
            dv_ref[...] = jnp.zeros_like(dv_ref)

        @pl.when((g <= 1) | (g == groups - 1))
        def _():
            db_ref[...] = jnp.zeros_like(db_ref)

        k0 = pl.multiple_of(_na_win_start(g, rows) * GRID_W, GRID_W)
        kw = k_ref[pl.ds(k0, nk), :]
        vw = v_ref[pl.ds(k0, nk), :]
        qv = q_ref[...]
        dov = do_ref[...]
        lane = lax.broadcasted_iota(jnp.int32, (nq, _LANES), 1)
        dqs = []
        dkw = jnp.zeros((nk, _LANES), _F32)
        dvw = jnp.zeros((nk, _LANES), _F32)
        for hh in range(2):
            mine = (lane >= hh * NA_HD) & (lane < (hh + 1) * NA_HD)
            qm = jnp.where(mine, qv, jnp.zeros_like(qv))
            dom = jnp.where(mine, dov, 0.0)
            s = _dg(qm, kw, 1, 1) * sc + b_ref[0, hh]
            e = jnp.exp(s - jnp.max(s, axis=-1, keepdims=True))
            p = e / jnp.sum(e, axis=-1, keepdims=True)
            dp = _dg(dom, vw, 1, 1)
            ds = p * (dp - jnp.sum(p * dp, axis=-1, keepdims=True))
            db_ref[0, hh] += ds
            dqs.append(_dg(ds, kw, 1, 0) * sc)
            dkw = dkw + _dg(ds, qm, 0, 0) * sc
            dvw = dvw + _dg(p, dom, 0, 0)
        dq_ref[...] = jnp.where(lane < NA_HD, dqs[0], dqs[1])
        dk_ref[pl.ds(k0, nk), :] += dkw
        dv_ref[pl.ds(k0, nk), :] += dvw

    blk = pl.BlockSpec((nq, _LANES), lambda p, g: (g, p))
    full = pl.BlockSpec((S, _LANES), lambda p, g: (0, p))
    tspec = pl.BlockSpec((1, 2, nq, nk), lambda p, g: (_na_class(g, groups), p, 0, 0))
    return pl.pallas_call(
        body, name=name,
        out_shape=(jax.ShapeDtypeStruct((S, W), _F32), jax.ShapeDtypeStruct((S, W), _F32),
                   jax.ShapeDtypeStruct((S, W), _F32), jax.ShapeDtypeStruct(tab.shape, _F32)),
        grid=(npair, groups), in_specs=[blk, full, full, tspec, blk],
        out_specs=(blk, full, full, tspec),
        compiler_params=_params(("arbitrary", "arbitrary")))(qn, kn, vb, tab, do)


def _mem_fwd(qn, km, vm, name):
    S, W = qn.shape
    hd = W // MEM_HEADS
    tq = _pick(S, (512, 256))
    sc = float(hd) ** -0.5

    def body(q_ref, k_ref, v_ref, o_ref):
        for h in range(MEM_HEADS):
            cs = slice(h * hd, (h + 1) * hd)
            s = _dg(q_ref[:, cs], k_ref[:, cs], 1, 1) * sc
            e = jnp.exp(s - jnp.max(s, axis=-1, keepdims=True))
            p = e / jnp.sum(e, axis=-1, keepdims=True)
            o_ref[:, cs] = _dg(p, v_ref[:, cs], 1, 0)

    full = pl.BlockSpec(km.shape, lambda i: (0, 0))
    return pl.pallas_call(
        body, name=name, out_shape=jax.ShapeDtypeStruct((S, W), _F32), grid=(S // tq,),
        in_specs=[pl.BlockSpec((tq, W), lambda i: (i, 0)), full, full],
        out_specs=pl.BlockSpec((tq, W), lambda i: (i, 0)),
        compiler_params=_params(("parallel",)))(qn, km, vm)


def _mem_bwd(qn, km, vm, do, name):
    S, W = qn.shape
    hd = W // MEM_HEADS
    tq = _pick(S, (512, 256))
    sc = float(hd) ** -0.5

    def body(q_ref, k_ref, v_ref, do_ref, dq_ref, dk_ref, dv_ref):
        @pl.when(pl.program_id(0) == 0)
        def _():
            dk_ref[...] = jnp.zeros_like(dk_ref)
            dv_ref[...] = jnp.zeros_like(dv_ref)

        for h in range(MEM_HEADS):
            cs = slice(h * hd, (h + 1) * hd)
            qh = q_ref[:, cs]
            kh = k_ref[:, cs]
            doh = do_ref[:, cs]
            s = _dg(qh, kh, 1, 1) * sc
            e = jnp.exp(s - jnp.max(s, axis=-1, keepdims=True))
            p = e / jnp.sum(e, axis=-1, keepdims=True)
            dp = _dg(doh, v_ref[:, cs], 1, 1)
            ds = p * (dp - jnp.sum(p * dp, axis=-1, keepdims=True))
            dq_ref[:, cs] = _dg(ds, kh, 1, 0) * sc
            dk_ref[:, cs] += _dg(ds, qh, 0, 0) * sc
            dv_ref[:, cs] += _dg(p, doh, 0, 0)

    full = pl.BlockSpec(km.shape, lambda i: (0, 0))
    row = pl.BlockSpec((tq, W), lambda i: (i, 0))
    return pl.pallas_call(
        body, name=name,
        out_shape=(jax.ShapeDtypeStruct((S, W), _F32), jax.ShapeDtypeStruct(km.shape, _F32),
                   jax.ShapeDtypeStruct(km.shape, _F32)),
        grid=(S // tq,), in_specs=[row, full, full, row], out_specs=(row, full, full),
        compiler_params=_params(("arbitrary",)))(qn, km, vm, do)


def _place():
    x, y, c = lax.axis_index("x"), lax.axis_index("y"), lax.axis_index("c")
    chips = [(1 - x, y), (x, 1 - y), (1 - x, 1 - y)]
    return x, y, c, chips


def _exchange(rider, name):
    ri = len(rider.ins)
    ro = len(rider.outs)

    def body(*refs):
        ins, outs, sems = refs[:ri], refs[ri:ri + ro], refs[ri + ro:]
        rider.start(ins, outs, sems)
        rider.finish(ins, outs, sems)

    return pl.pallas_call(
        body, name=name, out_shape=tuple(rider.outs), in_specs=[_ANY] * ri, out_specs=(_ANY,) * ro,
        scratch_shapes=rider.sems, input_output_aliases={i: o for i, o in rider.aliases},
        compiler_params=_params())(*rider.ins)


def _pair_sems(*shape):
    return [pltpu.SemaphoreType.DMA(shape), pltpu.SemaphoreType.DMA(shape)]


def _gather_ici(shards):
    n = len(shards)

    def copies(w, g, sems):
        send, recv = sems
        x, y, c, chips = _place()
        me = 2 * x + y
        out, back = [], []
        for t in range(n):
            for j, (cx, cy) in enumerate(chips):
                out.append(pltpu.make_async_remote_copy(
                    src_ref=w[t].at[c], dst_ref=g[t].at[me, c], send_sem=send.at[t, j], recv_sem=recv.at[t, j],
                    device_id=(cx, cy, c), device_id_type=_MESH))
                back.append(functools.partial(
                    pltpu.make_async_remote_copy,
                    src_ref=w[t].at[c], dst_ref=g[t].at[2 * cx + cy, c], send_sem=send.at[t, j],
                    recv_sem=recv.at[t, j], device_id=(cx, cy, c), device_id_type=_MESH))
        return out, back

    def start(w, g, sems):
        for cp in copies(w, g, sems)[0]:
            cp.start()

    def finish(w, g, sems):
        out, back = copies(w, g, sems)
        for make in back:
            make().wait_recv()
        for cp in out:
            cp.wait_send()

    return _Rider(shards, [jax.ShapeDtypeStruct((4,) + s.shape, s.dtype) for s in shards], _pair_sems(n, 3),
                  start, finish)


def _gather_d2d(gs):
    n = len(gs)

    def copies(g, sems):
        send, recv = sems
        x, y, c, chips = _place()
        out, back = [], []
        for t in range(n):
            for j, (cx, cy) in enumerate(chips):
                mine, theirs = g[t].at[2 * cx + cy, c], g[t].at[2 * cx + cy, 1 - c]
                out.append(pltpu.make_async_remote_copy(
                    src_ref=mine, dst_ref=mine, send_sem=send.at[t, j], recv_sem=recv.at[t, j],
                    device_id=(x, y, 1 - c), device_id_type=_MESH))
                back.append(functools.partial(
                    pltpu.make_async_remote_copy,
                    src_ref=mine, dst_ref=theirs, send_sem=send.at[t, j], recv_sem=recv.at[t, j],
                    device_id=(x, y, 1 - c), device_id_type=_MESH))
        return out, back

    def start(_, g, sems):
        for cp in copies(g, sems)[0]:
            cp.start()

    def finish(_, g, sems):
        out, back = copies(g, sems)
        for make in back:
            make().wait_recv()
        for cp in out:
            cp.wait_send()

    return _Rider(gs, [jax.ShapeDtypeStruct(g.shape, g.dtype) for g in gs], _pair_sems(n, 3), start, finish,
                  aliases=[(t, t) for t in range(n)])


def _swap_halves(ts):
    n = len(ts)

    def copies(t_in, land, sems):
        send, recv = sems
        x, y, c, _ = _place()
        return [pltpu.make_async_remote_copy(
            src_ref=t_in[t].at[:, 1 - c], dst_ref=land[t], send_sem=send.at[t], recv_sem=recv.at[t],
            device_id=(x, y, 1 - c), device_id_type=_MESH) for t in range(n)]

    def start(t_in, land, sems):
        for cp in copies(t_in, land, sems):
            cp.start()

    def finish(t_in, land, sems):
        for cp in copies(t_in, land, sems):
            cp.wait()

    return _Rider(ts, [jax.ShapeDtypeStruct((4,) + t.shape[2:], t.dtype) for t in ts], _pair_sems(n), start, finish)


def _scatter_chips(ps):
    n = len(ps)

    def copies(p, land, sems):
        send, recv = sems
        x, y, c, chips = _place()
        me = 2 * x + y
        out, back = [], []
        for t in range(n):
            for j, (cx, cy) in enumerate(chips):
                out.append(pltpu.make_async_remote_copy(
                    src_ref=p[t].at[2 * cx + cy], dst_ref=land[t].at[me], send_sem=send.at[t, j],
                    recv_sem=recv.at[t, j], device_id=(cx, cy, c), device_id_type=_MESH))
                back.append(functools.partial(
                    pltpu.make_async_remote_copy,
                    src_ref=p[t].at[me], dst_ref=land[t].at[2 * cx + cy], send_sem=send.at[t, j],
                    recv_sem=recv.at[t, j], device_id=(cx, cy, c), device_id_type=_MESH))
        return out, back

    def start(p, land, sems):
        for cp in copies(p, land, sems)[0]:
            cp.start()

    def finish(p, land, sems):
        out, back = copies(p, land, sems)
        for make in back:
            make().wait_recv()
        for cp in out:
            cp.wait_send()

    return _Rider(ps, [jax.ShapeDtypeStruct(t.shape, t.dtype) for t in ps], _pair_sems(n, 3), start, finish)


def _swap_reduced(rs):
    n = len(rs)

    def copies(r_in, out, sems):
        send, recv = sems
        x, y, c, _ = _place()
        return [pltpu.make_async_remote_copy(
            src_ref=r_in[t], dst_ref=out[t], send_sem=send.at[t], recv_sem=recv.at[t],
            device_id=(x, y, 1 - c), device_id_type=_MESH) for t in range(n)]

    def start(r_in, out, sems):
        for cp in copies(r_in, out, sems):
            cp.start()

    def finish(r_in, out, sems):
        for cp in copies(r_in, out, sems):
            cp.wait()

    return _Rider(rs, [jax.ShapeDtypeStruct(t.shape, t.dtype) for t in rs], _pair_sems(n), start, finish)


def _all_devices(v, reduce, name):
    rows = v.shape[0]

    def body(v_ref, o_ref, *rest):
        if reduce:
            all_ref, send, recv = rest
        else:
            send, recv = rest
            all_ref = o_ref
        x, y, c, _ = _place()
        me = 4 * x + 2 * y + c
        all_ref[me] = v_ref[...]
        cps = []
        for k in range(1, 8):
            fx, fy, fc = (k >> 2) & 1, (k >> 1) & 1, k & 1
            to = (x ^ fx, y ^ fy, c ^ fc)
            cps.append(pltpu.make_async_remote_copy(
                src_ref=v_ref, dst_ref=all_ref.at[me], send_sem=send.at[k - 1], recv_sem=recv.at[k - 1],
                device_id=to, device_id_type=_MESH))
        for cp in cps:
            cp.start()
        for k in range(1, 8):
            fx, fy, fc = (k >> 2) & 1, (k >> 1) & 1, k & 1
            frm = 4 * (x ^ fx) + 2 * (y ^ fy) + (c ^ fc)
            pltpu.make_async_remote_copy(
                src_ref=v_ref, dst_ref=all_ref.at[frm], send_sem=send.at[k - 1], recv_sem=recv.at[k - 1],
                device_id=(x, y, c), device_id_type=_MESH).wait_recv()
        for cp in cps:
            cp.wait_send()
        if reduce:
            acc = all_ref[0]
            for d in range(1, 8):
                acc = acc + all_ref[d]
            o_ref[...] = acc

    vm = pl.BlockSpec(memory_space=pltpu.VMEM)
    if reduce:
        out_shape = jax.ShapeDtypeStruct((rows, _LANES), _F32)
        scratch = [pltpu.VMEM((8, rows, _LANES), _F32)]
    else:
        out_shape = jax.ShapeDtypeStruct((8, rows, _LANES), _F32)
        scratch = []
    return pl.pallas_call(
        body, name=name, out_shape=out_shape, in_specs=[vm], out_specs=vm,
        scratch_shapes=scratch + [pltpu.SemaphoreType.DMA((7,)), pltpu.SemaphoreType.DMA((7,))],
        compiler_params=_params())(v)


def _weight_halves(shards):
    return [shards[k].reshape((2, shards[k].shape[0] // 2) + shards[k].shape[1:]) for k in BIG]


def _gather_layer_weights(halves, name):
    g1 = _exchange(_gather_ici(halves), name + "_ici")
    g2 = _exchange(_gather_d2d(g1), name + "_d2d")
    return _assemble_weights(g2, halves)


def _assemble_weights(got, halves):
    chip = 2 * lax.axis_index("x") + lax.axis_index("y")
    full = {}
    for k, g, own in zip(BIG, got, halves):
        g = lax.dynamic_update_slice(g, own[None], (chip, 0, 0, 0))
        _, _, r, c = g.shape
        if k == 'ffn_w13':
            full[k] = g
        elif k == 'w_in':
            full[k] = _shards_to_cols(g)
        else:
            full[k] = g.reshape(8 * r, c)
    return full


def _grads_as_shards(grads):
    ts = []
    for k in BIG:
        g = grads[k]
        if k == 'ffn_w13':
            ts.append(g)
        elif k == 'w_in':
            ts.append(_cols_to_shards(g))
        else:
            r, c = g.shape[0] // 8, g.shape[1]
            ts.append(g.reshape(4, 2, r, c))
    return ts


def _pair_sums(ts, landed, cidx):
    return [_add_pair(t, la, cidx, "reduce_add2_" + k) for k, t, la in zip(BIG, ts, landed)]


def _chip_sums(partial, slots):
    chip = (2 * lax.axis_index("x") + lax.axis_index("y")).astype(jnp.int32).reshape(1)
    return [_add_four(p, s, chip, "reduce_add4_" + k) for k, p, s in zip(BIG, partial, slots)]


def _assemble_shards(reduced, theirs, cidx):
    out = {}
    for k, mine, other in zip(BIG, reduced, theirs):
        both = jnp.stack([mine, other])
        both = jnp.where(cidx[0] == 0, both, both[::-1])
        out[k] = both.reshape((2 * mine.shape[0], mine.shape[1]))
    return out


def _reduce_layer_grads(ts, cidx, name):
    landed = _exchange(_swap_halves(ts), name + "_swap")
    partial = _pair_sums(ts, landed, cidx)
    slots = _exchange(_scatter_chips(partial), name + "_scatter")
    reduced = _chip_sums(partial, slots)
    theirs = _exchange(_swap_reduced(reduced), name + "_join")
    return _assemble_shards(reduced, theirs, cidx)


def _pack_rows(arrs):
    parts, spans = [], []
    off = 0
    for a in arrs:
        n = int(np.prod(a.shape))
        pad = (-n) % (8 * _LANES)
        parts.append(jnp.pad(a.reshape(-1), (0, pad)))
        spans.append((off, n, a.shape))
        off += n + pad
    return jnp.concatenate(parts).reshape(-1, _LANES), spans


def _unpack_rows(packed, spans):
    flat = packed.reshape(-1)
    return [flat[o:o + n].reshape(shape) for o, n, shape in spans]


def _in_split(d):
    dk, dv, w = d // 4, d // 2, d // 4
    names = [('g_q', dk), ('g_k', dk), ('g_v', dv), ('g_r', dv), ('g_f', GLA_GATE_RANK), ('g_b', GLA_GATE_RANK),
             ('n_q', w), ('n_k', w), ('n_v', w), ('m_q', w)]
    out, off = {}, 0
    for nme, wd in names:
        out[nme] = (off, wd)
        off += wd
    return out


def _permute_w_in(w):
    sp = _in_split(w.shape[0])
    f0 = sp['g_f'][0]
    n0 = sp['n_q'][0]
    main = jnp.concatenate([w[:, :f0], w[:, n0:]], axis=1)
    gate = jnp.pad(w[:, f0:n0], ((0, 0), (0, _LANES - (n0 - f0))))
    return main, gate


def _unpermute_w_in(main, gate, d):
    sp = _in_split(d)
    f0 = sp['g_f'][0]
    n0 = sp['n_q'][0]
    return jnp.concatenate([main[:, :f0], gate[:, :n0 - f0], main[:, f0:]], axis=1)


def _gate_weight(wg2_f, wg2_b):
    r, dk = wg2_f.shape
    top = jnp.concatenate([wg2_f, jnp.zeros_like(wg2_f)], axis=1)
    mid = jnp.concatenate([jnp.zeros_like(wg2_b), wg2_b], axis=1)
    return jnp.concatenate([top, mid, jnp.zeros((_LANES - 2 * r, 2 * dk), wg2_f.dtype)], axis=0)


def _layer_fwd(l, x, mem, w, full, tab, next_halves=None):
    S, D = x.shape
    dk, dv, nw = D // 4, D // 2, D // 4
    sv = {}
    w_main, w_gate = _permute_w_in(full['w_in'])
    sv['w_main'], sv['w_gate'] = w_main, w_gate
    xn, r1 = _rmsnorm_fwd(x, w['attn_norm'][None], f"rms1_fwd")
    proj = _mm(xn, w_main, 'nn', "mm_proj")
    gfb = _mm(xn, w_gate, 'nn', "mm_gate")
    sv.update(x=x, xn=xn, r1=r1, proj=proj, gfb=gfb)
    wg = _gate_weight(w['gla_wg2_f'], w['gla_wg2_b']).astype(_BF)
    bg = jnp.concatenate([w['gla_bg_f'], w['gla_bg_b']])[None]
    o_f, st_f = _gla_fwd(proj, gfb, wg, bg, False, "gla_fwd_f")
    o_b, st_b = _gla_fwd(proj, gfb, wg, bg, True, "gla_fwd_b")
    y_gla = _gla_post_fwd(o_f, o_b, proj, (2 * dk) // dv + 1, w['gla_out_norm'][None], "gla_post_fwd")
    sv.update(wg=wg, bg=bg, o_f=o_f, o_b=o_b, st_f=st_f, st_b=st_b)
    c0 = (2 * dk + 2 * dv) // nw
    qn = _headnorm_fwd(proj, c0, nw, NA_HD, jnp.tile(w['na_q_norm'], nw // NA_HD)[None], "na_qnorm_fwd")
    kn = _headnorm_fwd(proj, c0 + 1, nw, NA_HD, jnp.tile(w['na_k_norm'], nw // NA_HD)[None], "na_knorm_fwd")
    vb = proj[:, (c0 + 2) * nw:(c0 + 3) * nw].astype(_BF)
    o_na = _na_fwd(qn, kn, vb, tab, "na_fwd")
    y_na = _headnorm_fwd(o_na, 0, nw, NA_HD, w['na_out_norm'][None], "na_onorm_fwd")
    sv.update(qn=qn, kn=kn, vb=vb, o_na=o_na)
    mhd = nw // MEM_HEADS
    mqn = _headnorm_fwd(proj, c0 + 3, nw, mhd, jnp.tile(w['mem_q_norm'], MEM_HEADS)[None], "mem_qnorm_fwd")
    memn, rm = _rmsnorm_fwd(mem, w['mem_norm'][None], "mem_rms_fwd")
    kv = _mm(memn, full['mem_wkv'], 'nn', "mm_memkv")
    km = _headnorm_fwd(kv, 0, nw, mhd, jnp.tile(w['mem_k_norm'], MEM_HEADS)[None], "mem_knorm_fwd")
    vm = kv[:, nw:].astype(_BF)
    o_mem = _mem_fwd(mqn, km, vm, "mem_fwd")
    y_mem = _headnorm_fwd(o_mem, 0, nw, mhd, w['mem_out_norm'][None], "mem_onorm_fwd")
    sv.update(mqn=mqn, memn=memn, rm=rm, kv=kv, km=km, vm=vm, o_mem=o_mem)
    y = jnp.concatenate([y_gla, y_na, y_mem], axis=1)
    x1 = _mm(y, full['w_out'], 'nn', "mm_out", res=x)
    h, r2 = _rmsnorm_fwd(x1, w['ffn_norm'][None], "rms2_fwd")
    if next_halves is None:
        a, gate, up = _ffn_up_fwd(h, full['ffn_w13'], "ffn_up_fwd")
        x2 = _mm(a, full['ffn_w2'], 'nn', "mm_w2", res=x1)
        got = None
    else:
        (a, gate, up), g1 = _ffn_up_fwd(h, full['ffn_w13'], "ffn_up_fwd_gather", _gather_ici(next_halves))
        x2, got = _mm(a, full['ffn_w2'], 'nn', "mm_w2_gather", res=x1, rider=_gather_d2d(g1))
    sv.update(y=y, x1=x1, h=h, r2=r2, gate=gate, up=up, a=a)
    return x2, sv, got


def _layer_bwd(l, dx2, dx2b, mem, w, full, tab, sv, pending=None, cidx=None):
    S, D = dx2.shape
    dk, dv, nw = D // 4, D // 2, D // 4
    gb, gs = {}, {}
    if pending is None:
        dgate, dup = _ffn_up_bwd(dx2b, full['ffn_w2'], sv['gate'], sv['up'], "ffn_up_bwd")
    else:
        (dgate, dup), landed = _ffn_up_bwd(dx2b, full['ffn_w2'], sv['gate'], sv['up'], "ffn_up_bwd_swap",
                                           _swap_halves(pending))
        partial = _pair_sums(pending, landed, cidx)
    gb['ffn_w2'] = _mm(sv['a'], dx2b, 'tn', "mm_dw2")
    dh = _ffn_up_dh(dgate, dup, full['ffn_w13'], "ffn_up_dh")
    if pending is None:
        gb['ffn_w13'] = _ffn_up_dw(sv['h'], dgate, dup, "ffn_up_dw")
    else:
        gb['ffn_w13'], slots = _ffn_up_dw(sv['h'], dgate, dup, "ffn_up_dw_scatter", _scatter_chips(partial))
        reduced = _chip_sums(partial, slots)
    dx1, dx1b, g = _rmsnorm_bwd(sv['x1'], sv['r2'], w['ffn_norm'][None], dh, dx2, "rms2_bwd")
    gs['ffn_norm'] = g[0]
    dy = _mm(dx1b, full['w_out'], 'nt', "mm_dy")
    gb['w_out'] = _mm(sv['y'], dx1b, 'tn', "mm_dwout")
    c0 = (2 * dk + 2 * dv) // nw
    mhd = nw // MEM_HEADS
    do_mem, g = _headnorm_bwd(sv['o_mem'], 0, nw, mhd, w['mem_out_norm'][None], dy, (dv + nw) // nw, "mem_onorm_bwd")
    gs['mem_out_norm'] = g[0]
    dmqn, dkm, dvm = _mem_bwd(sv['mqn'], sv['km'], sv['vm'], do_mem, "mem_bwd")
    dmq, g = _headnorm_bwd(sv['proj'], c0 + 3, nw, mhd, jnp.tile(w['mem_q_norm'], MEM_HEADS)[None], dmqn, 0,
                           "mem_qnorm_bwd", _BF)
    gs['mem_q_norm'] = g[0].reshape(MEM_HEADS, mhd).sum(0)
    dkvk, g = _headnorm_bwd(sv['kv'], 0, nw, mhd, jnp.tile(w['mem_k_norm'], MEM_HEADS)[None], dkm, 0,
                            "mem_knorm_bwd")
    gs['mem_k_norm'] = g[0].reshape(MEM_HEADS, mhd).sum(0)
    dkv = jnp.concatenate([dkvk, dvm], axis=1).astype(_BF)
    gb['mem_wkv'] = _mm(sv['memn'], dkv, 'tn', "mm_dwkv")
    dmemn = _mm(dkv, full['mem_wkv'], 'nt', "mm_dmemn")
    _, _, g = _rmsnorm_bwd(mem, sv['rm'], w['mem_norm'][None], dmemn, None, "mem_rms_bwd")
    gs['mem_norm'] = g[0]
    do_na, g = _headnorm_bwd(sv['o_na'], 0, nw, NA_HD, w['na_out_norm'][None], dy, dv // nw, "na_onorm_bwd")
    gs['na_out_norm'] = g[0]
    dqn, dkn, dnv, dtab = _na_bwd(sv['qn'], sv['kn'], sv['vb'], tab, do_na, "na_bwd")
    gs['na_rpb'] = _na_bias_grad(dtab, S // GRID_W)
    dnq, g = _headnorm_bwd(sv['proj'], c0, nw, NA_HD, jnp.tile(w['na_q_norm'], nw // NA_HD)[None], dqn, 0,
                           "na_qnorm_bwd", _BF)
    gs['na_q_norm'] = g[0].reshape(nw // NA_HD, NA_HD).sum(0)
    dnk, g = _headnorm_bwd(sv['proj'], c0 + 1, nw, NA_HD, jnp.tile(w['na_k_norm'], nw // NA_HD)[None], dkn, 0,
                           "na_knorm_bwd", _BF)
    gs['na_k_norm'] = g[0].reshape(nw // NA_HD, NA_HD).sum(0)
    do_gla, dgr, g = _gla_post_bwd(sv['o_f'], sv['o_b'], sv['proj'], (2 * dk) // dv + 1, w['gla_out_norm'][None], dy,
                                   "gla_post_bwd")
    gs['gla_out_norm'] = g[0]
    dq1, dk1, dv1, dz_f = _gla_bwd(sv['proj'], sv['gfb'], sv['wg'], sv['bg'], sv['st_f'], do_gla, False, None,
                                   "gla_bwd_f")
    dgq, dgk, dgv, dz_b = _gla_bwd(sv['proj'], sv['gfb'], sv['wg'], sv['bg'], sv['st_b'], do_gla, True,
                                   (dq1, dk1, dv1), "gla_bwd_b")
    dz = jnp.concatenate([dz_f, dz_b], axis=1).astype(_BF)
    ones_lane = (jnp.arange(_LANES) == 2 * GLA_GATE_RANK)[None]
    gfb_aug = jnp.where(ones_lane, 1.0, sv['gfb']).astype(_BF)
    dwg = _mm(gfb_aug, dz, 'tn', "mm_dwg")
    r16 = GLA_GATE_RANK
    gs['gla_wg2_f'] = dwg[:r16, :dk]
    gs['gla_wg2_b'] = dwg[r16:2 * r16, dk:]
    gs['gla_bg_f'] = dwg[2 * r16, :dk]
    gs['gla_bg_b'] = dwg[2 * r16, dk:]
    dgfb = _mm(dz, sv['wg'], 'nt', "mm_dgfb", out_dtype=_BF)
    dproj = jnp.concatenate([dgq.astype(_BF), dgk.astype(_BF), dgv.astype(_BF), dgr, dnq, dnk, dnv.astype(_BF), dmq],
                            axis=1)
    t = _mm(dgfb, sv['w_gate'], 'nt', "mm_dxn_gate")
    if pending is None:
        dxn = _mm(dproj, sv['w_main'], 'nt', "mm_dxn", res=t)
        done = None
    else:
        dxn, theirs = _mm(dproj, sv['w_main'], 'nt', "mm_dxn_join", res=t, rider=_swap_reduced(reduced))
        done = _assemble_shards(reduced, theirs, cidx)
    dw_main = _mm(sv['xn'], dproj, 'tn', "mm_dwmain")
    dw_gate = _mm(sv['xn'], dgfb, 'tn', "mm_dwgate")
    gb['w_in'] = _unpermute_w_in(dw_main, dw_gate, D)
    dx, dxb, g = _rmsnorm_bwd(sv['x'], sv['r1'], w['attn_norm'][None], dxn, dx1, "rms1_bwd")
    gs['attn_norm'] = g[0]
    return dx, dxb, gb, gs, done


def kernel(x, mem, attn_norm, w_in, gla_wg2_f, gla_bg_f, gla_wg2_b, gla_bg_b, gla_out_norm, na_q_norm, na_k_norm, na_rpb, na_out_norm, mem_norm, mem_wkv, mem_q_norm, mem_k_norm, mem_out_norm, w_out, ffn_norm, ffn_w13, ffn_w2, loss_target, m_attn_norm, m_w_in, m_gla_wg2_f, m_gla_bg_f, m_gla_wg2_b, m_gla_bg_b, m_gla_out_norm, m_na_q_norm, m_na_k_norm, m_na_rpb, m_na_out_norm, m_mem_norm, m_mem_wkv, m_mem_q_norm, m_mem_k_norm, m_mem_out_norm, m_w_out, m_ffn_norm, m_ffn_w13, m_ffn_w2, v_attn_norm, v_w_in, v_gla_wg2_f, v_gla_bg_f, v_gla_wg2_b, v_gla_bg_b, v_gla_out_norm, v_na_q_norm, v_na_k_norm, v_na_rpb, v_na_out_norm, v_mem_norm, v_mem_wkv, v_mem_q_norm, v_mem_k_norm, v_mem_out_norm, v_w_out, v_ffn_norm, v_ffn_w13, v_ffn_w2):
    args = locals()
    W = {k: args[k] for k in WEIGHTS}
    M = {k: args['m_' + k] for k in WEIGHTS}
    V = {k: args['v_' + k] for k in WEIGHTS}
    depth = attn_norm.shape[0]
    xs, mems, tgt = x[0], mem[0], loss_target[0]
    cidx = lax.axis_index("c").astype(jnp.int32).reshape(1)

    gate_rows, gate_spans = _pack_rows([W[k] for k in SMALL_SHARDED])
    gate_all = _all_devices(gate_rows, False, "gate_weights_gather")
    gate_full = {}
    for i, k in enumerate(SMALL_SHARDED):
        per_chip = [_unpack_rows(gate_all[2 * j], gate_spans)[i] for j in range(4)]
        gate_full[k] = jnp.concatenate(per_chip, axis=-1)

    small = [k for k in WEIGHTS if k not in BIG]

    def layer_small(l):
        d = {k: W[k][l] for k in small if k not in SMALL_SHARDED}
        d.update({k: gate_full[k][l] for k in SMALL_SHARDED})
        return d

    saved, fulls, tabs = [], [], []
    cur = xs
    halves = [_weight_halves({k: W[k][l].astype(_BF) for k in BIG}) for l in range(depth)]
    full = _gather_layer_weights(halves[0], "gather_weights")
    for l in range(depth):
        tab = _na_bias_tables(W['na_rpb'][l], xs.shape[0] // GRID_W)
        nxt = halves[l + 1] if l + 1 < depth else None
        cur, sv, got = _layer_fwd(l, cur, mems, layer_small(l), full, tab, nxt)
        saved.append(sv)
        fulls.append(full)
        tabs.append(tab)
        if nxt is not None:
            full = _assemble_weights(got, nxt)
    loss_tile, dy, dyb = _loss_head(cur, tgt, "loss_head")
    loss = lax.psum(loss_tile[0, 0], ("x", "y", "c"))

    big_grads = [None] * depth
    small_grads = [None] * depth
    pending = None
    for l in range(depth - 1, -1, -1):
        dy, dyb, gb, gs, done = _layer_bwd(l, dy, dyb, mems, layer_small(l), fulls[l], tabs[l], saved[l], pending, cidx)
        if pending is not None:
            big_grads[l + 1] = done
        pending = _grads_as_shards(gb)
        small_grads[l] = gs
    big_grads[0] = _reduce_layer_grads(pending, cidx, "reduce_grads")
    grad_x = dy[None]

    small_stack = [jnp.stack([small_grads[l][k] for l in range(depth)]) for k in small]
    packed, spans = _pack_rows(small_stack)
    summed = _unpack_rows(_all_devices(packed, True, "small_grads_sum"), spans)
    G = dict(zip(small, summed))
    chip = 2 * lax.axis_index("x") + lax.axis_index("y")
    for k in SMALL_SHARDED:
        wdt = W[k].shape[-1]
        G[k] = lax.dynamic_slice_in_dim(G[k], chip * wdt, wdt, axis=2)
    for k in BIG:
        G[k] = jnp.stack([big_grads[l][k] for l in range(depth)])

    delta, new_m, new_v = {}, {}, {}
    for k in WEIGHTS:
        delta[k], new_m[k], new_v[k] = _adamw(W[k], G[k], M[k], V[k], "adamw_" + k)
    return (loss, grad_x, *[G[k] for k in WEIGHTS], *[delta[k] for k in WEIGHTS], *[new_m[k] for k in WEIGHTS],
            *[new_v[k] for k in WEIGHTS])
```

```python
import functools

import numpy as np
import jax
import jax.numpy as jnp
from jax import lax
from jax.experimental import pallas as pl
from jax.experimental.pallas import tpu as pltpu

_F32 = jnp.float32
_BF = jnp.bfloat16
_MESH = pl.DeviceIdType.MESH

_VMEM_LIMIT_BYTES = 56 * 1024 * 1024
_LANES = 128

RMS_EPS = 1e-6
GLA_HEADS = 4
GLA_GATE_RANK = 16
GLA_TAU = 16.0
GLA_CHUNK = 64
GRID_W = 64
NA_HD = 64
NA_ROWS = 8
NA_COLS = 16
MEM_HEADS = 4
ADAM_LR = 0.001
ADAM_B1 = 0.9
ADAM_B2 = 0.999
ADAM_EPS = 1e-08
ADAM_WD = 0.01
ADAM_STEP = 10

WEIGHTS = ['attn_norm', 'w_in', 'gla_wg2_f', 'gla_bg_f', 'gla_wg2_b', 'gla_bg_b', 'gla_out_norm', 'na_q_norm',
           'na_k_norm', 'na_rpb', 'na_out_norm', 'mem_norm', 'mem_wkv', 'mem_q_norm', 'mem_k_norm', 'mem_out_norm',
           'w_out', 'ffn_norm', 'ffn_w13', 'ffn_w2']
BIG = ['w_in', 'mem_wkv', 'w_out', 'ffn_w13', 'ffn_w2']
SMALL_SHARDED = ['gla_wg2_f', 'gla_wg2_b']


def _pick(n, cands):
    for c in cands:
        if n % c == 0:
            return c
    return n


def _row_block(rows, row_bytes, cap_bytes):
    best = 8
    for rb in range(8, rows + 1, 8):
        if rows % rb == 0 and rb * row_bytes <= cap_bytes:
            best = rb
    return best


def _params(sem=None):
    return pltpu.CompilerParams(dimension_semantics=sem, vmem_limit_bytes=_VMEM_LIMIT_BYTES)


def _dg(a, b, ca, cb):
    return lax.dot_general(a.astype(_BF), b.astype(_BF), (((ca,), (cb,)), ((), ())), preferred_element_type=_F32)


def _split(a):
    hi = a.astype(_BF)
    return hi, (a - hi.astype(_F32)).astype(_BF)


def _dg_hl(a, b, ca, cb):
    hi, lo = _split(a)
    return _dg(hi, b, ca, cb) + _dg(lo, b, ca, cb)


def _dg_lh(a, b, ca, cb):
    hi, lo = _split(b)
    return _dg(a, hi, ca, cb) + _dg(a, lo, ca, cb)


def _sigmoid(x):
    return 1.0 / (1.0 + jnp.exp(-x))


def _log_sigmoid(z):
    return jnp.minimum(z, 0.0) - jnp.log(1.0 + jnp.exp(-jnp.abs(z)))


def _block_diag(width, hd):
    i = np.arange(width) // hd
    return jnp.asarray((i[:, None] == i[None, :]).astype(np.float32), dtype=_BF)


class _Rider:
    def __init__(self, ins, outs, sems, start, finish, aliases=()):
        self.ins, self.outs, self.sems = list(ins), list(outs), list(sems)
        self.start, self.finish, self.aliases = start, finish, tuple(aliases)


def _call(core, name, grid, in_specs, out_specs, out_shape, scratch, sem, args, rider=None):
    out_specs, out_shape = tuple(out_specs), tuple(out_shape)
    if rider is None:
        return pl.pallas_call(core, name=name, out_shape=out_shape, grid=grid, in_specs=list(in_specs),
                              out_specs=out_specs, scratch_shapes=list(scratch), compiler_params=_params(sem))(*args)
    ni, no, ns = len(in_specs), len(out_specs), len(scratch)
    ri, ro = len(rider.ins), len(rider.outs)

    def body(*refs):
        ins, rins = refs[:ni], refs[ni:ni + ri]
        outs, routs = refs[ni + ri:ni + ri + no], refs[ni + ri + no:ni + ri + no + ro]
        scr, rsem = refs[ni + ri + no + ro:ni + ri + no + ro + ns], refs[ni + ri + no + ro + ns:]
        first = pl.program_id(0) == 0
        last = pl.program_id(0) == grid[0] - 1
        for ax in range(1, len(grid)):
            first = first & (pl.program_id(ax) == 0)
            last = last & (pl.program_id(ax) == grid[ax] - 1)

        @pl.when(first)
        def _():
            rider.start(rins, routs, rsem)

        core(*ins, *outs, *scr)

        @pl.when(last)
        def _():
            rider.finish(rins, routs, rsem)

    res = pl.pallas_call(
        body, name=name, out_shape=out_shape + tuple(rider.outs), grid=grid,
        in_specs=list(in_specs) + [_ANY] * ri, out_specs=out_specs + (_ANY,) * ro,
        scratch_shapes=list(scratch) + rider.sems,
        input_output_aliases={ni + i: no + o for i, o in rider.aliases},
        compiler_params=_params(("arbitrary",) * len(grid)))(*args, *rider.ins)
    return tuple(res[:no]), tuple(res[no:])


_ANY = pl.BlockSpec(memory_space=pl.ANY)


def _mm(a, b, mode, name, out_dtype=_F32, res=None, rider=None):
    def finish(r):
        return r[0] if rider is None else (r[0][0], r[1])

    if mode == 'nn':
        (M, K), N = a.shape, b.shape[1]
    elif mode == 'nt':
        (M, K), N = a.shape, b.shape[0]
    else:
        (K, M), N = a.shape, b.shape[1]
    ca, cb = {'nn': (1, 0), 'nt': (1, 1), 'tn': (0, 0)}[mode]
    has_res = res is not None
    args = (a, b) + ((res,) if has_res else ())
    out_shape = jax.ShapeDtypeStruct((M, N), out_dtype)
    if K <= _K_RESIDENT:
        tm = _pick(M, (1024, 512, 256, 128))
        tn = _pick(N, (1024, 512, 256, 128))
        a_spec = (pl.BlockSpec((K, tm), lambda i, j: (0, i)) if mode == 'tn'
                  else pl.BlockSpec((tm, K), lambda i, j: (i, 0)))
        b_spec = (pl.BlockSpec((tn, K), lambda i, j: (j, 0)) if mode == 'nt'
                  else pl.BlockSpec((K, tn), lambda i, j: (0, j)))
        o_spec = pl.BlockSpec((tm, tn), lambda i, j: (i, j))

        def body1(*refs):
            a_ref, b_ref = refs[:2]
            o_ref = refs[-1]
            out = _dg(a_ref[...], b_ref[...], ca, cb)
            if has_res:
                out = out + refs[2][...]
            o_ref[...] = out.astype(o_ref.dtype)

        return finish(_call(body1, name, (M // tm, N // tn), [a_spec, b_spec] + ([o_spec] if has_res else []),
                            [o_spec], [out_shape], [], ("parallel", "parallel"), args, rider))

    tm, tn, tk = _loop_tiles(M, N, K, has_res, out_dtype)
    nk = K // tk
    if mode == 'tn':
        a_spec = pl.BlockSpec((tk, tm), lambda i, j, k: (k, i))
    else:
        a_spec = pl.BlockSpec((tm, tk), lambda i, j, k: (i, k))
    if mode == 'nt':
        b_spec = pl.BlockSpec((tn, tk), lambda i, j, k: (j, k))
    else:
        b_spec = pl.BlockSpec((tk, tn), lambda i, j, k: (k, j))
    o_spec = pl.BlockSpec((tm, tn), lambda i, j, k: (i, j))
    in_out = out_dtype == _F32

    def body(*refs):
        a_ref, b_ref = refs[:2]
        o_ref = refs[-1] if in_out else refs[-2]
        acc = o_ref if in_out else refs[-1]
        k = pl.program_id(2)

        @pl.when(k == 0)
        def _():
            part = _dg(a_ref[...], b_ref[...], ca, cb)
            acc[...] = part + refs[2][...] if has_res else part

        @pl.when(k > 0)
        def _():
            acc[...] += _dg(a_ref[...], b_ref[...], ca, cb)

        if not in_out:
            @pl.when(k == nk - 1)
            def _():
                o_ref[...] = acc[...].astype(o_ref.dtype)

    return finish(_call(body, name, (M // tm, N // tn, nk), [a_spec, b_spec] + ([o_spec] if has_res else []),
                        [o_spec], [out_shape], [] if in_out else [pltpu.VMEM((tm, tn), _F32)],
                        ("parallel", "parallel", "arbitrary"), args, rider))


_K_RESIDENT = 2048
_LOOP_TILE_BYTES = 40 * 1024 * 1024
_LOOP_TILE_AREA = 1024 * 1024


def _loop_tiles(M, N, K, has_res, out_dtype):
    best, best_score = None, None
    obytes = 4 if out_dtype == _F32 else 2
    for tk in (2816, 2560, 2048, 1408, 1024, 512, 256, 128):
        if K % tk:
            continue
        for tm in (2048, 1408, 1024, 512, 256, 128):
            for tn in (2048, 1024, 512, 256, 128):
                if M % tm or N % tn:
                    continue
                need = 2 * 2 * tk * (tm + tn) + tm * tn * (2 * obytes + (8 if has_res else 0) + (0 if obytes == 4 else 4))
                score = (min(tm * tn, _LOOP_TILE_AREA), tk, tm * tn)
                if need <= _LOOP_TILE_BYTES and (best is None or score > best_score):
                    best, best_score = (tm, tn, tk), score
    return best


def _cols_to_shards(w):
    r, c = w.shape[0] // 2, w.shape[1] // 4
    return jnp.transpose(w.reshape(2, r, 4, c), (2, 0, 1, 3))


def _shards_to_cols(g):
    _, _, r, c = g.shape
    return jnp.transpose(g, (1, 2, 0, 3)).reshape(2 * r, 4 * c)


_FFN_SUB = 256


def _ffn_up_fwd(h, g13, name, rider=None):
    S = h.shape[0]
    _, _, R, C = g13.shape
    tm = _pick(S, (512, 256, 128))
    cw, nc = _ffn_cols(C)
    subs = [(s, min(s + _FFN_SUB, cw)) for s in range(0, cw, _FFN_SUB)]

    def body(h_ref, wg_ref, wu_ref, a_ref, g_ref, u_ref):
        h0, h1 = h_ref[:, :R], h_ref[:, R:]
        for lo, hi in subs:
            sl = slice(lo, hi)
            gv = _dg(h0, wg_ref[0, 0, :, sl], 1, 0) + _dg(h1, wg_ref[0, 1, :, sl], 1, 0)
            uv = _dg(h0, wu_ref[0, 0, :, sl], 1, 0) + _dg(h1, wu_ref[0, 1, :, sl], 1, 0)
            a_ref[:, sl] = (gv * _sigmoid(gv) * uv).astype(a_ref.dtype)
            g_ref[:, sl] = gv.astype(g_ref.dtype)
            u_ref[:, sl] = uv.astype(u_ref.dtype)

    out = jax.ShapeDtypeStruct((S, 2 * C), _BF)
    ospec = pl.BlockSpec((tm, cw), lambda j, i: (i, j))
    return _call(
        body, name, (2 * nc, S // tm),
        [pl.BlockSpec((tm, 2 * R), lambda j, i: (i, 0)),
         pl.BlockSpec((1, 2, R, cw), lambda j, i: (lax.div(j, nc), 0, 0, lax.rem(j, nc))),
         pl.BlockSpec((1, 2, R, cw), lambda j, i: (2 + lax.div(j, nc), 0, 0, lax.rem(j, nc)))],
        (ospec, ospec, ospec), (out, out, out), [], ("parallel", "parallel"), (h, g13, g13), rider)


def _ffn_cols(C):
    if C % (2 * _LANES) == 0:
        return C // 2, 2
    return C, 1


def _ffn_up_bwd(dxb, w2, gate, up, name, rider=None):
    S, D = dxb.shape
    F = w2.shape[0]
    tf = 2 * _FFN_SUB
    tm = _pick(S, (1024, 512, 256, 128))

    def body(d_ref, w_ref, g_ref, u_ref, dg_ref, du_ref):
        dv = d_ref[...]
        for s in range(tf // _FFN_SUB):
            sl = slice(s * _FFN_SUB, (s + 1) * _FFN_SUB)
            da = _dg(dv, w_ref[sl, :], 1, 1)
            gv = g_ref[:, sl].astype(_F32)
            uv = u_ref[:, sl].astype(_F32)
            sg = _sigmoid(gv)
            dg_ref[:, sl] = (da * uv * (sg * (1.0 + gv * (1.0 - sg)))).astype(dg_ref.dtype)
            du_ref[:, sl] = (da * (gv * sg)).astype(du_ref.dtype)

    out = jax.ShapeDtypeStruct((S, F), _BF)
    tile = pl.BlockSpec((tm, tf), lambda i, j: (i, j))
    return _call(
        body, name, (S // tm, F // tf),
        [pl.BlockSpec((tm, D), lambda i, j: (i, 0)), pl.BlockSpec((tf, D), lambda i, j: (j, 0)), tile, tile],
        (tile, tile), (out, out), [], ("parallel", "parallel"), (dxb, w2, gate, up), rider)


def _ffn_up_dh(dgate, dup, g13, name):
    S = dgate.shape[0]
    _, _, R, C = g13.shape
    tm = _pick(S, (512, 256, 128))

    def body(dg_ref, du_ref, wg_ref, wu_ref, o_ref):
        j = pl.program_id(1)

        def part(hh):
            return _dg(dg_ref[...], wg_ref[0, hh], 1, 1) + _dg(du_ref[...], wu_ref[0, hh], 1, 1)

        @pl.when(j == 0)
        def _():
            for hh in range(2):
                o_ref[:, hh * R:(hh + 1) * R] = part(hh)

        @pl.when(j > 0)
        def _():
            for hh in range(2):
                o_ref[:, hh * R:(hh + 1) * R] += part(hh)

    cw, nc = _ffn_cols(C)
    tile = pl.BlockSpec((tm, cw), lambda i, j: (i, j))
    return pl.pallas_call(
        body, name=name, out_shape=jax.ShapeDtypeStruct((S, 2 * R), _F32), grid=(S // tm, 2 * nc),
        in_specs=[tile, tile, pl.BlockSpec((1, 2, R, cw), lambda i, j: (lax.div(j, nc), 0, 0, lax.rem(j, nc))),
                  pl.BlockSpec((1, 2, R, cw), lambda i, j: (2 + lax.div(j, nc), 0, 0, lax.rem(j, nc)))],
        out_specs=pl.BlockSpec((tm, 2 * R), lambda i, j: (i, 0)),
        compiler_params=_params(("parallel", "arbitrary")))(dgate, dup, g13, g13)


def _ffn_up_dw(hact, dgate, dup, name, rider=None):
    S, R2 = hact.shape
    R = R2 // 2
    C = dgate.shape[1] // 2
    tk = _pick(S, (1024, 512, 256, 128))
    nk = S // tk
    cw, nc = _ffn_cols(C)

    def body(h_ref, dg_ref, du_ref, o_ref):
        chip = pl.program_id(0)
        k = pl.program_id(3)

        def accumulate(d_ref):
            @pl.when(k == 0)
            def _():
                o_ref[0, 0] = _dg(h_ref[...], d_ref[...], 0, 0)

            @pl.when(k > 0)
            def _():
                o_ref[0, 0] += _dg(h_ref[...], d_ref[...], 0, 0)

        @pl.when(chip < 2)
        def _():
            accumulate(dg_ref)

        @pl.when(chip >= 2)
        def _():
            accumulate(du_ref)

    r = _call(
        body, name, (4, nc, 2, nk),
        [pl.BlockSpec((tk, R), lambda c, b, h, k: (k, h)),
         pl.BlockSpec((tk, cw), lambda c, b, h, k: (jnp.where(c < 2, k, 0), jnp.where(c < 2, c * nc + b, 0))),
         pl.BlockSpec((tk, cw), lambda c, b, h, k: (jnp.where(c >= 2, k, 0), jnp.where(c >= 2, (c - 2) * nc + b, 0)))],
        [pl.BlockSpec((1, 1, R, cw), lambda c, b, h, k: (c, h, 0, b))], [jax.ShapeDtypeStruct((4, 2, R, C), _F32)], [],
        ("parallel", "parallel", "parallel", "arbitrary"), (hact, dgate, dup), rider)
    return r[0] if rider is None else (r[0][0], r[1])


def _rmsnorm_fwd(x, g, name):
    S, D = x.shape
    ts = _pick(S, (256,))

    def body(x_ref, g_ref, o_ref, r_ref):
        xv = x_ref[...]
        r = lax.rsqrt(jnp.mean(xv * xv, axis=-1, keepdims=True) + RMS_EPS)
        o_ref[...] = (xv * r * g_ref[...]).astype(o_ref.dtype)
        r_ref[...] = r

    return pl.pallas_call(
        body, name=name,
        out_shape=(jax.ShapeDtypeStruct((S, D), _BF), jax.ShapeDtypeStruct((S, 1), _F32)),
        grid=(S // ts,),
        in_specs=[pl.BlockSpec((ts, D), lambda i: (i, 0)), pl.BlockSpec((1, D), lambda i: (0, 0))],
        out_specs=(pl.BlockSpec((ts, D), lambda i: (i, 0)), pl.BlockSpec((ts, 1), lambda i: (i, 0))),
        compiler_params=_params(("parallel",)))(x, g)


def _rmsnorm_bwd(x, r, g, dy, dres, name):
    S, D = x.shape
    ts = _pick(S, (256,))
    has_res = dres is not None

    def body(*refs):
        if has_res:
            x_ref, r_ref, g_ref, dy_ref, dr_ref, dx_ref, dxb_ref, dg_ref = refs
        else:
            x_ref, r_ref, g_ref, dy_ref, dx_ref, dxb_ref, dg_ref = refs
        rv = r_ref[...]
        n = x_ref[...] * rv
        dyv = dy_ref[...]
        dn = dyv * g_ref[...]
        c = jnp.mean(dn * n, axis=-1, keepdims=True)
        dx = rv * (dn - n * c)
        if has_res:
            dx = dx + dr_ref[...]
        dx_ref[...] = dx
        dxb_ref[...] = dx.astype(dxb_ref.dtype)

        @pl.when(pl.program_id(0) == 0)
        def _():
            dg_ref[...] = jnp.zeros_like(dg_ref)

        dg_ref[...] += jnp.sum(dyv * n, axis=0, keepdims=True)

    row = pl.BlockSpec((ts, D), lambda i: (i, 0))
    vec = pl.BlockSpec((1, D), lambda i: (0, 0))
    in_specs = [row, pl.BlockSpec((ts, 1), lambda i: (i, 0)), vec, row] + ([row] if has_res else [])
    args = (x, r, g, dy) + ((dres,) if has_res else ())
    return pl.pallas_call(
        body, name=name,
        out_shape=(jax.ShapeDtypeStruct((S, D), _F32), jax.ShapeDtypeStruct((S, D), _BF),
                   jax.ShapeDtypeStruct((1, D), _F32)),
        grid=(S // ts,), in_specs=in_specs, out_specs=(row, row, vec),
        compiler_params=_params(("arbitrary",)))(*args)


def _headnorm_fwd(t, cb, W, hd, g, name):
    S = t.shape[0]
    ts = _pick(S, (512, 256))
    bd = _block_diag(W, hd)

    def body(x_ref, g_ref, bd_ref, o_ref):
        xv = x_ref[...].astype(_F32)
        ms = _dg_hl(xv * xv, bd_ref[...], 1, 0) * (1.0 / hd)
        o_ref[...] = (xv * lax.rsqrt(ms + RMS_EPS) * g_ref[...]).astype(o_ref.dtype)

    return pl.pallas_call(
        body, name=name, out_shape=jax.ShapeDtypeStruct((S, W), _BF), grid=(S // ts,),
        in_specs=[pl.BlockSpec((ts, W), lambda i: (i, cb)), pl.BlockSpec((1, W), lambda i: (0, 0)),
                  pl.BlockSpec((W, W), lambda i: (0, 0))],
        out_specs=pl.BlockSpec((ts, W), lambda i: (i, 0)),
        compiler_params=_params(("parallel",)))(t, g, bd)


def _headnorm_bwd(t, cb, W, hd, g, dy, dcb, name, out_dtype=_F32):
    S = t.shape[0]
    ts = _pick(S, (512, 256))
    bd = _block_diag(W, hd)

    def body(x_ref, g_ref, bd_ref, dy_ref, dx_ref, dg_ref):
        xv = x_ref[...].astype(_F32)
        bdv = bd_ref[...]
        ms = _dg_hl(xv * xv, bdv, 1, 0) * (1.0 / hd)
        rv = lax.rsqrt(ms + RMS_EPS)
        n = xv * rv
        dyv = dy_ref[...].astype(_F32)
        dn = dyv * g_ref[...]
        c = _dg_hl(dn * n, bdv, 1, 0) * (1.0 / hd)
        dx_ref[...] = (rv * (dn - n * c)).astype(dx_ref.dtype)

        @pl.when(pl.program_id(0) == 0)
        def _():
            dg_ref[...] = jnp.zeros_like(dg_ref)

        dg_ref[...] += jnp.sum(dyv * n, axis=0, keepdims=True)

    return pl.pallas_call(
        body, name=name,
        out_shape=(jax.ShapeDtypeStruct((S, W), out_dtype), jax.ShapeDtypeStruct((1, W), _F32)),
        grid=(S // ts,),
        in_specs=[pl.BlockSpec((ts, W), lambda i: (i, cb)), pl.BlockSpec((1, W), lambda i: (0, 0)),
                  pl.BlockSpec((W, W), lambda i: (0, 0)), pl.BlockSpec((ts, W), lambda i: (i, dcb))],
        out_specs=(pl.BlockSpec((ts, W), lambda i: (i, 0)), pl.BlockSpec((1, W), lambda i: (0, 0))),
        compiler_params=_params(("arbitrary",)))(t, g, bd, dy)


def _gla_post_fwd(o_f, o_b, proj, r_cb, g, name):
    S, W = o_f.shape
    hd = W // GLA_HEADS
    ts = _pick(S, (256,))
    bd = _block_diag(W, hd)

    def body(of_ref, ob_ref, r_ref, g_ref, bd_ref, y_ref):
        o = of_ref[...] + ob_ref[...]
        ms = _dg_hl(o * o, bd_ref[...], 1, 0) * (1.0 / hd)
        u = o * lax.rsqrt(ms + RMS_EPS) * g_ref[...]
        rr = r_ref[...]
        y_ref[...] = (u * (rr * _sigmoid(rr))).astype(y_ref.dtype)

    row = pl.BlockSpec((ts, W), lambda i: (i, 0))
    return pl.pallas_call(
        body, name=name, out_shape=jax.ShapeDtypeStruct((S, W), _BF), grid=(S // ts,),
        in_specs=[row, row, pl.BlockSpec((ts, W), lambda i: (i, r_cb)), pl.BlockSpec((1, W), lambda i: (0, 0)),
                  pl.BlockSpec((W, W), lambda i: (0, 0))],
        out_specs=row, compiler_params=_params(("parallel",)))(o_f, o_b, proj, g, bd)


def _gla_post_bwd(o_f, o_b, proj, r_cb, g, dy, name):
    S, W = o_f.shape
    hd = W // GLA_HEADS
    ts = _pick(S, (256,))
    bd = _block_diag(W, hd)

    def body(of_ref, ob_ref, r_ref, g_ref, bd_ref, dy_ref, do_ref, dr_ref, dg_ref):
        o = of_ref[...] + ob_ref[...]
        bdv = bd_ref[...]
        ms = _dg_hl(o * o, bdv, 1, 0) * (1.0 / hd)
        rv = lax.rsqrt(ms + RMS_EPS)
        n = o * rv
        gv = g_ref[...]
        rr = r_ref[...]
        sg = _sigmoid(rr)
        dyv = dy_ref[...]
        dr_ref[...] = (dyv * (n * gv) * (sg * (1.0 + rr * (1.0 - sg)))).astype(dr_ref.dtype)
        du = dyv * (rr * sg)
        dn = du * gv
        c = _dg_hl(dn * n, bdv, 1, 0) * (1.0 / hd)
        do_ref[...] = rv * (dn - n * c)

        @pl.when(pl.program_id(0) == 0)
        def _():
            dg_ref[...] = jnp.zeros_like(dg_ref)

        dg_ref[...] += jnp.sum(du * n, axis=0, keepdims=True)

    row = pl.BlockSpec((ts, W), lambda i: (i, 0))
    vec = pl.BlockSpec((1, W), lambda i: (0, 0))
    return pl.pallas_call(
        body, name=name,
        out_shape=(jax.ShapeDtypeStruct((S, W), _F32), jax.ShapeDtypeStruct((S, W), _BF),
                   jax.ShapeDtypeStruct((1, W), _F32)),
        grid=(S // ts,),
        in_specs=[row, row, pl.BlockSpec((ts, W), lambda i: (i, r_cb)), vec, pl.BlockSpec((W, W), lambda i: (0, 0)),
                  row],
        out_specs=(row, row, vec), compiler_params=_params(("arbitrary",)))(o_f, o_b, proj, g, bd, dy)


def _loss_head(y, tgt, name):
    S, D = y.shape
    ts = _pick(S, (256,))

    def body(y_ref, t_ref, l_ref, d_ref, db_ref):
        err = y_ref[...] - t_ref[...]
        d = err * (1.0 / D)
        d_ref[...] = d
        db_ref[...] = d.astype(db_ref.dtype)

        @pl.when(pl.program_id(0) == 0)
        def _():
            l_ref[...] = jnp.zeros_like(l_ref)

        l_ref[...] += 0.5 * jnp.sum(jnp.mean(err * err, axis=-1, keepdims=True))

    row = pl.BlockSpec((ts, D), lambda i: (i, 0))
    return pl.pallas_call(
        body, name=name,
        out_shape=(jax.ShapeDtypeStruct((8, _LANES), _F32), jax.ShapeDtypeStruct((S, D), _F32),
                   jax.ShapeDtypeStruct((S, D), _BF)),
        grid=(S // ts,), in_specs=[row, row],
        out_specs=(pl.BlockSpec((8, _LANES), lambda i: (0, 0)), row, row),
        compiler_params=_params(("arbitrary",)))(y, tgt)


def _adamw(w, g, m, v, name):
    shape = w.shape
    if w.ndim == 3 and shape[1] * shape[2] > 256 * 1024:
        rb = _row_block(shape[1], shape[2] * 4, 1536 * 1024)
        grid = (shape[0], shape[1] // rb)
        spec = pl.BlockSpec((1, rb, shape[2]), lambda l, i: (l, i, 0))
        sem = ("parallel", "parallel")
    else:
        grid = ()
        spec = pl.BlockSpec(memory_space=pltpu.VMEM)
        sem = None

    def body(w_ref, g_ref, m_ref, v_ref, d_ref, nm_ref, nv_ref):
        gv = g_ref[...]
        mn = ADAM_B1 * m_ref[...] + (1.0 - ADAM_B1) * gv
        vn = ADAM_B2 * v_ref[...] + (1.0 - ADAM_B2) * (gv * gv)
        m_hat = mn / (1.0 - ADAM_B1 ** ADAM_STEP)
        v_hat = vn / (1.0 - ADAM_B2 ** ADAM_STEP)
        d_ref[...] = -ADAM_LR * (m_hat / (jnp.sqrt(v_hat) + ADAM_EPS) + ADAM_WD * w_ref[...])
        nm_ref[...] = mn
        nv_ref[...] = vn

    out = jax.ShapeDtypeStruct(shape, _F32)
    return pl.pallas_call(
        body, name=name, out_shape=(out, out, out), grid=grid, in_specs=[spec] * 4, out_specs=(spec,) * 3,
        compiler_params=_params(sem))(w, g, m, v)


def _add_pair(t, la, cidx, name):
    _, _, r, c = t.shape
    rb = _pick(r, (256, 352, 176, 88, 8))

    def body(c_ref, t_ref, l_ref, o_ref):
        o_ref[...] = (t_ref[...] + l_ref[...]).astype(o_ref.dtype)

    return pl.pallas_call(
        body, name=name, out_shape=jax.ShapeDtypeStruct((4, r, c), _BF),
        grid_spec=pltpu.PrefetchScalarGridSpec(
            num_scalar_prefetch=1, grid=(4, r // rb),
            in_specs=[pl.BlockSpec((1, None, rb, c), lambda j, i, cr: (j, cr[0], i, 0)),
                      pl.BlockSpec((1, rb, c), lambda j, i, cr: (j, i, 0))],
            out_specs=pl.BlockSpec((1, rb, c), lambda j, i, cr: (j, i, 0))),
        compiler_params=_params(("parallel", "parallel")))(cidx, t, la)


def _add_four(p, land, chip, core, name):
    _, r, c = p.shape
    rb = _pick(r, (256, 352, 176, 88, 16))

    def body(chip_ref, core_ref, p_ref, l1_ref, l2_ref, l3_ref, o_ref):
        o_ref[...] = ((p_ref[...].astype(_F32) + l1_ref[...].astype(_F32)) + l2_ref[...].astype(_F32)
                      ) + l3_ref[...].astype(_F32)

    def slot(flip):
        return pl.BlockSpec((None, rb, c), lambda i, ch, co: (jnp.bitwise_xor(ch[0], flip), i, 0))

    return pl.pallas_call(
        body, name=name, out_shape=jax.ShapeDtypeStruct((2, r, c), _F32),
        grid_spec=pltpu.PrefetchScalarGridSpec(
            num_scalar_prefetch=2, grid=(r // rb,),
            in_specs=[slot(0), slot(1), slot(2), slot(3)],
            out_specs=pl.BlockSpec((None, rb, c), lambda i, ch, co: (co[0], i, 0))),
        compiler_params=_params(("parallel",)))(chip, core, p, land, land, land)


def _gla_masks(tb, rev):
    i = np.arange(tb)
    same = (i[:, None] // GLA_CHUNK) == (i[None, :] // GLA_CHUNK)
    tri = same & (i[None, :] <= i[:, None])
    keep = (same & ~tri) if rev else tri
    return tuple(jnp.asarray(m.astype(np.float32), dtype=_BF) for m in (tri, same, keep))


def _gla_block(q, k, gfb, wg, bg, tri, same, keepm, rev, scale):
    z = _dg(gfb, wg, 1, 0) + bg
    la = _log_sigmoid(z) * (1.0 / GLA_TAU)
    cum = _dg_lh(tri, la, 0 if rev else 1, 0)
    tot = _dg_lh(same, la, 1, 0)
    e_a = jnp.exp(cum)
    e_na = jnp.exp(-cum)
    e_la = jnp.exp(tot - cum)
    qe = q * scale * e_a
    ke = k * e_na
    kend = k * e_la
    keep = keepm > 0
    p = jnp.where(keep, _dg(qe, ke, 1, 1), 0.0)
    return dict(z=z, tot=tot, e_a=e_a, e_na=e_na, e_la=e_la, qe=qe, ke=ke, kend=kend, keep=keep, p=p)


def _gla_fwd(proj, gfb, wg, bg, rev, name):
    S = proj.shape[0]
    H = GLA_HEADS
    dk = wg.shape[1] // (2 * H)
    dv = 2 * dk
    C = GLA_CHUNK
    cb_n = _pick(S // C, (8, 4, 2, 1))
    tb = cb_n * C
    nb = S // tb
    scale = float(dk) ** -0.5
    masks = _gla_masks(tb, rev)
    wcol = H if rev else 0

    def bmap(b):
        return nb - 1 - b if rev else b

    def body(q_ref, k_ref, v_ref, g_ref, wg_ref, bg_ref, tri_ref, same_ref, keep_ref, o_ref, st_ref, state):
        h = pl.program_id(1)

        @pl.when(pl.program_id(0) == 0)
        def _():
            state[h] = jnp.zeros((dv, dk), _F32)

        t = _gla_block(q_ref[...], k_ref[...], g_ref[...], wg_ref[...], bg_ref[...], tri_ref[...], same_ref[...],
                       keep_ref[...], rev, scale)
        vv = v_ref[...]
        o_ref[...] = _dg(t['p'], vv, 1, 0)
        order = range(cb_n - 1, -1, -1) if rev else range(cb_n)
        for ci in order:
            sl = slice(ci * C, (ci + 1) * C)
            st = state[h]
            o_ref[sl, :] += _dg(t['qe'][sl], st, 1, 1)
            st_ref[0, ci] = st
            state[h] = st * jnp.exp(t['tot'][ci * C:ci * C + 1]) + _dg(vv[sl], t['kend'][sl], 0, 0)

    nq = (H * dk) // dk
    msk = pl.BlockSpec((tb, tb), lambda b, h: (0, 0))
    return pl.pallas_call(
        body, name=name,
        out_shape=(jax.ShapeDtypeStruct((S, H * dv), _F32), jax.ShapeDtypeStruct((H, S // C, dv, dk), _F32)),
        grid=(nb, H),
        in_specs=[pl.BlockSpec((tb, dk), lambda b, h: (bmap(b), h)),
                  pl.BlockSpec((tb, dk), lambda b, h: (bmap(b), nq + h)),
                  pl.BlockSpec((tb, dv), lambda b, h: (bmap(b), (2 * H * dk) // dv + h)),
                  pl.BlockSpec((tb, _LANES), lambda b, h: (bmap(b), 0)),
                  pl.BlockSpec((_LANES, dk), lambda b, h: (0, wcol + h)),
                  pl.BlockSpec((1, dk), lambda b, h: (0, wcol + h)),
                  msk, msk, msk],
        out_specs=(pl.BlockSpec((tb, dv), lambda b, h: (bmap(b), h)),
                   pl.BlockSpec((1, cb_n, dv, dk), lambda b, h: (h, bmap(b), 0, 0))),
        scratch_shapes=[pltpu.VMEM((H, dv, dk), _F32)],
        compiler_params=_params(("arbitrary", "arbitrary")))(proj, proj, proj, gfb, wg, bg, *masks)


def _gla_bwd(proj, gfb, wg, bg, st, do, rev, prev, name):
    S = proj.shape[0]
    H = GLA_HEADS
    dk = wg.shape[1] // (2 * H)
    dv = 2 * dk
    C = GLA_CHUNK
    cb_n = _pick(S // C, (8, 4, 2, 1))
    tb = cb_n * C
    nb = S // tb
    scale = float(dk) ** -0.5
    masks = _gla_masks(tb, rev)
    wcol = H if rev else 0
    has_prev = prev is not None

    def bmap(b):
        return b if rev else nb - 1 - b

    def body(*refs):
        if has_prev:
            (q_ref, k_ref, v_ref, g_ref, wg_ref, bg_ref, tri_ref, same_ref, keep_ref, st_ref, do_ref,
             pq_ref, pk_ref, pv_ref, dq_ref, dk_ref, dv_ref, dz_ref, dstate) = refs
        else:
            (q_ref, k_ref, v_ref, g_ref, wg_ref, bg_ref, tri_ref, same_ref, keep_ref, st_ref, do_ref,
             dq_ref, dk_ref, dv_ref, dz_ref, dstate) = refs
        h = pl.program_id(1)

        @pl.when(pl.program_id(0) == 0)
        def _():
            dstate[h] = jnp.zeros((dv, dk), _F32)

        t = _gla_block(q_ref[...], k_ref[...], g_ref[...], wg_ref[...], bg_ref[...], tri_ref[...], same_ref[...],
                       keep_ref[...], rev, scale)
        vv = v_ref[...]
        dov = do_ref[...]
        order = range(cb_n) if rev else range(cb_n - 1, -1, -1)
        for ci in order:
            sl = slice(ci * C, (ci + 1) * C)
            stp = st_ref[0, ci]
            dst = dstate[h]
            e_l = jnp.exp(t['tot'][ci * C:ci * C + 1])
            kend_c = t['kend'][sl]
            dkend_c = _dg(vv[sl], dst, 1, 0)
            dq_ref[sl, :] = _dg(dov[sl], stp, 1, 0)
            dk_ref[sl, :] = dkend_c
            dv_ref[sl, :] = _dg(kend_c, dst, 1, 1)
            dtot = (e_l * jnp.sum(dst * stp, axis=0, keepdims=True)
                    + jnp.sum(dkend_c * kend_c, axis=0, keepdims=True))
            dz_ref[sl, :] = jnp.broadcast_to(dtot, (C, dk))
            dstate[h] = dst * e_l + _dg(dov[sl], t['qe'][sl], 0, 0)
        dp = jnp.where(t['keep'], _dg(dov, vv, 1, 1), 0.0)
        dqe = _dg(dp, t['ke'], 1, 0) + dq_ref[...]
        dke = _dg(dp, t['qe'], 0, 0)
        dkend = dk_ref[...]
        dvv = _dg(t['p'], dov, 0, 0) + dv_ref[...]
        dqv = dqe * t['e_a'] * scale
        dkv = dke * t['e_na'] + dkend * t['e_la']
        d_a = dqe * t['qe'] - dke * t['ke'] - dkend * t['kend']
        dla = _dg_lh(tri_ref[...], d_a, 1 if rev else 0, 0) + dz_ref[...]
        dz_ref[...] = dla * (1.0 / GLA_TAU) * _sigmoid(-t['z'])
        if has_prev:
            dqv = dqv + pq_ref[...]
            dkv = dkv + pk_ref[...]
            dvv = dvv + pv_ref[...]
        dq_ref[...] = dqv
        dk_ref[...] = dkv
        dv_ref[...] = dvv

    nq = (H * dk) // dk
    msk = pl.BlockSpec((tb, tb), lambda b, h: (0, 0))
    kblk = pl.BlockSpec((tb, dk), lambda b, h: (bmap(b), h))
    vblk = pl.BlockSpec((tb, dv), lambda b, h: (bmap(b), h))
    in_specs = [kblk,
                pl.BlockSpec((tb, dk), lambda b, h: (bmap(b), nq + h)),
                pl.BlockSpec((tb, dv), lambda b, h: (bmap(b), (2 * H * dk) // dv + h)),
                pl.BlockSpec((tb, _LANES), lambda b, h: (bmap(b), 0)),
                pl.BlockSpec((_LANES, dk), lambda b, h: (0, wcol + h)),
                pl.BlockSpec((1, dk), lambda b, h: (0, wcol + h)),
                msk, msk, msk,
                pl.BlockSpec((1, cb_n, dv, dk), lambda b, h: (h, bmap(b), 0, 0)),
                vblk]
    args = [proj, proj, proj, gfb, wg, bg, *masks, st, do]
    if has_prev:
        in_specs += [kblk, kblk, vblk]
        args += list(prev)
    return pl.pallas_call(
        body, name=name,
        out_shape=(jax.ShapeDtypeStruct((S, H * dk), _F32), jax.ShapeDtypeStruct((S, H * dk), _F32),
                   jax.ShapeDtypeStruct((S, H * dv), _F32), jax.ShapeDtypeStruct((S, H * dk), _F32)),
        grid=(nb, H), in_specs=in_specs, out_specs=(kblk, kblk, vblk, kblk),
        scratch_shapes=[pltpu.VMEM((H, dv, dk), _F32)],
        compiler_params=_params(("arbitrary", "arbitrary")))(*args)


NA_GROUP = 4
NA_WIN_ROWS = NA_GROUP + NA_ROWS


def _na_geometry(S):
    rows = S // GRID_W
    assert rows % NA_GROUP == 0 and rows >= NA_WIN_ROWS + NA_GROUP
    return rows, rows // NA_GROUP


def _na_win_start(g, rows):
    return jnp.clip(NA_GROUP * g - NA_ROWS // 2, 0, rows - NA_WIN_ROWS)


def _na_class(g, groups):
    return jnp.where(g == 0, 0, jnp.where(g == groups - 1, 2, 1))


def _na_onehots(rows):
    groups = rows // NA_GROUP
    by_row = np.zeros((3, NA_GROUP, NA_WIN_ROWS, 2 * NA_ROWS - 1), np.float32)
    for cls, g in enumerate((0, 1, groups - 1)):
        ws = int(np.clip(NA_GROUP * g - NA_ROWS // 2, 0, rows - NA_WIN_ROWS))
        for qr in range(NA_GROUP):
            r = NA_GROUP * g + qr
            rs = int(np.clip(r - NA_ROWS // 2, 0, rows - NA_ROWS))
            for kr in range(NA_WIN_ROWS):
                if rs <= ws + kr < rs + NA_ROWS:
                    by_row[cls, qr, kr, ws + kr - r + NA_ROWS - 1] = 1.0
    c = np.arange(GRID_W)
    cs = np.clip(c - NA_COLS // 2, 0, GRID_W - NA_COLS)
    kc = np.arange(GRID_W)
    win = (kc[None, :] >= cs[:, None]) & (kc[None, :] < cs[:, None] + NA_COLS)
    idx = np.clip(kc[None, :] - c[:, None], -(NA_COLS - 1), NA_COLS - 1) + (NA_COLS - 1)
    by_col = ((idx[:, :, None] == np.arange(2 * NA_COLS - 1)[None, None, :]) & win[:, :, None]).astype(np.float32)
    return by_row, by_col


def _na_bias_tables(rpb, rows):
    by_row, by_col = _na_onehots(rows)
    H = rpb.shape[0]
    e1 = jnp.einsum('hij,ckj->hick', rpb, by_col, precision=lax.Precision.HIGHEST)
    e1 = jnp.where((by_col.sum(-1) > 0)[None, None], e1, -jnp.inf)
    none = jnp.full((H, GRID_W, GRID_W), -jnp.inf, _F32)
    pick, valid = by_row.argmax(-1), by_row.sum(-1) > 0
    tabs = []
    for z in range(3):
        bands = [jnp.concatenate([e1[:, pick[z, q, r]] if valid[z, q, r] else none for r in range(NA_WIN_ROWS)],
                                 axis=-1) for q in range(NA_GROUP)]
        tabs.append(jnp.stack(bands, axis=1).reshape(H, NA_GROUP * GRID_W, NA_WIN_ROWS * GRID_W))
    return jnp.stack(tabs)


def _na_bias_grad(dtab, rows):
    by_row, by_col = _na_onehots(rows)
    H = dtab.shape[1]
    pick, valid = by_row.argmax(-1), by_row.sum(-1) > 0
    d6 = dtab.reshape(3, H, NA_GROUP, GRID_W, NA_WIN_ROWS, GRID_W)
    slabs = [[] for _ in range(2 * NA_ROWS - 1)]
    for z in range(3):
        for q in range(NA_GROUP):
            for r in range(NA_WIN_ROWS):
                if valid[z, q, r]:
                    slabs[pick[z, q, r]].append(d6[z, :, q, :, r, :])
    zero = jnp.zeros((H, GRID_W, GRID_W), _F32)
    de1 = jnp.stack([sum(s[1:], s[0]) if s else zero for s in slabs], axis=1)
    return jnp.einsum('hick,ckj->hij', de1, by_col, precision=lax.Precision.HIGHEST)


def _na_fwd(qn, kn, vb, tab, name):
    S, W = qn.shape
    rows, groups = _na_geometry(S)
    npair = W // _LANES
    nq = NA_GROUP * GRID_W
    nk = NA_WIN_ROWS * GRID_W
    sc = float(NA_HD) ** -0.5

    def body(q_ref, k_ref, v_ref, b_ref, o_ref):
        g = pl.program_id(1)
        k0 = pl.multiple_of(_na_win_start(g, rows) * GRID_W, GRID_W)
        kw = k_ref[pl.ds(k0, nk), :]
        vw = v_ref[pl.ds(k0, nk), :]
        qv = q_ref[...]
        lane = lax.broadcasted_iota(jnp.int32, (nq, _LANES), 1)
        outs = []
        for hh in range(2):
            mine = (lane >= hh * NA_HD) & (lane < (hh + 1) * NA_HD)
            s = _dg(jnp.where(mine, qv, jnp.zeros_like(qv)), kw, 1, 1) * sc + b_ref[0, hh]
            e = jnp.exp(s - jnp.max(s, axis=-1, keepdims=True))
            p = e / jnp.sum(e, axis=-1, keepdims=True)
            outs.append(_dg(p, vw, 1, 0))
        o_ref[...] = jnp.where(lane < NA_HD, outs[0], outs[1])

    return pl.pallas_call(
        body, name=name, out_shape=jax.ShapeDtypeStruct((S, W), _F32), grid=(npair, groups),
        in_specs=[pl.BlockSpec((nq, _LANES), lambda p, g: (g, p)),
                  pl.BlockSpec((S, _LANES), lambda p, g: (0, p)),
                  pl.BlockSpec((S, _LANES), lambda p, g: (0, p)),
                  pl.BlockSpec((1, 2, nq, nk), lambda p, g: (_na_class(g, groups), p, 0, 0))],
        out_specs=pl.BlockSpec((nq, _LANES), lambda p, g: (g, p)),
        compiler_params=_params(("parallel", "arbitrary")))(qn, kn, vb, tab)


def _na_bwd(qn, kn, vb, tab, do, name):
    S, W = qn.shape
    rows, groups = _na_geometry(S)
    npair = W // _LANES
    nq = NA_GROUP * GRID_W
    nk = NA_WIN_ROWS * GRID_W
    sc = float(NA_HD) ** -0.5

    def body(q_ref, k_ref, v_ref, b_ref, do_ref, dq_ref, dk_ref, dv_ref, db_ref):
        g = pl.program_id(1)

        @pl.when(g == 0)
        def _():
            dk_ref[...] = jnp.zeros_like(dk_ref)
            dv_ref[...] = jnp.zeros_like(dv_ref)

        @pl.when((g <= 1) | (g == groups - 1))
        def _():
            db_ref[...] = jnp.zeros_like(db_ref)

        k0 = pl.multiple_of(_na_win_start(g, rows) * GRID_W, GRID_W)
        kw = k_ref[pl.ds(k0, nk), :]
        vw = v_ref[pl.ds(k0, nk), :]
        qv = q_ref[...]
        dov = do_ref[...]
        lane = lax.broadcasted_iota(jnp.int32, (nq, _LANES), 1)
        dqs = []
        dkw = jnp.zeros((nk, _LANES), _F32)
        dvw = jnp.zeros((nk, _LANES), _F32)
        for hh in range(2):
            mine = (lane >= hh * NA_HD) & (lane < (hh + 1) * NA_HD)
            qm = jnp.where(mine, qv, jnp.zeros_like(qv))
            dom = jnp.where(mine, dov, 0.0)
            s = _dg(qm, kw, 1, 1) * sc + b_ref[0, hh]
            e = jnp.exp(s - jnp.max(s, axis=-1, keepdims=True))
            p = e / jnp.sum(e, axis=-1, keepdims=True)
            dp = _dg(dom, vw, 1, 1)
            ds = p * (dp - jnp.sum(p * dp, axis=-1, keepdims=True))
            db_ref[0, hh] += ds
            dqs.append(_dg(ds, kw, 1, 0) * sc)
            dkw = dkw + _dg(ds, qm, 0, 0) * sc
            dvw = dvw + _dg(p, dom, 0, 0)
        dq_ref[...] = jnp.where(lane < NA_HD, dqs[0], dqs[1])
        dk_ref[pl.ds(k0, nk), :] += dkw
        dv_ref[pl.ds(k0, nk), :] += dvw

    blk = pl.BlockSpec((nq, _LANES), lambda p, g: (g, p))
    full = pl.BlockSpec((S, _LANES), lambda p, g: (0, p))
    tspec = pl.BlockSpec((1, 2, nq, nk), lambda p, g: (_na_class(g, groups), p, 0, 0))
    return pl.pallas_call(
        body, name=name,
        out_shape=(jax.ShapeDtypeStruct((S, W), _F32), jax.ShapeDtypeStruct((S, W), _F32),
                   jax.ShapeDtypeStruct((S, W), _F32), jax.ShapeDtypeStruct(tab.shape, _F32)),
        grid=(npair, groups), in_specs=[blk, full, full, tspec, blk],
        out_specs=(blk, full, full, tspec),
        compiler_params=_params(("arbitrary", "arbitrary")))(qn, kn, vb, tab, do)


def _mem_fwd(qn, km, vm, name):
    S, W = qn.shape
    hd = W // MEM_HEADS
    tq = _pick(S, (512, 256))
    sc = float(hd) ** -0.5

    def body(q_ref, k_ref, v_ref, o_ref):
        for h in range(MEM_HEADS):
            cs = slice(h * hd, (h + 1) * hd)
            s = _dg(q_ref[:, cs], k_ref[:, cs], 1, 1) * sc
            e = jnp.exp(s - jnp.max(s, axis=-1, keepdims=True))
            p = e / jnp.sum(e, axis=-1, keepdims=True)
            o_ref[:, cs] = _dg(p, v_ref[:, cs], 1, 0)

    full = pl.BlockSpec(km.shape, lambda i: (0, 0))
    return pl.pallas_call(
        body, name=name, out_shape=jax.ShapeDtypeStruct((S, W), _F32), grid=(S // tq,),
        in_specs=[pl.BlockSpec((tq, W), lambda i: (i, 0)), full, full],
        out_specs=pl.BlockSpec((tq, W), lambda i: (i, 0)),
        compiler_params=_params(("parallel",)))(qn, km, vm)


def _mem_bwd(qn, km, vm, do, name):
    S, W = qn.shape
    hd = W // MEM_HEADS
    tq = _pick(S, (512, 256))
    sc = float(hd) ** -0.5

    def body(q_ref, k_ref, v_ref, do_ref, dq_ref, dk_ref, dv_ref):
        @pl.when(pl.program_id(0) == 0)
        def _():
            dk_ref[...] = jnp.zeros_like(dk_ref)
            dv_ref[...] = jnp.zeros_like(dv_ref)

        for h in range(MEM_HEADS):
            cs = slice(h * hd, (h + 1) * hd)
            qh = q_ref[:, cs]
            kh = k_ref[:, cs]
            doh = do_ref[:, cs]
            s = _dg(qh, kh, 1, 1) * sc
            e = jnp.exp(s - jnp.max(s, axis=-1, keepdims=True))
            p = e / jnp.sum(e, axis=-1, keepdims=True)
            dp = _dg(doh, v_ref[:, cs], 1, 1)
            ds = p * (dp - jnp.sum(p * dp, axis=-1, keepdims=True))
            dq_ref[:, cs] = _dg(ds, kh, 1, 0) * sc
            dk_ref[:, cs] += _dg(ds, qh, 0, 0) * sc
            dv_ref[:, cs] += _dg(p, doh, 0, 0)

    full = pl.BlockSpec(km.shape, lambda i: (0, 0))
    row = pl.BlockSpec((tq, W), lambda i: (i, 0))
    return pl.pallas_call(
        body, name=name,
        out_shape=(jax.ShapeDtypeStruct((S, W), _F32), jax.ShapeDtypeStruct(km.shape, _F32),
                   jax.ShapeDtypeStruct(km.shape, _F32)),
        grid=(S // tq,), in_specs=[row, full, full, row], out_specs=(row, full, full),
        compiler_params=_params(("arbitrary",)))(qn, km, vm, do)


def _place():
    x, y, c = lax.axis_index("x"), lax.axis_index("y"), lax.axis_index("c")
    chips = [(1 - x, y), (x, 1 - y), (1 - x, 1 - y)]
    return x, y, c, chips


def _exchange(rider, name):
    ri = len(rider.ins)
    ro = len(rider.outs)

    def body(*refs):
        ins, outs, sems = refs[:ri], refs[ri:ri + ro], refs[ri + ro:]
        rider.start(ins, outs, sems)
        rider.finish(ins, outs, sems)

    return pl.pallas_call(
        body, name=name, out_shape=tuple(rider.outs), in_specs=[_ANY] * ri, out_specs=(_ANY,) * ro,
        scratch_shapes=rider.sems, input_output_aliases={i: o for i, o in rider.aliases},
        compiler_params=_params())(*rider.ins)


def _pair_sems(*shape):
    return [pltpu.SemaphoreType.DMA(shape), pltpu.SemaphoreType.DMA(shape)]


def _gather_ici(shards):
    n = len(shards)

    def copies(w, g, sems):
        send, recv = sems
        x, y, c, chips = _place()
        me = 2 * x + y
        out, back = [], []
        for t in range(n):
            for j, (cx, cy) in enumerate(chips):
                out.append(pltpu.make_async_remote_copy(
                    src_ref=w[t].at[c], dst_ref=g[t].at[me, c], send_sem=send.at[t, j], recv_sem=recv.at[t, j],
                    device_id=(cx, cy, c), device_id_type=_MESH))
                back.append(functools.partial(
                    pltpu.make_async_remote_copy,
                    src_ref=w[t].at[c], dst_ref=g[t].at[2 * cx + cy, c], send_sem=send.at[t, j],
                    recv_sem=recv.at[t, j], device_id=(cx, cy, c), device_id_type=_MESH))
        return out, back

    def start(w, g, sems):
        for cp in copies(w, g, sems)[0]:
            cp.start()

    def finish(w, g, sems):
        out, back = copies(w, g, sems)
        for make in back:
            make().wait_recv()
        for cp in out:
            cp.wait_send()

    return _Rider(shards, [jax.ShapeDtypeStruct((4,) + s.shape, s.dtype) for s in shards], _pair_sems(n, 3),
                  start, finish)


def _gather_d2d(gs):
    n = len(gs)

    def copies(g, sems):
        send, recv = sems
        x, y, c, chips = _place()
        out, back = [], []
        for t in range(n):
            for j, (cx, cy) in enumerate(chips):
                mine, theirs = g[t].at[2 * cx + cy, c], g[t].at[2 * cx + cy, 1 - c]
                out.append(pltpu.make_async_remote_copy(
                    src_ref=mine, dst_ref=mine, send_sem=send.at[t, j], recv_sem=recv.at[t, j],
                    device_id=(x, y, 1 - c), device_id_type=_MESH))
                back.append(functools.partial(
                    pltpu.make_async_remote_copy,
                    src_ref=mine, dst_ref=theirs, send_sem=send.at[t, j], recv_sem=recv.at[t, j],
                    device_id=(x, y, 1 - c), device_id_type=_MESH))
        return out, back

    def start(_, g, sems):
        for cp in copies(g, sems)[0]:
            cp.start()

    def finish(_, g, sems):
        out, back = copies(g, sems)
        for make in back:
            make().wait_recv()
        for cp in out:
            cp.wait_send()

    return _Rider(gs, [jax.ShapeDtypeStruct(g.shape, g.dtype) for g in gs], _pair_sems(n, 3), start, finish,
                  aliases=[(t, t) for t in range(n)])


def _swap_halves(ts):
    n = len(ts)

    def copies(t_in, land, sems):
        send, recv = sems
        x, y, c, _ = _place()
        return [pltpu.make_async_remote_copy(
            src_ref=t_in[t].at[:, 1 - c], dst_ref=land[t], send_sem=send.at[t], recv_sem=recv.at[t],
            device_id=(x, y, 1 - c), device_id_type=_MESH) for t in range(n)]

    def start(t_in, land, sems):
        for cp in copies(t_in, land, sems):
            cp.start()

    def finish(t_in, land, sems):
        for cp in copies(t_in, land, sems):
            cp.wait()

    return _Rider(ts, [jax.ShapeDtypeStruct((4,) + t.shape[2:], t.dtype) for t in ts], _pair_sems(n), start, finish)


def _scatter_chips(ps):
    n = len(ps)

    def copies(p, land, sems):
        send, recv = sems
        x, y, c, chips = _place()
        me = 2 * x + y
        out, back = [], []
        for t in range(n):
            for j, (cx, cy) in enumerate(chips):
                out.append(pltpu.make_async_remote_copy(
                    src_ref=p[t].at[2 * cx + cy], dst_ref=land[t].at[me], send_sem=send.at[t, j],
                    recv_sem=recv.at[t, j], device_id=(cx, cy, c), device_id_type=_MESH))
                back.append(functools.partial(
                    pltpu.make_async_remote_copy,
                    src_ref=p[t].at[me], dst_ref=land[t].at[2 * cx + cy], send_sem=send.at[t, j],
                    recv_sem=recv.at[t, j], device_id=(cx, cy, c), device_id_type=_MESH))
        return out, back

    def start(p, land, sems):
        for cp in copies(p, land, sems)[0]:
            cp.start()

    def finish(p, land, sems):
        out, back = copies(p, land, sems)
        for make in back:
            make().wait_recv()
        for cp in out:
            cp.wait_send()

    return _Rider(ps, [jax.ShapeDtypeStruct(t.shape, t.dtype) for t in ps], _pair_sems(n, 3), start, finish)


def _swap_reduced(rs):
    n = len(rs)

    def copies(out, sems):
        send, recv = sems
        x, y, c, _ = _place()
        return [pltpu.make_async_remote_copy(
            src_ref=out[t].at[c], dst_ref=out[t].at[c], send_sem=send.at[t], recv_sem=recv.at[t],
            device_id=(x, y, 1 - c), device_id_type=_MESH) for t in range(n)]

    def start(_, out, sems):
        for cp in copies(out, sems):
            cp.start()

    def finish(_, out, sems):
        for cp in copies(out, sems):
            cp.wait()

    return _Rider(rs, [jax.ShapeDtypeStruct(t.shape, t.dtype) for t in rs], _pair_sems(n), start, finish,
                  aliases=[(t, t) for t in range(n)])


def _all_devices(v, reduce, name):
    rows = v.shape[0]

    def body(v_ref, o_ref, *rest):
        if reduce:
            all_ref, send, recv = rest
        else:
            send, recv = rest
            all_ref = o_ref
        x, y, c, _ = _place()
        me = 4 * x + 2 * y + c
        all_ref[me] = v_ref[...]
        cps = []
        for k in range(1, 8):
            fx, fy, fc = (k >> 2) & 1, (k >> 1) & 1, k & 1
            to = (x ^ fx, y ^ fy, c ^ fc)
            cps.append(pltpu.make_async_remote_copy(
                src_ref=v_ref, dst_ref=all_ref.at[me], send_sem=send.at[k - 1], recv_sem=recv.at[k - 1],
                device_id=to, device_id_type=_MESH))
        for cp in cps:
            cp.start()
        for k in range(1, 8):
            fx, fy, fc = (k >> 2) & 1, (k >> 1) & 1, k & 1
            frm = 4 * (x ^ fx) + 2 * (y ^ fy) + (c ^ fc)
            pltpu.make_async_remote_copy(
                src_ref=v_ref, dst_ref=all_ref.at[frm], send_sem=send.at[k - 1], recv_sem=recv.at[k - 1],
                device_id=(x, y, c), device_id_type=_MESH).wait_recv()
        for cp in cps:
            cp.wait_send()
        if reduce:
            acc = all_ref[0]
            for d in range(1, 8):
                acc = acc + all_ref[d]
            o_ref[...] = acc

    vm = pl.BlockSpec(memory_space=pltpu.VMEM)
    if reduce:
        out_shape = jax.ShapeDtypeStruct((rows, _LANES), _F32)
        scratch = [pltpu.VMEM((8, rows, _LANES), _F32)]
    else:
        out_shape = jax.ShapeDtypeStruct((8, rows, _LANES), _F32)
        scratch = []
    return pl.pallas_call(
        body, name=name, out_shape=out_shape, in_specs=[vm], out_specs=vm,
        scratch_shapes=scratch + [pltpu.SemaphoreType.DMA((7,)), pltpu.SemaphoreType.DMA((7,))],
        compiler_params=_params())(v)


def _weight_halves(shards):
    return [shards[k].reshape((2, shards[k].shape[0] // 2) + shards[k].shape[1:]) for k in BIG]


def _gather_layer_weights(halves, name):
    g1 = _exchange(_gather_ici(halves), name + "_ici")
    g2 = _exchange(_gather_d2d(g1), name + "_d2d")
    return _assemble_weights(g2, halves)


def _assemble_weights(got, halves):
    chip = 2 * lax.axis_index("x") + lax.axis_index("y")
    full = {}
    for k, g, own in zip(BIG, got, halves):
        g = lax.dynamic_update_slice(g, own[None], (chip, 0, 0, 0))
        _, _, r, c = g.shape
        if k in ('w_in', 'ffn_w13'):
            full[k] = g
        else:
            full[k] = g.reshape(8 * r, c)
    return full


def _grads_as_shards(grads):
    ts = []
    for k in BIG:
        g = grads[k]
        if k in ('w_in', 'ffn_w13'):
            ts.append(g)
        else:
            r, c = g.shape[0] // 8, g.shape[1]
            ts.append(g.reshape(4, 2, r, c))
    return ts


def _pair_sums(ts, landed, cidx):
    return [_add_pair(t, la, cidx, "reduce_add2_" + k) for k, t, la in zip(BIG, ts, landed)]


def _chip_sums(partial, slots, cidx):
    chip = (2 * lax.axis_index("x") + lax.axis_index("y")).astype(jnp.int32).reshape(1)
    return [_add_four(p, s, chip, cidx, "reduce_add4_" + k) for k, p, s in zip(BIG, partial, slots)]


def _assemble_shards(joined):
    return {k: j.reshape((2 * j.shape[1], j.shape[2])) for k, j in zip(BIG, joined)}


def _reduce_layer_grads(ts, cidx, name):
    landed = _exchange(_swap_halves(ts), name + "_swap")
    partial = _pair_sums(ts, landed, cidx)
    slots = _exchange(_scatter_chips(partial), name + "_scatter")
    reduced = _chip_sums(partial, slots, cidx)
    return _assemble_shards(_exchange(_swap_reduced(reduced), name + "_join"))


def _pack_rows(arrs):
    parts, spans = [], []
    off = 0
    for a in arrs:
        n = int(np.prod(a.shape))
        pad = (-n) % (8 * _LANES)
        parts.append(jnp.pad(a.reshape(-1), (0, pad)))
        spans.append((off, n, a.shape))
        off += n + pad
    return jnp.concatenate(parts).reshape(-1, _LANES), spans


def _unpack_rows(packed, spans):
    flat = packed.reshape(-1)
    return [flat[o:o + n].reshape(shape) for o, n, shape in spans]


def _in_split(d):
    dk, dv, w = d // 4, d // 2, d // 4
    names = [('g_q', dk), ('g_k', dk), ('g_v', dv), ('g_r', dv), ('g_f', GLA_GATE_RANK), ('g_b', GLA_GATE_RANK),
             ('n_q', w), ('n_k', w), ('n_v', w), ('m_q', w)]
    out, off = {}, 0
    for nme, wd in names:
        out[nme] = (off, wd)
        off += wd
    return out


def _w_in_from_shards(g):
    _, _, R, C = g.shape
    sp = _in_split(2 * R)
    f0, n0 = sp['g_f'][0], sp['n_q'][0]

    def cols(lo, hi, h):
        out = []
        for j in range(4):
            a, b = max(lo, j * C), min(hi, (j + 1) * C)
            if a < b:
                out.append(g[j, h, :, a - j * C:b - j * C])
        return out

    main = jnp.concatenate([jnp.concatenate(cols(0, f0, h) + cols(n0, 4 * C, h), axis=1) for h in range(2)], axis=0)
    gate = jnp.concatenate([jnp.concatenate(cols(f0, n0, h), axis=1) for h in range(2)], axis=0)
    return main, jnp.pad(gate, ((0, 0), (0, _LANES - (n0 - f0))))


def _w_in_to_shards(main, gate):
    R = main.shape[0] // 2
    sp = _in_split(2 * R)
    f0, n0 = sp['g_f'][0], sp['n_q'][0]
    C = (main.shape[1] + n0 - f0) // 4

    def cols(lo, hi, h):
        rows = slice(h * R, (h + 1) * R)
        out = []
        for src, a, b, shift in ((main, 0, f0, 0), (gate, f0, n0, -f0), (main, n0, 4 * C, f0 - n0)):
            lo2, hi2 = max(lo, a), min(hi, b)
            if lo2 < hi2:
                out.append(src[rows, lo2 + shift:hi2 + shift])
        return out

    return jnp.stack([jnp.stack([jnp.concatenate(cols(j * C, (j + 1) * C, h), axis=1) for h in range(2)])
                      for j in range(4)])


def _gate_weight(wg2_f, wg2_b):
    r, dk = wg2_f.shape
    top = jnp.concatenate([wg2_f, jnp.zeros_like(wg2_f)], axis=1)
    mid = jnp.concatenate([jnp.zeros_like(wg2_b), wg2_b], axis=1)
    return jnp.concatenate([top, mid, jnp.zeros((_LANES - 2 * r, 2 * dk), wg2_f.dtype)], axis=0)


def _layer_fwd(l, x, mem, w, full, tab, next_halves=None):
    S, D = x.shape
    dk, dv, nw = D // 4, D // 2, D // 4
    sv = {}
    w_main, w_gate = _w_in_from_shards(full['w_in'])
    sv['w_main'], sv['w_gate'] = w_main, w_gate
    xn, r1 = _rmsnorm_fwd(x, w['attn_norm'][None], f"rms1_fwd")
    if next_halves is None:
        proj = _mm(xn, w_main, 'nn', "mm_proj")
    else:
        proj, g_a = _mm(xn, w_main, 'nn', "mm_proj_gather", rider=_gather_ici(next_halves[:3]))
    gfb = _mm(xn, w_gate, 'nn', "mm_gate")
    sv.update(x=x, xn=xn, r1=r1, proj=proj, gfb=gfb)
    wg = _gate_weight(w['gla_wg2_f'], w['gla_wg2_b']).astype(_BF)
    bg = jnp.concatenate([w['gla_bg_f'], w['gla_bg_b']])[None]
    o_f, st_f = _gla_fwd(proj, gfb, wg, bg, False, "gla_fwd_f")
    o_b, st_b = _gla_fwd(proj, gfb, wg, bg, True, "gla_fwd_b")
    y_gla = _gla_post_fwd(o_f, o_b, proj, (2 * dk) // dv + 1, w['gla_out_norm'][None], "gla_post_fwd")
    sv.update(wg=wg, bg=bg, o_f=o_f, o_b=o_b, st_f=st_f, st_b=st_b)
    c0 = (2 * dk + 2 * dv) // nw
    qn = _headnorm_fwd(proj, c0, nw, NA_HD, jnp.tile(w['na_q_norm'], nw // NA_HD)[None], "na_qnorm_fwd")
    kn = _headnorm_fwd(proj, c0 + 1, nw, NA_HD, jnp.tile(w['na_k_norm'], nw // NA_HD)[None], "na_knorm_fwd")
    vb = proj[:, (c0 + 2) * nw:(c0 + 3) * nw].astype(_BF)
    o_na = _na_fwd(qn, kn, vb, tab, "na_fwd")
    y_na = _headnorm_fwd(o_na, 0, nw, NA_HD, w['na_out_norm'][None], "na_onorm_fwd")
    sv.update(qn=qn, kn=kn, vb=vb, o_na=o_na)
    mhd = nw // MEM_HEADS
    mqn = _headnorm_fwd(proj, c0 + 3, nw, mhd, jnp.tile(w['mem_q_norm'], MEM_HEADS)[None], "mem_qnorm_fwd")
    memn, rm = _rmsnorm_fwd(mem, w['mem_norm'][None], "mem_rms_fwd")
    kv = _mm(memn, full['mem_wkv'], 'nn', "mm_memkv")
    km = _headnorm_fwd(kv, 0, nw, mhd, jnp.tile(w['mem_k_norm'], MEM_HEADS)[None], "mem_knorm_fwd")
    vm = kv[:, nw:].astype(_BF)
    o_mem = _mem_fwd(mqn, km, vm, "mem_fwd")
    y_mem = _headnorm_fwd(o_mem, 0, nw, mhd, w['mem_out_norm'][None], "mem_onorm_fwd")
    sv.update(mqn=mqn, memn=memn, rm=rm, kv=kv, km=km, vm=vm, o_mem=o_mem)
    y = jnp.concatenate([y_gla, y_na, y_mem], axis=1)
    x1 = _mm(y, full['w_out'], 'nn', "mm_out", res=x)
    h, r2 = _rmsnorm_fwd(x1, w['ffn_norm'][None], "rms2_fwd")
    if next_halves is None:
        a, gate, up = _ffn_up_fwd(h, full['ffn_w13'], "ffn_up_fwd")
        x2 = _mm(a, full['ffn_w2'], 'nn', "mm_w2", res=x1)
        got = None
    else:
        (a, gate, up), g_b = _ffn_up_fwd(h, full['ffn_w13'], "ffn_up_fwd_gather", _gather_ici(next_halves[3:]))
        x2, got = _mm(a, full['ffn_w2'], 'nn', "mm_w2_gather", res=x1, rider=_gather_d2d(list(g_a) + list(g_b)))
    sv.update(y=y, x1=x1, h=h, r2=r2, gate=gate, up=up, a=a)
    return x2, sv, got


def _layer_bwd(l, dx2, dx2b, mem, w, full, tab, sv, pending=None, cidx=None):
    S, D = dx2.shape
    dk, dv, nw = D // 4, D // 2, D // 4
    gb, gs = {}, {}
    if pending is None:
        dgate, dup = _ffn_up_bwd(dx2b, full['ffn_w2'], sv['gate'], sv['up'], "ffn_up_bwd")
    else:
        (dgate, dup), landed = _ffn_up_bwd(dx2b, full['ffn_w2'], sv['gate'], sv['up'], "ffn_up_bwd_swap",
                                           _swap_halves(pending))
        partial = _pair_sums(pending, landed, cidx)
    gb['ffn_w2'] = _mm(sv['a'], dx2b, 'tn', "mm_dw2")
    dh = _ffn_up_dh(dgate, dup, full['ffn_w13'], "ffn_up_dh")
    if pending is None:
        gb['ffn_w13'] = _ffn_up_dw(sv['h'], dgate, dup, "ffn_up_dw")
    else:
        gb['ffn_w13'], slots = _ffn_up_dw(sv['h'], dgate, dup, "ffn_up_dw_scatter", _scatter_chips(partial))
        reduced = _chip_sums(partial, slots, cidx)
    dx1, dx1b, g = _rmsnorm_bwd(sv['x1'], sv['r2'], w['ffn_norm'][None], dh, dx2, "rms2_bwd")
    gs['ffn_norm'] = g[0]
    dy = _mm(dx1b, full['w_out'], 'nt', "mm_dy")
    gb['w_out'] = _mm(sv['y'], dx1b, 'tn', "mm_dwout")
    c0 = (2 * dk + 2 * dv) // nw
    mhd = nw // MEM_HEADS
    do_mem, g = _headnorm_bwd(sv['o_mem'], 0, nw, mhd, w['mem_out_norm'][None], dy, (dv + nw) // nw, "mem_onorm_bwd")
    gs['mem_out_norm'] = g[0]
    dmqn, dkm, dvm = _mem_bwd(sv['mqn'], sv['km'], sv['vm'], do_mem, "mem_bwd")
    dmq, g = _headnorm_bwd(sv['proj'], c0 + 3, nw, mhd, jnp.tile(w['mem_q_norm'], MEM_HEADS)[None], dmqn, 0,
                           "mem_qnorm_bwd", _BF)
    gs['mem_q_norm'] = g[0].reshape(MEM_HEADS, mhd).sum(0)
    dkvk, g = _headnorm_bwd(sv['kv'], 0, nw, mhd, jnp.tile(w['mem_k_norm'], MEM_HEADS)[None], dkm, 0,
                            "mem_knorm_bwd")
    gs['mem_k_norm'] = g[0].reshape(MEM_HEADS, mhd).sum(0)
    dkv = jnp.concatenate([dkvk, dvm], axis=1).astype(_BF)
    gb['mem_wkv'] = _mm(sv['memn'], dkv, 'tn', "mm_dwkv")
    dmemn = _mm(dkv, full['mem_wkv'], 'nt', "mm_dmemn")
    _, _, g = _rmsnorm_bwd(mem, sv['rm'], w['mem_norm'][None], dmemn, None, "mem_rms_bwd")
    gs['mem_norm'] = g[0]
    do_na, g = _headnorm_bwd(sv['o_na'], 0, nw, NA_HD, w['na_out_norm'][None], dy, dv // nw, "na_onorm_bwd")
    gs['na_out_norm'] = g[0]
    dqn, dkn, dnv, dtab = _na_bwd(sv['qn'], sv['kn'], sv['vb'], tab, do_na, "na_bwd")
    gs['na_rpb'] = _na_bias_grad(dtab, S // GRID_W)
    dnq, g = _headnorm_bwd(sv['proj'], c0, nw, NA_HD, jnp.tile(w['na_q_norm'], nw // NA_HD)[None], dqn, 0,
                           "na_qnorm_bwd", _BF)
    gs['na_q_norm'] = g[0].reshape(nw // NA_HD, NA_HD).sum(0)
    dnk, g = _headnorm_bwd(sv['proj'], c0 + 1, nw, NA_HD, jnp.tile(w['na_k_norm'], nw // NA_HD)[None], dkn, 0,
                           "na_knorm_bwd", _BF)
    gs['na_k_norm'] = g[0].reshape(nw // NA_HD, NA_HD).sum(0)
    do_gla, dgr, g = _gla_post_bwd(sv['o_f'], sv['o_b'], sv['proj'], (2 * dk) // dv + 1, w['gla_out_norm'][None], dy,
                                   "gla_post_bwd")
    gs['gla_out_norm'] = g[0]
    dq1, dk1, dv1, dz_f = _gla_bwd(sv['proj'], sv['gfb'], sv['wg'], sv['bg'], sv['st_f'], do_gla, False, None,
                                   "gla_bwd_f")
    dgq, dgk, dgv, dz_b = _gla_bwd(sv['proj'], sv['gfb'], sv['wg'], sv['bg'], sv['st_b'], do_gla, True,
                                   (dq1, dk1, dv1), "gla_bwd_b")
    dz = jnp.concatenate([dz_f, dz_b], axis=1).astype(_BF)
    ones_lane = (jnp.arange(_LANES) == 2 * GLA_GATE_RANK)[None]
    gfb_aug = jnp.where(ones_lane, 1.0, sv['gfb']).astype(_BF)
    dwg = _mm(gfb_aug, dz, 'tn', "mm_dwg")
    r16 = GLA_GATE_RANK
    gs['gla_wg2_f'] = dwg[:r16, :dk]
    gs['gla_wg2_b'] = dwg[r16:2 * r16, dk:]
    gs['gla_bg_f'] = dwg[2 * r16, :dk]
    gs['gla_bg_b'] = dwg[2 * r16, dk:]
    dgfb = _mm(dz, sv['wg'], 'nt', "mm_dgfb", out_dtype=_BF)
    dproj = jnp.concatenate([dgq.astype(_BF), dgk.astype(_BF), dgv.astype(_BF), dgr, dnq, dnk, dnv.astype(_BF), dmq],
                            axis=1)
    t = _mm(dgfb, sv['w_gate'], 'nt', "mm_dxn_gate")
    if pending is None:
        dxn = _mm(dproj, sv['w_main'], 'nt', "mm_dxn", res=t)
        done = None
    else:
        dxn, joined = _mm(dproj, sv['w_main'], 'nt', "mm_dxn_join", res=t, rider=_swap_reduced(reduced))
        done = _assemble_shards(joined)
    dw_main = _mm(sv['xn'], dproj, 'tn', "mm_dwmain")
    dw_gate = _mm(sv['xn'], dgfb, 'tn', "mm_dwgate")
    gb['w_in'] = _w_in_to_shards(dw_main, dw_gate)
    dx, dxb, g = _rmsnorm_bwd(sv['x'], sv['r1'], w['attn_norm'][None], dxn, dx1, "rms1_bwd")
    gs['attn_norm'] = g[0]
    return dx, dxb, gb, gs, done


def kernel(x, mem, attn_norm, w_in, gla_wg2_f, gla_bg_f, gla_wg2_b, gla_bg_b, gla_out_norm, na_q_norm, na_k_norm, na_rpb, na_out_norm, mem_norm, mem_wkv, mem_q_norm, mem_k_norm, mem_out_norm, w_out, ffn_norm, ffn_w13, ffn_w2, loss_target, m_attn_norm, m_w_in, m_gla_wg2_f, m_gla_bg_f, m_gla_wg2_b, m_gla_bg_b, m_gla_out_norm, m_na_q_norm, m_na_k_norm, m_na_rpb, m_na_out_norm, m_mem_norm, m_mem_wkv, m_mem_q_norm, m_mem_k_norm, m_mem_out_norm, m_w_out, m_ffn_norm, m_ffn_w13, m_ffn_w2, v_attn_norm, v_w_in, v_gla_wg2_f, v_gla_bg_f, v_gla_wg2_b, v_gla_bg_b, v_gla_out_norm, v_na_q_norm, v_na_k_norm, v_na_rpb, v_na_out_norm, v_mem_norm, v_mem_wkv, v_mem_q_norm, v_mem_k_norm, v_mem_out_norm, v_w_out, v_ffn_norm, v_ffn_w13, v_ffn_w2):
    args = locals()
    W = {k: args[k] for k in WEIGHTS}
    M = {k: args['m_' + k] for k in WEIGHTS}
    V = {k: args['v_' + k] for k in WEIGHTS}
    depth = attn_norm.shape[0]
    xs, mems, tgt = x[0], mem[0], loss_target[0]
    cidx = lax.axis_index("c").astype(jnp.int32).reshape(1)

    gate_rows, gate_spans = _pack_rows([W[k] for k in SMALL_SHARDED])
    gate_all = _all_devices(gate_rows, False, "gate_weights_gather")
    gate_full = {}
    for i, k in enumerate(SMALL_SHARDED):
        per_chip = [_unpack_rows(gate_all[2 * j], gate_spans)[i] for j in range(4)]
        gate_full[k] = jnp.concatenate(per_chip, axis=-1)

    small = [k for k in WEIGHTS if k not in BIG]

    def layer_small(l):
        d = {k: W[k][l] for k in small if k not in SMALL_SHARDED}
        d.update({k: gate_full[k][l] for k in SMALL_SHARDED})
        return d

    saved, fulls, tabs = [], [], []
    cur = xs
    halves = [_weight_halves({k: W[k][l].astype(_BF) for k in BIG}) for l in range(depth)]
    full = _gather_layer_weights(halves[0], "gather_weights")
    for l in range(depth):
        tab = _na_bias_tables(W['na_rpb'][l], xs.shape[0] // GRID_W)
        nxt = halves[l + 1] if l + 1 < depth else None
        cur, sv, got = _layer_fwd(l, cur, mems, layer_small(l), full, tab, nxt)
        saved.append(sv)
        fulls.append(full)
        tabs.append(tab)
        if nxt is not None:
            full = _assemble_weights(got, nxt)
    loss_tile, dy, dyb = _loss_head(cur, tgt, "loss_head")
    loss = lax.psum(loss_tile[0, 0], ("x", "y", "c"))

    big_grads = [None] * depth
    small_grads = [None] * depth
    pending = None
    for l in range(depth - 1, -1, -1):
        dy, dyb, gb, gs, done = _layer_bwd(l, dy, dyb, mems, layer_small(l), fulls[l], tabs[l], saved[l], pending, cidx)
        if pending is not None:
            big_grads[l + 1] = done
        pending = _grads_as_shards(gb)
        small_grads[l] = gs
    big_grads[0] = _reduce_layer_grads(pending, cidx, "reduce_grads")
    grad_x = dy[None]

    small_stack = [jnp.stack([small_grads[l][k] for l in range(depth)]) for k in small]
    packed, spans = _pack_rows(small_stack)
    summed = _unpack_rows(_all_devices(packed, True, "small_grads_sum"), spans)
    G = dict(zip(small, summed))
    chip = 2 * lax.axis_index("x") + lax.axis_index("y")
    for k in SMALL_SHARDED:
        wdt = W[k].shape[-1]
        G[k] = lax.dynamic_slice_in_dim(G[k], chip * wdt, wdt, axis=2)
    for k in BIG:
        G[k] = jnp.stack([big_grads[l][k] for l in range(depth)])

    delta, new_m, new_v = {}, {}, {}
    for k in WEIGHTS:
        delta[k], new_m[k], new_v[k] = _adamw(W[k], G[k], M[k], V[k], "adamw_" + k)
    return (loss, grad_x, *[G[k] for k in WEIGHTS], *[delta[k] for k in WEIGHTS], *[new_m[k] for k in WEIGHTS],
            *[new_v[k] for k in WEIGHTS])
```

```python
import functools

import numpy as np
import jax
import jax.numpy as jnp
from jax import lax
from jax.experimental import pallas as pl
from jax.experimental.pallas import tpu as pltpu

_F32 = jnp.float32
_BF = jnp.bfloat16
_MESH = pl.DeviceIdType.MESH

_VMEM_LIMIT_BYTES = 56 * 1024 * 1024
_LANES = 128

RMS_EPS = 1e-6
GLA_HEADS = 4
GLA_GATE_RANK = 16
GLA_TAU = 16.0
GLA_CHUNK = 64
GRID_W = 64
NA_HD = 64
NA_ROWS = 8
NA_COLS = 16
MEM_HEADS = 4
ADAM_LR = 0.001
ADAM_B1 = 0.9
ADAM_B2 = 0.999
ADAM_EPS = 1e-08
ADAM_WD = 0.01
ADAM_STEP = 10

WEIGHTS = ['attn_norm', 'w_in', 'gla_wg2_f', 'gla_bg_f', 'gla_wg2_b', 'gla_bg_b', 'gla_out_norm', 'na_q_norm',
           'na_k_norm', 'na_rpb', 'na_out_norm', 'mem_norm', 'mem_wkv', 'mem_q_norm', 'mem_k_norm', 'mem_out_norm',
           'w_out', 'ffn_norm', 'ffn_w13', 'ffn_w2']
BIG = ['w_in', 'mem_wkv', 'w_out', 'ffn_w13', 'ffn_w2']
SMALL_SHARDED = ['gla_wg2_f', 'gla_wg2_b']


def _pick(n, cands):
    for c in cands:
        if n % c == 0:
            return c
    return n


def _row_block(rows, row_bytes, cap_bytes):
    best = 8
    for rb in range(8, rows + 1, 8):
        if rows % rb == 0 and rb * row_bytes <= cap_bytes:
            best = rb
    return best


def _params(sem=None):
    return pltpu.CompilerParams(dimension_semantics=sem, vmem_limit_bytes=_VMEM_LIMIT_BYTES)


def _dg(a, b, ca, cb):
    return lax.dot_general(a.astype(_BF), b.astype(_BF), (((ca,), (cb,)), ((), ())), preferred_element_type=_F32)


def _split(a):
    hi = a.astype(_BF)
    return hi, (a - hi.astype(_F32)).astype(_BF)


def _dg_hl(a, b, ca, cb):
    hi, lo = _split(a)
    return _dg(hi, b, ca, cb) + _dg(lo, b, ca, cb)


def _dg_lh(a, b, ca, cb):
    hi, lo = _split(b)
    return _dg(a, hi, ca, cb) + _dg(a, lo, ca, cb)


def _sigmoid(x):
    return 1.0 / (1.0 + jnp.exp(-x))


def _log_sigmoid(z):
    return jnp.minimum(z, 0.0) - jnp.log(1.0 + jnp.exp(-jnp.abs(z)))


def _block_diag(width, hd):
    i = np.arange(width) // hd
    return jnp.asarray((i[:, None] == i[None, :]).astype(np.float32), dtype=_BF)


class _Rider:
    def __init__(self, ins, outs, sems, start, finish, aliases=()):
        self.ins, self.outs, self.sems = list(ins), list(outs), list(sems)
        self.start, self.finish, self.aliases = start, finish, tuple(aliases)


def _call(core, name, grid, in_specs, out_specs, out_shape, scratch, sem, args, rider=None):
    out_specs, out_shape = tuple(out_specs), tuple(out_shape)
    if rider is None:
        return pl.pallas_call(core, name=name, out_shape=out_shape, grid=grid, in_specs=list(in_specs),
                              out_specs=out_specs, scratch_shapes=list(scratch), compiler_params=_params(sem))(*args)
    ni, no, ns = len(in_specs), len(out_specs), len(scratch)
    ri, ro = len(rider.ins), len(rider.outs)

    def body(*refs):
        ins, rins = refs[:ni], refs[ni:ni + ri]
        outs, routs = refs[ni + ri:ni + ri + no], refs[ni + ri + no:ni + ri + no + ro]
        scr, rsem = refs[ni + ri + no + ro:ni + ri + no + ro + ns], refs[ni + ri + no + ro + ns:]
        first = pl.program_id(0) == 0
        last = pl.program_id(0) == grid[0] - 1
        for ax in range(1, len(grid)):
            first = first & (pl.program_id(ax) == 0)
            last = last & (pl.program_id(ax) == grid[ax] - 1)

        @pl.when(first)
        def _():
            rider.start(rins, routs, rsem)

        core(*ins, *outs, *scr)

        @pl.when(last)
        def _():
            rider.finish(rins, routs, rsem)

    res = pl.pallas_call(
        body, name=name, out_shape=out_shape + tuple(rider.outs), grid=grid,
        in_specs=list(in_specs) + [_ANY] * ri, out_specs=out_specs + (_ANY,) * ro,
        scratch_shapes=list(scratch) + rider.sems,
        input_output_aliases={ni + i: no + o for i, o in rider.aliases},
        compiler_params=_params(("arbitrary",) * len(grid)))(*args, *rider.ins)
    return tuple(res[:no]), tuple(res[no:])


_ANY = pl.BlockSpec(memory_space=pl.ANY)


def _mm(a, b, mode, name, out_dtype=_F32, res=None, rider=None):
    def finish(r):
        return r[0] if rider is None else (r[0][0], r[1])

    if mode == 'nn':
        (M, K), N = a.shape, b.shape[1]
    elif mode == 'nt':
        (M, K), N = a.shape, b.shape[0]
    else:
        (K, M), N = a.shape, b.shape[1]
    ca, cb = {'nn': (1, 0), 'nt': (1, 1), 'tn': (0, 0)}[mode]
    has_res = res is not None
    args = (a, b) + ((res,) if has_res else ())
    out_shape = jax.ShapeDtypeStruct((M, N), out_dtype)
    if K <= _K_RESIDENT:
        tm = _pick(M, (1024, 512, 256, 128))
        tn = _pick(N, (1024, 512, 256, 128))
        a_spec = (pl.BlockSpec((K, tm), lambda i, j: (0, i)) if mode == 'tn'
                  else pl.BlockSpec((tm, K), lambda i, j: (i, 0)))
        b_spec = (pl.BlockSpec((tn, K), lambda i, j: (j, 0)) if mode == 'nt'
                  else pl.BlockSpec((K, tn), lambda i, j: (0, j)))
        o_spec = pl.BlockSpec((tm, tn), lambda i, j: (i, j))

        def body1(*refs):
            a_ref, b_ref = refs[:2]
            o_ref = refs[-1]
            out = _dg(a_ref[...], b_ref[...], ca, cb)
            if has_res:
                out = out + refs[2][...]
            o_ref[...] = out.astype(o_ref.dtype)

        return finish(_call(body1, name, (M // tm, N // tn), [a_spec, b_spec] + ([o_spec] if has_res else []),
                            [o_spec], [out_shape], [], ("parallel", "parallel"), args, rider))

    tm, tn, tk = _loop_tiles(M, N, K, has_res, out_dtype)
    nk = K // tk
    if mode == 'tn':
        a_spec = pl.BlockSpec((tk, tm), lambda i, j, k: (k, i))
    else:
        a_spec = pl.BlockSpec((tm, tk), lambda i, j, k: (i, k))
    if mode == 'nt':
        b_spec = pl.BlockSpec((tn, tk), lambda i, j, k: (j, k))
    else:
        b_spec = pl.BlockSpec((tk, tn), lambda i, j, k: (k, j))
    o_spec = pl.BlockSpec((tm, tn), lambda i, j, k: (i, j))
    in_out = out_dtype == _F32

    def body(*refs):
        a_ref, b_ref = refs[:2]
        o_ref = refs[-1] if in_out else refs[-2]
        acc = o_ref if in_out else refs[-1]
        k = pl.program_id(2)

        @pl.when(k == 0)
        def _():
            part = _dg(a_ref[...], b_ref[...], ca, cb)
            acc[...] = part + refs[2][...] if has_res else part

        @pl.when(k > 0)
        def _():
            acc[...] += _dg(a_ref[...], b_ref[...], ca, cb)

        if not in_out:
            @pl.when(k == nk - 1)
            def _():
                o_ref[...] = acc[...].astype(o_ref.dtype)

    return finish(_call(body, name, (M // tm, N // tn, nk), [a_spec, b_spec] + ([o_spec] if has_res else []),
                        [o_spec], [out_shape], [] if in_out else [pltpu.VMEM((tm, tn), _F32)],
                        ("parallel", "parallel", "arbitrary"), args, rider))


_K_RESIDENT = 2048
_LOOP_TILE_BYTES = 40 * 1024 * 1024
_LOOP_TILE_AREA = 1024 * 1024


def _loop_tiles(M, N, K, has_res, out_dtype):
    best, best_score = None, None
    obytes = 4 if out_dtype == _F32 else 2
    for tk in (2816, 2560, 2048, 1408, 1024, 512, 256, 128):
        if K % tk:
            continue
        for tm in (2048, 1408, 1024, 512, 256, 128):
            for tn in (2048, 1024, 512, 256, 128):
                if M % tm or N % tn:
                    continue
                need = 2 * 2 * tk * (tm + tn) + tm * tn * (2 * obytes + (8 if has_res else 0) + (0 if obytes == 4 else 4))
                score = (min(tm * tn, _LOOP_TILE_AREA), tk, tm * tn)
                if need <= _LOOP_TILE_BYTES and (best is None or score > best_score):
                    best, best_score = (tm, tn, tk), score
    return best


def _cols_to_shards(w):
    r, c = w.shape[0] // 2, w.shape[1] // 4
    return jnp.transpose(w.reshape(2, r, 4, c), (2, 0, 1, 3))


def _shards_to_cols(g):
    _, _, r, c = g.shape
    return jnp.transpose(g, (1, 2, 0, 3)).reshape(2 * r, 4 * c)


_FFN_SUB = 256


def _ffn_up_fwd(h, g13, name, rider=None):
    S = h.shape[0]
    _, _, R, C = g13.shape
    tm = _pick(S, (512, 256, 128))
    cw, nc = _ffn_cols(C)
    subs = [(s, min(s + _FFN_SUB, cw)) for s in range(0, cw, _FFN_SUB)]

    def body(h_ref, wg_ref, wu_ref, a_ref, g_ref, u_ref):
        h0, h1 = h_ref[:, :R], h_ref[:, R:]
        for lo, hi in subs:
            sl = slice(lo, hi)
            gv = _dg(h0, wg_ref[0, 0, :, sl], 1, 0) + _dg(h1, wg_ref[0, 1, :, sl], 1, 0)
            uv = _dg(h0, wu_ref[0, 0, :, sl], 1, 0) + _dg(h1, wu_ref[0, 1, :, sl], 1, 0)
            a_ref[:, sl] = (gv * _sigmoid(gv) * uv).astype(a_ref.dtype)
            g_ref[:, sl] = gv.astype(g_ref.dtype)
            u_ref[:, sl] = uv.astype(u_ref.dtype)

    out = jax.ShapeDtypeStruct((S, 2 * C), _BF)
    ospec = pl.BlockSpec((tm, cw), lambda j, i: (i, j))
    return _call(
        body, name, (2 * nc, S // tm),
        [pl.BlockSpec((tm, 2 * R), lambda j, i: (i, 0)),
         pl.BlockSpec((1, 2, R, cw), lambda j, i: (lax.div(j, nc), 0, 0, lax.rem(j, nc))),
         pl.BlockSpec((1, 2, R, cw), lambda j, i: (2 + lax.div(j, nc), 0, 0, lax.rem(j, nc)))],
        (ospec, ospec, ospec), (out, out, out), [], ("parallel", "parallel"), (h, g13, g13), rider)


def _ffn_cols(C):
    if C % (2 * _LANES) == 0:
        return C // 2, 2
    return C, 1


def _ffn_up_bwd(dxb, w2, gate, up, name, rider=None):
    S, D = dxb.shape
    F = w2.shape[0]
    tf = 2 * _FFN_SUB
    tm = _pick(S, (1024, 512, 256, 128))

    def body(d_ref, w_ref, g_ref, u_ref, dg_ref, du_ref):
        dv = d_ref[...]
        for s in range(tf // _FFN_SUB):
            sl = slice(s * _FFN_SUB, (s + 1) * _FFN_SUB)
            da = _dg(dv, w_ref[sl, :], 1, 1)
            gv = g_ref[:, sl].astype(_F32)
            uv = u_ref[:, sl].astype(_F32)
            sg = _sigmoid(gv)
            dg_ref[:, sl] = (da * uv * (sg * (1.0 + gv * (1.0 - sg)))).astype(dg_ref.dtype)
            du_ref[:, sl] = (da * (gv * sg)).astype(du_ref.dtype)

    out = jax.ShapeDtypeStruct((S, F), _BF)
    tile = pl.BlockSpec((tm, tf), lambda i, j: (i, j))
    return _call(
        body, name, (S // tm, F // tf),
        [pl.BlockSpec((tm, D), lambda i, j: (i, 0)), pl.BlockSpec((tf, D), lambda i, j: (j, 0)), tile, tile],
        (tile, tile), (out, out), [], ("parallel", "parallel"), (dxb, w2, gate, up), rider)


def _ffn_up_dh(dgate, dup, g13, name, rider=None):
    S = dgate.shape[0]
    _, _, R, C = g13.shape
    tm = _pick(S, (512, 256, 128))

    def body(dg_ref, du_ref, wg_ref, wu_ref, o_ref):
        j = pl.program_id(1)

        def part(hh):
            return _dg(dg_ref[...], wg_ref[0, hh], 1, 1) + _dg(du_ref[...], wu_ref[0, hh], 1, 1)

        @pl.when(j == 0)
        def _():
            for hh in range(2):
                o_ref[:, hh * R:(hh + 1) * R] = part(hh)

        @pl.when(j > 0)
        def _():
            for hh in range(2):
                o_ref[:, hh * R:(hh + 1) * R] += part(hh)

    cw, nc = _ffn_cols(C)
    tile = pl.BlockSpec((tm, cw), lambda i, j: (i, j))
    r = _call(
        body, name, (S // tm, 2 * nc),
        [tile, tile, pl.BlockSpec((1, 2, R, cw), lambda i, j: (lax.div(j, nc), 0, 0, lax.rem(j, nc))),
         pl.BlockSpec((1, 2, R, cw), lambda i, j: (2 + lax.div(j, nc), 0, 0, lax.rem(j, nc)))],
        [pl.BlockSpec((tm, 2 * R), lambda i, j: (i, 0))], [jax.ShapeDtypeStruct((S, 2 * R), _F32)], [],
        ("parallel", "arbitrary"), (dgate, dup, g13, g13), rider)
    return r[0] if rider is None else (r[0][0], r[1])


def _ffn_up_dw(hact, dgate, dup, name, rider=None):
    S, R2 = hact.shape
    R = R2 // 2
    C = dgate.shape[1] // 2
    tk = _pick(S, (1024, 512, 256, 128))
    nk = S // tk
    cw, nc = _ffn_cols(C)

    def body(h_ref, dg_ref, du_ref, o_ref):
        chip = pl.program_id(0)
        k = pl.program_id(3)

        def accumulate(d_ref):
            @pl.when(k == 0)
            def _():
                o_ref[0, 0] = _dg(h_ref[...], d_ref[...], 0, 0)

            @pl.when(k > 0)
            def _():
                o_ref[0, 0] += _dg(h_ref[...], d_ref[...], 0, 0)

        @pl.when(chip < 2)
        def _():
            accumulate(dg_ref)

        @pl.when(chip >= 2)
        def _():
            accumulate(du_ref)

    r = _call(
        body, name, (4, nc, 2, nk),
        [pl.BlockSpec((tk, R), lambda c, b, h, k: (k, h)),
         pl.BlockSpec((tk, cw), lambda c, b, h, k: (jnp.where(c < 2, k, 0), jnp.where(c < 2, c * nc + b, 0))),
         pl.BlockSpec((tk, cw), lambda c, b, h, k: (jnp.where(c >= 2, k, 0), jnp.where(c >= 2, (c - 2) * nc + b, 0)))],
        [pl.BlockSpec((1, 1, R, cw), lambda c, b, h, k: (c, h, 0, b))], [jax.ShapeDtypeStruct((4, 2, R, C), _F32)], [],
        ("parallel", "parallel", "parallel", "arbitrary"), (hact, dgate, dup), rider)
    return r[0] if rider is None else (r[0][0], r[1])


def _rmsnorm_fwd(x, g, name):
    S, D = x.shape
    ts = _pick(S, (256,))

    def body(x_ref, g_ref, o_ref, r_ref):
        xv = x_ref[...]
        r = lax.rsqrt(jnp.mean(xv * xv, axis=-1, keepdims=True) + RMS_EPS)
        o_ref[...] = (xv * r * g_ref[...]).astype(o_ref.dtype)
        r_ref[...] = r

    return pl.pallas_call(
        body, name=name,
        out_shape=(jax.ShapeDtypeStruct((S, D), _BF), jax.ShapeDtypeStruct((S, 1), _F32)),
        grid=(S // ts,),
        in_specs=[pl.BlockSpec((ts, D), lambda i: (i, 0)), pl.BlockSpec((1, D), lambda i: (0, 0))],
        out_specs=(pl.BlockSpec((ts, D), lambda i: (i, 0)), pl.BlockSpec((ts, 1), lambda i: (i, 0))),
        compiler_params=_params(("parallel",)))(x, g)


def _rmsnorm_bwd(x, r, g, dy, dres, name):
    S, D = x.shape
    ts = _pick(S, (256,))
    has_res = dres is not None

    def body(*refs):
        if has_res:
            x_ref, r_ref, g_ref, dy_ref, dr_ref, dx_ref, dxb_ref, dg_ref = refs
        else:
            x_ref, r_ref, g_ref, dy_ref, dx_ref, dxb_ref, dg_ref = refs
        rv = r_ref[...]
        n = x_ref[...] * rv
        dyv = dy_ref[...]
        dn = dyv * g_ref[...]
        c = jnp.mean(dn * n, axis=-1, keepdims=True)
        dx = rv * (dn - n * c)
        if has_res:
            dx = dx + dr_ref[...]
        dx_ref[...] = dx
        dxb_ref[...] = dx.astype(dxb_ref.dtype)

        @pl.when(pl.program_id(0) == 0)
        def _():
            dg_ref[...] = jnp.zeros_like(dg_ref)

        dg_ref[...] += jnp.sum(dyv * n, axis=0, keepdims=True)

    row = pl.BlockSpec((ts, D), lambda i: (i, 0))
    vec = pl.BlockSpec((1, D), lambda i: (0, 0))
    in_specs = [row, pl.BlockSpec((ts, 1), lambda i: (i, 0)), vec, row] + ([row] if has_res else [])
    args = (x, r, g, dy) + ((dres,) if has_res else ())
    return pl.pallas_call(
        body, name=name,
        out_shape=(jax.ShapeDtypeStruct((S, D), _F32), jax.ShapeDtypeStruct((S, D), _BF),
                   jax.ShapeDtypeStruct((1, D), _F32)),
        grid=(S // ts,), in_specs=in_specs, out_specs=(row, row, vec),
        compiler_params=_params(("arbitrary",)))(*args)


def _headnorm_fwd(t, cb, W, hd, g, name):
    S = t.shape[0]
    ts = _pick(S, (512, 256))
    bd = _block_diag(W, hd)

    def body(x_ref, g_ref, bd_ref, o_ref):
        xv = x_ref[...].astype(_F32)
        ms = _dg_hl(xv * xv, bd_ref[...], 1, 0) * (1.0 / hd)
        o_ref[...] = (xv * lax.rsqrt(ms + RMS_EPS) * g_ref[...]).astype(o_ref.dtype)

    return pl.pallas_call(
        body, name=name, out_shape=jax.ShapeDtypeStruct((S, W), _BF), grid=(S // ts,),
        in_specs=[pl.BlockSpec((ts, W), lambda i: (i, cb)), pl.BlockSpec((1, W), lambda i: (0, 0)),
                  pl.BlockSpec((W, W), lambda i: (0, 0))],
        out_specs=pl.BlockSpec((ts, W), lambda i: (i, 0)),
        compiler_params=_params(("parallel",)))(t, g, bd)


def _headnorm_bwd(t, cb, W, hd, g, dy, dcb, name, out_dtype=_F32):
    S = t.shape[0]
    ts = _pick(S, (512, 256))
    bd = _block_diag(W, hd)

    def body(x_ref, g_ref, bd_ref, dy_ref, dx_ref, dg_ref):
        xv = x_ref[...].astype(_F32)
        bdv = bd_ref[...]
        ms = _dg_hl(xv * xv, bdv, 1, 0) * (1.0 / hd)
        rv = lax.rsqrt(ms + RMS_EPS)
        n = xv * rv
        dyv = dy_ref[...].astype(_F32)
        dn = dyv * g_ref[...]
        c = _dg_hl(dn * n, bdv, 1, 0) * (1.0 / hd)
        dx_ref[...] = (rv * (dn - n * c)).astype(dx_ref.dtype)

        @pl.when(pl.program_id(0) == 0)
        def _():
            dg_ref[...] = jnp.zeros_like(dg_ref)

        dg_ref[...] += jnp.sum(dyv * n, axis=0, keepdims=True)

    return pl.pallas_call(
        body, name=name,
        out_shape=(jax.ShapeDtypeStruct((S, W), out_dtype), jax.ShapeDtypeStruct((1, W), _F32)),
        grid=(S // ts,),
        in_specs=[pl.BlockSpec((ts, W), lambda i: (i, cb)), pl.BlockSpec((1, W), lambda i: (0, 0)),
                  pl.BlockSpec((W, W), lambda i: (0, 0)), pl.BlockSpec((ts, W), lambda i: (i, dcb))],
        out_specs=(pl.BlockSpec((ts, W), lambda i: (i, 0)), pl.BlockSpec((1, W), lambda i: (0, 0))),
        compiler_params=_params(("arbitrary",)))(t, g, bd, dy)


def _gla_post_fwd(o_f, o_b, proj, r_cb, g, name):
    S, W = o_f.shape
    hd = W // GLA_HEADS
    ts = _pick(S, (256,))
    bd = _block_diag(W, hd)

    def body(of_ref, ob_ref, r_ref, g_ref, bd_ref, y_ref):
        o = of_ref[...] + ob_ref[...]
        ms = _dg_hl(o * o, bd_ref[...], 1, 0) * (1.0 / hd)
        u = o * lax.rsqrt(ms + RMS_EPS) * g_ref[...]
        rr = r_ref[...]
        y_ref[...] = (u * (rr * _sigmoid(rr))).astype(y_ref.dtype)

    row = pl.BlockSpec((ts, W), lambda i: (i, 0))
    return pl.pallas_call(
        body, name=name, out_shape=jax.ShapeDtypeStruct((S, W), _BF), grid=(S // ts,),
        in_specs=[row, row, pl.BlockSpec((ts, W), lambda i: (i, r_cb)), pl.BlockSpec((1, W), lambda i: (0, 0)),
                  pl.BlockSpec((W, W), lambda i: (0, 0))],
        out_specs=row, compiler_params=_params(("parallel",)))(o_f, o_b, proj, g, bd)


def _gla_post_bwd(o_f, o_b, proj, r_cb, g, dy, name):
    S, W = o_f.shape
    hd = W // GLA_HEADS
    ts = _pick(S, (256,))
    bd = _block_diag(W, hd)

    def body(of_ref, ob_ref, r_ref, g_ref, bd_ref, dy_ref, do_ref, dr_ref, dg_ref):
        o = of_ref[...] + ob_ref[...]
        bdv = bd_ref[...]
        ms = _dg_hl(o * o, bdv, 1, 0) * (1.0 / hd)
        rv = lax.rsqrt(ms + RMS_EPS)
        n = o * rv
        gv = g_ref[...]
        rr = r_ref[...]
        sg = _sigmoid(rr)
        dyv = dy_ref[...]
        dr_ref[...] = (dyv * (n * gv) * (sg * (1.0 + rr * (1.0 - sg)))).astype(dr_ref.dtype)
        du = dyv * (rr * sg)
        dn = du * gv
        c = _dg_hl(dn * n, bdv, 1, 0) * (1.0 / hd)
        do_ref[...] = rv * (dn - n * c)

        @pl.when(pl.program_id(0) == 0)
        def _():
            dg_ref[...] = jnp.zeros_like(dg_ref)

        dg_ref[...] += jnp.sum(du * n, axis=0, keepdims=True)

    row = pl.BlockSpec((ts, W), lambda i: (i, 0))
    vec = pl.BlockSpec((1, W), lambda i: (0, 0))
    return pl.pallas_call(
        body, name=name,
        out_shape=(jax.ShapeDtypeStruct((S, W), _F32), jax.ShapeDtypeStruct((S, W), _BF),
                   jax.ShapeDtypeStruct((1, W), _F32)),
        grid=(S // ts,),
        in_specs=[row, row, pl.BlockSpec((ts, W), lambda i: (i, r_cb)), vec, pl.BlockSpec((W, W), lambda i: (0, 0)),
                  row],
        out_specs=(row, row, vec), compiler_params=_params(("arbitrary",)))(o_f, o_b, proj, g, bd, dy)


def _loss_head(y, tgt, name):
    S, D = y.shape
    ts = _pick(S, (256,))

    def body(y_ref, t_ref, l_ref, d_ref, db_ref):
        err = y_ref[...] - t_ref[...]
        d = err * (1.0 / D)
        d_ref[...] = d
        db_ref[...] = d.astype(db_ref.dtype)

        @pl.when(pl.program_id(0) == 0)
        def _():
            l_ref[...] = jnp.zeros_like(l_ref)

        l_ref[...] += 0.5 * jnp.sum(jnp.mean(err * err, axis=-1, keepdims=True))

    row = pl.BlockSpec((ts, D), lambda i: (i, 0))
    return pl.pallas_call(
        body, name=name,
        out_shape=(jax.ShapeDtypeStruct((8, _LANES), _F32), jax.ShapeDtypeStruct((S, D), _F32),
                   jax.ShapeDtypeStruct((S, D), _BF)),
        grid=(S // ts,), in_specs=[row, row],
        out_specs=(pl.BlockSpec((8, _LANES), lambda i: (0, 0)), row, row),
        compiler_params=_params(("arbitrary",)))(y, tgt)


def _adamw(w, g, m, v, name):
    shape = w.shape
    if w.ndim == 3 and shape[1] * shape[2] > 256 * 1024:
        rb = _row_block(shape[1], shape[2] * 4, 1536 * 1024)
        grid = (shape[0], shape[1] // rb)
        spec = pl.BlockSpec((1, rb, shape[2]), lambda l, i: (l, i, 0))
        sem = ("parallel", "parallel")
    else:
        grid = ()
        spec = pl.BlockSpec(memory_space=pltpu.VMEM)
        sem = None

    def body(w_ref, g_ref, m_ref, v_ref, d_ref, nm_ref, nv_ref):
        gv = g_ref[...]
        mn = ADAM_B1 * m_ref[...] + (1.0 - ADAM_B1) * gv
        vn = ADAM_B2 * v_ref[...] + (1.0 - ADAM_B2) * (gv * gv)
        m_hat = mn / (1.0 - ADAM_B1 ** ADAM_STEP)
        v_hat = vn / (1.0 - ADAM_B2 ** ADAM_STEP)
        d_ref[...] = -ADAM_LR * (m_hat / (jnp.sqrt(v_hat) + ADAM_EPS) + ADAM_WD * w_ref[...])
        nm_ref[...] = mn
        nv_ref[...] = vn

    out = jax.ShapeDtypeStruct(shape, _F32)
    return pl.pallas_call(
        body, name=name, out_shape=(out, out, out), grid=grid, in_specs=[spec] * 4, out_specs=(spec,) * 3,
        compiler_params=_params(sem))(w, g, m, v)


def _add_pair(t, la, cidx, name):
    _, _, r, c = t.shape
    rb = _pick(r, (256, 352, 176, 88, 8))

    def body(c_ref, t_ref, l_ref, o_ref):
        o_ref[...] = (t_ref[...] + l_ref[...]).astype(o_ref.dtype)

    return pl.pallas_call(
        body, name=name, out_shape=jax.ShapeDtypeStruct((4, r, c), _BF),
        grid_spec=pltpu.PrefetchScalarGridSpec(
            num_scalar_prefetch=1, grid=(4, r // rb),
            in_specs=[pl.BlockSpec((1, None, rb, c), lambda j, i, cr: (j, cr[0], i, 0)),
                      pl.BlockSpec((1, rb, c), lambda j, i, cr: (j, i, 0))],
            out_specs=pl.BlockSpec((1, rb, c), lambda j, i, cr: (j, i, 0))),
        compiler_params=_params(("parallel", "parallel")))(cidx, t, la)


def _add_four(p, land, chip, core, name):
    _, r, c = p.shape
    rb = _pick(r, (256, 352, 176, 88, 16))

    def body(chip_ref, core_ref, p_ref, l1_ref, l2_ref, l3_ref, o_ref):
        o_ref[...] = ((p_ref[...].astype(_F32) + l1_ref[...].astype(_F32)) + l2_ref[...].astype(_F32)
                      ) + l3_ref[...].astype(_F32)

    def slot(flip):
        return pl.BlockSpec((None, rb, c), lambda i, ch, co: (jnp.bitwise_xor(ch[0], flip), i, 0))

    return pl.pallas_call(
        body, name=name, out_shape=jax.ShapeDtypeStruct((2, r, c), _F32),
        grid_spec=pltpu.PrefetchScalarGridSpec(
            num_scalar_prefetch=2, grid=(r // rb,),
            in_specs=[slot(0), slot(1), slot(2), slot(3)],
            out_specs=pl.BlockSpec((None, rb, c), lambda i, ch, co: (co[0], i, 0))),
        compiler_params=_params(("parallel",)))(chip, core, p, land, land, land)


def _gla_masks(tb, rev):
    i = np.arange(tb)
    same = (i[:, None] // GLA_CHUNK) == (i[None, :] // GLA_CHUNK)
    tri = same & (i[None, :] <= i[:, None])
    keep = (same & ~tri) if rev else tri
    return tuple(jnp.asarray(m.astype(np.float32), dtype=_BF) for m in (tri, same, keep))


def _gla_block(q, k, gfb, wg, bg, tri, same, keepm, rev, scale):
    z = _dg(gfb, wg, 1, 0) + bg
    la = _log_sigmoid(z) * (1.0 / GLA_TAU)
    cum = _dg_lh(tri, la, 0 if rev else 1, 0)
    tot = _dg_lh(same, la, 1, 0)
    e_a = jnp.exp(cum)
    e_na = jnp.exp(-cum)
    e_la = jnp.exp(tot - cum)
    qe = q * scale * e_a
    ke = k * e_na
    kend = k * e_la
    keep = keepm > 0
    p = jnp.where(keep, _dg(qe, ke, 1, 1), 0.0)
    return dict(z=z, tot=tot, e_a=e_a, e_na=e_na, e_la=e_la, qe=qe, ke=ke, kend=kend, keep=keep, p=p)


def _gla_fwd(proj, gfb, wg, bg, rev, name):
    S = proj.shape[0]
    H = GLA_HEADS
    dk = wg.shape[1] // (2 * H)
    dv = 2 * dk
    C = GLA_CHUNK
    cb_n = _pick(S // C, (8, 4, 2, 1))
    tb = cb_n * C
    nb = S // tb
    scale = float(dk) ** -0.5
    masks = _gla_masks(tb, rev)
    wcol = H if rev else 0

    def bmap(b):
        return nb - 1 - b if rev else b

    def body(q_ref, k_ref, v_ref, g_ref, wg_ref, bg_ref, tri_ref, same_ref, keep_ref, o_ref, st_ref, state):
        h = pl.program_id(1)

        @pl.when(pl.program_id(0) == 0)
        def _():
            state[h] = jnp.zeros((dv, dk), _F32)

        t = _gla_block(q_ref[...], k_ref[...], g_ref[...], wg_ref[...], bg_ref[...], tri_ref[...], same_ref[...],
                       keep_ref[...], rev, scale)
        vv = v_ref[...]
        o_ref[...] = _dg(t['p'], vv, 1, 0)
        order = range(cb_n - 1, -1, -1) if rev else range(cb_n)
        for ci in order:
            sl = slice(ci * C, (ci + 1) * C)
            st = state[h]
            o_ref[sl, :] += _dg(t['qe'][sl], st, 1, 1)
            st_ref[0, ci] = st
            state[h] = st * jnp.exp(t['tot'][ci * C:ci * C + 1]) + _dg(vv[sl], t['kend'][sl], 0, 0)

    nq = (H * dk) // dk
    msk = pl.BlockSpec((tb, tb), lambda b, h: (0, 0))
    return pl.pallas_call(
        body, name=name,
        out_shape=(jax.ShapeDtypeStruct((S, H * dv), _F32), jax.ShapeDtypeStruct((H, S // C, dv, dk), _F32)),
        grid=(nb, H),
        in_specs=[pl.BlockSpec((tb, dk), lambda b, h: (bmap(b), h)),
                  pl.BlockSpec((tb, dk), lambda b, h: (bmap(b), nq + h)),
                  pl.BlockSpec((tb, dv), lambda b, h: (bmap(b), (2 * H * dk) // dv + h)),
                  pl.BlockSpec((tb, _LANES), lambda b, h: (bmap(b), 0)),
                  pl.BlockSpec((_LANES, dk), lambda b, h: (0, wcol + h)),
                  pl.BlockSpec((1, dk), lambda b, h: (0, wcol + h)),
                  msk, msk, msk],
        out_specs=(pl.BlockSpec((tb, dv), lambda b, h: (bmap(b), h)),
                   pl.BlockSpec((1, cb_n, dv, dk), lambda b, h: (h, bmap(b), 0, 0))),
        scratch_shapes=[pltpu.VMEM((H, dv, dk), _F32)],
        compiler_params=_params(("arbitrary", "arbitrary")))(proj, proj, proj, gfb, wg, bg, *masks)


def _gla_bwd(proj, gfb, wg, bg, st, do, rev, prev, name):
    S = proj.shape[0]
    H = GLA_HEADS
    dk = wg.shape[1] // (2 * H)
    dv = 2 * dk
    C = GLA_CHUNK
    cb_n = _pick(S // C, (8, 4, 2, 1))
    tb = cb_n * C
    nb = S // tb
    scale = float(dk) ** -0.5
    masks = _gla_masks(tb, rev)
    wcol = H if rev else 0
    has_prev = prev is not None

    def bmap(b):
        return b if rev else nb - 1 - b

    def body(*refs):
        if has_prev:
            (q_ref, k_ref, v_ref, g_ref, wg_ref, bg_ref, tri_ref, same_ref, keep_ref, st_ref, do_ref,
             pq_ref, pk_ref, pv_ref, dq_ref, dk_ref, dv_ref, dz_ref, dstate) = refs
        else:
            (q_ref, k_ref, v_ref, g_ref, wg_ref, bg_ref, tri_ref, same_ref, keep_ref, st_ref, do_ref,
             dq_ref, dk_ref, dv_ref, dz_ref, dstate) = refs
        h = pl.program_id(1)

        @pl.when(pl.program_id(0) == 0)
        def _():
            dstate[h] = jnp.zeros((dv, dk), _F32)

        t = _gla_block(q_ref[...], k_ref[...], g_ref[...], wg_ref[...], bg_ref[...], tri_ref[...], same_ref[...],
                       keep_ref[...], rev, scale)
        vv = v_ref[...]
        dov = do_ref[...]
        order = range(cb_n) if rev else range(cb_n - 1, -1, -1)
        for ci in order:
            sl = slice(ci * C, (ci + 1) * C)
            stp = st_ref[0, ci]
            dst = dstate[h]
            e_l = jnp.exp(t['tot'][ci * C:ci * C + 1])
            kend_c = t['kend'][sl]
            dkend_c = _dg(vv[sl], dst, 1, 0)
            dq_ref[sl, :] = _dg(dov[sl], stp, 1, 0)
            dk_ref[sl, :] = dkend_c
            dv_ref[sl, :] = _dg(kend_c, dst, 1, 1)
            dtot = (e_l * jnp.sum(dst * stp, axis=0, keepdims=True)
                    + jnp.sum(dkend_c * kend_c, axis=0, keepdims=True))
            dz_ref[sl, :] = jnp.broadcast_to(dtot, (C, dk))
            dstate[h] = dst * e_l + _dg(dov[sl], t['qe'][sl], 0, 0)
        dp = jnp.where(t['keep'], _dg(dov, vv, 1, 1), 0.0)
        dqe = _dg(dp, t['ke'], 1, 0) + dq_ref[...]
        dke = _dg(dp, t['qe'], 0, 0)
        dkend = dk_ref[...]
        dvv = _dg(t['p'], dov, 0, 0) + dv_ref[...]
        dqv = dqe * t['e_a'] * scale
        dkv = dke * t['e_na'] + dkend * t['e_la']
        d_a = dqe * t['qe'] - dke * t['ke'] - dkend * t['kend']
        dla = _dg_lh(tri_ref[...], d_a, 1 if rev else 0, 0) + dz_ref[...]
        dz_ref[...] = dla * (1.0 / GLA_TAU) * _sigmoid(-t['z'])
        if has_prev:
            dqv = dqv + pq_ref[...]
            dkv = dkv + pk_ref[...]
            dvv = dvv + pv_ref[...]
        dq_ref[...] = dqv
        dk_ref[...] = dkv
        dv_ref[...] = dvv

    nq = (H * dk) // dk
    msk = pl.BlockSpec((tb, tb), lambda b, h: (0, 0))
    kblk = pl.BlockSpec((tb, dk), lambda b, h: (bmap(b), h))
    vblk = pl.BlockSpec((tb, dv), lambda b, h: (bmap(b), h))
    in_specs = [kblk,
                pl.BlockSpec((tb, dk), lambda b, h: (bmap(b), nq + h)),
                pl.BlockSpec((tb, dv), lambda b, h: (bmap(b), (2 * H * dk) // dv + h)),
                pl.BlockSpec((tb, _LANES), lambda b, h: (bmap(b), 0)),
                pl.BlockSpec((_LANES, dk), lambda b, h: (0, wcol + h)),
                pl.BlockSpec((1, dk), lambda b, h: (0, wcol + h)),
                msk, msk, msk,
                pl.BlockSpec((1, cb_n, dv, dk), lambda b, h: (h, bmap(b), 0, 0)),
                vblk]
    args = [proj, proj, proj, gfb, wg, bg, *masks, st, do]
    if has_prev:
        in_specs += [kblk, kblk, vblk]
        args += list(prev)
    return pl.pallas_call(
        body, name=name,
        out_shape=(jax.ShapeDtypeStruct((S, H * dk), _F32), jax.ShapeDtypeStruct((S, H * dk), _F32),
                   jax.ShapeDtypeStruct((S, H * dv), _F32), jax.ShapeDtypeStruct((S, H * dk), _F32)),
        grid=(nb, H), in_specs=in_specs, out_specs=(kblk, kblk, vblk, kblk),
        scratch_shapes=[pltpu.VMEM((H, dv, dk), _F32)],
        compiler_params=_params(("arbitrary", "arbitrary")))(*args)


NA_GROUP = 4
NA_WIN_ROWS = NA_GROUP + NA_ROWS


def _na_geometry(S):
    rows = S // GRID_W
    assert rows % NA_GROUP == 0 and rows >= NA_WIN_ROWS + NA_GROUP
    return rows, rows // NA_GROUP


def _na_win_start(g, rows):
    return jnp.clip(NA_GROUP * g - NA_ROWS // 2, 0, rows - NA_WIN_ROWS)


def _na_class(g, groups):
    return jnp.where(g == 0, 0, jnp.where(g == groups - 1, 2, 1))


def _na_onehots(rows):
    groups = rows // NA_GROUP
    by_row = np.zeros((3, NA_GROUP, NA_WIN_ROWS, 2 * NA_ROWS - 1), np.float32)
    for cls, g in enumerate((0, 1, groups - 1)):
        ws = int(np.clip(NA_GROUP * g - NA_ROWS // 2, 0, rows - NA_WIN_ROWS))
        for qr in range(NA_GROUP):
            r = NA_GROUP * g + qr
            rs = int(np.clip(r - NA_ROWS // 2, 0, rows - NA_ROWS))
            for kr in range(NA_WIN_ROWS):
                if rs <= ws + kr < rs + NA_ROWS:
                    by_row[cls, qr, kr, ws + kr - r + NA_ROWS - 1] = 1.0
    c = np.arange(GRID_W)
    cs = np.clip(c - NA_COLS // 2, 0, GRID_W - NA_COLS)
    kc = np.arange(GRID_W)
    win = (kc[None, :] >= cs[:, None]) & (kc[None, :] < cs[:, None] + NA_COLS)
    idx = np.clip(kc[None, :] - c[:, None], -(NA_COLS - 1), NA_COLS - 1) + (NA_COLS - 1)
    by_col = ((idx[:, :, None] == np.arange(2 * NA_COLS - 1)[None, None, :]) & win[:, :, None]).astype(np.float32)
    return by_row, by_col


def _na_bias_tables(rpb, rows):
    by_row, by_col = _na_onehots(rows)
    H = rpb.shape[0]
    e1 = jnp.einsum('hij,ckj->hick', rpb, by_col, precision=lax.Precision.HIGHEST)
    e1 = jnp.where((by_col.sum(-1) > 0)[None, None], e1, -jnp.inf)
    none = jnp.full((H, GRID_W, GRID_W), -jnp.inf, _F32)
    pick, valid = by_row.argmax(-1), by_row.sum(-1) > 0
    tabs = []
    for z in range(3):
        bands = [jnp.concatenate([e1[:, pick[z, q, r]] if valid[z, q, r] else none for r in range(NA_WIN_ROWS)],
                                 axis=-1) for q in range(NA_GROUP)]
        tabs.append(jnp.stack(bands, axis=1).reshape(H, NA_GROUP * GRID_W, NA_WIN_ROWS * GRID_W))
    return jnp.stack(tabs)


def _na_bias_grad(dtab, rows):
    by_row, by_col = _na_onehots(rows)
    H = dtab.shape[1]
    pick, valid = by_row.argmax(-1), by_row.sum(-1) > 0
    d6 = dtab.reshape(3, H, NA_GROUP, GRID_W, NA_WIN_ROWS, GRID_W)
    slabs = [[] for _ in range(2 * NA_ROWS - 1)]
    for z in range(3):
        for q in range(NA_GROUP):
            for r in range(NA_WIN_ROWS):
                if valid[z, q, r]:
                    slabs[pick[z, q, r]].append(d6[z, :, q, :, r, :])
    zero = jnp.zeros((H, GRID_W, GRID_W), _F32)
    de1 = jnp.stack([sum(s[1:], s[0]) if s else zero for s in slabs], axis=1)
    return jnp.einsum('hick,ckj->hij', de1, by_col, precision=lax.Precision.HIGHEST)


def _na_fwd(qn, kn, vb, tab, name):
    S, W = qn.shape
    rows, groups = _na_geometry(S)
    npair = W // _LANES
    nq = NA_GROUP * GRID_W
    nk = NA_WIN_ROWS * GRID_W
    sc = float(NA_HD) ** -0.5

    def body(q_ref, k_ref, v_ref, b_ref, o_ref):
        g = pl.program_id(1)
        k0 = pl.multiple_of(_na_win_start(g, rows) * GRID_W, GRID_W)
        kw = k_ref[pl.ds(k0, nk), :]
        vw = v_ref[pl.ds(k0, nk), :]
        qv = q_ref[...]
        lane = lax.broadcasted_iota(jnp.int32, (nq, _LANES), 1)
        outs = []
        for hh in range(2):
            mine = (lane >= hh * NA_HD) & (lane < (hh + 1) * NA_HD)
            s = _dg(jnp.where(mine, qv, jnp.zeros_like(qv)), kw, 1, 1) * sc + b_ref[0, hh]
            e = jnp.exp(s - jnp.max(s, axis=-1, keepdims=True))
            p = e / jnp.sum(e, axis=-1, keepdims=True)
            outs.append(_dg(p, vw, 1, 0))
        o_ref[...] = jnp.where(lane < NA_HD, outs[0], outs[1])

    return pl.pallas_call(
        body, name=name, out_shape=jax.ShapeDtypeStruct((S, W), _F32), grid=(npair, groups),
        in_specs=[pl.BlockSpec((nq, _LANES), lambda p, g: (g, p)),
                  pl.BlockSpec((S, _LANES), lambda p, g: (0, p)),
                  pl.BlockSpec((S, _LANES), lambda p, g: (0, p)),
                  pl.BlockSpec((1, 2, nq, nk), lambda p, g: (_na_class(g, groups), p, 0, 0))],
        out_specs=pl.BlockSpec((nq, _LANES), lambda p, g: (g, p)),
        compiler_params=_params(("parallel", "arbitrary")))(qn, kn, vb, tab)


def _na_bwd(qn, kn, vb, tab, do, name):
    S, W = qn.shape
    rows, groups = _na_geometry(S)
    npair = W // _LANES
    nq = NA_GROUP * GRID_W
    nk = NA_WIN_ROWS * GRID_W
    sc = float(NA_HD) ** -0.5

    def body(q_ref, k_ref, v_ref, b_ref, do_ref, dq_ref, dk_ref, dv_ref, db_ref):
        g = pl.program_id(1)

        @pl.when(g == 0)
        def _():
            dk_ref[...] = jnp.zeros_like(dk_ref)
            dv_ref[...] = jnp.zeros_like(dv_ref)

        @pl.when((g <= 1) | (g == groups - 1))
        def _():
            db_ref[...] = jnp.zeros_like(db_ref)

        k0 = pl.multiple_of(_na_win_start(g, rows) * GRID_W, GRID_W)
        kw = k_ref[pl.ds(k0, nk), :]
        vw = v_ref[pl.ds(k0, nk), :]
        qv = q_ref[...]
        dov = do_ref[...]
        lane = lax.broadcasted_iota(jnp.int32, (nq, _LANES), 1)
        dqs = []
        dkw = jnp.zeros((nk, _LANES), _F32)
        dvw = jnp.zeros((nk, _LANES), _F32)
        for hh in range(2):
            mine = (lane >= hh * NA_HD) & (lane < (hh + 1) * NA_HD)
            qm = jnp.where(mine, qv, jnp.zeros_like(qv))
            dom = jnp.where(mine, dov, 0.0)
            s = _dg(qm, kw, 1, 1) * sc + b_ref[0, hh]
            e = jnp.exp(s - jnp.max(s, axis=-1, keepdims=True))
            p = e / jnp.sum(e, axis=-1, keepdims=True)
            dp = _dg(dom, vw, 1, 1)
            ds = p * (dp - jnp.sum(p * dp, axis=-1, keepdims=True))
            db_ref[0, hh] += ds
            dqs.append(_dg(ds, kw, 1, 0) * sc)
            dkw = dkw + _dg(ds, qm, 0, 0) * sc
            dvw = dvw + _dg(p, dom, 0, 0)
        dq_ref[...] = jnp.where(lane < NA_HD, dqs[0], dqs[1])
        dk_ref[pl.ds(k0, nk), :] += dkw
        dv_ref[pl.ds(k0, nk), :] += dvw

    blk = pl.BlockSpec((nq, _LANES), lambda p, g: (g, p))
    full = pl.BlockSpec((S, _LANES), lambda p, g: (0, p))
    tspec = pl.BlockSpec((1, 2, nq, nk), lambda p, g: (_na_class(g, groups), p, 0, 0))
    return pl.pallas_call(
        body, name=name,
        out_shape=(jax.ShapeDtypeStruct((S, W), _F32), jax.ShapeDtypeStruct((S, W), _F32),
                   jax.ShapeDtypeStruct((S, W), _F32), jax.ShapeDtypeStruct(tab.shape, _F32)),
        grid=(npair, groups), in_specs=[blk, full, full, tspec, blk],
        out_specs=(blk, full, full, tspec),
        compiler_params=_params(("arbitrary", "arbitrary")))(qn, kn, vb, tab, do)


def _mem_fwd(qn, km, vm, name):
    S, W = qn.shape
    hd = W // MEM_HEADS
    tq = _pick(S, (512, 256))
    sc = float(hd) ** -0.5

    def body(q_ref, k_ref, v_ref, o_ref):
        for h in range(MEM_HEADS):
            cs = slice(h * hd, (h + 1) * hd)
            s = _dg(q_ref[:, cs], k_ref[:, cs], 1, 1) * sc
            e = jnp.exp(s - jnp.max(s, axis=-1, keepdims=True))
            p = e / jnp.sum(e, axis=-1, keepdims=True)
            o_ref[:, cs] = _dg(p, v_ref[:, cs], 1, 0)

    full = pl.BlockSpec(km.shape, lambda i: (0, 0))
    return pl.pallas_call(
        body, name=name, out_shape=jax.ShapeDtypeStruct((S, W), _F32), grid=(S // tq,),
        in_specs=[pl.BlockSpec((tq, W), lambda i: (i, 0)), full, full],
        out_specs=pl.BlockSpec((tq, W), lambda i: (i, 0)),
        compiler_params=_params(("parallel",)))(qn, km, vm)


def _mem_bwd(qn, km, vm, do, name):
    S, W = qn.shape
    hd = W // MEM_HEADS
    tq = _pick(S, (512, 256))
    sc = float(hd) ** -0.5

    def body(q_ref, k_ref, v_ref, do_ref, dq_ref, dk_ref, dv_ref):
        @pl.when(pl.program_id(0) == 0)
        def _():
            dk_ref[...] = jnp.zeros_like(dk_ref)
            dv_ref[...] = jnp.zeros_like(dv_ref)

        for h in range(MEM_HEADS):
            cs = slice(h * hd, (h + 1) * hd)
            qh = q_ref[:, cs]
            kh = k_ref[:, cs]
            doh = do_ref[:, cs]
            s = _dg(qh, kh, 1, 1) * sc
            e = jnp.exp(s - jnp.max(s, axis=-1, keepdims=True))
            p = e / jnp.sum(e, axis=-1, keepdims=True)
            dp = _dg(doh, v_ref[:, cs], 1, 1)
            ds = p * (dp - jnp.sum(p * dp, axis=-1, keepdims=True))
            dq_ref[:, cs] = _dg(ds, kh, 1, 0) * sc
            dk_ref[:, cs] += _dg(ds, qh, 0, 0) * sc
            dv_ref[:, cs] += _dg(p, doh, 0, 0)

    full = pl.BlockSpec(km.shape, lambda i: (0, 0))
    row = pl.BlockSpec((tq, W), lambda i: (i, 0))
    return pl.pallas_call(
        body, name=name,
        out_shape=(jax.ShapeDtypeStruct((S, W), _F32), jax.ShapeDtypeStruct(km.shape, _F32),
                   jax.ShapeDtypeStruct(km.shape, _F32)),
        grid=(S // tq,), in_specs=[row, full, full, row], out_specs=(row, full, full),
        compiler_params=_params(("arbitrary",)))(qn, km, vm, do)


def _place():
    x, y, c = lax.axis_index("x"), lax.axis_index("y"), lax.axis_index("c")
    chips = [(1 - x, y), (x, 1 - y), (1 - x, 1 - y)]
    return x, y, c, chips


def _exchange(rider, name):
    ri = len(rider.ins)
    ro = len(rider.outs)

    def body(*refs):
        ins, outs, sems = refs[:ri], refs[ri:ri + ro], refs[ri + ro:]
        rider.start(ins, outs, sems)
        rider.finish(ins, outs, sems)

    return pl.pallas_call(
        body, name=name, out_shape=tuple(rider.outs), in_specs=[_ANY] * ri, out_specs=(_ANY,) * ro,
        scratch_shapes=rider.sems, input_output_aliases={i: o for i, o in rider.aliases},
        compiler_params=_params())(*rider.ins)


def _pair_sems(*shape):
    return [pltpu.SemaphoreType.DMA(shape), pltpu.SemaphoreType.DMA(shape)]


def _gather_ici(shards):
    n = len(shards)

    def copies(w, g, sems):
        send, recv = sems
        x, y, c, chips = _place()
        me = 2 * x + y
        out, back = [], []
        for t in range(n):
            for j, (cx, cy) in enumerate(chips):
                out.append(pltpu.make_async_remote_copy(
                    src_ref=w[t].at[c], dst_ref=g[t].at[me, c], send_sem=send.at[t, j], recv_sem=recv.at[t, j],
                    device_id=(cx, cy, c), device_id_type=_MESH))
                back.append(functools.partial(
                    pltpu.make_async_remote_copy,
                    src_ref=w[t].at[c], dst_ref=g[t].at[2 * cx + cy, c], send_sem=send.at[t, j],
                    recv_sem=recv.at[t, j], device_id=(cx, cy, c), device_id_type=_MESH))
        return out, back

    def start(w, g, sems):
        for cp in copies(w, g, sems)[0]:
            cp.start()

    def finish(w, g, sems):
        out, back = copies(w, g, sems)
        for make in back:
            make().wait_recv()
        for cp in out:
            cp.wait_send()

    return _Rider(shards, [jax.ShapeDtypeStruct((4,) + s.shape, s.dtype) for s in shards], _pair_sems(n, 3),
                  start, finish)


def _gather_d2d(gs):
    n = len(gs)

    def copies(g, sems):
        send, recv = sems
        x, y, c, chips = _place()
        out, back = [], []
        for t in range(n):
            for j, (cx, cy) in enumerate(chips):
                mine, theirs = g[t].at[2 * cx + cy, c], g[t].at[2 * cx + cy, 1 - c]
                out.append(pltpu.make_async_remote_copy(
                    src_ref=mine, dst_ref=mine, send_sem=send.at[t, j], recv_sem=recv.at[t, j],
                    device_id=(x, y, 1 - c), device_id_type=_MESH))
                back.append(functools.partial(
                    pltpu.make_async_remote_copy,
                    src_ref=mine, dst_ref=theirs, send_sem=send.at[t, j], recv_sem=recv.at[t, j],
                    device_id=(x, y, 1 - c), device_id_type=_MESH))
        return out, back

    def start(_, g, sems):
        for cp in copies(g, sems)[0]:
            cp.start()

    def finish(_, g, sems):
        out, back = copies(g, sems)
        for make in back:
            make().wait_recv()
        for cp in out:
            cp.wait_send()

    return _Rider(gs, [jax.ShapeDtypeStruct(g.shape, g.dtype) for g in gs], _pair_sems(n, 3), start, finish,
                  aliases=[(t, t) for t in range(n)])


def _swap_halves(ts):
    n = len(ts)

    def copies(t_in, land, sems):
        send, recv = sems
        x, y, c, _ = _place()
        return [pltpu.make_async_remote_copy(
            src_ref=t_in[t].at[:, 1 - c], dst_ref=land[t], send_sem=send.at[t], recv_sem=recv.at[t],
            device_id=(x, y, 1 - c), device_id_type=_MESH) for t in range(n)]

    def start(t_in, land, sems):
        for cp in copies(t_in, land, sems):
            cp.start()

    def finish(t_in, land, sems):
        for cp in copies(t_in, land, sems):
            cp.wait()

    return _Rider(ts, [jax.ShapeDtypeStruct((4,) + t.shape[2:], t.dtype) for t in ts], _pair_sems(n), start, finish)


def _scatter_chips(ps):
    n = len(ps)

    def copies(p, land, sems):
        send, recv = sems
        x, y, c, chips = _place()
        me = 2 * x + y
        out, back = [], []
        for t in range(n):
            for j, (cx, cy) in enumerate(chips):
                out.append(pltpu.make_async_remote_copy(
                    src_ref=p[t].at[2 * cx + cy], dst_ref=land[t].at[me], send_sem=send.at[t, j],
                    recv_sem=recv.at[t, j], device_id=(cx, cy, c), device_id_type=_MESH))
                back.append(functools.partial(
                    pltpu.make_async_remote_copy,
                    src_ref=p[t].at[me], dst_ref=land[t].at[2 * cx + cy], send_sem=send.at[t, j],
                    recv_sem=recv.at[t, j], device_id=(cx, cy, c), device_id_type=_MESH))
        return out, back

    def start(p, land, sems):
        for cp in copies(p, land, sems)[0]:
            cp.start()

    def finish(p, land, sems):
        out, back = copies(p, land, sems)
        for make in back:
            make().wait_recv()
        for cp in out:
            cp.wait_send()

    return _Rider(ps, [jax.ShapeDtypeStruct(t.shape, t.dtype) for t in ps], _pair_sems(n, 3), start, finish)


def _swap_reduced(rs):
    n = len(rs)

    def copies(out, sems):
        send, recv = sems
        x, y, c, _ = _place()
        return [pltpu.make_async_remote_copy(
            src_ref=out[t].at[c], dst_ref=out[t].at[c], send_sem=send.at[t], recv_sem=recv.at[t],
            device_id=(x, y, 1 - c), device_id_type=_MESH) for t in range(n)]

    def start(_, out, sems):
        for cp in copies(out, sems):
            cp.start()

    def finish(_, out, sems):
        for cp in copies(out, sems):
            cp.wait()

    return _Rider(rs, [jax.ShapeDtypeStruct(t.shape, t.dtype) for t in rs], _pair_sems(n), start, finish,
                  aliases=[(t, t) for t in range(n)])


def _all_devices(v, reduce, name):
    rows = v.shape[0]

    def body(v_ref, o_ref, *rest):
        if reduce:
            all_ref, send, recv = rest
        else:
            send, recv = rest
            all_ref = o_ref
        x, y, c, _ = _place()
        me = 4 * x + 2 * y + c
        all_ref[me] = v_ref[...]
        cps = []
        for k in range(1, 8):
            fx, fy, fc = (k >> 2) & 1, (k >> 1) & 1, k & 1
            to = (x ^ fx, y ^ fy, c ^ fc)
            cps.append(pltpu.make_async_remote_copy(
                src_ref=v_ref, dst_ref=all_ref.at[me], send_sem=send.at[k - 1], recv_sem=recv.at[k - 1],
                device_id=to, device_id_type=_MESH))
        for cp in cps:
            cp.start()
        for k in range(1, 8):
            fx, fy, fc = (k >> 2) & 1, (k >> 1) & 1, k & 1
            frm = 4 * (x ^ fx) + 2 * (y ^ fy) + (c ^ fc)
            pltpu.make_async_remote_copy(
                src_ref=v_ref, dst_ref=all_ref.at[frm], send_sem=send.at[k - 1], recv_sem=recv.at[k - 1],
                device_id=(x, y, c), device_id_type=_MESH).wait_recv()
        for cp in cps:
            cp.wait_send()
        if reduce:
            acc = all_ref[0]
            for d in range(1, 8):
                acc = acc + all_ref[d]
            o_ref[...] = acc

    vm = pl.BlockSpec(memory_space=pltpu.VMEM)
    if reduce:
        out_shape = jax.ShapeDtypeStruct((rows, _LANES), _F32)
        scratch = [pltpu.VMEM((8, rows, _LANES), _F32)]
    else:
        out_shape = jax.ShapeDtypeStruct((8, rows, _LANES), _F32)
        scratch = []
    return pl.pallas_call(
        body, name=name, out_shape=out_shape, in_specs=[vm], out_specs=vm,
        scratch_shapes=scratch + [pltpu.SemaphoreType.DMA((7,)), pltpu.SemaphoreType.DMA((7,))],
        compiler_params=_params())(v)


def _weight_halves(shards):
    return [shards[k].reshape((2, shards[k].shape[0] // 2) + shards[k].shape[1:]) for k in BIG]


def _gather_layer_weights(halves, name):
    g1 = _exchange(_gather_ici(halves), name + "_ici")
    g2 = _exchange(_gather_d2d(g1), name + "_d2d")
    return _assemble_weights(g2, halves)


def _assemble_weights(got, halves):
    chip = 2 * lax.axis_index("x") + lax.axis_index("y")
    full = {}
    for k, g, own in zip(BIG, got, halves):
        g = lax.dynamic_update_slice(g, own[None], (chip, 0, 0, 0))
        _, _, r, c = g.shape
        if k in ('w_in', 'ffn_w13'):
            full[k] = g
        else:
            full[k] = g.reshape(8 * r, c)
    return full


def _grads_as_shards(grads):
    ts = []
    for k in BIG:
        g = grads[k]
        if k in ('w_in', 'ffn_w13'):
            ts.append(g)
        else:
            r, c = g.shape[0] // 8, g.shape[1]
            ts.append(g.reshape(4, 2, r, c))
    return ts


def _pair_sums(ts, landed, cidx):
    return [_add_pair(t, la, cidx, "reduce_add2_" + k) for k, t, la in zip(BIG, ts, landed)]


def _chip_sums(partial, slots, cidx):
    chip = (2 * lax.axis_index("x") + lax.axis_index("y")).astype(jnp.int32).reshape(1)
    return [_add_four(p, s, chip, cidx, "reduce_add4_" + k) for k, p, s in zip(BIG, partial, slots)]


def _assemble_shards(joined):
    return {k: j.reshape((2 * j.shape[1], j.shape[2])) for k, j in zip(BIG, joined)}


def _reduce_layer_grads(ts, cidx, name):
    landed = _exchange(_swap_halves(ts), name + "_swap")
    partial = _pair_sums(ts, landed, cidx)
    slots = _exchange(_scatter_chips(partial), name + "_scatter")
    reduced = _chip_sums(partial, slots, cidx)
    return _assemble_shards(_exchange(_swap_reduced(reduced), name + "_join"))


def _pack_rows(arrs):
    parts, spans = [], []
    off = 0
    for a in arrs:
        n = int(np.prod(a.shape))
        pad = (-n) % (8 * _LANES)
        parts.append(jnp.pad(a.reshape(-1), (0, pad)))
        spans.append((off, n, a.shape))
        off += n + pad
    return jnp.concatenate(parts).reshape(-1, _LANES), spans


def _unpack_rows(packed, spans):
    flat = packed.reshape(-1)
    return [flat[o:o + n].reshape(shape) for o, n, shape in spans]


def _in_split(d):
    dk, dv, w = d // 4, d // 2, d // 4
    names = [('g_q', dk), ('g_k', dk), ('g_v', dv), ('g_r', dv), ('g_f', GLA_GATE_RANK), ('g_b', GLA_GATE_RANK),
             ('n_q', w), ('n_k', w), ('n_v', w), ('m_q', w)]
    out, off = {}, 0
    for nme, wd in names:
        out[nme] = (off, wd)
        off += wd
    return out


def _w_in_from_shards(g):
    _, _, R, C = g.shape
    sp = _in_split(2 * R)
    f0, n0 = sp['g_f'][0], sp['n_q'][0]

    def cols(lo, hi, h):
        out = []
        for j in range(4):
            a, b = max(lo, j * C), min(hi, (j + 1) * C)
            if a < b:
                out.append(g[j, h, :, a - j * C:b - j * C])
        return out

    main = jnp.concatenate([jnp.concatenate(cols(0, f0, h) + cols(n0, 4 * C, h), axis=1) for h in range(2)], axis=0)
    gate = jnp.concatenate([jnp.concatenate(cols(f0, n0, h), axis=1) for h in range(2)], axis=0)
    return main, jnp.pad(gate, ((0, 0), (0, _LANES - (n0 - f0))))


def _w_in_to_shards(main, gate):
    R = main.shape[0] // 2
    sp = _in_split(2 * R)
    f0, n0 = sp['g_f'][0], sp['n_q'][0]
    C = (main.shape[1] + n0 - f0) // 4

    def cols(lo, hi, h):
        rows = slice(h * R, (h + 1) * R)
        out = []
        for src, a, b, shift in ((main, 0, f0, 0), (gate, f0, n0, -f0), (main, n0, 4 * C, f0 - n0)):
            lo2, hi2 = max(lo, a), min(hi, b)
            if lo2 < hi2:
                out.append(src[rows, lo2 + shift:hi2 + shift])
        return out

    return jnp.stack([jnp.stack([jnp.concatenate(cols(j * C, (j + 1) * C, h), axis=1) for h in range(2)])
                      for j in range(4)])


def _gate_weight(wg2_f, wg2_b):
    r, dk = wg2_f.shape
    top = jnp.concatenate([wg2_f, jnp.zeros_like(wg2_f)], axis=1)
    mid = jnp.concatenate([jnp.zeros_like(wg2_b), wg2_b], axis=1)
    return jnp.concatenate([top, mid, jnp.zeros((_LANES - 2 * r, 2 * dk), wg2_f.dtype)], axis=0)


def _layer_fwd(l, x, mem, w, full, tab, next_halves=None):
    S, D = x.shape
    dk, dv, nw = D // 4, D // 2, D // 4
    sv = {}
    w_main, w_gate = _w_in_from_shards(full['w_in'])
    sv['w_main'], sv['w_gate'] = w_main, w_gate
    xn, r1 = _rmsnorm_fwd(x, w['attn_norm'][None], f"rms1_fwd")
    if next_halves is None:
        proj = _mm(xn, w_main, 'nn', "mm_proj")
    else:
        proj, g_a = _mm(xn, w_main, 'nn', "mm_proj_gather", rider=_gather_ici(next_halves[:3]))
    gfb = _mm(xn, w_gate, 'nn', "mm_gate")
    sv.update(x=x, xn=xn, r1=r1, proj=proj, gfb=gfb)
    wg = _gate_weight(w['gla_wg2_f'], w['gla_wg2_b']).astype(_BF)
    bg = jnp.concatenate([w['gla_bg_f'], w['gla_bg_b']])[None]
    o_f, st_f = _gla_fwd(proj, gfb, wg, bg, False, "gla_fwd_f")
    o_b, st_b = _gla_fwd(proj, gfb, wg, bg, True, "gla_fwd_b")
    y_gla = _gla_post_fwd(o_f, o_b, proj, (2 * dk) // dv + 1, w['gla_out_norm'][None], "gla_post_fwd")
    sv.update(wg=wg, bg=bg, o_f=o_f, o_b=o_b, st_f=st_f, st_b=st_b)
    c0 = (2 * dk + 2 * dv) // nw
    qn = _headnorm_fwd(proj, c0, nw, NA_HD, jnp.tile(w['na_q_norm'], nw // NA_HD)[None], "na_qnorm_fwd")
    kn = _headnorm_fwd(proj, c0 + 1, nw, NA_HD, jnp.tile(w['na_k_norm'], nw // NA_HD)[None], "na_knorm_fwd")
    vb = proj[:, (c0 + 2) * nw:(c0 + 3) * nw].astype(_BF)
    o_na = _na_fwd(qn, kn, vb, tab, "na_fwd")
    y_na = _headnorm_fwd(o_na, 0, nw, NA_HD, w['na_out_norm'][None], "na_onorm_fwd")
    sv.update(qn=qn, kn=kn, vb=vb, o_na=o_na)
    mhd = nw // MEM_HEADS
    mqn = _headnorm_fwd(proj, c0 + 3, nw, mhd, jnp.tile(w['mem_q_norm'], MEM_HEADS)[None], "mem_qnorm_fwd")
    memn, rm = _rmsnorm_fwd(mem, w['mem_norm'][None], "mem_rms_fwd")
    kv = _mm(memn, full['mem_wkv'], 'nn', "mm_memkv")
    km = _headnorm_fwd(kv, 0, nw, mhd, jnp.tile(w['mem_k_norm'], MEM_HEADS)[None], "mem_knorm_fwd")
    vm = kv[:, nw:].astype(_BF)
    o_mem = _mem_fwd(mqn, km, vm, "mem_fwd")
    y_mem = _headnorm_fwd(o_mem, 0, nw, mhd, w['mem_out_norm'][None], "mem_onorm_fwd")
    sv.update(mqn=mqn, memn=memn, rm=rm, kv=kv, km=km, vm=vm, o_mem=o_mem)
    y = jnp.concatenate([y_gla, y_na, y_mem], axis=1)
    x1 = _mm(y, full['w_out'], 'nn', "mm_out", res=x)
    h, r2 = _rmsnorm_fwd(x1, w['ffn_norm'][None], "rms2_fwd")
    if next_halves is None:
        a, gate, up = _ffn_up_fwd(h, full['ffn_w13'], "ffn_up_fwd")
        x2 = _mm(a, full['ffn_w2'], 'nn', "mm_w2", res=x1)
        got = None
    else:
        (a, gate, up), g_b = _ffn_up_fwd(h, full['ffn_w13'], "ffn_up_fwd_gather", _gather_ici(next_halves[3:]))
        x2, got = _mm(a, full['ffn_w2'], 'nn', "mm_w2_gather", res=x1, rider=_gather_d2d(list(g_a) + list(g_b)))
    sv.update(y=y, x1=x1, h=h, r2=r2, gate=gate, up=up, a=a)
    return x2, sv, got


def _layer_bwd(l, dx2, dx2b, mem, w, full, tab, sv, pending=None, cidx=None):
    S, D = dx2.shape
    dk, dv, nw = D // 4, D // 2, D // 4
    gb, gs = {}, {}
    if pending is None:
        dgate, dup = _ffn_up_bwd(dx2b, full['ffn_w2'], sv['gate'], sv['up'], "ffn_up_bwd")
    else:
        (dgate, dup), landed = _ffn_up_bwd(dx2b, full['ffn_w2'], sv['gate'], sv['up'], "ffn_up_bwd_swap",
                                           _swap_halves(pending))
        partial = _pair_sums(pending, landed, cidx)
    gb['ffn_w2'] = _mm(sv['a'], dx2b, 'tn', "mm_dw2")
    if pending is None:
        dh = _ffn_up_dh(dgate, dup, full['ffn_w13'], "ffn_up_dh")
        gb['ffn_w13'] = _ffn_up_dw(sv['h'], dgate, dup, "ffn_up_dw")
    else:
        dh, slots_b = _ffn_up_dh(dgate, dup, full['ffn_w13'], "ffn_up_dh_scatter", _scatter_chips(partial[3:]))
        gb['ffn_w13'], slots_a = _ffn_up_dw(sv['h'], dgate, dup, "ffn_up_dw_scatter", _scatter_chips(partial[:3]))
        reduced = _chip_sums(partial, list(slots_a) + list(slots_b), cidx)
    dx1, dx1b, g = _rmsnorm_bwd(sv['x1'], sv['r2'], w['ffn_norm'][None], dh, dx2, "rms2_bwd")
    gs['ffn_norm'] = g[0]
    dy = _mm(dx1b, full['w_out'], 'nt', "mm_dy")
    gb['w_out'] = _mm(sv['y'], dx1b, 'tn', "mm_dwout")
    c0 = (2 * dk + 2 * dv) // nw
    mhd = nw // MEM_HEADS
    do_mem, g = _headnorm_bwd(sv['o_mem'], 0, nw, mhd, w['mem_out_norm'][None], dy, (dv + nw) // nw, "mem_onorm_bwd")
    gs['mem_out_norm'] = g[0]
    dmqn, dkm, dvm = _mem_bwd(sv['mqn'], sv['km'], sv['vm'], do_mem, "mem_bwd")
    dmq, g = _headnorm_bwd(sv['proj'], c0 + 3, nw, mhd, jnp.tile(w['mem_q_norm'], MEM_HEADS)[None], dmqn, 0,
                           "mem_qnorm_bwd", _BF)
    gs['mem_q_norm'] = g[0].reshape(MEM_HEADS, mhd).sum(0)
    dkvk, g = _headnorm_bwd(sv['kv'], 0, nw, mhd, jnp.tile(w['mem_k_norm'], MEM_HEADS)[None], dkm, 0,
                            "mem_knorm_bwd")
    gs['mem_k_norm'] = g[0].reshape(MEM_HEADS, mhd).sum(0)
    dkv = jnp.concatenate([dkvk, dvm], axis=1).astype(_BF)
    gb['mem_wkv'] = _mm(sv['memn'], dkv, 'tn', "mm_dwkv")
    dmemn = _mm(dkv, full['mem_wkv'], 'nt', "mm_dmemn")
    _, _, g = _rmsnorm_bwd(mem, sv['rm'], w['mem_norm'][None], dmemn, None, "mem_rms_bwd")
    gs['mem_norm'] = g[0]
    do_na, g = _headnorm_bwd(sv['o_na'], 0, nw, NA_HD, w['na_out_norm'][None], dy, dv // nw, "na_onorm_bwd")
    gs['na_out_norm'] = g[0]
    dqn, dkn, dnv, dtab = _na_bwd(sv['qn'], sv['kn'], sv['vb'], tab, do_na, "na_bwd")
    gs['na_rpb'] = dtab
    dnq, g = _headnorm_bwd(sv['proj'], c0, nw, NA_HD, jnp.tile(w['na_q_norm'], nw // NA_HD)[None], dqn, 0,
                           "na_qnorm_bwd", _BF)
    gs['na_q_norm'] = g[0].reshape(nw // NA_HD, NA_HD).sum(0)
    dnk, g = _headnorm_bwd(sv['proj'], c0 + 1, nw, NA_HD, jnp.tile(w['na_k_norm'], nw // NA_HD)[None], dkn, 0,
                           "na_knorm_bwd", _BF)
    gs['na_k_norm'] = g[0].reshape(nw // NA_HD, NA_HD).sum(0)
    do_gla, dgr, g = _gla_post_bwd(sv['o_f'], sv['o_b'], sv['proj'], (2 * dk) // dv + 1, w['gla_out_norm'][None], dy,
                                   "gla_post_bwd")
    gs['gla_out_norm'] = g[0]
    dq1, dk1, dv1, dz_f = _gla_bwd(sv['proj'], sv['gfb'], sv['wg'], sv['bg'], sv['st_f'], do_gla, False, None,
                                   "gla_bwd_f")
    dgq, dgk, dgv, dz_b = _gla_bwd(sv['proj'], sv['gfb'], sv['wg'], sv['bg'], sv['st_b'], do_gla, True,
                                   (dq1, dk1, dv1), "gla_bwd_b")
    dz = jnp.concatenate([dz_f, dz_b], axis=1).astype(_BF)
    ones_lane = (jnp.arange(_LANES) == 2 * GLA_GATE_RANK)[None]
    gfb_aug = jnp.where(ones_lane, 1.0, sv['gfb']).astype(_BF)
    dwg = _mm(gfb_aug, dz, 'tn', "mm_dwg")
    r16 = GLA_GATE_RANK
    gs['gla_wg2_f'] = dwg[:r16, :dk]
    gs['gla_wg2_b'] = dwg[r16:2 * r16, dk:]
    gs['gla_bg_f'] = dwg[2 * r16, :dk]
    gs['gla_bg_b'] = dwg[2 * r16, dk:]
    dgfb = _mm(dz, sv['wg'], 'nt', "mm_dgfb", out_dtype=_BF)
    dproj = jnp.concatenate([dgq.astype(_BF), dgk.astype(_BF), dgv.astype(_BF), dgr, dnq, dnk, dnv.astype(_BF), dmq],
                            axis=1)
    t = _mm(dgfb, sv['w_gate'], 'nt', "mm_dxn_gate")
    if pending is None:
        dxn = _mm(dproj, sv['w_main'], 'nt', "mm_dxn", res=t)
        done = None
    else:
        dxn, joined = _mm(dproj, sv['w_main'], 'nt', "mm_dxn_join", res=t, rider=_swap_reduced(reduced))
        done = _assemble_shards(joined)
    dw_main = _mm(sv['xn'], dproj, 'tn', "mm_dwmain")
    dw_gate = _mm(sv['xn'], dgfb, 'tn', "mm_dwgate")
    gb['w_in'] = _w_in_to_shards(dw_main, dw_gate)
    dx, dxb, g = _rmsnorm_bwd(sv['x'], sv['r1'], w['attn_norm'][None], dxn, dx1, "rms1_bwd")
    gs['attn_norm'] = g[0]
    return dx, dxb, gb, gs, done


def kernel(x, mem, attn_norm, w_in, gla_wg2_f, gla_bg_f, gla_wg2_b, gla_bg_b, gla_out_norm, na_q_norm, na_k_norm, na_rpb, na_out_norm, mem_norm, mem_wkv, mem_q_norm, mem_k_norm, mem_out_norm, w_out, ffn_norm, ffn_w13, ffn_w2, loss_target, m_attn_norm, m_w_in, m_gla_wg2_f, m_gla_bg_f, m_gla_wg2_b, m_gla_bg_b, m_gla_out_norm, m_na_q_norm, m_na_k_norm, m_na_rpb, m_na_out_norm, m_mem_norm, m_mem_wkv, m_mem_q_norm, m_mem_k_norm, m_mem_out_norm, m_w_out, m_ffn_norm, m_ffn_w13, m_ffn_w2, v_attn_norm, v_w_in, v_gla_wg2_f, v_gla_bg_f, v_gla_wg2_b, v_gla_bg_b, v_gla_out_norm, v_na_q_norm, v_na_k_norm, v_na_rpb, v_na_out_norm, v_mem_norm, v_mem_wkv, v_mem_q_norm, v_mem_k_norm, v_mem_out_norm, v_w_out, v_ffn_norm, v_ffn_w13, v_ffn_w2):
    args = locals()
    W = {k: args[k] for k in WEIGHTS}
    M = {k: args['m_' + k] for k in WEIGHTS}
    V = {k: args['v_' + k] for k in WEIGHTS}
    depth = attn_norm.shape[0]
    xs, mems, tgt = x[0], mem[0], loss_target[0]
    cidx = lax.axis_index("c").astype(jnp.int32).reshape(1)

    gate_rows, gate_spans = _pack_rows([W[k] for k in SMALL_SHARDED])
    gate_all = _all_devices(gate_rows, False, "gate_weights_gather")
    gate_full = {}
    for i, k in enumerate(SMALL_SHARDED):
        per_chip = [_unpack_rows(gate_all[2 * j], gate_spans)[i] for j in range(4)]
        gate_full[k] = jnp.concatenate(per_chip, axis=-1)

    small = [k for k in WEIGHTS if k not in BIG]

    def layer_small(l):
        d = {k: W[k][l] for k in small if k not in SMALL_SHARDED}
        d.update({k: gate_full[k][l] for k in SMALL_SHARDED})
        return d

    saved, fulls, tabs = [], [], []
    cur = xs
    grid_rows = xs.shape[0] // GRID_W
    all_tabs = jax.vmap(lambda r: _na_bias_tables(r, grid_rows))(W['na_rpb'])
    halves = [_weight_halves({k: W[k][l].astype(_BF) for k in BIG}) for l in range(depth)]
    full = _gather_layer_weights(halves[0], "gather_weights")
    for l in range(depth):
        tab = all_tabs[l]
        nxt = halves[l + 1] if l + 1 < depth else None
        cur, sv, got = _layer_fwd(l, cur, mems, layer_small(l), full, tab, nxt)
        saved.append(sv)
        fulls.append(full)
        tabs.append(tab)
        if nxt is not None:
            full = _assemble_weights(got, nxt)
    loss_tile, dy, dyb = _loss_head(cur, tgt, "loss_head")
    loss = lax.psum(loss_tile[0, 0], ("x", "y", "c"))

    big_grads = [None] * depth
    small_grads = [None] * depth
    pending = None
    for l in range(depth - 1, -1, -1):
        dy, dyb, gb, gs, done = _layer_bwd(l, dy, dyb, mems, layer_small(l), fulls[l], tabs[l], saved[l], pending, cidx)
        if pending is not None:
            big_grads[l + 1] = done
        pending = _grads_as_shards(gb)
        small_grads[l] = gs
    big_grads[0] = _reduce_layer_grads(pending, cidx, "reduce_grads")
    grad_x = dy[None]

    small_stack = {k: jnp.stack([small_grads[l][k] for l in range(depth)]) for k in small}
    small_stack['na_rpb'] = jax.vmap(lambda d: _na_bias_grad(d, grid_rows))(small_stack['na_rpb'])
    packed, spans = _pack_rows([small_stack[k] for k in small])
    summed = _unpack_rows(_all_devices(packed, True, "small_grads_sum"), spans)
    G = dict(zip(small, summed))
    chip = 2 * lax.axis_index("x") + lax.axis_index("y")
    for k in SMALL_SHARDED:
        wdt = W[k].shape[-1]
        G[k] = lax.dynamic_slice_in_dim(G[k], chip * wdt, wdt, axis=2)
    for k in BIG:
        G[k] = jnp.stack([big_grads[l][k] for l in range(depth)])

    delta, new_m, new_v = {}, {}, {}
    for k in WEIGHTS:
        delta[k], new_m[k], new_v[k] = _adamw(W[k], G[k], M[k], V[k], "adamw_" + k)
    return (loss, grad_x, *[G[k] for k in WEIGHTS], *[delta[k] for k in WEIGHTS], *[new_m[k] for k in WEIGHTS],
            *[new_v[k] for k in WEIGHTS])
```

```python
import functools

import numpy as np
import jax
import jax.numpy as jnp
from jax import lax
from jax.experimental import pallas as pl
from jax.experimental.pallas import tpu as pltpu

_F32 = jnp.float32
_BF = jnp.bfloat16
_MESH = pl.DeviceIdType.MESH

_VMEM_LIMIT_BYTES = 56 * 1024 * 1024
_LANES = 128

RMS_EPS = 1e-6
GLA_HEADS = 4
GLA_GATE_RANK = 16
GLA_TAU = 16.0
GLA_CHUNK = 64
GRID_W = 64
NA_HD = 64
NA_ROWS = 8
NA_COLS = 16
MEM_HEADS = 4
ADAM_LR = 0.001
ADAM_B1 = 0.9
ADAM_B2 = 0.999
ADAM_EPS = 1e-08
ADAM_WD = 0.01
ADAM_STEP = 10

WEIGHTS = ['attn_norm', 'w_in', 'gla_wg2_f', 'gla_bg_f', 'gla_wg2_b', 'gla_bg_b', 'gla_out_norm', 'na_q_norm',
           'na_k_norm', 'na_rpb', 'na_out_norm', 'mem_norm', 'mem_wkv', 'mem_q_norm', 'mem_k_norm', 'mem_out_norm',
           'w_out', 'ffn_norm', 'ffn_w13', 'ffn_w2']
BIG = ['w_in', 'mem_wkv', 'w_out', 'ffn_w13', 'ffn_w2']
SMALL_SHARDED = ['gla_wg2_f', 'gla_wg2_b']


def _pick(n, cands):
    for c in cands:
        if n % c == 0:
            return c
    return n


def _row_block(rows, row_bytes, cap_bytes):
    best = 8
    for rb in range(8, rows + 1, 8):
        if rows % rb == 0 and rb * row_bytes <= cap_bytes:
            best = rb
    return best


def _params(sem=None):
    return pltpu.CompilerParams(dimension_semantics=sem, vmem_limit_bytes=_VMEM_LIMIT_BYTES)


def _dg(a, b, ca, cb):
    return lax.dot_general(a.astype(_BF), b.astype(_BF), (((ca,), (cb,)), ((), ())), preferred_element_type=_F32)


def _split(a):
    hi = a.astype(_BF)
    return hi, (a - hi.astype(_F32)).astype(_BF)


def _dg_hl(a, b, ca, cb):
    hi, lo = _split(a)
    return _dg(hi, b, ca, cb) + _dg(lo, b, ca, cb)


def _dg_lh(a, b, ca, cb):
    hi, lo = _split(b)
    return _dg(a, hi, ca, cb) + _dg(a, lo, ca, cb)


def _sigmoid(x):
    return 1.0 / (1.0 + jnp.exp(-x))


def _log_sigmoid(z):
    return jnp.minimum(z, 0.0) - jnp.log(1.0 + jnp.exp(-jnp.abs(z)))


def _block_diag(width, hd):
    i = np.arange(width) // hd
    return jnp.asarray((i[:, None] == i[None, :]).astype(np.float32), dtype=_BF)


class _Rider:
    def __init__(self, ins, outs, sems, start, finish, aliases=()):
        self.ins, self.outs, self.sems = list(ins), list(outs), list(sems)
        self.start, self.finish, self.aliases = start, finish, tuple(aliases)


def _call(core, name, grid, in_specs, out_specs, out_shape, scratch, sem, args, rider=None):
    out_specs, out_shape = tuple(out_specs), tuple(out_shape)
    if rider is None:
        return pl.pallas_call(core, name=name, out_shape=out_shape, grid=grid, in_specs=list(in_specs),
                              out_specs=out_specs, scratch_shapes=list(scratch), compiler_params=_params(sem))(*args)
    ni, no, ns = len(in_specs), len(out_specs), len(scratch)
    ri, ro = len(rider.ins), len(rider.outs)

    def body(*refs):
        ins, rins = refs[:ni], refs[ni:ni + ri]
        outs, routs = refs[ni + ri:ni + ri + no], refs[ni + ri + no:ni + ri + no + ro]
        scr, rsem = refs[ni + ri + no + ro:ni + ri + no + ro + ns], refs[ni + ri + no + ro + ns:]
        first = pl.program_id(0) == 0
        last = pl.program_id(0) == grid[0] - 1
        for ax in range(1, len(grid)):
            first = first & (pl.program_id(ax) == 0)
            last = last & (pl.program_id(ax) == grid[ax] - 1)

        @pl.when(first)
        def _():
            rider.start(rins, routs, rsem)

        core(*ins, *outs, *scr)

        @pl.when(last)
        def _():
            rider.finish(rins, routs, rsem)

    res = pl.pallas_call(
        body, name=name, out_shape=out_shape + tuple(rider.outs), grid=grid,
        in_specs=list(in_specs) + [_ANY] * ri, out_specs=out_specs + (_ANY,) * ro,
        scratch_shapes=list(scratch) + rider.sems,
        input_output_aliases={ni + i: no + o for i, o in rider.aliases},
        compiler_params=_params(("arbitrary",) * len(grid)))(*args, *rider.ins)
    return tuple(res[:no]), tuple(res[no:])


_ANY = pl.BlockSpec(memory_space=pl.ANY)


def _mm(a, b, mode, name, out_dtype=_F32, res=None, rider=None):
    def finish(r):
        return r[0] if rider is None else (r[0][0], r[1])

    if mode == 'nn':
        (M, K), N = a.shape, b.shape[1]
    elif mode == 'nt':
        (M, K), N = a.shape, b.shape[0]
    else:
        (K, M), N = a.shape, b.shape[1]
    ca, cb = {'nn': (1, 0), 'nt': (1, 1), 'tn': (0, 0)}[mode]
    has_res = res is not None
    args = (a, b) + ((res,) if has_res else ())
    out_shape = jax.ShapeDtypeStruct((M, N), out_dtype)
    if K <= _K_RESIDENT:
        tm = _pick(M, (1024, 512, 256, 128))
        tn = _pick(N, (1024, 512, 256, 128))
        a_spec = (pl.BlockSpec((K, tm), lambda i, j: (0, i)) if mode == 'tn'
                  else pl.BlockSpec((tm, K), lambda i, j: (i, 0)))
        b_spec = (pl.BlockSpec((tn, K), lambda i, j: (j, 0)) if mode == 'nt'
                  else pl.BlockSpec((K, tn), lambda i, j: (0, j)))
        o_spec = pl.BlockSpec((tm, tn), lambda i, j: (i, j))

        def body1(*refs):
            a_ref, b_ref = refs[:2]
            o_ref = refs[-1]
            out = _dg(a_ref[...], b_ref[...], ca, cb)
            if has_res:
                out = out + refs[2][...]
            o_ref[...] = out.astype(o_ref.dtype)

        return finish(_call(body1, name, (M // tm, N // tn), [a_spec, b_spec] + ([o_spec] if has_res else []),
                            [o_spec], [out_shape], [], ("parallel", "parallel"), args, rider))

    tm, tn, tk = _loop_tiles(M, N, K, has_res, out_dtype)
    nk = K // tk
    if mode == 'tn':
        a_spec = pl.BlockSpec((tk, tm), lambda i, j, k: (k, i))
    else:
        a_spec = pl.BlockSpec((tm, tk), lambda i, j, k: (i, k))
    if mode == 'nt':
        b_spec = pl.BlockSpec((tn, tk), lambda i, j, k: (j, k))
    else:
        b_spec = pl.BlockSpec((tk, tn), lambda i, j, k: (k, j))
    o_spec = pl.BlockSpec((tm, tn), lambda i, j, k: (i, j))
    in_out = out_dtype == _F32

    def body(*refs):
        a_ref, b_ref = refs[:2]
        o_ref = refs[-1] if in_out else refs[-2]
        acc = o_ref if in_out else refs[-1]
        k = pl.program_id(2)

        @pl.when(k == 0)
        def _():
            part = _dg(a_ref[...], b_ref[...], ca, cb)
            acc[...] = part + refs[2][...] if has_res else part

        @pl.when(k > 0)
        def _():
            acc[...] += _dg(a_ref[...], b_ref[...], ca, cb)

        if not in_out:
            @pl.when(k == nk - 1)
            def _():
                o_ref[...] = acc[...].astype(o_ref.dtype)

    return finish(_call(body, name, (M // tm, N // tn, nk), [a_spec, b_spec] + ([o_spec] if has_res else []),
                        [o_spec], [out_shape], [] if in_out else [pltpu.VMEM((tm, tn), _F32)],
                        ("parallel", "parallel", "arbitrary"), args, rider))


_K_RESIDENT = 2048
_LOOP_TILE_BYTES = 40 * 1024 * 1024
_LOOP_TILE_AREA = 1024 * 1024


def _loop_tiles(M, N, K, has_res, out_dtype):
    best, best_score = None, None
    obytes = 4 if out_dtype == _F32 else 2
    for tk in (2816, 2560, 2048, 1408, 1024, 512, 256, 128):
        if K % tk:
            continue
        for tm in (2048, 1408, 1024, 512, 256, 128):
            for tn in (2048, 1024, 512, 256, 128):
                if M % tm or N % tn:
                    continue
                need = 2 * 2 * tk * (tm + tn) + tm * tn * (2 * obytes + (8 if has_res else 0) + (0 if obytes == 4 else 4))
                score = (min(tm * tn, _LOOP_TILE_AREA), tk, tm * tn)
                if need <= _LOOP_TILE_BYTES and (best is None or score > best_score):
                    best, best_score = (tm, tn, tk), score
    return best


def _cols_to_shards(w):
    r, c = w.shape[0] // 2, w.shape[1] // 4
    return jnp.transpose(w.reshape(2, r, 4, c), (2, 0, 1, 3))


def _shards_to_cols(g):
    _, _, r, c = g.shape
    return jnp.transpose(g, (1, 2, 0, 3)).reshape(2 * r, 4 * c)


_FFN_SUB = 256


def _ffn_up_fwd(h, g13, name, rider=None):
    S = h.shape[0]
    _, _, R, C = g13.shape
    tm = _pick(S, (512, 256, 128))
    cw, nc = _ffn_cols(C)
    subs = [(s, min(s + _FFN_SUB, cw)) for s in range(0, cw, _FFN_SUB)]

    def body(h_ref, wg_ref, wu_ref, a_ref, g_ref, u_ref):
        h0, h1 = h_ref[:, :R], h_ref[:, R:]
        for lo, hi in subs:
            sl = slice(lo, hi)
            gv = _dg(h0, wg_ref[0, 0, :, sl], 1, 0) + _dg(h1, wg_ref[0, 1, :, sl], 1, 0)
            uv = _dg(h0, wu_ref[0, 0, :, sl], 1, 0) + _dg(h1, wu_ref[0, 1, :, sl], 1, 0)
            sg = _sigmoid(gv)
            silu = gv * sg
            a_ref[:, sl] = (silu * uv).astype(a_ref.dtype)
            g_ref[:, sl] = (uv * (sg * (1.0 + gv * (1.0 - sg)))).astype(g_ref.dtype)
            u_ref[:, sl] = silu.astype(u_ref.dtype)

    out = jax.ShapeDtypeStruct((S, 2 * C), _BF)
    ospec = pl.BlockSpec((tm, cw), lambda j, i: (i, j))
    return _call(
        body, name, (2 * nc, S // tm),
        [pl.BlockSpec((tm, 2 * R), lambda j, i: (i, 0)),
         pl.BlockSpec((1, 2, R, cw), lambda j, i: (lax.div(j, nc), 0, 0, lax.rem(j, nc))),
         pl.BlockSpec((1, 2, R, cw), lambda j, i: (2 + lax.div(j, nc), 0, 0, lax.rem(j, nc)))],
        (ospec, ospec, ospec), (out, out, out), [], ("parallel", "parallel"), (h, g13, g13), rider)


def _ffn_cols(C):
    if C % (2 * _LANES) == 0:
        return C // 2, 2
    return C, 1


def _ffn_up_bwd(dxb, w2, gate, up, name, rider=None):
    S, D = dxb.shape
    F = w2.shape[0]
    tf = 2 * _FFN_SUB
    tm = _pick(S, (1024, 512, 256, 128))

    def body(d_ref, w_ref, g_ref, u_ref, dg_ref, du_ref):
        dv = d_ref[...]
        for s in range(tf // _FFN_SUB):
            sl = slice(s * _FFN_SUB, (s + 1) * _FFN_SUB)
            da = _dg(dv, w_ref[sl, :], 1, 1)
            dg_ref[:, sl] = (da * g_ref[:, sl].astype(_F32)).astype(dg_ref.dtype)
            du_ref[:, sl] = (da * u_ref[:, sl].astype(_F32)).astype(du_ref.dtype)

    out = jax.ShapeDtypeStruct((S, F), _BF)
    tile = pl.BlockSpec((tm, tf), lambda i, j: (i, j))
    return _call(
        body, name, (S // tm, F // tf),
        [pl.BlockSpec((tm, D), lambda i, j: (i, 0)), pl.BlockSpec((tf, D), lambda i, j: (j, 0)), tile, tile],
        (tile, tile), (out, out), [], ("parallel", "parallel"), (dxb, w2, gate, up), rider)


def _ffn_up_dh(dgate, dup, g13, name, rider=None):
    S = dgate.shape[0]
    _, _, R, C = g13.shape
    tm = _pick(S, (512, 256, 128))

    def body(dg_ref, du_ref, wg_ref, wu_ref, o_ref):
        j = pl.program_id(1)

        def part(hh):
            return _dg(dg_ref[...], wg_ref[0, hh], 1, 1) + _dg(du_ref[...], wu_ref[0, hh], 1, 1)

        @pl.when(j == 0)
        def _():
            for hh in range(2):
                o_ref[:, hh * R:(hh + 1) * R] = part(hh)

        @pl.when(j > 0)
        def _():
            for hh in range(2):
                o_ref[:, hh * R:(hh + 1) * R] += part(hh)

    cw, nc = _ffn_cols(C)
    tile = pl.BlockSpec((tm, cw), lambda i, j: (i, j))
    r = _call(
        body, name, (S // tm, 2 * nc),
        [tile, tile, pl.BlockSpec((1, 2, R, cw), lambda i, j: (lax.div(j, nc), 0, 0, lax.rem(j, nc))),
         pl.BlockSpec((1, 2, R, cw), lambda i, j: (2 + lax.div(j, nc), 0, 0, lax.rem(j, nc)))],
        [pl.BlockSpec((tm, 2 * R), lambda i, j: (i, 0))], [jax.ShapeDtypeStruct((S, 2 * R), _F32)], [],
        ("parallel", "arbitrary"), (dgate, dup, g13, g13), rider)
    return r[0] if rider is None else (r[0][0], r[1])


def _ffn_up_dw(hact, dgate, dup, name, rider=None):
    S, R2 = hact.shape
    R = R2 // 2
    C = dgate.shape[1] // 2
    tk = _pick(S, (1024, 512, 256, 128))
    nk = S // tk
    cw, nc = _ffn_cols(C)

    def body(h_ref, dg_ref, du_ref, o_ref):
        chip = pl.program_id(0)
        k = pl.program_id(3)

        def accumulate(d_ref):
            @pl.when(k == 0)
            def _():
                o_ref[0, 0] = _dg(h_ref[...], d_ref[...], 0, 0)

            @pl.when(k > 0)
            def _():
                o_ref[0, 0] += _dg(h_ref[...], d_ref[...], 0, 0)

        @pl.when(chip < 2)
        def _():
            accumulate(dg_ref)

        @pl.when(chip >= 2)
        def _():
            accumulate(du_ref)

    r = _call(
        body, name, (4, nc, 2, nk),
        [pl.BlockSpec((tk, R), lambda c, b, h, k: (k, h)),
         pl.BlockSpec((tk, cw), lambda c, b, h, k: (jnp.where(c < 2, k, 0), jnp.where(c < 2, c * nc + b, 0))),
         pl.BlockSpec((tk, cw), lambda c, b, h, k: (jnp.where(c >= 2, k, 0), jnp.where(c >= 2, (c - 2) * nc + b, 0)))],
        [pl.BlockSpec((1, 1, R, cw), lambda c, b, h, k: (c, h, 0, b))], [jax.ShapeDtypeStruct((4, 2, R, C), _F32)], [],
        ("parallel", "parallel", "parallel", "arbitrary"), (hact, dgate, dup), rider)
    return r[0] if rider is None else (r[0][0], r[1])


def _rmsnorm_fwd(x, g, name):
    S, D = x.shape
    ts = _pick(S, (256,))

    def body(x_ref, g_ref, o_ref, r_ref):
        xv = x_ref[...]
        r = lax.rsqrt(jnp.mean(xv * xv, axis=-1, keepdims=True) + RMS_EPS)
        o_ref[...] = (xv * r * g_ref[...]).astype(o_ref.dtype)
        r_ref[...] = r

    return pl.pallas_call(
        body, name=name,
        out_shape=(jax.ShapeDtypeStruct((S, D), _BF), jax.ShapeDtypeStruct((S, 1), _F32)),
        grid=(S // ts,),
        in_specs=[pl.BlockSpec((ts, D), lambda i: (i, 0)), pl.BlockSpec((1, D), lambda i: (0, 0))],
        out_specs=(pl.BlockSpec((ts, D), lambda i: (i, 0)), pl.BlockSpec((ts, 1), lambda i: (i, 0))),
        compiler_params=_params(("parallel",)))(x, g)


def _rmsnorm_bwd(x, r, g, dy, dres, name):
    S, D = x.shape
    ts = _pick(S, (256,))
    has_res = dres is not None

    def body(*refs):
        if has_res:
            x_ref, r_ref, g_ref, dy_ref, dr_ref, dx_ref, dxb_ref, dg_ref = refs
        else:
            x_ref, r_ref, g_ref, dy_ref, dx_ref, dxb_ref, dg_ref = refs
        rv = r_ref[...]
        n = x_ref[...] * rv
        dyv = dy_ref[...]
        dn = dyv * g_ref[...]
        c = jnp.mean(dn * n, axis=-1, keepdims=True)
        dx = rv * (dn - n * c)
        if has_res:
            dx = dx + dr_ref[...]
        dx_ref[...] = dx
        dxb_ref[...] = dx.astype(dxb_ref.dtype)

        @pl.when(pl.program_id(0) == 0)
        def _():
            dg_ref[...] = jnp.zeros_like(dg_ref)

        dg_ref[...] += jnp.sum(dyv * n, axis=0, keepdims=True)

    row = pl.BlockSpec((ts, D), lambda i: (i, 0))
    vec = pl.BlockSpec((1, D), lambda i: (0, 0))
    in_specs = [row, pl.BlockSpec((ts, 1), lambda i: (i, 0)), vec, row] + ([row] if has_res else [])
    args = (x, r, g, dy) + ((dres,) if has_res else ())
    return pl.pallas_call(
        body, name=name,
        out_shape=(jax.ShapeDtypeStruct((S, D), _F32), jax.ShapeDtypeStruct((S, D), _BF),
                   jax.ShapeDtypeStruct((1, D), _F32)),
        grid=(S // ts,), in_specs=in_specs, out_specs=(row, row, vec),
        compiler_params=_params(("arbitrary",)))(*args)


def _headnorm_fwd(t, cb, W, hd, g, name):
    S = t.shape[0]
    ts = _pick(S, (512, 256))
    bd = _block_diag(W, hd)

    def body(x_ref, g_ref, bd_ref, o_ref):
        xv = x_ref[...].astype(_F32)
        ms = _dg_hl(xv * xv, bd_ref[...], 1, 0) * (1.0 / hd)
        o_ref[...] = (xv * lax.rsqrt(ms + RMS_EPS) * g_ref[...]).astype(o_ref.dtype)

    return pl.pallas_call(
        body, name=name, out_shape=jax.ShapeDtypeStruct((S, W), _BF), grid=(S // ts,),
        in_specs=[pl.BlockSpec((ts, W), lambda i: (i, cb)), pl.BlockSpec((1, W), lambda i: (0, 0)),
                  pl.BlockSpec((W, W), lambda i: (0, 0))],
        out_specs=pl.BlockSpec((ts, W), lambda i: (i, 0)),
        compiler_params=_params(("parallel",)))(t, g, bd)


def _headnorm_bwd(t, cb, W, hd, g, dy, dcb, name, out_dtype=_F32):
    S = t.shape[0]
    ts = _pick(S, (512, 256))
    bd = _block_diag(W, hd)

    def body(x_ref, g_ref, bd_ref, dy_ref, dx_ref, dg_ref):
        xv = x_ref[...].astype(_F32)
        bdv = bd_ref[...]
        ms = _dg_hl(xv * xv, bdv, 1, 0) * (1.0 / hd)
        rv = lax.rsqrt(ms + RMS_EPS)
        n = xv * rv
        dyv = dy_ref[...].astype(_F32)
        dn = dyv * g_ref[...]
        c = _dg_hl(dn * n, bdv, 1, 0) * (1.0 / hd)
        dx_ref[...] = (rv * (dn - n * c)).astype(dx_ref.dtype)

        @pl.when(pl.program_id(0) == 0)
        def _():
            dg_ref[...] = jnp.zeros_like(dg_ref)

        dg_ref[...] += jnp.sum(dyv * n, axis=0, keepdims=True)

    return pl.pallas_call(
        body, name=name,
        out_shape=(jax.ShapeDtypeStruct((S, W), out_dtype), jax.ShapeDtypeStruct((1, W), _F32)),
        grid=(S // ts,),
        in_specs=[pl.BlockSpec((ts, W), lambda i: (i, cb)), pl.BlockSpec((1, W), lambda i: (0, 0)),
                  pl.BlockSpec((W, W), lambda i: (0, 0)), pl.BlockSpec((ts, W), lambda i: (i, dcb))],
        out_specs=(pl.BlockSpec((ts, W), lambda i: (i, 0)), pl.BlockSpec((1, W), lambda i: (0, 0))),
        compiler_params=_params(("arbitrary",)))(t, g, bd, dy)


def _gla_post_fwd(o_f, o_b, proj, r_cb, g, name):
    S, W = o_f.shape
    hd = W // GLA_HEADS
    ts = _pick(S, (256,))
    bd = _block_diag(W, hd)

    def body(of_ref, ob_ref, r_ref, g_ref, bd_ref, y_ref):
        o = of_ref[...] + ob_ref[...]
        ms = _dg_hl(o * o, bd_ref[...], 1, 0) * (1.0 / hd)
        u = o * lax.rsqrt(ms + RMS_EPS) * g_ref[...]
        rr = r_ref[...].astype(_F32)
        y_ref[...] = (u * (rr * _sigmoid(rr))).astype(y_ref.dtype)

    row = pl.BlockSpec((ts, W), lambda i: (i, 0))
    return pl.pallas_call(
        body, name=name, out_shape=jax.ShapeDtypeStruct((S, W), _BF), grid=(S // ts,),
        in_specs=[row, row, pl.BlockSpec((ts, W), lambda i: (i, r_cb)), pl.BlockSpec((1, W), lambda i: (0, 0)),
                  pl.BlockSpec((W, W), lambda i: (0, 0))],
        out_specs=row, compiler_params=_params(("parallel",)))(o_f, o_b, proj, g, bd)


def _gla_post_bwd(o_f, o_b, proj, r_cb, g, dy, name):
    S, W = o_f.shape
    hd = W // GLA_HEADS
    ts = _pick(S, (256,))
    bd = _block_diag(W, hd)

    def body(of_ref, ob_ref, r_ref, g_ref, bd_ref, dy_ref, do_ref, dr_ref, dg_ref):
        o = of_ref[...] + ob_ref[...]
        bdv = bd_ref[...]
        ms = _dg_hl(o * o, bdv, 1, 0) * (1.0 / hd)
        rv = lax.rsqrt(ms + RMS_EPS)
        n = o * rv
        gv = g_ref[...]
        rr = r_ref[...].astype(_F32)
        sg = _sigmoid(rr)
        dyv = dy_ref[...].astype(_F32)
        dr_ref[...] = (dyv * (n * gv) * (sg * (1.0 + rr * (1.0 - sg)))).astype(dr_ref.dtype)
        du = dyv * (rr * sg)
        dn = du * gv
        c = _dg_hl(dn * n, bdv, 1, 0) * (1.0 / hd)
        do_ref[...] = rv * (dn - n * c)

        @pl.when(pl.program_id(0) == 0)
        def _():
            dg_ref[...] = jnp.zeros_like(dg_ref)

        dg_ref[...] += jnp.sum(du * n, axis=0, keepdims=True)

    row = pl.BlockSpec((ts, W), lambda i: (i, 0))
    vec = pl.BlockSpec((1, W), lambda i: (0, 0))
    return pl.pallas_call(
        body, name=name,
        out_shape=(jax.ShapeDtypeStruct((S, W), _F32), jax.ShapeDtypeStruct((S, W), _BF),
                   jax.ShapeDtypeStruct((1, W), _F32)),
        grid=(S // ts,),
        in_specs=[row, row, pl.BlockSpec((ts, W), lambda i: (i, r_cb)), vec, pl.BlockSpec((W, W), lambda i: (0, 0)),
                  row],
        out_specs=(row, row, vec), compiler_params=_params(("arbitrary",)))(o_f, o_b, proj, g, bd, dy)


def _loss_head(y, tgt, name):
    S, D = y.shape
    ts = _pick(S, (256,))

    def body(y_ref, t_ref, l_ref, d_ref, db_ref):
        err = y_ref[...] - t_ref[...]
        d = err * (1.0 / D)
        d_ref[...] = d
        db_ref[...] = d.astype(db_ref.dtype)

        @pl.when(pl.program_id(0) == 0)
        def _():
            l_ref[...] = jnp.zeros_like(l_ref)

        l_ref[...] += 0.5 * jnp.sum(jnp.mean(err * err, axis=-1, keepdims=True))

    row = pl.BlockSpec((ts, D), lambda i: (i, 0))
    return pl.pallas_call(
        body, name=name,
        out_shape=(jax.ShapeDtypeStruct((8, _LANES), _F32), jax.ShapeDtypeStruct((S, D), _F32),
                   jax.ShapeDtypeStruct((S, D), _BF)),
        grid=(S // ts,), in_specs=[row, row],
        out_specs=(pl.BlockSpec((8, _LANES), lambda i: (0, 0)), row, row),
        compiler_params=_params(("arbitrary",)))(y, tgt)


def _adamw(w, g, m, v, name):
    shape = w.shape
    if w.ndim == 3 and shape[1] * shape[2] > 256 * 1024:
        rb = _row_block(shape[1], shape[2] * 4, 1536 * 1024)
        grid = (shape[0], shape[1] // rb)
        spec = pl.BlockSpec((1, rb, shape[2]), lambda l, i: (l, i, 0))
        sem = ("parallel", "parallel")
    else:
        grid = ()
        spec = pl.BlockSpec(memory_space=pltpu.VMEM)
        sem = None

    def body(w_ref, g_ref, m_ref, v_ref, d_ref, nm_ref, nv_ref):
        gv = g_ref[...]
        mn = ADAM_B1 * m_ref[...] + (1.0 - ADAM_B1) * gv
        vn = ADAM_B2 * v_ref[...] + (1.0 - ADAM_B2) * (gv * gv)
        m_hat = mn / (1.0 - ADAM_B1 ** ADAM_STEP)
        v_hat = vn / (1.0 - ADAM_B2 ** ADAM_STEP)
        d_ref[...] = -ADAM_LR * (m_hat / (jnp.sqrt(v_hat) + ADAM_EPS) + ADAM_WD * w_ref[...])
        nm_ref[...] = mn
        nv_ref[...] = vn

    out = jax.ShapeDtypeStruct(shape, _F32)
    return pl.pallas_call(
        body, name=name, out_shape=(out, out, out), grid=grid, in_specs=[spec] * 4, out_specs=(spec,) * 3,
        compiler_params=_params(sem))(w, g, m, v)


def _add_pair(t, la, cidx, name):
    _, _, r, c = t.shape
    rb = _pick(r, (256, 352, 176, 88, 8))

    def body(c_ref, t_ref, l_ref, o_ref):
        o_ref[...] = (t_ref[...] + l_ref[...]).astype(o_ref.dtype)

    return pl.pallas_call(
        body, name=name, out_shape=jax.ShapeDtypeStruct((4, r, c), _BF),
        grid_spec=pltpu.PrefetchScalarGridSpec(
            num_scalar_prefetch=1, grid=(4, r // rb),
            in_specs=[pl.BlockSpec((1, None, rb, c), lambda j, i, cr: (j, cr[0], i, 0)),
                      pl.BlockSpec((1, rb, c), lambda j, i, cr: (j, i, 0))],
            out_specs=pl.BlockSpec((1, rb, c), lambda j, i, cr: (j, i, 0))),
        compiler_params=_params(("parallel", "parallel")))(cidx, t, la)


def _add_four(p, land, chip, core, name):
    _, r, c = p.shape
    rb = _pick(r, (256, 352, 176, 88, 16))

    def body(chip_ref, core_ref, p_ref, l1_ref, l2_ref, l3_ref, o_ref):
        o_ref[...] = ((p_ref[...].astype(_F32) + l1_ref[...].astype(_F32)) + l2_ref[...].astype(_F32)
                      ) + l3_ref[...].astype(_F32)

    def slot(flip):
        return pl.BlockSpec((None, rb, c), lambda i, ch, co: (jnp.bitwise_xor(ch[0], flip), i, 0))

    return pl.pallas_call(
        body, name=name, out_shape=jax.ShapeDtypeStruct((2, r, c), _F32),
        grid_spec=pltpu.PrefetchScalarGridSpec(
            num_scalar_prefetch=2, grid=(r // rb,),
            in_specs=[slot(0), slot(1), slot(2), slot(3)],
            out_specs=pl.BlockSpec((None, rb, c), lambda i, ch, co: (co[0], i, 0))),
        compiler_params=_params(("parallel",)))(chip, core, p, land, land, land)


def _gla_masks(tb, rev):
    i = np.arange(tb)
    same = (i[:, None] // GLA_CHUNK) == (i[None, :] // GLA_CHUNK)
    tri = same & (i[None, :] <= i[:, None])
    keep = (same & ~tri) if rev else tri
    return tuple(jnp.asarray(m.astype(np.float32), dtype=_BF) for m in (tri, same, keep))


def _gla_block(q, k, gfb, wg, bg, tri, same, keepm, rev, scale):
    z = _dg(gfb, wg, 1, 0) + bg
    la = _log_sigmoid(z) * (1.0 / GLA_TAU)
    cum = _dg_lh(tri, la, 0 if rev else 1, 0)
    tot = _dg_lh(same, la, 1, 0)
    e_a = jnp.exp(cum)
    e_na = jnp.exp(-cum)
    e_la = jnp.exp(tot - cum)
    qe = q * scale * e_a
    ke = k * e_na
    kend = k * e_la
    keep = keepm > 0
    p = jnp.where(keep, _dg(qe, ke, 1, 1), 0.0)
    return dict(z=z, tot=tot, e_a=e_a, e_na=e_na, e_la=e_la, qe=qe, ke=ke, kend=kend, keep=keep, p=p)


def _gla_fwd(proj, gfb, wg, bg, rev, name):
    S = proj.shape[0]
    H = GLA_HEADS
    dk = wg.shape[1] // (2 * H)
    dv = 2 * dk
    C = GLA_CHUNK
    cb_n = _pick(S // C, (8, 4, 2, 1))
    tb = cb_n * C
    nb = S // tb
    scale = float(dk) ** -0.5
    masks = _gla_masks(tb, rev)
    wcol = H if rev else 0

    def bmap(b):
        return nb - 1 - b if rev else b

    def body(q_ref, k_ref, v_ref, g_ref, wg_ref, bg_ref, tri_ref, same_ref, keep_ref, o_ref, st_ref, state):
        h = pl.program_id(1)

        @pl.when(pl.program_id(0) == 0)
        def _():
            state[h] = jnp.zeros((dv, dk), _F32)

        t = _gla_block(q_ref[...].astype(_F32), k_ref[...].astype(_F32), g_ref[...], wg_ref[...], bg_ref[...],
                       tri_ref[...], same_ref[...],
                       keep_ref[...], rev, scale)
        vv = v_ref[...]
        o_ref[...] = _dg(t['p'], vv, 1, 0)
        order = range(cb_n - 1, -1, -1) if rev else range(cb_n)
        for ci in order:
            sl = slice(ci * C, (ci + 1) * C)
            st = state[h]
            o_ref[sl, :] += _dg(t['qe'][sl], st, 1, 1)
            st_ref[0, ci] = st
            state[h] = st * jnp.exp(t['tot'][ci * C:ci * C + 1]) + _dg(vv[sl], t['kend'][sl], 0, 0)

    nq = (H * dk) // dk
    msk = pl.BlockSpec((tb, tb), lambda b, h: (0, 0))
    return pl.pallas_call(
        body, name=name,
        out_shape=(jax.ShapeDtypeStruct((S, H * dv), _F32), jax.ShapeDtypeStruct((H, S // C, dv, dk), _F32)),
        grid=(nb, H),
        in_specs=[pl.BlockSpec((tb, dk), lambda b, h: (bmap(b), h)),
                  pl.BlockSpec((tb, dk), lambda b, h: (bmap(b), nq + h)),
                  pl.BlockSpec((tb, dv), lambda b, h: (bmap(b), (2 * H * dk) // dv + h)),
                  pl.BlockSpec((tb, _LANES), lambda b, h: (bmap(b), 0)),
                  pl.BlockSpec((_LANES, dk), lambda b, h: (0, wcol + h)),
                  pl.BlockSpec((1, dk), lambda b, h: (0, wcol + h)),
                  msk, msk, msk],
        out_specs=(pl.BlockSpec((tb, dv), lambda b, h: (bmap(b), h)),
                   pl.BlockSpec((1, cb_n, dv, dk), lambda b, h: (h, bmap(b), 0, 0))),
        scratch_shapes=[pltpu.VMEM((H, dv, dk), _F32)],
        compiler_params=_params(("arbitrary", "arbitrary")))(proj, proj, proj, gfb, wg, bg, *masks)


def _gla_bwd(proj, gfb, wg, bg, st, do, rev, prev, name):
    S = proj.shape[0]
    H = GLA_HEADS
    dk = wg.shape[1] // (2 * H)
    dv = 2 * dk
    C = GLA_CHUNK
    cb_n = _pick(S // C, (8, 4, 2, 1))
    tb = cb_n * C
    nb = S // tb
    scale = float(dk) ** -0.5
    masks = _gla_masks(tb, rev)
    wcol = H if rev else 0
    has_prev = prev is not None

    def bmap(b):
        return b if rev else nb - 1 - b

    def body(*refs):
        if has_prev:
            (q_ref, k_ref, v_ref, g_ref, wg_ref, bg_ref, tri_ref, same_ref, keep_ref, st_ref, do_ref,
             pq_ref, pk_ref, pv_ref, dq_ref, dk_ref, dv_ref, dz_ref, dstate) = refs
        else:
            (q_ref, k_ref, v_ref, g_ref, wg_ref, bg_ref, tri_ref, same_ref, keep_ref, st_ref, do_ref,
             dq_ref, dk_ref, dv_ref, dz_ref, dstate) = refs
        h = pl.program_id(1)

        @pl.when(pl.program_id(0) == 0)
        def _():
            dstate[h] = jnp.zeros((dv, dk), _F32)

        t = _gla_block(q_ref[...].astype(_F32), k_ref[...].astype(_F32), g_ref[...], wg_ref[...], bg_ref[...],
                       tri_ref[...], same_ref[...],
                       keep_ref[...], rev, scale)
        vv = v_ref[...]
        dov = do_ref[...]
        order = range(cb_n) if rev else range(cb_n - 1, -1, -1)
        for ci in order:
            sl = slice(ci * C, (ci + 1) * C)
            stp = st_ref[0, ci]
            dst = dstate[h]
            e_l = jnp.exp(t['tot'][ci * C:ci * C + 1])
            kend_c = t['kend'][sl]
            dkend_c = _dg(vv[sl], dst, 1, 0)
            dq_ref[sl, :] = _dg(dov[sl], stp, 1, 0)
            dk_ref[sl, :] = dkend_c
            dv_ref[sl, :] = _dg(kend_c, dst, 1, 1)
            dtot = (e_l * jnp.sum(dst * stp, axis=0, keepdims=True)
                    + jnp.sum(dkend_c * kend_c, axis=0, keepdims=True))
            dz_ref[sl, :] = jnp.broadcast_to(dtot, (C, dk))
            dstate[h] = dst * e_l + _dg(dov[sl], t['qe'][sl], 0, 0)
        dp = jnp.where(t['keep'], _dg(dov, vv, 1, 1), 0.0)
        dqe = _dg(dp, t['ke'], 1, 0) + dq_ref[...]
        dke = _dg(dp, t['qe'], 0, 0)
        dkend = dk_ref[...]
        dvv = _dg(t['p'], dov, 0, 0) + dv_ref[...]
        dqv = dqe * t['e_a'] * scale
        dkv = dke * t['e_na'] + dkend * t['e_la']
        d_a = dqe * t['qe'] - dke * t['ke'] - dkend * t['kend']
        dla = _dg_lh(tri_ref[...], d_a, 1 if rev else 0, 0) + dz_ref[...]
        dz_ref[...] = dla * (1.0 / GLA_TAU) * _sigmoid(-t['z'])
        if has_prev:
            dqv = dqv + pq_ref[...]
            dkv = dkv + pk_ref[...]
            dvv = dvv + pv_ref[...]
        dq_ref[...] = dqv
        dk_ref[...] = dkv
        dv_ref[...] = dvv

    nq = (H * dk) // dk
    msk = pl.BlockSpec((tb, tb), lambda b, h: (0, 0))
    kblk = pl.BlockSpec((tb, dk), lambda b, h: (bmap(b), h))
    vblk = pl.BlockSpec((tb, dv), lambda b, h: (bmap(b), h))
    in_specs = [kblk,
                pl.BlockSpec((tb, dk), lambda b, h: (bmap(b), nq + h)),
                pl.BlockSpec((tb, dv), lambda b, h: (bmap(b), (2 * H * dk) // dv + h)),
                pl.BlockSpec((tb, _LANES), lambda b, h: (bmap(b), 0)),
                pl.BlockSpec((_LANES, dk), lambda b, h: (0, wcol + h)),
                pl.BlockSpec((1, dk), lambda b, h: (0, wcol + h)),
                msk, msk, msk,
                pl.BlockSpec((1, cb_n, dv, dk), lambda b, h: (h, bmap(b), 0, 0)),
                vblk]
    args = [proj, proj, proj, gfb, wg, bg, *masks, st, do]
    if has_prev:
        in_specs += [kblk, kblk, vblk]
        args += list(prev)
    return pl.pallas_call(
        body, name=name,
        out_shape=(jax.ShapeDtypeStruct((S, H * dk), _F32), jax.ShapeDtypeStruct((S, H * dk), _F32),
                   jax.ShapeDtypeStruct((S, H * dv), _F32), jax.ShapeDtypeStruct((S, H * dk), _F32)),
        grid=(nb, H), in_specs=in_specs, out_specs=(kblk, kblk, vblk, kblk),
        scratch_shapes=[pltpu.VMEM((H, dv, dk), _F32)],
        compiler_params=_params(("arbitrary", "arbitrary")))(*args)


NA_GROUP = 4
NA_WIN_ROWS = NA_GROUP + NA_ROWS


def _na_geometry(S):
    rows = S // GRID_W
    assert rows % NA_GROUP == 0 and rows >= NA_WIN_ROWS + NA_GROUP
    return rows, rows // NA_GROUP


def _na_win_start(g, rows):
    return jnp.clip(NA_GROUP * g - NA_ROWS // 2, 0, rows - NA_WIN_ROWS)


def _na_class(g, groups):
    return jnp.where(g == 0, 0, jnp.where(g == groups - 1, 2, 1))


def _na_onehots(rows):
    groups = rows // NA_GROUP
    by_row = np.zeros((3, NA_GROUP, NA_WIN_ROWS, 2 * NA_ROWS - 1), np.float32)
    for cls, g in enumerate((0, 1, groups - 1)):
        ws = int(np.clip(NA_GROUP * g - NA_ROWS // 2, 0, rows - NA_WIN_ROWS))
        for qr in range(NA_GROUP):
            r = NA_GROUP * g + qr
            rs = int(np.clip(r - NA_ROWS // 2, 0, rows - NA_ROWS))
            for kr in range(NA_WIN_ROWS):
                if rs <= ws + kr < rs + NA_ROWS:
                    by_row[cls, qr, kr, ws + kr - r + NA_ROWS - 1] = 1.0
    c = np.arange(GRID_W)
    cs = np.clip(c - NA_COLS // 2, 0, GRID_W - NA_COLS)
    kc = np.arange(GRID_W)
    win = (kc[None, :] >= cs[:, None]) & (kc[None, :] < cs[:, None] + NA_COLS)
    idx = np.clip(kc[None, :] - c[:, None], -(NA_COLS - 1), NA_COLS - 1) + (NA_COLS - 1)
    by_col = ((idx[:, :, None] == np.arange(2 * NA_COLS - 1)[None, None, :]) & win[:, :, None]).astype(np.float32)
    return by_row, by_col


def _na_bias_tables(rpb, rows):
    by_row, by_col = _na_onehots(rows)
    H = rpb.shape[0]
    e1 = jnp.einsum('hij,ckj->hick', rpb, by_col, precision=lax.Precision.HIGHEST)
    e1 = jnp.where((by_col.sum(-1) > 0)[None, None], e1, -jnp.inf)
    none = jnp.full((H, GRID_W, GRID_W), -jnp.inf, _F32)
    pick, valid = by_row.argmax(-1), by_row.sum(-1) > 0
    tabs = []
    for z in range(3):
        bands = [jnp.concatenate([e1[:, pick[z, q, r]] if valid[z, q, r] else none for r in range(NA_WIN_ROWS)],
                                 axis=-1) for q in range(NA_GROUP)]
        tabs.append(jnp.stack(bands, axis=1).reshape(H, NA_GROUP * GRID_W, NA_WIN_ROWS * GRID_W))
    return jnp.stack(tabs)


def _na_bias_grad(dtab, rows):
    by_row, by_col = _na_onehots(rows)
    H = dtab.shape[1]
    pick, valid = by_row.argmax(-1), by_row.sum(-1) > 0
    d6 = dtab.reshape(3, H, NA_GROUP, GRID_W, NA_WIN_ROWS, GRID_W)
    slabs = [[] for _ in range(2 * NA_ROWS - 1)]
    for z in range(3):
        for q in range(NA_GROUP):
            for r in range(NA_WIN_ROWS):
                if valid[z, q, r]:
                    slabs[pick[z, q, r]].append(d6[z, :, q, :, r, :])
    zero = jnp.zeros((H, GRID_W, GRID_W), _F32)
    de1 = jnp.stack([sum(s[1:], s[0]) if s else zero for s in slabs], axis=1)
    return jnp.einsum('hick,ckj->hij', de1, by_col, precision=lax.Precision.HIGHEST)


def _na_fwd(qn, kn, vb, tab, name):
    S, W = qn.shape
    rows, groups = _na_geometry(S)
    npair = W // _LANES
    nq = NA_GROUP * GRID_W
    nk = NA_WIN_ROWS * GRID_W
    sc = float(NA_HD) ** -0.5

    def body(q_ref, k_ref, v_ref, b_ref, o_ref):
        g = pl.program_id(1)
        k0 = pl.multiple_of(_na_win_start(g, rows) * GRID_W, GRID_W)
        kw = k_ref[pl.ds(k0, nk), :]
        vw = v_ref[pl.ds(k0, nk), :]
        qv = q_ref[...]
        lane = lax.broadcasted_iota(jnp.int32, (nq, _LANES), 1)
        outs = []
        for hh in range(2):
            mine = (lane >= hh * NA_HD) & (lane < (hh + 1) * NA_HD)
            s = _dg(jnp.where(mine, qv, jnp.zeros_like(qv)), kw, 1, 1) * sc + b_ref[0, hh]
            e = jnp.exp(s - jnp.max(s, axis=-1, keepdims=True))
            p = e / jnp.sum(e, axis=-1, keepdims=True)
            outs.append(_dg(p, vw, 1, 0))
        o_ref[...] = jnp.where(lane < NA_HD, outs[0], outs[1])

    return pl.pallas_call(
        body, name=name, out_shape=jax.ShapeDtypeStruct((S, W), _F32), grid=(npair, groups),
        in_specs=[pl.BlockSpec((nq, _LANES), lambda p, g: (g, p)),
                  pl.BlockSpec((S, _LANES), lambda p, g: (0, p)),
                  pl.BlockSpec((S, _LANES), lambda p, g: (0, p)),
                  pl.BlockSpec((1, 2, nq, nk), lambda p, g: (_na_class(g, groups), p, 0, 0))],
        out_specs=pl.BlockSpec((nq, _LANES), lambda p, g: (g, p)),
        compiler_params=_params(("parallel", "arbitrary")))(qn, kn, vb, tab)


def _na_bwd(qn, kn, vb, tab, do, name):
    S, W = qn.shape
    rows, groups = _na_geometry(S)
    npair = W // _LANES
    nq = NA_GROUP * GRID_W
    nk = NA_WIN_ROWS * GRID_W
    sc = float(NA_HD) ** -0.5

    def body(q_ref, k_ref, v_ref, b_ref, do_ref, dq_ref, dk_ref, dv_ref, db_ref):
        g = pl.program_id(1)

        @pl.when(g == 0)
        def _():
            dk_ref[...] = jnp.zeros_like(dk_ref)
            dv_ref[...] = jnp.zeros_like(dv_ref)

        @pl.when((g <= 1) | (g == groups - 1))
        def _():
            db_ref[...] = jnp.zeros_like(db_ref)

        k0 = pl.multiple_of(_na_win_start(g, rows) * GRID_W, GRID_W)
        kw = k_ref[pl.ds(k0, nk), :]
        vw = v_ref[pl.ds(k0, nk), :]
        qv = q_ref[...]
        dov = do_ref[...]
        lane = lax.broadcasted_iota(jnp.int32, (nq, _LANES), 1)
        dqs = []
        dkw = jnp.zeros((nk, _LANES), _F32)
        dvw = jnp.zeros((nk, _LANES), _F32)
        for hh in range(2):
            mine = (lane >= hh * NA_HD) & (lane < (hh + 1) * NA_HD)
            qm = jnp.where(mine, qv, jnp.zeros_like(qv))
            dom = jnp.where(mine, dov, 0.0)
            s = _dg(qm, kw, 1, 1) * sc + b_ref[0, hh]
            e = jnp.exp(s - jnp.max(s, axis=-1, keepdims=True))
            p = e / jnp.sum(e, axis=-1, keepdims=True)
            dp = _dg(dom, vw, 1, 1)
            ds = p * (dp - jnp.sum(p * dp, axis=-1, keepdims=True))
            db_ref[0, hh] += ds
            dqs.append(_dg(ds, kw, 1, 0) * sc)
            dkw = dkw + _dg(ds, qm, 0, 0) * sc
            dvw = dvw + _dg(p, dom, 0, 0)
        dq_ref[...] = jnp.where(lane < NA_HD, dqs[0], dqs[1])
        dk_ref[pl.ds(k0, nk), :] += dkw
        dv_ref[pl.ds(k0, nk), :] += dvw

    blk = pl.BlockSpec((nq, _LANES), lambda p, g: (g, p))
    full = pl.BlockSpec((S, _LANES), lambda p, g: (0, p))
    tspec = pl.BlockSpec((1, 2, nq, nk), lambda p, g: (_na_class(g, groups), p, 0, 0))
    return pl.pallas_call(
        body, name=name,
        out_shape=(jax.ShapeDtypeStruct((S, W), _F32), jax.ShapeDtypeStruct((S, W), _F32),
                   jax.ShapeDtypeStruct((S, W), _F32), jax.ShapeDtypeStruct(tab.shape, _F32)),
        grid=(npair, groups), in_specs=[blk, full, full, tspec, blk],
        out_specs=(blk, full, full, tspec),
        compiler_params=_params(("arbitrary", "arbitrary")))(qn, kn, vb, tab, do)


def _mem_fwd(qn, km, vm, name):
    S, W = qn.shape
    hd = W // MEM_HEADS
    tq = _pick(S, (512, 256))
    sc = float(hd) ** -0.5

    def body(q_ref, k_ref, v_ref, o_ref):
        for h in range(MEM_HEADS):
            cs = slice(h * hd, (h + 1) * hd)
            s = _dg(q_ref[:, cs], k_ref[:, cs], 1, 1) * sc
            e = jnp.exp(s - jnp.max(s, axis=-1, keepdims=True))
            p = e / jnp.sum(e, axis=-1, keepdims=True)
            o_ref[:, cs] = _dg(p, v_ref[:, cs], 1, 0)

    full = pl.BlockSpec(km.shape, lambda i: (0, 0))
    return pl.pallas_call(
        body, name=name, out_shape=jax.ShapeDtypeStruct((S, W), _F32), grid=(S // tq,),
        in_specs=[pl.BlockSpec((tq, W), lambda i: (i, 0)), full, full],
        out_specs=pl.BlockSpec((tq, W), lambda i: (i, 0)),
        compiler_params=_params(("parallel",)))(qn, km, vm)


def _mem_bwd(qn, km, vm, do, name):
    S, W = qn.shape
    hd = W // MEM_HEADS
    tq = _pick(S, (512, 256))
    sc = float(hd) ** -0.5

    def body(q_ref, k_ref, v_ref, do_ref, dq_ref, dk_ref, dv_ref):
        @pl.when(pl.program_id(0) == 0)
        def _():
            dk_ref[...] = jnp.zeros_like(dk_ref)
            dv_ref[...] = jnp.zeros_like(dv_ref)

        for h in range(MEM_HEADS):
            cs = slice(h * hd, (h + 1) * hd)
            qh = q_ref[:, cs]
            kh = k_ref[:, cs]
            doh = do_ref[:, cs]
            s = _dg(qh, kh, 1, 1) * sc
            e = jnp.exp(s - jnp.max(s, axis=-1, keepdims=True))
            p = e / jnp.sum(e, axis=-1, keepdims=True)
            dp = _dg(doh, v_ref[:, cs], 1, 1)
            ds = p * (dp - jnp.sum(p * dp, axis=-1, keepdims=True))
            dq_ref[:, cs] = _dg(ds, kh, 1, 0) * sc
            dk_ref[:, cs] += _dg(ds, qh, 0, 0) * sc
            dv_ref[:, cs] += _dg(p, doh, 0, 0)

    full = pl.BlockSpec(km.shape, lambda i: (0, 0))
    row = pl.BlockSpec((tq, W), lambda i: (i, 0))
    return pl.pallas_call(
        body, name=name,
        out_shape=(jax.ShapeDtypeStruct((S, W), _F32), jax.ShapeDtypeStruct(km.shape, _F32),
                   jax.ShapeDtypeStruct(km.shape, _F32)),
        grid=(S // tq,), in_specs=[row, full, full, row], out_specs=(row, full, full),
        compiler_params=_params(("arbitrary",)))(qn, km, vm, do)


def _place():
    x, y, c = lax.axis_index("x"), lax.axis_index("y"), lax.axis_index("c")
    chips = [(1 - x, y), (x, 1 - y), (1 - x, 1 - y)]
    return x, y, c, chips


def _exchange(rider, name):
    ri = len(rider.ins)
    ro = len(rider.outs)

    def body(*refs):
        ins, outs, sems = refs[:ri], refs[ri:ri + ro], refs[ri + ro:]
        rider.start(ins, outs, sems)
        rider.finish(ins, outs, sems)

    return pl.pallas_call(
        body, name=name, out_shape=tuple(rider.outs), in_specs=[_ANY] * ri, out_specs=(_ANY,) * ro,
        scratch_shapes=rider.sems, input_output_aliases={i: o for i, o in rider.aliases},
        compiler_params=_params())(*rider.ins)


def _pair_sems(*shape):
    return [pltpu.SemaphoreType.DMA(shape), pltpu.SemaphoreType.DMA(shape)]


def _gather_ici(shards):
    n = len(shards)

    def copies(w, g, sems):
        send, recv = sems
        x, y, c, chips = _place()
        me = 2 * x + y
        out, back = [], []
        for t in range(n):
            for j, (cx, cy) in enumerate(chips):
                out.append(pltpu.make_async_remote_copy(
                    src_ref=w[t].at[c], dst_ref=g[t].at[me, c], send_sem=send.at[t, j], recv_sem=recv.at[t, j],
                    device_id=(cx, cy, c), device_id_type=_MESH))
                back.append(functools.partial(
                    pltpu.make_async_remote_copy,
                    src_ref=w[t].at[c], dst_ref=g[t].at[2 * cx + cy, c], send_sem=send.at[t, j],
                    recv_sem=recv.at[t, j], device_id=(cx, cy, c), device_id_type=_MESH))
        return out, back

    def start(w, g, sems):
        for cp in copies(w, g, sems)[0]:
            cp.start()

    def finish(w, g, sems):
        out, back = copies(w, g, sems)
        for make in back:
            make().wait_recv()
        for cp in out:
            cp.wait_send()

    return _Rider(shards, [jax.ShapeDtypeStruct((4,) + s.shape, s.dtype) for s in shards], _pair_sems(n, 3),
                  start, finish)


def _gather_d2d(gs):
    n = len(gs)

    def copies(g, sems):
        send, recv = sems
        x, y, c, chips = _place()
        out, back = [], []
        for t in range(n):
            for j, (cx, cy) in enumerate(chips):
                mine, theirs = g[t].at[2 * cx + cy, c], g[t].at[2 * cx + cy, 1 - c]
                out.append(pltpu.make_async_remote_copy(
                    src_ref=mine, dst_ref=mine, send_sem=send.at[t, j], recv_sem=recv.at[t, j],
                    device_id=(x, y, 1 - c), device_id_type=_MESH))
                back.append(functools.partial(
                    pltpu.make_async_remote_copy,
                    src_ref=mine, dst_ref=theirs, send_sem=send.at[t, j], recv_sem=recv.at[t, j],
                    device_id=(x, y, 1 - c), device_id_type=_MESH))
        return out, back

    def start(_, g, sems):
        for cp in copies(g, sems)[0]:
            cp.start()

    def finish(_, g, sems):
        out, back = copies(g, sems)
        for make in back:
            make().wait_recv()
        for cp in out:
            cp.wait_send()

    return _Rider(gs, [jax.ShapeDtypeStruct(g.shape, g.dtype) for g in gs], _pair_sems(n, 3), start, finish,
                  aliases=[(t, t) for t in range(n)])


def _swap_halves(ts):
    n = len(ts)

    def copies(t_in, land, sems):
        send, recv = sems
        x, y, c, _ = _place()
        return [pltpu.make_async_remote_copy(
            src_ref=t_in[t].at[:, 1 - c], dst_ref=land[t], send_sem=send.at[t], recv_sem=recv.at[t],
            device_id=(x, y, 1 - c), device_id_type=_MESH) for t in range(n)]

    def start(t_in, land, sems):
        for cp in copies(t_in, land, sems):
            cp.start()

    def finish(t_in, land, sems):
        for cp in copies(t_in, land, sems):
            cp.wait()

    return _Rider(ts, [jax.ShapeDtypeStruct((4,) + t.shape[2:], t.dtype) for t in ts], _pair_sems(n), start, finish)


def _scatter_chips(ps):
    n = len(ps)

    def copies(p, land, sems):
        send, recv = sems
        x, y, c, chips = _place()
        me = 2 * x + y
        out, back = [], []
        for t in range(n):
            for j, (cx, cy) in enumerate(chips):
                out.append(pltpu.make_async_remote_copy(
                    src_ref=p[t].at[2 * cx + cy], dst_ref=land[t].at[me], send_sem=send.at[t, j],
                    recv_sem=recv.at[t, j], device_id=(cx, cy, c), device_id_type=_MESH))
                back.append(functools.partial(
                    pltpu.make_async_remote_copy,
                    src_ref=p[t].at[me], dst_ref=land[t].at[2 * cx + cy], send_sem=send.at[t, j],
                    recv_sem=recv.at[t, j], device_id=(cx, cy, c), device_id_type=_MESH))
        return out, back

    def start(p, land, sems):
        for cp in copies(p, land, sems)[0]:
            cp.start()

    def finish(p, land, sems):
        out, back = copies(p, land, sems)
        for make in back:
            make().wait_recv()
        for cp in out:
            cp.wait_send()

    return _Rider(ps, [jax.ShapeDtypeStruct(t.shape, t.dtype) for t in ps], _pair_sems(n, 3), start, finish)


def _swap_reduced(rs):
    n = len(rs)

    def copies(out, sems):
        send, recv = sems
        x, y, c, _ = _place()
        return [pltpu.make_async_remote_copy(
            src_ref=out[t].at[c], dst_ref=out[t].at[c], send_sem=send.at[t], recv_sem=recv.at[t],
            device_id=(x, y, 1 - c), device_id_type=_MESH) for t in range(n)]

    def start(_, out, sems):
        for cp in copies(out, sems):
            cp.start()

    def finish(_, out, sems):
        for cp in copies(out, sems):
            cp.wait()

    return _Rider(rs, [jax.ShapeDtypeStruct(t.shape, t.dtype) for t in rs], _pair_sems(n), start, finish,
                  aliases=[(t, t) for t in range(n)])


def _all_devices(v, reduce, name):
    rows = v.shape[0]

    def body(v_ref, o_ref, *rest):
        if reduce:
            all_ref, send, recv = rest
        else:
            send, recv = rest
            all_ref = o_ref
        x, y, c, _ = _place()
        me = 4 * x + 2 * y + c
        all_ref[me] = v_ref[...]
        cps = []
        for k in range(1, 8):
            fx, fy, fc = (k >> 2) & 1, (k >> 1) & 1, k & 1
            to = (x ^ fx, y ^ fy, c ^ fc)
            cps.append(pltpu.make_async_remote_copy(
                src_ref=v_ref, dst_ref=all_ref.at[me], send_sem=send.at[k - 1], recv_sem=recv.at[k - 1],
                device_id=to, device_id_type=_MESH))
        for cp in cps:
            cp.start()
        for k in range(1, 8):
            fx, fy, fc = (k >> 2) & 1, (k >> 1) & 1, k & 1
            frm = 4 * (x ^ fx) + 2 * (y ^ fy) + (c ^ fc)
            pltpu.make_async_remote_copy(
                src_ref=v_ref, dst_ref=all_ref.at[frm], send_sem=send.at[k - 1], recv_sem=recv.at[k - 1],
                device_id=(x, y, c), device_id_type=_MESH).wait_recv()
        for cp in cps:
            cp.wait_send()
        if reduce:
            acc = all_ref[0]
            for d in range(1, 8):
                acc = acc + all_ref[d]
            o_ref[...] = acc

    vm = pl.BlockSpec(memory_space=pltpu.VMEM)
    if reduce:
        out_shape = jax.ShapeDtypeStruct((rows, _LANES), _F32)
        scratch = [pltpu.VMEM((8, rows, _LANES), _F32)]
    else:
        out_shape = jax.ShapeDtypeStruct((8, rows, _LANES), _F32)
        scratch = []
    return pl.pallas_call(
        body, name=name, out_shape=out_shape, in_specs=[vm], out_specs=vm,
        scratch_shapes=scratch + [pltpu.SemaphoreType.DMA((7,)), pltpu.SemaphoreType.DMA((7,))],
        compiler_params=_params())(v)


def _weight_halves(shards):
    return [shards[k].reshape((2, shards[k].shape[0] // 2) + shards[k].shape[1:]) for k in BIG]


def _gather_layer_weights(halves, name):
    g1 = _exchange(_gather_ici(halves), name + "_ici")
    g2 = _exchange(_gather_d2d(g1), name + "_d2d")
    return _assemble_weights(g2, halves)


def _assemble_weights(got, halves):
    chip = 2 * lax.axis_index("x") + lax.axis_index("y")
    full = {}
    for k, g, own in zip(BIG, got, halves):
        g = lax.dynamic_update_slice(g, own[None], (chip, 0, 0, 0))
        _, _, r, c = g.shape
        if k in ('w_in', 'ffn_w13'):
            full[k] = g
        else:
            full[k] = g.reshape(8 * r, c)
    return full


def _grads_as_shards(grads):
    ts = []
    for k in BIG:
        g = grads[k]
        if k in ('w_in', 'ffn_w13'):
            ts.append(g)
        else:
            r, c = g.shape[0] // 8, g.shape[1]
            ts.append(g.reshape(4, 2, r, c))
    return ts


def _pair_sums(ts, landed, cidx):
    return [_add_pair(t, la, cidx, "reduce_add2_" + k) for k, t, la in zip(BIG, ts, landed)]


def _chip_sums(partial, slots, cidx):
    chip = (2 * lax.axis_index("x") + lax.axis_index("y")).astype(jnp.int32).reshape(1)
    return [_add_four(p, s, chip, cidx, "reduce_add4_" + k) for k, p, s in zip(BIG, partial, slots)]


def _assemble_shards(joined):
    return {k: j.reshape((2 * j.shape[1], j.shape[2])) for k, j in zip(BIG, joined)}


def _reduce_layer_grads(ts, cidx, name):
    landed = _exchange(_swap_halves(ts), name + "_swap")
    partial = _pair_sums(ts, landed, cidx)
    slots = _exchange(_scatter_chips(partial), name + "_scatter")
    reduced = _chip_sums(partial, slots, cidx)
    return _assemble_shards(_exchange(_swap_reduced(reduced), name + "_join"))


def _pack_rows(arrs):
    parts, spans = [], []
    off = 0
    for a in arrs:
        n = int(np.prod(a.shape))
        pad = (-n) % (8 * _LANES)
        parts.append(jnp.pad(a.reshape(-1), (0, pad)))
        spans.append((off, n, a.shape))
        off += n + pad
    return jnp.concatenate(parts).reshape(-1, _LANES), spans


def _unpack_rows(packed, spans):
    flat = packed.reshape(-1)
    return [flat[o:o + n].reshape(shape) for o, n, shape in spans]


def _in_split(d):
    dk, dv, w = d // 4, d // 2, d // 4
    names = [('g_q', dk), ('g_k', dk), ('g_v', dv), ('g_r', dv), ('g_f', GLA_GATE_RANK), ('g_b', GLA_GATE_RANK),
             ('n_q', w), ('n_k', w), ('n_v', w), ('m_q', w)]
    out, off = {}, 0
    for nme, wd in names:
        out[nme] = (off, wd)
        off += wd
    return out


def _w_in_from_shards(g):
    _, _, R, C = g.shape
    sp = _in_split(2 * R)
    f0, n0 = sp['g_f'][0], sp['n_q'][0]

    def cols(lo, hi, h):
        out = []
        for j in range(4):
            a, b = max(lo, j * C), min(hi, (j + 1) * C)
            if a < b:
                out.append(g[j, h, :, a - j * C:b - j * C])
        return out

    main = jnp.concatenate([jnp.concatenate(cols(0, f0, h) + cols(n0, 4 * C, h), axis=1) for h in range(2)], axis=0)
    gate = jnp.concatenate([jnp.concatenate(cols(f0, n0, h), axis=1) for h in range(2)], axis=0)
    return main, jnp.pad(gate, ((0, 0), (0, _LANES - (n0 - f0))))


def _w_in_to_shards(main, gate):
    R = main.shape[0] // 2
    sp = _in_split(2 * R)
    f0, n0 = sp['g_f'][0], sp['n_q'][0]
    C = (main.shape[1] + n0 - f0) // 4

    def cols(lo, hi, h):
        rows = slice(h * R, (h + 1) * R)
        out = []
        for src, a, b, shift in ((main, 0, f0, 0), (gate, f0, n0, -f0), (main, n0, 4 * C, f0 - n0)):
            lo2, hi2 = max(lo, a), min(hi, b)
            if lo2 < hi2:
                out.append(src[rows, lo2 + shift:hi2 + shift])
        return out

    return jnp.stack([jnp.stack([jnp.concatenate(cols(j * C, (j + 1) * C, h), axis=1) for h in range(2)])
                      for j in range(4)])


def _gate_weight(wg2_f, wg2_b):
    r, dk = wg2_f.shape
    top = jnp.concatenate([wg2_f, jnp.zeros_like(wg2_f)], axis=1)
    mid = jnp.concatenate([jnp.zeros_like(wg2_b), wg2_b], axis=1)
    return jnp.concatenate([top, mid, jnp.zeros((_LANES - 2 * r, 2 * dk), wg2_f.dtype)], axis=0)


def _layer_fwd(l, x, mem, w, full, tab, next_halves=None):
    S, D = x.shape
    dk, dv, nw = D // 4, D // 2, D // 4
    sv = {}
    w_main, w_gate = _w_in_from_shards(full['w_in'])
    sv['w_main'], sv['w_gate'] = w_main, w_gate
    xn, r1 = _rmsnorm_fwd(x, w['attn_norm'][None], f"rms1_fwd")
    if next_halves is None:
        proj = _mm(xn, w_main, 'nn', "mm_proj", out_dtype=_BF)
    else:
        proj, g_a = _mm(xn, w_main, 'nn', "mm_proj_gather", out_dtype=_BF, rider=_gather_ici(next_halves[:3]))
    gfb = _mm(xn, w_gate, 'nn', "mm_gate")
    sv.update(x=x, xn=xn, r1=r1, proj=proj, gfb=gfb)
    wg = _gate_weight(w['gla_wg2_f'], w['gla_wg2_b']).astype(_BF)
    bg = jnp.concatenate([w['gla_bg_f'], w['gla_bg_b']])[None]
    o_f, st_f = _gla_fwd(proj, gfb, wg, bg, False, "gla_fwd_f")
    o_b, st_b = _gla_fwd(proj, gfb, wg, bg, True, "gla_fwd_b")
    y_gla = _gla_post_fwd(o_f, o_b, proj, (2 * dk) // dv + 1, w['gla_out_norm'][None], "gla_post_fwd")
    sv.update(wg=wg, bg=bg, o_f=o_f, o_b=o_b, st_f=st_f, st_b=st_b)
    c0 = (2 * dk + 2 * dv) // nw
    qn = _headnorm_fwd(proj, c0, nw, NA_HD, jnp.tile(w['na_q_norm'], nw // NA_HD)[None], "na_qnorm_fwd")
    kn = _headnorm_fwd(proj, c0 + 1, nw, NA_HD, jnp.tile(w['na_k_norm'], nw // NA_HD)[None], "na_knorm_fwd")
    vb = proj[:, (c0 + 2) * nw:(c0 + 3) * nw].astype(_BF)
    o_na = _na_fwd(qn, kn, vb, tab, "na_fwd")
    y_na = _headnorm_fwd(o_na, 0, nw, NA_HD, w['na_out_norm'][None], "na_onorm_fwd")
    sv.update(qn=qn, kn=kn, vb=vb, o_na=o_na)
    mhd = nw // MEM_HEADS
    mqn = _headnorm_fwd(proj, c0 + 3, nw, mhd, jnp.tile(w['mem_q_norm'], MEM_HEADS)[None], "mem_qnorm_fwd")
    memn, rm = _rmsnorm_fwd(mem, w['mem_norm'][None], "mem_rms_fwd")
    kv = _mm(memn, full['mem_wkv'], 'nn', "mm_memkv")
    km = _headnorm_fwd(kv, 0, nw, mhd, jnp.tile(w['mem_k_norm'], MEM_HEADS)[None], "mem_knorm_fwd")
    vm = kv[:, nw:].astype(_BF)
    o_mem = _mem_fwd(mqn, km, vm, "mem_fwd")
    y_mem = _headnorm_fwd(o_mem, 0, nw, mhd, w['mem_out_norm'][None], "mem_onorm_fwd")
    sv.update(mqn=mqn, memn=memn, rm=rm, kv=kv, km=km, vm=vm, o_mem=o_mem)
    y = jnp.concatenate([y_gla, y_na, y_mem], axis=1)
    x1 = _mm(y, full['w_out'], 'nn', "mm_out", res=x)
    h, r2 = _rmsnorm_fwd(x1, w['ffn_norm'][None], "rms2_fwd")
    if next_halves is None:
        a, gate, up = _ffn_up_fwd(h, full['ffn_w13'], "ffn_up_fwd")
        x2 = _mm(a, full['ffn_w2'], 'nn', "mm_w2", res=x1)
        got = None
    else:
        (a, gate, up), g_b = _ffn_up_fwd(h, full['ffn_w13'], "ffn_up_fwd_gather", _gather_ici(next_halves[3:]))
        x2, got = _mm(a, full['ffn_w2'], 'nn', "mm_w2_gather", res=x1, rider=_gather_d2d(list(g_a) + list(g_b)))
    sv.update(y=y, x1=x1, h=h, r2=r2, gate=gate, up=up, a=a)
    return x2, sv, got


def _layer_bwd(l, dx2, dx2b, mem, w, full, tab, sv, pending=None, cidx=None):
    S, D = dx2.shape
    dk, dv, nw = D // 4, D // 2, D // 4
    gb, gs = {}, {}
    if pending is None:
        dgate, dup = _ffn_up_bwd(dx2b, full['ffn_w2'], sv['gate'], sv['up'], "ffn_up_bwd")
    else:
        (dgate, dup), landed = _ffn_up_bwd(dx2b, full['ffn_w2'], sv['gate'], sv['up'], "ffn_up_bwd_swap",
                                           _swap_halves(pending))
        partial = _pair_sums(pending, landed, cidx)
    gb['ffn_w2'] = _mm(sv['a'], dx2b, 'tn', "mm_dw2")
    if pending is None:
        dh = _ffn_up_dh(dgate, dup, full['ffn_w13'], "ffn_up_dh")
        gb['ffn_w13'] = _ffn_up_dw(sv['h'], dgate, dup, "ffn_up_dw")
    else:
        dh, slots_b = _ffn_up_dh(dgate, dup, full['ffn_w13'], "ffn_up_dh_scatter", _scatter_chips(partial[3:]))
        gb['ffn_w13'], slots_a = _ffn_up_dw(sv['h'], dgate, dup, "ffn_up_dw_scatter", _scatter_chips(partial[:3]))
        reduced = _chip_sums(partial, list(slots_a) + list(slots_b), cidx)
    dx1, dx1b, g = _rmsnorm_bwd(sv['x1'], sv['r2'], w['ffn_norm'][None], dh, dx2, "rms2_bwd")
    gs['ffn_norm'] = g[0]
    dy = _mm(dx1b, full['w_out'], 'nt', "mm_dy", out_dtype=_BF)
    gb['w_out'] = _mm(sv['y'], dx1b, 'tn', "mm_dwout")
    c0 = (2 * dk + 2 * dv) // nw
    mhd = nw // MEM_HEADS
    do_mem, g = _headnorm_bwd(sv['o_mem'], 0, nw, mhd, w['mem_out_norm'][None], dy, (dv + nw) // nw, "mem_onorm_bwd")
    gs['mem_out_norm'] = g[0]
    dmqn, dkm, dvm = _mem_bwd(sv['mqn'], sv['km'], sv['vm'], do_mem, "mem_bwd")
    dmq, g = _headnorm_bwd(sv['proj'], c0 + 3, nw, mhd, jnp.tile(w['mem_q_norm'], MEM_HEADS)[None], dmqn, 0,
                           "mem_qnorm_bwd", _BF)
    gs['mem_q_norm'] = g[0].reshape(MEM_HEADS, mhd).sum(0)
    dkvk, g = _headnorm_bwd(sv['kv'], 0, nw, mhd, jnp.tile(w['mem_k_norm'], MEM_HEADS)[None], dkm, 0,
                            "mem_knorm_bwd")
    gs['mem_k_norm'] = g[0].reshape(MEM_HEADS, mhd).sum(0)
    dkv = jnp.concatenate([dkvk, dvm], axis=1).astype(_BF)
    gb['mem_wkv'] = _mm(sv['memn'], dkv, 'tn', "mm_dwkv")
    dmemn = _mm(dkv, full['mem_wkv'], 'nt', "mm_dmemn")
    _, _, g = _rmsnorm_bwd(mem, sv['rm'], w['mem_norm'][None], dmemn, None, "mem_rms_bwd")
    gs['mem_norm'] = g[0]
    do_na, g = _headnorm_bwd(sv['o_na'], 0, nw, NA_HD, w['na_out_norm'][None], dy, dv // nw, "na_onorm_bwd")
    gs['na_out_norm'] = g[0]
    dqn, dkn, dnv, dtab = _na_bwd(sv['qn'], sv['kn'], sv['vb'], tab, do_na, "na_bwd")
    gs['na_rpb'] = dtab
    dnq, g = _headnorm_bwd(sv['proj'], c0, nw, NA_HD, jnp.tile(w['na_q_norm'], nw // NA_HD)[None], dqn, 0,
                           "na_qnorm_bwd", _BF)
    gs['na_q_norm'] = g[0].reshape(nw // NA_HD, NA_HD).sum(0)
    dnk, g = _headnorm_bwd(sv['proj'], c0 + 1, nw, NA_HD, jnp.tile(w['na_k_norm'], nw // NA_HD)[None], dkn, 0,
                           "na_knorm_bwd", _BF)
    gs['na_k_norm'] = g[0].reshape(nw // NA_HD, NA_HD).sum(0)
    do_gla, dgr, g = _gla_post_bwd(sv['o_f'], sv['o_b'], sv['proj'], (2 * dk) // dv + 1, w['gla_out_norm'][None], dy,
                                   "gla_post_bwd")
    gs['gla_out_norm'] = g[0]
    dq1, dk1, dv1, dz_f = _gla_bwd(sv['proj'], sv['gfb'], sv['wg'], sv['bg'], sv['st_f'], do_gla, False, None,
                                   "gla_bwd_f")
    dgq, dgk, dgv, dz_b = _gla_bwd(sv['proj'], sv['gfb'], sv['wg'], sv['bg'], sv['st_b'], do_gla, True,
                                   (dq1, dk1, dv1), "gla_bwd_b")
    dz = jnp.concatenate([dz_f, dz_b], axis=1).astype(_BF)
    ones_lane = (jnp.arange(_LANES) == 2 * GLA_GATE_RANK)[None]
    gfb_aug = jnp.where(ones_lane, 1.0, sv['gfb']).astype(_BF)
    dwg = _mm(gfb_aug, dz, 'tn', "mm_dwg")
    r16 = GLA_GATE_RANK
    gs['gla_wg2_f'] = dwg[:r16, :dk]
    gs['gla_wg2_b'] = dwg[r16:2 * r16, dk:]
    gs['gla_bg_f'] = dwg[2 * r16, :dk]
    gs['gla_bg_b'] = dwg[2 * r16, dk:]
    dgfb = _mm(dz, sv['wg'], 'nt', "mm_dgfb", out_dtype=_BF)
    dproj = jnp.concatenate([dgq.astype(_BF), dgk.astype(_BF), dgv.astype(_BF), dgr, dnq, dnk, dnv.astype(_BF), dmq],
                            axis=1)
    t = _mm(dgfb, sv['w_gate'], 'nt', "mm_dxn_gate")
    if pending is None:
        dxn = _mm(dproj, sv['w_main'], 'nt', "mm_dxn", res=t)
        done = None
    else:
        dxn, joined = _mm(dproj, sv['w_main'], 'nt', "mm_dxn_join", res=t, rider=_swap_reduced(reduced))
        done = _assemble_shards(joined)
    dw_main = _mm(sv['xn'], dproj, 'tn', "mm_dwmain")
    dw_gate = _mm(sv['xn'], dgfb, 'tn', "mm_dwgate")
    gb['w_in'] = _w_in_to_shards(dw_main, dw_gate)
    dx, dxb, g = _rmsnorm_bwd(sv['x'], sv['r1'], w['attn_norm'][None], dxn, dx1, "rms1_bwd")
    gs['attn_norm'] = g[0]
    return dx, dxb, gb, gs, done


def kernel(x, mem, attn_norm, w_in, gla_wg2_f, gla_bg_f, gla_wg2_b, gla_bg_b, gla_out_norm, na_q_norm, na_k_norm, na_rpb, na_out_norm, mem_norm, mem_wkv, mem_q_norm, mem_k_norm, mem_out_norm, w_out, ffn_norm, ffn_w13, ffn_w2, loss_target, m_attn_norm, m_w_in, m_gla_wg2_f, m_gla_bg_f, m_gla_wg2_b, m_gla_bg_b, m_gla_out_norm, m_na_q_norm, m_na_k_norm, m_na_rpb, m_na_out_norm, m_mem_norm, m_mem_wkv, m_mem_q_norm, m_mem_k_norm, m_mem_out_norm, m_w_out, m_ffn_norm, m_ffn_w13, m_ffn_w2, v_attn_norm, v_w_in, v_gla_wg2_f, v_gla_bg_f, v_gla_wg2_b, v_gla_bg_b, v_gla_out_norm, v_na_q_norm, v_na_k_norm, v_na_rpb, v_na_out_norm, v_mem_norm, v_mem_wkv, v_mem_q_norm, v_mem_k_norm, v_mem_out_norm, v_w_out, v_ffn_norm, v_ffn_w13, v_ffn_w2):
    args = locals()
    W = {k: args[k] for k in WEIGHTS}
    M = {k: args['m_' + k] for k in WEIGHTS}
    V = {k: args['v_' + k] for k in WEIGHTS}
    depth = attn_norm.shape[0]
    xs, mems, tgt = x[0], mem[0], loss_target[0]
    cidx = lax.axis_index("c").astype(jnp.int32).reshape(1)

    gate_rows, gate_spans = _pack_rows([W[k] for k in SMALL_SHARDED])
    gate_all = _all_devices(gate_rows, False, "gate_weights_gather")
    gate_full = {}
    for i, k in enumerate(SMALL_SHARDED):
        per_chip = [_unpack_rows(gate_all[2 * j], gate_spans)[i] for j in range(4)]
        gate_full[k] = jnp.concatenate(per_chip, axis=-1)

    small = [k for k in WEIGHTS if k not in BIG]

    def layer_small(l):
        d = {k: W[k][l] for k in small if k not in SMALL_SHARDED}
        d.update({k: gate_full[k][l] for k in SMALL_SHARDED})
        return d

    saved, fulls, tabs = [], [], []
    cur = xs
    grid_rows = xs.shape[0] // GRID_W
    all_tabs = jax.vmap(lambda r: _na_bias_tables(r, grid_rows))(W['na_rpb'])
    halves = [_weight_halves({k: W[k][l].astype(_BF) for k in BIG}) for l in range(depth)]
    full = _gather_layer_weights(halves[0], "gather_weights")
    for l in range(depth):
        tab = all_tabs[l]
        nxt = halves[l + 1] if l + 1 < depth else None
        cur, sv, got = _layer_fwd(l, cur, mems, layer_small(l), full, tab, nxt)
        saved.append(sv)
        fulls.append(full)
        tabs.append(tab)
        if nxt is not None:
            full = _assemble_weights(got, nxt)
    loss_tile, dy, dyb = _loss_head(cur, tgt, "loss_head")
    loss = lax.psum(loss_tile[0, 0], ("x", "y", "c"))

    big_grads = [None] * depth
    small_grads = [None] * depth
    pending = None
    for l in range(depth - 1, -1, -1):
        dy, dyb, gb, gs, done = _layer_bwd(l, dy, dyb, mems, layer_small(l), fulls[l], tabs[l], saved[l], pending, cidx)
        if pending is not None:
            big_grads[l + 1] = done
        pending = _grads_as_shards(gb)
        small_grads[l] = gs
    big_grads[0] = _reduce_layer_grads(pending, cidx, "reduce_grads")
    grad_x = dy[None]

    small_stack = {k: jnp.stack([small_grads[l][k] for l in range(depth)]) for k in small}
    small_stack['na_rpb'] = jax.vmap(lambda d: _na_bias_grad(d, grid_rows))(small_stack['na_rpb'])
    packed, spans = _pack_rows([small_stack[k] for k in small])
    summed = _unpack_rows(_all_devices(packed, True, "small_grads_sum"), spans)
    G = dict(zip(small, summed))
    chip = 2 * lax.axis_index("x") + lax.axis_index("y")
    for k in SMALL_SHARDED:
        wdt = W[k].shape[-1]
        G[k] = lax.dynamic_slice_in_dim(G[k], chip * wdt, wdt, axis=2)
    for k in BIG:
        G[k] = jnp.stack([big_grads[l][k] for l in range(depth)])

    delta, new_m, new_v = {}, {}, {}
    for k in WEIGHTS:
        delta[k], new_m[k], new_v[k] = _adamw(W[k], G[k], M[k], V[k], "adamw_" + k)
    return (loss, grad_x, *[G[k] for k in WEIGHTS], *[delta[k] for k in WEIGHTS], *[new_m[k] for k in WEIGHTS],
            *[new_v[k] for k in WEIGHTS])
```

```python
import functools

import numpy as np
import jax
import jax.numpy as jnp
from jax import lax
from jax.experimental import pallas as pl
from jax.experimental.pallas import tpu as pltpu

_F32 = jnp.float32
_BF = jnp.bfloat16
_MESH = pl.DeviceIdType.MESH

_VMEM_LIMIT_BYTES = 56 * 1024 * 1024
_LANES = 128

RMS_EPS = 1e-6
GLA_HEADS = 4
GLA_GATE_RANK = 16
GLA_TAU = 16.0
GLA_CHUNK = 64
GRID_W = 64
NA_HD = 64
NA_ROWS = 8
NA_COLS = 16
MEM_HEADS = 4
ADAM_LR = 0.001
ADAM_B1 = 0.9
ADAM_B2 = 0.999
ADAM_EPS = 1e-08
ADAM_WD = 0.01
ADAM_STEP = 10

WEIGHTS = ['attn_norm', 'w_in', 'gla_wg2_f', 'gla_bg_f', 'gla_wg2_b', 'gla_bg_b', 'gla_out_norm', 'na_q_norm',
           'na_k_norm', 'na_rpb', 'na_out_norm', 'mem_norm', 'mem_wkv', 'mem_q_norm', 'mem_k_norm', 'mem_out_norm',
           'w_out', 'ffn_norm', 'ffn_w13', 'ffn_w2']
BIG = ['w_in', 'mem_wkv', 'w_out', 'ffn_w13', 'ffn_w2']
SMALL_SHARDED = ['gla_wg2_f', 'gla_wg2_b']


def _pick(n, cands):
    for c in cands:
        if n % c == 0:
            return c
    return n


def _row_block(rows, row_bytes, cap_bytes):
    best = 8
    for rb in range(8, rows + 1, 8):
        if rows % rb == 0 and rb * row_bytes <= cap_bytes:
            best = rb
    return best


def _params(sem=None):
    return pltpu.CompilerParams(dimension_semantics=sem, vmem_limit_bytes=_VMEM_LIMIT_BYTES)


def _dg(a, b, ca, cb):
    return lax.dot_general(a.astype(_BF), b.astype(_BF), (((ca,), (cb,)), ((), ())), preferred_element_type=_F32)


def _split(a):
    hi = a.astype(_BF)
    return hi, (a - hi.astype(_F32)).astype(_BF)


def _dg_hl(a, b, ca, cb):
    hi, lo = _split(a)
    return _dg(hi, b, ca, cb) + _dg(lo, b, ca, cb)


def _dg_lh(a, b, ca, cb):
    hi, lo = _split(b)
    return _dg(a, hi, ca, cb) + _dg(a, lo, ca, cb)


def _sigmoid(x):
    return 1.0 / (1.0 + jnp.exp(-x))


def _log_sigmoid(z):
    return jnp.minimum(z, 0.0) - jnp.log(1.0 + jnp.exp(-jnp.abs(z)))


def _block_diag(width, hd):
    i = np.arange(width) // hd
    return jnp.asarray((i[:, None] == i[None, :]).astype(np.float32), dtype=_BF)


class _Rider:
    def __init__(self, ins, outs, sems, start, finish, aliases=()):
        self.ins, self.outs, self.sems = list(ins), list(outs), list(sems)
        self.start, self.finish, self.aliases = start, finish, tuple(aliases)


def _call(core, name, grid, in_specs, out_specs, out_shape, scratch, sem, args, rider=None):
    out_specs, out_shape = tuple(out_specs), tuple(out_shape)
    if rider is None:
        return pl.pallas_call(core, name=name, out_shape=out_shape, grid=grid, in_specs=list(in_specs),
                              out_specs=out_specs, scratch_shapes=list(scratch), compiler_params=_params(sem))(*args)
    ni, no, ns = len(in_specs), len(out_specs), len(scratch)
    ri, ro = len(rider.ins), len(rider.outs)

    def body(*refs):
        ins, rins = refs[:ni], refs[ni:ni + ri]
        outs, routs = refs[ni + ri:ni + ri + no], refs[ni + ri + no:ni + ri + no + ro]
        scr, rsem = refs[ni + ri + no + ro:ni + ri + no + ro + ns], refs[ni + ri + no + ro + ns:]
        first = pl.program_id(0) == 0
        last = pl.program_id(0) == grid[0] - 1
        for ax in range(1, len(grid)):
            first = first & (pl.program_id(ax) == 0)
            last = last & (pl.program_id(ax) == grid[ax] - 1)

        @pl.when(first)
        def _():
            rider.start(rins, routs, rsem)

        core(*ins, *outs, *scr)

        @pl.when(last)
        def _():
            rider.finish(rins, routs, rsem)

    res = pl.pallas_call(
        body, name=name, out_shape=out_shape + tuple(rider.outs), grid=grid,
        in_specs=list(in_specs) + [_ANY] * ri, out_specs=out_specs + (_ANY,) * ro,
        scratch_shapes=list(scratch) + rider.sems,
        input_output_aliases={ni + i: no + o for i, o in rider.aliases},
        compiler_params=_params(("arbitrary",) * len(grid)))(*args, *rider.ins)
    return tuple(res[:no]), tuple(res[no:])


_ANY = pl.BlockSpec(memory_space=pl.ANY)


def _mm(a, b, mode, name, out_dtype=_F32, res=None, rider=None):
    def finish(r):
        return r[0] if rider is None else (r[0][0], r[1])

    if mode == 'nn':
        (M, K), N = a.shape, b.shape[1]
    elif mode == 'nt':
        (M, K), N = a.shape, b.shape[0]
    else:
        (K, M), N = a.shape, b.shape[1]
    ca, cb = {'nn': (1, 0), 'nt': (1, 1), 'tn': (0, 0)}[mode]
    has_res = res is not None
    args = (a, b) + ((res,) if has_res else ())
    out_shape = jax.ShapeDtypeStruct((M, N), out_dtype)
    if K <= _K_RESIDENT:
        tm = _pick(M, (1024, 512, 256, 128))
        tn = _pick(N, (1024, 512, 256, 128))
        a_spec = (pl.BlockSpec((K, tm), lambda i, j: (0, i)) if mode == 'tn'
                  else pl.BlockSpec((tm, K), lambda i, j: (i, 0)))
        b_spec = (pl.BlockSpec((tn, K), lambda i, j: (j, 0)) if mode == 'nt'
                  else pl.BlockSpec((K, tn), lambda i, j: (0, j)))
        o_spec = pl.BlockSpec((tm, tn), lambda i, j: (i, j))

        def body1(*refs):
            a_ref, b_ref = refs[:2]
            o_ref = refs[-1]
            out = _dg(a_ref[...], b_ref[...], ca, cb)
            if has_res:
                out = out + refs[2][...]
            o_ref[...] = out.astype(o_ref.dtype)

        return finish(_call(body1, name, (M // tm, N // tn), [a_spec, b_spec] + ([o_spec] if has_res else []),
                            [o_spec], [out_shape], [], ("parallel", "parallel"), args, rider))

    tm, tn, tk = _loop_tiles(M, N, K, has_res, out_dtype)
    nk = K // tk
    if mode == 'tn':
        a_spec = pl.BlockSpec((tk, tm), lambda i, j, k: (k, i))
    else:
        a_spec = pl.BlockSpec((tm, tk), lambda i, j, k: (i, k))
    if mode == 'nt':
        b_spec = pl.BlockSpec((tn, tk), lambda i, j, k: (j, k))
    else:
        b_spec = pl.BlockSpec((tk, tn), lambda i, j, k: (k, j))
    o_spec = pl.BlockSpec((tm, tn), lambda i, j, k: (i, j))
    in_out = out_dtype == _F32

    def body(*refs):
        a_ref, b_ref = refs[:2]
        o_ref = refs[-1] if in_out else refs[-2]
        acc = o_ref if in_out else refs[-1]
        k = pl.program_id(2)

        @pl.when(k == 0)
        def _():
            part = _dg(a_ref[...], b_ref[...], ca, cb)
            acc[...] = part + refs[2][...] if has_res else part

        @pl.when(k > 0)
        def _():
            acc[...] += _dg(a_ref[...], b_ref[...], ca, cb)

        if not in_out:
            @pl.when(k == nk - 1)
            def _():
                o_ref[...] = acc[...].astype(o_ref.dtype)

    return finish(_call(body, name, (M // tm, N // tn, nk), [a_spec, b_spec] + ([o_spec] if has_res else []),
                        [o_spec], [out_shape], [] if in_out else [pltpu.VMEM((tm, tn), _F32)],
                        ("parallel", "parallel", "arbitrary"), args, rider))


_K_RESIDENT = 2048
_LOOP_TILE_BYTES = 40 * 1024 * 1024
_LOOP_TILE_AREA = 1024 * 1024


def _loop_tiles(M, N, K, has_res, out_dtype):
    best, best_score = None, None
    obytes = 4 if out_dtype == _F32 else 2
    for tk in (2816, 2560, 2048, 1408, 1024, 512, 256, 128):
        if K % tk:
            continue
        for tm in (2048, 1408, 1024, 512, 256, 128):
            for tn in (2048, 1024, 512, 256, 128):
                if M % tm or N % tn:
                    continue
                need = 2 * 2 * tk * (tm + tn) + tm * tn * (2 * obytes + (8 if has_res else 0) + (0 if obytes == 4 else 4))
                score = (min(tm * tn, _LOOP_TILE_AREA), tk, tm * tn)
                if need <= _LOOP_TILE_BYTES and (best is None or score > best_score):
                    best, best_score = (tm, tn, tk), score
    return best


def _cols_to_shards(w):
    r, c = w.shape[0] // 2, w.shape[1] // 4
    return jnp.transpose(w.reshape(2, r, 4, c), (2, 0, 1, 3))


def _shards_to_cols(g):
    _, _, r, c = g.shape
    return jnp.transpose(g, (1, 2, 0, 3)).reshape(2 * r, 4 * c)


_FFN_SUB = 256


def _ffn_up_fwd(h, g13, name, rider=None):
    S = h.shape[0]
    _, _, R, C = g13.shape
    tm = _pick(S, (512, 256, 128))
    cw, nc = _ffn_cols(C)
    subs = [(s, min(s + _FFN_SUB, cw)) for s in range(0, cw, _FFN_SUB)]

    def body(h_ref, wg_ref, wu_ref, a_ref, g_ref, u_ref):
        h0, h1 = h_ref[:, :R], h_ref[:, R:]
        for lo, hi in subs:
            sl = slice(lo, hi)
            gv = _dg(h0, wg_ref[0, 0, :, sl], 1, 0) + _dg(h1, wg_ref[0, 1, :, sl], 1, 0)
            uv = _dg(h0, wu_ref[0, 0, :, sl], 1, 0) + _dg(h1, wu_ref[0, 1, :, sl], 1, 0)
            sg = _sigmoid(gv)
            silu = gv * sg
            a_ref[:, sl] = (silu * uv).astype(a_ref.dtype)
            g_ref[:, sl] = (uv * (sg * (1.0 + gv * (1.0 - sg)))).astype(g_ref.dtype)
            u_ref[:, sl] = silu.astype(u_ref.dtype)

    out = jax.ShapeDtypeStruct((S, 2 * C), _BF)
    ospec = pl.BlockSpec((tm, cw), lambda j, i: (i, j))
    return _call(
        body, name, (2 * nc, S // tm),
        [pl.BlockSpec((tm, 2 * R), lambda j, i: (i, 0)),
         pl.BlockSpec((1, 2, R, cw), lambda j, i: (lax.div(j, nc), 0, 0, lax.rem(j, nc))),
         pl.BlockSpec((1, 2, R, cw), lambda j, i: (2 + lax.div(j, nc), 0, 0, lax.rem(j, nc)))],
        (ospec, ospec, ospec), (out, out, out), [], ("parallel", "parallel"), (h, g13, g13), rider)


def _ffn_cols(C):
    if C % (2 * _LANES) == 0:
        return C // 2, 2
    return C, 1


def _ffn_up_bwd(dxb, w2, gate, up, name, rider=None):
    S, D = dxb.shape
    F = w2.shape[0]
    tf = 2 * _FFN_SUB
    tm = _pick(S, (1024, 512, 256, 128))

    def body(d_ref, w_ref, g_ref, u_ref, dg_ref, du_ref):
        dv = d_ref[...]
        for s in range(tf // _FFN_SUB):
            sl = slice(s * _FFN_SUB, (s + 1) * _FFN_SUB)
            da = _dg(dv, w_ref[sl, :], 1, 1)
            dg_ref[:, sl] = (da * g_ref[:, sl].astype(_F32)).astype(dg_ref.dtype)
            du_ref[:, sl] = (da * u_ref[:, sl].astype(_F32)).astype(du_ref.dtype)

    out = jax.ShapeDtypeStruct((S, F), _BF)
    tile = pl.BlockSpec((tm, tf), lambda i, j: (i, j))
    return _call(
        body, name, (S // tm, F // tf),
        [pl.BlockSpec((tm, D), lambda i, j: (i, 0)), pl.BlockSpec((tf, D), lambda i, j: (j, 0)), tile, tile],
        (tile, tile), (out, out), [], ("parallel", "parallel"), (dxb, w2, gate, up), rider)


def _ffn_up_dh(dgate, dup, g13, name, rider=None):
    S = dgate.shape[0]
    _, _, R, C = g13.shape
    tm = _pick(S, (512, 256, 128))

    def body(dg_ref, du_ref, wg_ref, wu_ref, o_ref):
        j = pl.program_id(1)

        def part(hh):
            return _dg(dg_ref[...], wg_ref[0, hh], 1, 1) + _dg(du_ref[...], wu_ref[0, hh], 1, 1)

        @pl.when(j == 0)
        def _():
            for hh in range(2):
                o_ref[:, hh * R:(hh + 1) * R] = part(hh)

        @pl.when(j > 0)
        def _():
            for hh in range(2):
                o_ref[:, hh * R:(hh + 1) * R] += part(hh)

    cw, nc = _ffn_cols(C)
    tile = pl.BlockSpec((tm, cw), lambda i, j: (i, j))
    r = _call(
        body, name, (S // tm, 2 * nc),
        [tile, tile, pl.BlockSpec((1, 2, R, cw), lambda i, j: (lax.div(j, nc), 0, 0, lax.rem(j, nc))),
         pl.BlockSpec((1, 2, R, cw), lambda i, j: (2 + lax.div(j, nc), 0, 0, lax.rem(j, nc)))],
        [pl.BlockSpec((tm, 2 * R), lambda i, j: (i, 0))], [jax.ShapeDtypeStruct((S, 2 * R), _F32)], [],
        ("parallel", "arbitrary"), (dgate, dup, g13, g13), rider)
    return r[0] if rider is None else (r[0][0], r[1])


def _ffn_up_dw(hact, dgate, dup, name, rider=None):
    S, R2 = hact.shape
    R = R2 // 2
    C = dgate.shape[1] // 2
    tk = _pick(S, (1024, 512, 256, 128))
    nk = S // tk
    cw, nc = _ffn_cols(C)

    def body(h_ref, dg_ref, du_ref, o_ref):
        chip = pl.program_id(0)
        k = pl.program_id(3)

        def accumulate(d_ref):
            @pl.when(k == 0)
            def _():
                o_ref[0, 0] = _dg(h_ref[...], d_ref[...], 0, 0)

            @pl.when(k > 0)
            def _():
                o_ref[0, 0] += _dg(h_ref[...], d_ref[...], 0, 0)

        @pl.when(chip < 2)
        def _():
            accumulate(dg_ref)

        @pl.when(chip >= 2)
        def _():
            accumulate(du_ref)

    r = _call(
        body, name, (4, nc, 2, nk),
        [pl.BlockSpec((tk, R), lambda c, b, h, k: (k, h)),
         pl.BlockSpec((tk, cw), lambda c, b, h, k: (jnp.where(c < 2, k, 0), jnp.where(c < 2, c * nc + b, 0))),
         pl.BlockSpec((tk, cw), lambda c, b, h, k: (jnp.where(c >= 2, k, 0), jnp.where(c >= 2, (c - 2) * nc + b, 0)))],
        [pl.BlockSpec((1, 1, R, cw), lambda c, b, h, k: (c, h, 0, b))], [jax.ShapeDtypeStruct((4, 2, R, C), _F32)], [],
        ("parallel", "parallel", "parallel", "arbitrary"), (hact, dgate, dup), rider)
    return r[0] if rider is None else (r[0][0], r[1])


def _rmsnorm_fwd(x, g, name):
    S, D = x.shape
    ts = _pick(S, (256,))

    def body(x_ref, g_ref, o_ref, r_ref):
        xv = x_ref[...]
        r = lax.rsqrt(jnp.mean(xv * xv, axis=-1, keepdims=True) + RMS_EPS)
        o_ref[...] = (xv * r * g_ref[...]).astype(o_ref.dtype)
        r_ref[...] = r

    return pl.pallas_call(
        body, name=name,
        out_shape=(jax.ShapeDtypeStruct((S, D), _BF), jax.ShapeDtypeStruct((S, 1), _F32)),
        grid=(S // ts,),
        in_specs=[pl.BlockSpec((ts, D), lambda i: (i, 0)), pl.BlockSpec((1, D), lambda i: (0, 0))],
        out_specs=(pl.BlockSpec((ts, D), lambda i: (i, 0)), pl.BlockSpec((ts, 1), lambda i: (i, 0))),
        compiler_params=_params(("parallel",)))(x, g)


def _rmsnorm_bwd(x, r, g, dy, dres, name):
    S, D = x.shape
    ts = _pick(S, (256,))
    has_res = dres is not None

    def body(*refs):
        if has_res:
            x_ref, r_ref, g_ref, dy_ref, dr_ref, dx_ref, dxb_ref, dg_ref = refs
        else:
            x_ref, r_ref, g_ref, dy_ref, dx_ref, dxb_ref, dg_ref = refs
        rv = r_ref[...]
        n = x_ref[...] * rv
        dyv = dy_ref[...]
        dn = dyv * g_ref[...]
        c = jnp.mean(dn * n, axis=-1, keepdims=True)
        dx = rv * (dn - n * c)
        if has_res:
            dx = dx + dr_ref[...]
        dx_ref[...] = dx
        dxb_ref[...] = dx.astype(dxb_ref.dtype)

        @pl.when(pl.program_id(0) == 0)
        def _():
            dg_ref[...] = jnp.zeros_like(dg_ref)

        dg_ref[...] += jnp.sum(dyv * n, axis=0, keepdims=True)

    row = pl.BlockSpec((ts, D), lambda i: (i, 0))
    vec = pl.BlockSpec((1, D), lambda i: (0, 0))
    in_specs = [row, pl.BlockSpec((ts, 1), lambda i: (i, 0)), vec, row] + ([row] if has_res else [])
    args = (x, r, g, dy) + ((dres,) if has_res else ())
    return pl.pallas_call(
        body, name=name,
        out_shape=(jax.ShapeDtypeStruct((S, D), _F32), jax.ShapeDtypeStruct((S, D), _BF),
                   jax.ShapeDtypeStruct((1, D), _F32)),
        grid=(S // ts,), in_specs=in_specs, out_specs=(row, row, vec),
        compiler_params=_params(("arbitrary",)))(*args)


def _headnorm_fwd(t, cb, W, hd, g, name):
    S = t.shape[0]
    ts = _pick(S, (512, 256))
    bd = _block_diag(W, hd)

    def body(x_ref, g_ref, bd_ref, o_ref):
        xv = x_ref[...].astype(_F32)
        ms = _dg_hl(xv * xv, bd_ref[...], 1, 0) * (1.0 / hd)
        o_ref[...] = (xv * lax.rsqrt(ms + RMS_EPS) * g_ref[...]).astype(o_ref.dtype)

    return pl.pallas_call(
        body, name=name, out_shape=jax.ShapeDtypeStruct((S, W), _BF), grid=(S // ts,),
        in_specs=[pl.BlockSpec((ts, W), lambda i: (i, cb)), pl.BlockSpec((1, W), lambda i: (0, 0)),
                  pl.BlockSpec((W, W), lambda i: (0, 0))],
        out_specs=pl.BlockSpec((ts, W), lambda i: (i, 0)),
        compiler_params=_params(("parallel",)))(t, g, bd)


def _headnorm_bwd(t, cb, W, hd, g, dy, dcb, name, out_dtype=_F32):
    S = t.shape[0]
    ts = _pick(S, (512, 256))
    bd = _block_diag(W, hd)

    def body(x_ref, g_ref, bd_ref, dy_ref, dx_ref, dg_ref):
        xv = x_ref[...].astype(_F32)
        bdv = bd_ref[...]
        ms = _dg_hl(xv * xv, bdv, 1, 0) * (1.0 / hd)
        rv = lax.rsqrt(ms + RMS_EPS)
        n = xv * rv
        dyv = dy_ref[...].astype(_F32)
        dn = dyv * g_ref[...]
        c = _dg_hl(dn * n, bdv, 1, 0) * (1.0 / hd)
        dx_ref[...] = (rv * (dn - n * c)).astype(dx_ref.dtype)

        @pl.when(pl.program_id(0) == 0)
        def _():
            dg_ref[...] = jnp.zeros_like(dg_ref)

        dg_ref[...] += jnp.sum(dyv * n, axis=0, keepdims=True)

    return pl.pallas_call(
        body, name=name,
        out_shape=(jax.ShapeDtypeStruct((S, W), out_dtype), jax.ShapeDtypeStruct((1, W), _F32)),
        grid=(S // ts,),
        in_specs=[pl.BlockSpec((ts, W), lambda i: (i, cb)), pl.BlockSpec((1, W), lambda i: (0, 0)),
                  pl.BlockSpec((W, W), lambda i: (0, 0)), pl.BlockSpec((ts, W), lambda i: (i, dcb))],
        out_specs=(pl.BlockSpec((ts, W), lambda i: (i, 0)), pl.BlockSpec((1, W), lambda i: (0, 0))),
        compiler_params=_params(("arbitrary",)))(t, g, bd, dy)


def _gla_post_fwd(o_f, o_b, proj, r_cb, g, name):
    S, W = o_f.shape
    hd = W // GLA_HEADS
    ts = _pick(S, (256,))
    bd = _block_diag(W, hd)

    def body(of_ref, ob_ref, r_ref, g_ref, bd_ref, y_ref):
        o = of_ref[...] + ob_ref[...]
        ms = _dg_hl(o * o, bd_ref[...], 1, 0) * (1.0 / hd)
        u = o * lax.rsqrt(ms + RMS_EPS) * g_ref[...]
        rr = r_ref[...].astype(_F32)
        y_ref[...] = (u * (rr * _sigmoid(rr))).astype(y_ref.dtype)

    row = pl.BlockSpec((ts, W), lambda i: (i, 0))
    return pl.pallas_call(
        body, name=name, out_shape=jax.ShapeDtypeStruct((S, W), _BF), grid=(S // ts,),
        in_specs=[row, row, pl.BlockSpec((ts, W), lambda i: (i, r_cb)), pl.BlockSpec((1, W), lambda i: (0, 0)),
                  pl.BlockSpec((W, W), lambda i: (0, 0))],
        out_specs=row, compiler_params=_params(("parallel",)))(o_f, o_b, proj, g, bd)


def _gla_post_bwd(o_f, o_b, proj, r_cb, g, dy, name):
    S, W = o_f.shape
    hd = W // GLA_HEADS
    ts = _pick(S, (256,))
    bd = _block_diag(W, hd)

    def body(of_ref, ob_ref, r_ref, g_ref, bd_ref, dy_ref, do_ref, dr_ref, dg_ref):
        o = of_ref[...] + ob_ref[...]
        bdv = bd_ref[...]
        ms = _dg_hl(o * o, bdv, 1, 0) * (1.0 / hd)
        rv = lax.rsqrt(ms + RMS_EPS)
        n = o * rv
        gv = g_ref[...]
        rr = r_ref[...].astype(_F32)
        sg = _sigmoid(rr)
        dyv = dy_ref[...].astype(_F32)
        dr_ref[...] = (dyv * (n * gv) * (sg * (1.0 + rr * (1.0 - sg)))).astype(dr_ref.dtype)
        du = dyv * (rr * sg)
        dn = du * gv
        c = _dg_hl(dn * n, bdv, 1, 0) * (1.0 / hd)
        do_ref[...] = rv * (dn - n * c)

        @pl.when(pl.program_id(0) == 0)
        def _():
            dg_ref[...] = jnp.zeros_like(dg_ref)

        dg_ref[...] += jnp.sum(du * n, axis=0, keepdims=True)

    row = pl.BlockSpec((ts, W), lambda i: (i, 0))
    vec = pl.BlockSpec((1, W), lambda i: (0, 0))
    return pl.pallas_call(
        body, name=name,
        out_shape=(jax.ShapeDtypeStruct((S, W), _F32), jax.ShapeDtypeStruct((S, W), _BF),
                   jax.ShapeDtypeStruct((1, W), _F32)),
        grid=(S // ts,),
        in_specs=[row, row, pl.BlockSpec((ts, W), lambda i: (i, r_cb)), vec, pl.BlockSpec((W, W), lambda i: (0, 0)),
                  row],
        out_specs=(row, row, vec), compiler_params=_params(("arbitrary",)))(o_f, o_b, proj, g, bd, dy)


def _loss_head(y, tgt, name):
    S, D = y.shape
    ts = _pick(S, (256,))

    def body(y_ref, t_ref, l_ref, d_ref, db_ref):
        err = y_ref[...] - t_ref[...]
        d = err * (1.0 / D)
        d_ref[...] = d
        db_ref[...] = d.astype(db_ref.dtype)

        @pl.when(pl.program_id(0) == 0)
        def _():
            l_ref[...] = jnp.zeros_like(l_ref)

        l_ref[...] += 0.5 * jnp.sum(jnp.mean(err * err, axis=-1, keepdims=True))

    row = pl.BlockSpec((ts, D), lambda i: (i, 0))
    return pl.pallas_call(
        body, name=name,
        out_shape=(jax.ShapeDtypeStruct((8, _LANES), _F32), jax.ShapeDtypeStruct((S, D), _F32),
                   jax.ShapeDtypeStruct((S, D), _BF)),
        grid=(S // ts,), in_specs=[row, row],
        out_specs=(pl.BlockSpec((8, _LANES), lambda i: (0, 0)), row, row),
        compiler_params=_params(("arbitrary",)))(y, tgt)


def _adamw(w, g, m, v, name):
    shape = w.shape
    if w.ndim == 3 and shape[1] * shape[2] > 256 * 1024:
        rb = _row_block(shape[1], shape[2] * 4, 1536 * 1024)
        grid = (shape[0], shape[1] // rb)
        spec = pl.BlockSpec((1, rb, shape[2]), lambda l, i: (l, i, 0))
        sem = ("parallel", "parallel")
    else:
        grid = ()
        spec = pl.BlockSpec(memory_space=pltpu.VMEM)
        sem = None

    def body(w_ref, g_ref, m_ref, v_ref, d_ref, nm_ref, nv_ref):
        gv = g_ref[...]
        mn = ADAM_B1 * m_ref[...] + (1.0 - ADAM_B1) * gv
        vn = ADAM_B2 * v_ref[...] + (1.0 - ADAM_B2) * (gv * gv)
        m_hat = mn / (1.0 - ADAM_B1 ** ADAM_STEP)
        v_hat = vn / (1.0 - ADAM_B2 ** ADAM_STEP)
        d_ref[...] = -ADAM_LR * (m_hat / (jnp.sqrt(v_hat) + ADAM_EPS) + ADAM_WD * w_ref[...])
        nm_ref[...] = mn
        nv_ref[...] = vn

    out = jax.ShapeDtypeStruct(shape, _F32)
    return pl.pallas_call(
        body, name=name, out_shape=(out, out, out), grid=grid, in_specs=[spec] * 4, out_specs=(spec,) * 3,
        compiler_params=_params(sem))(w, g, m, v)


def _add_pair(t, la, cidx, name):
    _, _, r, c = t.shape
    rb = _pick(r, (256, 352, 176, 88, 8))

    def body(c_ref, t_ref, l_ref, o_ref):
        o_ref[...] = (t_ref[...] + l_ref[...]).astype(o_ref.dtype)

    return pl.pallas_call(
        body, name=name, out_shape=jax.ShapeDtypeStruct((4, r, c), _BF),
        grid_spec=pltpu.PrefetchScalarGridSpec(
            num_scalar_prefetch=1, grid=(4, r // rb),
            in_specs=[pl.BlockSpec((1, None, rb, c), lambda j, i, cr: (j, cr[0], i, 0)),
                      pl.BlockSpec((1, rb, c), lambda j, i, cr: (j, i, 0))],
            out_specs=pl.BlockSpec((1, rb, c), lambda j, i, cr: (j, i, 0))),
        compiler_params=_params(("parallel", "parallel")))(cidx, t, la)


def _add_four(p, land, chip, core, name):
    _, r, c = p.shape
    rb = _pick(r, (256, 352, 176, 88, 16))

    def body(chip_ref, core_ref, p_ref, l1_ref, l2_ref, l3_ref, o_ref):
        o_ref[...] = ((p_ref[...].astype(_F32) + l1_ref[...].astype(_F32)) + l2_ref[...].astype(_F32)
                      ) + l3_ref[...].astype(_F32)

    def slot(flip):
        return pl.BlockSpec((None, rb, c), lambda i, ch, co: (jnp.bitwise_xor(ch[0], flip), i, 0))

    return pl.pallas_call(
        body, name=name, out_shape=jax.ShapeDtypeStruct((2, r, c), _F32),
        grid_spec=pltpu.PrefetchScalarGridSpec(
            num_scalar_prefetch=2, grid=(r // rb,),
            in_specs=[slot(0), slot(1), slot(2), slot(3)],
            out_specs=pl.BlockSpec((None, rb, c), lambda i, ch, co: (co[0], i, 0))),
        compiler_params=_params(("parallel",)))(chip, core, p, land, land, land)


def _gla_masks(tb, rev):
    i = np.arange(tb)
    same = (i[:, None] // GLA_CHUNK) == (i[None, :] // GLA_CHUNK)
    tri = same & (i[None, :] <= i[:, None])
    keep = (same & ~tri) if rev else tri
    return tuple(jnp.asarray(m.astype(np.float32), dtype=_BF) for m in (tri, same, keep))


def _gla_block(q, k, gfb, wg, bg, tri, same, keepm, rev, scale):
    z = _dg(gfb, wg, 1, 0) + bg
    la = _log_sigmoid(z) * (1.0 / GLA_TAU)
    cum = _dg_lh(tri, la, 0 if rev else 1, 0)
    tot = _dg_lh(same, la, 1, 0)
    e_a = jnp.exp(cum)
    e_na = jnp.exp(-cum)
    e_la = jnp.exp(tot - cum)
    qe = q * scale * e_a
    ke = k * e_na
    kend = k * e_la
    keep = keepm > 0
    p = jnp.where(keep, _dg(qe, ke, 1, 1), 0.0)
    return dict(z=z, tot=tot, e_a=e_a, e_na=e_na, e_la=e_la, qe=qe, ke=ke, kend=kend, keep=keep, p=p)


def _gla_fwd(proj, gfb, wg, bg, rev, name):
    S = proj.shape[0]
    H = GLA_HEADS
    dk = wg.shape[1] // (2 * H)
    dv = 2 * dk
    C = GLA_CHUNK
    cb_n = _pick(S // C, (8, 4, 2, 1))
    tb = cb_n * C
    nb = S // tb
    scale = float(dk) ** -0.5
    masks = _gla_masks(tb, rev)
    wcol = H if rev else 0

    def bmap(b):
        return nb - 1 - b if rev else b

    def body(q_ref, k_ref, v_ref, g_ref, wg_ref, bg_ref, tri_ref, same_ref, keep_ref, o_ref, st_ref, state):
        h = pl.program_id(1)

        @pl.when(pl.program_id(0) == 0)
        def _():
            state[h] = jnp.zeros((dv, dk), _F32)

        t = _gla_block(q_ref[...].astype(_F32), k_ref[...].astype(_F32), g_ref[...], wg_ref[...], bg_ref[...],
                       tri_ref[...], same_ref[...],
                       keep_ref[...], rev, scale)
        vv = v_ref[...]
        o_ref[...] = _dg(t['p'], vv, 1, 0)
        order = range(cb_n - 1, -1, -1) if rev else range(cb_n)
        for ci in order:
            sl = slice(ci * C, (ci + 1) * C)
            st = state[h]
            o_ref[sl, :] += _dg(t['qe'][sl], st, 1, 1)
            st_ref[0, ci] = st
            state[h] = st * jnp.exp(t['tot'][ci * C:ci * C + 1]) + _dg(vv[sl], t['kend'][sl], 0, 0)

    nq = (H * dk) // dk
    msk = pl.BlockSpec((tb, tb), lambda b, h: (0, 0))
    return pl.pallas_call(
        body, name=name,
        out_shape=(jax.ShapeDtypeStruct((S, H * dv), _F32), jax.ShapeDtypeStruct((H, S // C, dv, dk), _F32)),
        grid=(nb, H),
        in_specs=[pl.BlockSpec((tb, dk), lambda b, h: (bmap(b), h)),
                  pl.BlockSpec((tb, dk), lambda b, h: (bmap(b), nq + h)),
                  pl.BlockSpec((tb, dv), lambda b, h: (bmap(b), (2 * H * dk) // dv + h)),
                  pl.BlockSpec((tb, _LANES), lambda b, h: (bmap(b), 0)),
                  pl.BlockSpec((_LANES, dk), lambda b, h: (0, wcol + h)),
                  pl.BlockSpec((1, dk), lambda b, h: (0, wcol + h)),
                  msk, msk, msk],
        out_specs=(pl.BlockSpec((tb, dv), lambda b, h: (bmap(b), h)),
                   pl.BlockSpec((1, cb_n, dv, dk), lambda b, h: (h, bmap(b), 0, 0))),
        scratch_shapes=[pltpu.VMEM((H, dv, dk), _F32)],
        compiler_params=_params(("arbitrary", "arbitrary")))(proj, proj, proj, gfb, wg, bg, *masks)


def _gla_bwd(proj, gfb, wg, bg, st, do, rev, prev, name):
    S = proj.shape[0]
    H = GLA_HEADS
    dk = wg.shape[1] // (2 * H)
    dv = 2 * dk
    C = GLA_CHUNK
    cb_n = _pick(S // C, (8, 4, 2, 1))
    tb = cb_n * C
    nb = S // tb
    scale = float(dk) ** -0.5
    masks = _gla_masks(tb, rev)
    wcol = H if rev else 0
    has_prev = prev is not None

    def bmap(b):
        return b if rev else nb - 1 - b

    def body(*refs):
        if has_prev:
            (q_ref, k_ref, v_ref, g_ref, wg_ref, bg_ref, tri_ref, same_ref, keep_ref, st_ref, do_ref,
             pq_ref, pk_ref, pv_ref, dq_ref, dk_ref, dv_ref, dz_ref, dstate) = refs
        else:
            (q_ref, k_ref, v_ref, g_ref, wg_ref, bg_ref, tri_ref, same_ref, keep_ref, st_ref, do_ref,
             dq_ref, dk_ref, dv_ref, dz_ref, dstate) = refs
        h = pl.program_id(1)

        @pl.when(pl.program_id(0) == 0)
        def _():
            dstate[h] = jnp.zeros((dv, dk), _F32)

        t = _gla_block(q_ref[...].astype(_F32), k_ref[...].astype(_F32), g_ref[...], wg_ref[...], bg_ref[...],
                       tri_ref[...], same_ref[...],
                       keep_ref[...], rev, scale)
        vv = v_ref[...]
        dov = do_ref[...]
        order = range(cb_n) if rev else range(cb_n - 1, -1, -1)
        for ci in order:
            sl = slice(ci * C, (ci + 1) * C)
            stp = st_ref[0, ci]
            dst = dstate[h]
            e_l = jnp.exp(t['tot'][ci * C:ci * C + 1])
            kend_c = t['kend'][sl]
            dkend_c = _dg(vv[sl], dst, 1, 0)
            dq_ref[sl, :] = _dg(dov[sl], stp, 1, 0)
            dk_ref[sl, :] = dkend_c
            dv_ref[sl, :] = _dg(kend_c, dst, 1, 1)
            dtot = (e_l * jnp.sum(dst * stp, axis=0, keepdims=True)
                    + jnp.sum(dkend_c * kend_c, axis=0, keepdims=True))
            dz_ref[sl, :] = jnp.broadcast_to(dtot, (C, dk))
            dstate[h] = dst * e_l + _dg(dov[sl], t['qe'][sl], 0, 0)
        dp = jnp.where(t['keep'], _dg(dov, vv, 1, 1), 0.0)
        dqe = _dg(dp, t['ke'], 1, 0) + dq_ref[...]
        dke = _dg(dp, t['qe'], 0, 0)
        dkend = dk_ref[...]
        dvv = _dg(t['p'], dov, 0, 0) + dv_ref[...]
        dqv = dqe * t['e_a'] * scale
        dkv = dke * t['e_na'] + dkend * t['e_la']
        d_a = dqe * t['qe'] - dke * t['ke'] - dkend * t['kend']
        dla = _dg_lh(tri_ref[...], d_a, 1 if rev else 0, 0) + dz_ref[...]
        dz_ref[...] = dla * (1.0 / GLA_TAU) * _sigmoid(-t['z'])
        if has_prev:
            dqv = dqv + pq_ref[...]
            dkv = dkv + pk_ref[...]
            dvv = dvv + pv_ref[...]
        dq_ref[...] = dqv
        dk_ref[...] = dkv
        dv_ref[...] = dvv

    nq = (H * dk) // dk
    msk = pl.BlockSpec((tb, tb), lambda b, h: (0, 0))
    kblk = pl.BlockSpec((tb, dk), lambda b, h: (bmap(b), h))
    vblk = pl.BlockSpec((tb, dv), lambda b, h: (bmap(b), h))
    in_specs = [kblk,
                pl.BlockSpec((tb, dk), lambda b, h: (bmap(b), nq + h)),
                pl.BlockSpec((tb, dv), lambda b, h: (bmap(b), (2 * H * dk) // dv + h)),
                pl.BlockSpec((tb, _LANES), lambda b, h: (bmap(b), 0)),
                pl.BlockSpec((_LANES, dk), lambda b, h: (0, wcol + h)),
                pl.BlockSpec((1, dk), lambda b, h: (0, wcol + h)),
                msk, msk, msk,
                pl.BlockSpec((1, cb_n, dv, dk), lambda b, h: (h, bmap(b), 0, 0)),
                vblk]
    args = [proj, proj, proj, gfb, wg, bg, *masks, st, do]
    if has_prev:
        in_specs += [kblk, kblk, vblk]
        args += list(prev)
    return pl.pallas_call(
        body, name=name,
        out_shape=(jax.ShapeDtypeStruct((S, H * dk), _F32), jax.ShapeDtypeStruct((S, H * dk), _F32),
                   jax.ShapeDtypeStruct((S, H * dv), _F32), jax.ShapeDtypeStruct((S, H * dk), _F32)),
        grid=(nb, H), in_specs=in_specs, out_specs=(kblk, kblk, vblk, kblk),
        scratch_shapes=[pltpu.VMEM((H, dv, dk), _F32)],
        compiler_params=_params(("arbitrary", "arbitrary")))(*args)


NA_GROUP = 4
NA_WIN_ROWS = NA_GROUP + NA_ROWS


def _na_geometry(S):
    rows = S // GRID_W
    assert rows % NA_GROUP == 0 and rows >= NA_WIN_ROWS + NA_GROUP
    return rows, rows // NA_GROUP


def _na_win_start(g, rows):
    return jnp.clip(NA_GROUP * g - NA_ROWS // 2, 0, rows - NA_WIN_ROWS)


def _na_class(g, groups):
    return jnp.where(g == 0, 0, jnp.where(g == groups - 1, 2, 1))


def _na_onehots(rows):
    groups = rows // NA_GROUP
    by_row = np.zeros((3, NA_GROUP, NA_WIN_ROWS, 2 * NA_ROWS - 1), np.float32)
    for cls, g in enumerate((0, 1, groups - 1)):
        ws = int(np.clip(NA_GROUP * g - NA_ROWS // 2, 0, rows - NA_WIN_ROWS))
        for qr in range(NA_GROUP):
            r = NA_GROUP * g + qr
            rs = int(np.clip(r - NA_ROWS // 2, 0, rows - NA_ROWS))
            for kr in range(NA_WIN_ROWS):
                if rs <= ws + kr < rs + NA_ROWS:
                    by_row[cls, qr, kr, ws + kr - r + NA_ROWS - 1] = 1.0
    c = np.arange(GRID_W)
    cs = np.clip(c - NA_COLS // 2, 0, GRID_W - NA_COLS)
    kc = np.arange(GRID_W)
    win = (kc[None, :] >= cs[:, None]) & (kc[None, :] < cs[:, None] + NA_COLS)
    idx = np.clip(kc[None, :] - c[:, None], -(NA_COLS - 1), NA_COLS - 1) + (NA_COLS - 1)
    by_col = ((idx[:, :, None] == np.arange(2 * NA_COLS - 1)[None, None, :]) & win[:, :, None]).astype(np.float32)
    return by_row, by_col


def _na_bias_tables(rpb, rows):
    by_row, by_col = _na_onehots(rows)
    H = rpb.shape[0]
    e1 = jnp.einsum('hij,ckj->hick', rpb, by_col, precision=lax.Precision.HIGHEST)
    e1 = jnp.where((by_col.sum(-1) > 0)[None, None], e1, -jnp.inf)
    none = jnp.full((H, GRID_W, GRID_W), -jnp.inf, _F32)
    pick, valid = by_row.argmax(-1), by_row.sum(-1) > 0
    tabs = []
    for z in range(3):
        bands = [jnp.concatenate([e1[:, pick[z, q, r]] if valid[z, q, r] else none for r in range(NA_WIN_ROWS)],
                                 axis=-1) for q in range(NA_GROUP)]
        tabs.append(jnp.stack(bands, axis=1).reshape(H, NA_GROUP * GRID_W, NA_WIN_ROWS * GRID_W))
    return jnp.stack(tabs)


def _na_bias_grad(dtab, rows):
    by_row, by_col = _na_onehots(rows)
    H = dtab.shape[1]
    pick, valid = by_row.argmax(-1), by_row.sum(-1) > 0
    d6 = dtab.reshape(3, H, NA_GROUP, GRID_W, NA_WIN_ROWS, GRID_W)
    slabs = [[] for _ in range(2 * NA_ROWS - 1)]
    for z in range(3):
        for q in range(NA_GROUP):
            for r in range(NA_WIN_ROWS):
                if valid[z, q, r]:
                    slabs[pick[z, q, r]].append(d6[z, :, q, :, r, :])
    zero = jnp.zeros((H, GRID_W, GRID_W), _F32)
    de1 = jnp.stack([sum(s[1:], s[0]) if s else zero for s in slabs], axis=1)
    return jnp.einsum('hick,ckj->hij', de1, by_col, precision=lax.Precision.HIGHEST)


def _na_fwd(qn, kn, vb, tab, name):
    S, W = qn.shape
    rows, groups = _na_geometry(S)
    npair = W // _LANES
    nq = NA_GROUP * GRID_W
    nk = NA_WIN_ROWS * GRID_W
    sc = float(NA_HD) ** -0.5

    def body(q_ref, k_ref, v_ref, be_ref, bo_ref, o_ref):
        i = pl.program_id(1)
        lane = lax.broadcasted_iota(jnp.int32, (nq, _LANES), 1)
        for gi, b_ref in enumerate((be_ref, bo_ref)):
            k0 = pl.multiple_of(_na_win_start(2 * i + gi, rows) * GRID_W, GRID_W)
            kw = k_ref[pl.ds(k0, nk), :]
            vw = v_ref[pl.ds(k0, nk), :]
            qv = q_ref[gi * nq:(gi + 1) * nq, :]
            outs = []
            for hh in range(2):
                mine = (lane >= hh * NA_HD) & (lane < (hh + 1) * NA_HD)
                s = _dg(jnp.where(mine, qv, jnp.zeros_like(qv)), kw, 1, 1) * sc + b_ref[0, hh]
                e = jnp.exp(s - jnp.max(s, axis=-1, keepdims=True))
                p = e / jnp.sum(e, axis=-1, keepdims=True)
                outs.append(_dg(p, vw, 1, 0))
            o_ref[gi * nq:(gi + 1) * nq, :] = jnp.where(lane < NA_HD, outs[0], outs[1])

    def tspec(gi):
        return pl.BlockSpec((1, 2, nq, nk), lambda p, i: (_na_class(2 * i + gi, groups), p, 0, 0))

    return pl.pallas_call(
        body, name=name, out_shape=jax.ShapeDtypeStruct((S, W), _F32), grid=(npair, groups // 2),
        in_specs=[pl.BlockSpec((2 * nq, _LANES), lambda p, i: (i, p)),
                  pl.BlockSpec((S, _LANES), lambda p, i: (0, p)),
                  pl.BlockSpec((S, _LANES), lambda p, i: (0, p)),
                  tspec(0), tspec(1)],
        out_specs=pl.BlockSpec((2 * nq, _LANES), lambda p, i: (i, p)),
        compiler_params=_params(("parallel", "arbitrary")))(qn, kn, vb, tab, tab)


def _na_bwd(qn, kn, vb, tab, do, name):
    S, W = qn.shape
    rows, groups = _na_geometry(S)
    npair = W // _LANES
    nq = NA_GROUP * GRID_W
    nk = NA_WIN_ROWS * GRID_W
    sc = float(NA_HD) ** -0.5

    half = groups // 2

    def body(q_ref, k_ref, v_ref, be_ref, bo_ref, do_ref, dq_ref, dk_ref, dv_ref, dbe_ref, dbo_ref):
        i = pl.program_id(1)

        @pl.when(i == 0)
        def _():
            dk_ref[...] = jnp.zeros_like(dk_ref)
            dv_ref[...] = jnp.zeros_like(dv_ref)

        @pl.when(i <= 1)
        def _():
            dbe_ref[...] = jnp.zeros_like(dbe_ref)

        @pl.when((i == 0) | (i == half - 1))
        def _():
            dbo_ref[...] = jnp.zeros_like(dbo_ref)

        lane = lax.broadcasted_iota(jnp.int32, (nq, _LANES), 1)
        for gi, (b_ref, db_ref) in enumerate(((be_ref, dbe_ref), (bo_ref, dbo_ref))):
            k0 = pl.multiple_of(_na_win_start(2 * i + gi, rows) * GRID_W, GRID_W)
            kw = k_ref[pl.ds(k0, nk), :]
            vw = v_ref[pl.ds(k0, nk), :]
            qv = q_ref[gi * nq:(gi + 1) * nq, :]
            dov = do_ref[gi * nq:(gi + 1) * nq, :]
            dqs = []
            dkw = jnp.zeros((nk, _LANES), _F32)
            dvw = jnp.zeros((nk, _LANES), _F32)
            for hh in range(2):
                mine = (lane >= hh * NA_HD) & (lane < (hh + 1) * NA_HD)
                qm = jnp.where(mine, qv, jnp.zeros_like(qv))
                dom = jnp.where(mine, dov, 0.0)
                s = _dg(qm, kw, 1, 1) * sc + b_ref[0, hh]
                e = jnp.exp(s - jnp.max(s, axis=-1, keepdims=True))
                p = e / jnp.sum(e, axis=-1, keepdims=True)
                dp = _dg(dom, vw, 1, 1)
                ds = p * (dp - jnp.sum(p * dp, axis=-1, keepdims=True))
                db_ref[0, hh] += ds
                dqs.append(_dg(ds, kw, 1, 0) * sc)
                dkw = dkw + _dg(ds, qm, 0, 0) * sc
                dvw = dvw + _dg(p, dom, 0, 0)
            dq_ref[gi * nq:(gi + 1) * nq, :] = jnp.where(lane < NA_HD, dqs[0], dqs[1])
            dk_ref[pl.ds(k0, nk), :] += dkw
            dv_ref[pl.ds(k0, nk), :] += dvw

    def tspec(gi):
        return pl.BlockSpec((1, 2, nq, nk), lambda p, i: (_na_class(2 * i + gi, groups), p, 0, 0))

    blk = pl.BlockSpec((2 * nq, _LANES), lambda p, i: (i, p))
    full = pl.BlockSpec((S, _LANES), lambda p, i: (0, p))
    tshape = jax.ShapeDtypeStruct(tab.shape, _F32)
    dq, dk, dv, dbe, dbo = pl.pallas_call(
        body, name=name,
        out_shape=(jax.ShapeDtypeStruct((S, W), _F32), jax.ShapeDtypeStruct((S, W), _F32),
                   jax.ShapeDtypeStruct((S, W), _F32), tshape, tshape),
        grid=(npair, half), in_specs=[blk, full, full, tspec(0), tspec(1), blk],
        out_specs=(blk, full, full, tspec(0), tspec(1)),
        compiler_params=_params(("arbitrary", "arbitrary")))(qn, kn, vb, tab, tab, do)
    return dq, dk, dv, jnp.stack([dbe[0], dbe[1] + dbo[1], dbo[2]])


def _mem_fwd(qn, km, vm, name):
    S, W = qn.shape
    hd = W // MEM_HEADS
    tq = _pick(S, (512, 256))
    sc = float(hd) ** -0.5

    def body(q_ref, k_ref, v_ref, o_ref):
        for h in range(MEM_HEADS):
            cs = slice(h * hd, (h + 1) * hd)
            s = _dg(q_ref[:, cs], k_ref[:, cs], 1, 1) * sc
            e = jnp.exp(s - jnp.max(s, axis=-1, keepdims=True))
            p = e / jnp.sum(e, axis=-1, keepdims=True)
            o_ref[:, cs] = _dg(p, v_ref[:, cs], 1, 0)

    full = pl.BlockSpec(km.shape, lambda i: (0, 0))
    return pl.pallas_call(
        body, name=name, out_shape=jax.ShapeDtypeStruct((S, W), _F32), grid=(S // tq,),
        in_specs=[pl.BlockSpec((tq, W), lambda i: (i, 0)), full, full],
        out_specs=pl.BlockSpec((tq, W), lambda i: (i, 0)),
        compiler_params=_params(("parallel",)))(qn, km, vm)


def _mem_bwd(qn, km, vm, do, name):
    S, W = qn.shape
    hd = W // MEM_HEADS
    tq = _pick(S, (512, 256))
    sc = float(hd) ** -0.5

    def body(q_ref, k_ref, v_ref, do_ref, dq_ref, dk_ref, dv_ref):
        @pl.when(pl.program_id(0) == 0)
        def _():
            dk_ref[...] = jnp.zeros_like(dk_ref)
            dv_ref[...] = jnp.zeros_like(dv_ref)

        for h in range(MEM_HEADS):
            cs = slice(h * hd, (h + 1) * hd)
            qh = q_ref[:, cs]
            kh = k_ref[:, cs]
            doh = do_ref[:, cs]
            s = _dg(qh, kh, 1, 1) * sc
            e = jnp.exp(s - jnp.max(s, axis=-1, keepdims=True))
            p = e / jnp.sum(e, axis=-1, keepdims=True)
            dp = _dg(doh, v_ref[:, cs], 1, 1)
            ds = p * (dp - jnp.sum(p * dp, axis=-1, keepdims=True))
            dq_ref[:, cs] = _dg(ds, kh, 1, 0) * sc
            dk_ref[:, cs] += _dg(ds, qh, 0, 0) * sc
            dv_ref[:, cs] += _dg(p, doh, 0, 0)

    full = pl.BlockSpec(km.shape, lambda i: (0, 0))
    row = pl.BlockSpec((tq, W), lambda i: (i, 0))
    return pl.pallas_call(
        body, name=name,
        out_shape=(jax.ShapeDtypeStruct((S, W), _F32), jax.ShapeDtypeStruct(km.shape, _F32),
                   jax.ShapeDtypeStruct(km.shape, _F32)),
        grid=(S // tq,), in_specs=[row, full, full, row], out_specs=(row, full, full),
        compiler_params=_params(("arbitrary",)))(qn, km, vm, do)


def _place():
    x, y, c = lax.axis_index("x"), lax.axis_index("y"), lax.axis_index("c")
    chips = [(1 - x, y), (x, 1 - y), (1 - x, 1 - y)]
    return x, y, c, chips


def _exchange(rider, name):
    ri = len(rider.ins)
    ro = len(rider.outs)

    def body(*refs):
        ins, outs, sems = refs[:ri], refs[ri:ri + ro], refs[ri + ro:]
        rider.start(ins, outs, sems)
        rider.finish(ins, outs, sems)

    return pl.pallas_call(
        body, name=name, out_shape=tuple(rider.outs), in_specs=[_ANY] * ri, out_specs=(_ANY,) * ro,
        scratch_shapes=rider.sems, input_output_aliases={i: o for i, o in rider.aliases},
        compiler_params=_params())(*rider.ins)


def _pair_sems(*shape):
    return [pltpu.SemaphoreType.DMA(shape), pltpu.SemaphoreType.DMA(shape)]


def _gather_ici(shards):
    n = len(shards)

    def copies(w, g, sems):
        send, recv = sems
        x, y, c, chips = _place()
        me = 2 * x + y
        out, back = [], []
        for t in range(n):
            for j, (cx, cy) in enumerate(chips):
                out.append(pltpu.make_async_remote_copy(
                    src_ref=w[t].at[c], dst_ref=g[t].at[me, c], send_sem=send.at[t, j], recv_sem=recv.at[t, j],
                    device_id=(cx, cy, c), device_id_type=_MESH))
                back.append(functools.partial(
                    pltpu.make_async_remote_copy,
                    src_ref=w[t].at[c], dst_ref=g[t].at[2 * cx + cy, c], send_sem=send.at[t, j],
                    recv_sem=recv.at[t, j], device_id=(cx, cy, c), device_id_type=_MESH))
        return out, back

    def start(w, g, sems):
        for cp in copies(w, g, sems)[0]:
            cp.start()

    def finish(w, g, sems):
        out, back = copies(w, g, sems)
        for make in back:
            make().wait_recv()
        for cp in out:
            cp.wait_send()

    return _Rider(shards, [jax.ShapeDtypeStruct((4,) + s.shape, s.dtype) for s in shards], _pair_sems(n, 3),
                  start, finish)


def _gather_d2d(gs):
    n = len(gs)

    def copies(g, sems):
        send, recv = sems
        x, y, c, chips = _place()
        out, back = [], []
        for t in range(n):
            for j, (cx, cy) in enumerate(chips):
                mine, theirs = g[t].at[2 * cx + cy, c], g[t].at[2 * cx + cy, 1 - c]
                out.append(pltpu.make_async_remote_copy(
                    src_ref=mine, dst_ref=mine, send_sem=send.at[t, j], recv_sem=recv.at[t, j],
                    device_id=(x, y, 1 - c), device_id_type=_MESH))
                back.append(functools.partial(
                    pltpu.make_async_remote_copy,
                    src_ref=mine, dst_ref=theirs, send_sem=send.at[t, j], recv_sem=recv.at[t, j],
                    device_id=(x, y, 1 - c), device_id_type=_MESH))
        return out, back

    def start(_, g, sems):
        for cp in copies(g, sems)[0]:
            cp.start()

    def finish(_, g, sems):
        out, back = copies(g, sems)
        for make in back:
            make().wait_recv()
        for cp in out:
            cp.wait_send()

    return _Rider(gs, [jax.ShapeDtypeStruct(g.shape, g.dtype) for g in gs], _pair_sems(n, 3), start, finish,
                  aliases=[(t, t) for t in range(n)])


def _swap_halves(ts):
    n = len(ts)

    def copies(t_in, land, sems):
        send, recv = sems
        x, y, c, _ = _place()
        return [pltpu.make_async_remote_copy(
            src_ref=t_in[t].at[:, 1 - c], dst_ref=land[t], send_sem=send.at[t], recv_sem=recv.at[t],
            device_id=(x, y, 1 - c), device_id_type=_MESH) for t in range(n)]

    def start(t_in, land, sems):
        for cp in copies(t_in, land, sems):
            cp.start()

    def finish(t_in, land, sems):
        for cp in copies(t_in, land, sems):
            cp.wait()

    return _Rider(ts, [jax.ShapeDtypeStruct((4,) + t.shape[2:], t.dtype) for t in ts], _pair_sems(n), start, finish)


def _scatter_chips(ps):
    n = len(ps)

    def copies(p, land, sems):
        send, recv = sems
        x, y, c, chips = _place()
        me = 2 * x + y
        out, back = [], []
        for t in range(n):
            for j, (cx, cy) in enumerate(chips):
                out.append(pltpu.make_async_remote_copy(
                    src_ref=p[t].at[2 * cx + cy], dst_ref=land[t].at[me], send_sem=send.at[t, j],
                    recv_sem=recv.at[t, j], device_id=(cx, cy, c), device_id_type=_MESH))
                back.append(functools.partial(
                    pltpu.make_async_remote_copy,
                    src_ref=p[t].at[me], dst_ref=land[t].at[2 * cx + cy], send_sem=send.at[t, j],
                    recv_sem=recv.at[t, j], device_id=(cx, cy, c), device_id_type=_MESH))
        return out, back

    def start(p, land, sems):
        for cp in copies(p, land, sems)[0]:
            cp.start()

    def finish(p, land, sems):
        out, back = copies(p, land, sems)
        for make in back:
            make().wait_recv()
        for cp in out:
            cp.wait_send()

    return _Rider(ps, [jax.ShapeDtypeStruct(t.shape, t.dtype) for t in ps], _pair_sems(n, 3), start, finish)


def _swap_reduced(rs):
    n = len(rs)

    def copies(out, sems):
        send, recv = sems
        x, y, c, _ = _place()
        return [pltpu.make_async_remote_copy(
            src_ref=out[t].at[c], dst_ref=out[t].at[c], send_sem=send.at[t], recv_sem=recv.at[t],
            device_id=(x, y, 1 - c), device_id_type=_MESH) for t in range(n)]

    def start(_, out, sems):
        for cp in copies(out, sems):
            cp.start()

    def finish(_, out, sems):
        for cp in copies(out, sems):
            cp.wait()

    return _Rider(rs, [jax.ShapeDtypeStruct(t.shape, t.dtype) for t in rs], _pair_sems(n), start, finish,
                  aliases=[(t, t) for t in range(n)])


def _all_devices(v, reduce, name):
    rows = v.shape[0]

    def body(v_ref, o_ref, *rest):
        if reduce:
            all_ref, send, recv = rest
        else:
            send, recv = rest
            all_ref = o_ref
        x, y, c, _ = _place()
        me = 4 * x + 2 * y + c
        all_ref[me] = v_ref[...]
        cps = []
        for k in range(1, 8):
            fx, fy, fc = (k >> 2) & 1, (k >> 1) & 1, k & 1
            to = (x ^ fx, y ^ fy, c ^ fc)
            cps.append(pltpu.make_async_remote_copy(
                src_ref=v_ref, dst_ref=all_ref.at[me], send_sem=send.at[k - 1], recv_sem=recv.at[k - 1],
                device_id=to, device_id_type=_MESH))
        for cp in cps:
            cp.start()
        for k in range(1, 8):
            fx, fy, fc = (k >> 2) & 1, (k >> 1) & 1, k & 1
            frm = 4 * (x ^ fx) + 2 * (y ^ fy) + (c ^ fc)
            pltpu.make_async_remote_copy(
                src_ref=v_ref, dst_ref=all_ref.at[frm], send_sem=send.at[k - 1], recv_sem=recv.at[k - 1],
                device_id=(x, y, c), device_id_type=_MESH).wait_recv()
        for cp in cps:
            cp.wait_send()
        if reduce:
            acc = all_ref[0]
            for d in range(1, 8):
                acc = acc + all_ref[d]
            o_ref[...] = acc

    vm = pl.BlockSpec(memory_space=pltpu.VMEM)
    if reduce:
        out_shape = jax.ShapeDtypeStruct((rows, _LANES), _F32)
        scratch = [pltpu.VMEM((8, rows, _LANES), _F32)]
    else:
        out_shape = jax.ShapeDtypeStruct((8, rows, _LANES), _F32)
        scratch = []
    return pl.pallas_call(
        body, name=name, out_shape=out_shape, in_specs=[vm], out_specs=vm,
        scratch_shapes=scratch + [pltpu.SemaphoreType.DMA((7,)), pltpu.SemaphoreType.DMA((7,))],
        compiler_params=_params())(v)


def _weight_halves(shards):
    return [shards[k].reshape((2, shards[k].shape[0] // 2) + shards[k].shape[1:]) for k in BIG]


def _gather_layer_weights(halves, name):
    g1 = _exchange(_gather_ici(halves), name + "_ici")
    g2 = _exchange(_gather_d2d(g1), name + "_d2d")
    return _assemble_weights(g2, halves)


def _assemble_weights(got, halves):
    chip = 2 * lax.axis_index("x") + lax.axis_index("y")
    full = {}
    for k, g, own in zip(BIG, got, halves):
        g = lax.dynamic_update_slice(g, own[None], (chip, 0, 0, 0))
        _, _, r, c = g.shape
        if k in ('w_in', 'ffn_w13'):
            full[k] = g
        else:
            full[k] = g.reshape(8 * r, c)
    return full


def _grads_as_shards(grads):
    ts = []
    for k in BIG:
        g = grads[k]
        if k in ('w_in', 'ffn_w13'):
            ts.append(g)
        else:
            r, c = g.shape[0] // 8, g.shape[1]
            ts.append(g.reshape(4, 2, r, c))
    return ts


def _pair_sums(ts, landed, cidx):
    return [_add_pair(t, la, cidx, "reduce_add2_" + k) for k, t, la in zip(BIG, ts, landed)]


def _chip_sums(partial, slots, cidx):
    chip = (2 * lax.axis_index("x") + lax.axis_index("y")).astype(jnp.int32).reshape(1)
    return [_add_four(p, s, chip, cidx, "reduce_add4_" + k) for k, p, s in zip(BIG, partial, slots)]


def _assemble_shards(joined):
    return {k: j.reshape((2 * j.shape[1], j.shape[2])) for k, j in zip(BIG, joined)}


def _reduce_layer_grads(ts, cidx, name):
    landed = _exchange(_swap_halves(ts), name + "_swap")
    partial = _pair_sums(ts, landed, cidx)
    slots = _exchange(_scatter_chips(partial), name + "_scatter")
    reduced = _chip_sums(partial, slots, cidx)
    return _assemble_shards(_exchange(_swap_reduced(reduced), name + "_join"))


def _pack_rows(arrs):
    parts, spans = [], []
    off = 0
    for a in arrs:
        n = int(np.prod(a.shape))
        pad = (-n) % (8 * _LANES)
        parts.append(jnp.pad(a.reshape(-1), (0, pad)))
        spans.append((off, n, a.shape))
        off += n + pad
    return jnp.concatenate(parts).reshape(-1, _LANES), spans


def _unpack_rows(packed, spans):
    flat = packed.reshape(-1)
    return [flat[o:o + n].reshape(shape) for o, n, shape in spans]


def _in_split(d):
    dk, dv, w = d // 4, d // 2, d // 4
    names = [('g_q', dk), ('g_k', dk), ('g_v', dv), ('g_r', dv), ('g_f', GLA_GATE_RANK), ('g_b', GLA_GATE_RANK),
             ('n_q', w), ('n_k', w), ('n_v', w), ('m_q', w)]
    out, off = {}, 0
    for nme, wd in names:
        out[nme] = (off, wd)
        off += wd
    return out


def _w_in_from_shards(g):
    _, _, R, C = g.shape
    sp = _in_split(2 * R)
    f0, n0 = sp['g_f'][0], sp['n_q'][0]

    def cols(lo, hi, h):
        out = []
        for j in range(4):
            a, b = max(lo, j * C), min(hi, (j + 1) * C)
            if a < b:
                out.append(g[j, h, :, a - j * C:b - j * C])
        return out

    main = jnp.concatenate([jnp.concatenate(cols(0, f0, h) + cols(n0, 4 * C, h), axis=1) for h in range(2)], axis=0)
    gate = jnp.concatenate([jnp.concatenate(cols(f0, n0, h), axis=1) for h in range(2)], axis=0)
    return main, jnp.pad(gate, ((0, 0), (0, _LANES - (n0 - f0))))


def _w_in_to_shards(main, gate):
    R = main.shape[0] // 2
    sp = _in_split(2 * R)
    f0, n0 = sp['g_f'][0], sp['n_q'][0]
    C = (main.shape[1] + n0 - f0) // 4

    def cols(lo, hi, h):
        rows = slice(h * R, (h + 1) * R)
        out = []
        for src, a, b, shift in ((main, 0, f0, 0), (gate, f0, n0, -f0), (main, n0, 4 * C, f0 - n0)):
            lo2, hi2 = max(lo, a), min(hi, b)
            if lo2 < hi2:
                out.append(src[rows, lo2 + shift:hi2 + shift])
        return out

    return jnp.stack([jnp.stack([jnp.concatenate(cols(j * C, (j + 1) * C, h), axis=1) for h in range(2)])
                      for j in range(4)])


def _gate_weight(wg2_f, wg2_b):
    r, dk = wg2_f.shape
    top = jnp.concatenate([wg2_f, jnp.zeros_like(wg2_f)], axis=1)
    mid = jnp.concatenate([jnp.zeros_like(wg2_b), wg2_b], axis=1)
    return jnp.concatenate([top, mid, jnp.zeros((_LANES - 2 * r, 2 * dk), wg2_f.dtype)], axis=0)


def _layer_fwd(l, x, mem, w, full, tab, next_halves=None):
    S, D = x.shape
    dk, dv, nw = D // 4, D // 2, D // 4
    sv = {}
    w_main, w_gate = _w_in_from_shards(full['w_in'])
    sv['w_main'], sv['w_gate'] = w_main, w_gate
    xn, r1 = _rmsnorm_fwd(x, w['attn_norm'][None], f"rms1_fwd")
    if next_halves is None:
        proj = _mm(xn, w_main, 'nn', "mm_proj", out_dtype=_BF)
    else:
        proj, g_a = _mm(xn, w_main, 'nn', "mm_proj_gather", out_dtype=_BF, rider=_gather_ici(next_halves[:3]))
    gfb = _mm(xn, w_gate, 'nn', "mm_gate")
    sv.update(x=x, xn=xn, r1=r1, proj=proj, gfb=gfb)
    wg = _gate_weight(w['gla_wg2_f'], w['gla_wg2_b']).astype(_BF)
    bg = jnp.concatenate([w['gla_bg_f'], w['gla_bg_b']])[None]
    o_f, st_f = _gla_fwd(proj, gfb, wg, bg, False, "gla_fwd_f")
    o_b, st_b = _gla_fwd(proj, gfb, wg, bg, True, "gla_fwd_b")
    y_gla = _gla_post_fwd(o_f, o_b, proj, (2 * dk) // dv + 1, w['gla_out_norm'][None], "gla_post_fwd")
    sv.update(wg=wg, bg=bg, o_f=o_f, o_b=o_b, st_f=st_f, st_b=st_b)
    c0 = (2 * dk + 2 * dv) // nw
    qn = _headnorm_fwd(proj, c0, nw, NA_HD, jnp.tile(w['na_q_norm'], nw // NA_HD)[None], "na_qnorm_fwd")
    kn = _headnorm_fwd(proj, c0 + 1, nw, NA_HD, jnp.tile(w['na_k_norm'], nw // NA_HD)[None], "na_knorm_fwd")
    vb = proj[:, (c0 + 2) * nw:(c0 + 3) * nw].astype(_BF)
    o_na = _na_fwd(qn, kn, vb, tab, "na_fwd")
    y_na = _headnorm_fwd(o_na, 0, nw, NA_HD, w['na_out_norm'][None], "na_onorm_fwd")
    sv.update(qn=qn, kn=kn, vb=vb, o_na=o_na)
    mhd = nw // MEM_HEADS
    mqn = _headnorm_fwd(proj, c0 + 3, nw, mhd, jnp.tile(w['mem_q_norm'], MEM_HEADS)[None], "mem_qnorm_fwd")
    memn, rm = _rmsnorm_fwd(mem, w['mem_norm'][None], "mem_rms_fwd")
    kv = _mm(memn, full['mem_wkv'], 'nn', "mm_memkv")
    km = _headnorm_fwd(kv, 0, nw, mhd, jnp.tile(w['mem_k_norm'], MEM_HEADS)[None], "mem_knorm_fwd")
    vm = kv[:, nw:].astype(_BF)
    o_mem = _mem_fwd(mqn, km, vm, "mem_fwd")
    y_mem = _headnorm_fwd(o_mem, 0, nw, mhd, w['mem_out_norm'][None], "mem_onorm_fwd")
    sv.update(mqn=mqn, memn=memn, rm=rm, kv=kv, km=km, vm=vm, o_mem=o_mem)
    y = jnp.concatenate([y_gla, y_na, y_mem], axis=1)
    x1 = _mm(y, full['w_out'], 'nn', "mm_out", res=x)
    h, r2 = _rmsnorm_fwd(x1, w['ffn_norm'][None], "rms2_fwd")
    if next_halves is None:
        a, gate, up = _ffn_up_fwd(h, full['ffn_w13'], "ffn_up_fwd")
        x2 = _mm(a, full['ffn_w2'], 'nn', "mm_w2", res=x1)
        got = None
    else:
        (a, gate, up), g_b = _ffn_up_fwd(h, full['ffn_w13'], "ffn_up_fwd_gather", _gather_ici(next_halves[3:]))
        x2, got = _mm(a, full['ffn_w2'], 'nn', "mm_w2_gather", res=x1, rider=_gather_d2d(list(g_a) + list(g_b)))
    sv.update(y=y, x1=x1, h=h, r2=r2, gate=gate, up=up, a=a)
    return x2, sv, got


def _layer_bwd(l, dx2, dx2b, mem, w, full, tab, sv, pending=None, cidx=None):
    S, D = dx2.shape
    dk, dv, nw = D // 4, D // 2, D // 4
    gb, gs = {}, {}
    if pending is None:
        dgate, dup = _ffn_up_bwd(dx2b, full['ffn_w2'], sv['gate'], sv['up'], "ffn_up_bwd")
    else:
        (dgate, dup), landed = _ffn_up_bwd(dx2b, full['ffn_w2'], sv['gate'], sv['up'], "ffn_up_bwd_swap",
                                           _swap_halves(pending))
        partial = _pair_sums(pending, landed, cidx)
    gb['ffn_w2'] = _mm(sv['a'], dx2b, 'tn', "mm_dw2")
    if pending is None:
        dh = _ffn_up_dh(dgate, dup, full['ffn_w13'], "ffn_up_dh")
        gb['ffn_w13'] = _ffn_up_dw(sv['h'], dgate, dup, "ffn_up_dw")
    else:
        dh, slots_b = _ffn_up_dh(dgate, dup, full['ffn_w13'], "ffn_up_dh_scatter", _scatter_chips(partial[3:]))
        gb['ffn_w13'], slots_a = _ffn_up_dw(sv['h'], dgate, dup, "ffn_up_dw_scatter", _scatter_chips(partial[:3]))
        reduced = _chip_sums(partial, list(slots_a) + list(slots_b), cidx)
    dx1, dx1b, g = _rmsnorm_bwd(sv['x1'], sv['r2'], w['ffn_norm'][None], dh, dx2, "rms2_bwd")
    gs['ffn_norm'] = g[0]
    dy = _mm(dx1b, full['w_out'], 'nt', "mm_dy", out_dtype=_BF)
    gb['w_out'] = _mm(sv['y'], dx1b, 'tn', "mm_dwout")
    c0 = (2 * dk + 2 * dv) // nw
    mhd = nw // MEM_HEADS
    do_mem, g = _headnorm_bwd(sv['o_mem'], 0, nw, mhd, w['mem_out_norm'][None], dy, (dv + nw) // nw, "mem_onorm_bwd")
    gs['mem_out_norm'] = g[0]
    dmqn, dkm, dvm = _mem_bwd(sv['mqn'], sv['km'], sv['vm'], do_mem, "mem_bwd")
    dmq, g = _headnorm_bwd(sv['proj'], c0 + 3, nw, mhd, jnp.tile(w['mem_q_norm'], MEM_HEADS)[None], dmqn, 0,
                           "mem_qnorm_bwd", _BF)
    gs['mem_q_norm'] = g[0].reshape(MEM_HEADS, mhd).sum(0)
    dkvk, g = _headnorm_bwd(sv['kv'], 0, nw, mhd, jnp.tile(w['mem_k_norm'], MEM_HEADS)[None], dkm, 0,
                            "mem_knorm_bwd")
    gs['mem_k_norm'] = g[0].reshape(MEM_HEADS, mhd).sum(0)
    dkv = jnp.concatenate([dkvk, dvm], axis=1).astype(_BF)
    gb['mem_wkv'] = _mm(sv['memn'], dkv, 'tn', "mm_dwkv")
    dmemn = _mm(dkv, full['mem_wkv'], 'nt', "mm_dmemn")
    _, _, g = _rmsnorm_bwd(mem, sv['rm'], w['mem_norm'][None], dmemn, None, "mem_rms_bwd")
    gs['mem_norm'] = g[0]
    do_na, g = _headnorm_bwd(sv['o_na'], 0, nw, NA_HD, w['na_out_norm'][None], dy, dv // nw, "na_onorm_bwd")
    gs['na_out_norm'] = g[0]
    dqn, dkn, dnv, dtab = _na_bwd(sv['qn'], sv['kn'], sv['vb'], tab, do_na, "na_bwd")
    gs['na_rpb'] = dtab
    dnq, g = _headnorm_bwd(sv['proj'], c0, nw, NA_HD, jnp.tile(w['na_q_norm'], nw // NA_HD)[None], dqn, 0,
                           "na_qnorm_bwd", _BF)
    gs['na_q_norm'] = g[0].reshape(nw // NA_HD, NA_HD).sum(0)
    dnk, g = _headnorm_bwd(sv['proj'], c0 + 1, nw, NA_HD, jnp.tile(w['na_k_norm'], nw // NA_HD)[None], dkn, 0,
                           "na_knorm_bwd", _BF)
    gs['na_k_norm'] = g[0].reshape(nw // NA_HD, NA_HD).sum(0)
    do_gla, dgr, g = _gla_post_bwd(sv['o_f'], sv['o_b'], sv['proj'], (2 * dk) // dv + 1, w['gla_out_norm'][None], dy,
                                   "gla_post_bwd")
    gs['gla_out_norm'] = g[0]
    dq1, dk1, dv1, dz_f = _gla_bwd(sv['proj'], sv['gfb'], sv['wg'], sv['bg'], sv['st_f'], do_gla, False, None,
                                   "gla_bwd_f")
    dgq, dgk, dgv, dz_b = _gla_bwd(sv['proj'], sv['gfb'], sv['wg'], sv['bg'], sv['st_b'], do_gla, True,
                                   (dq1, dk1, dv1), "gla_bwd_b")
    dz = jnp.concatenate([dz_f, dz_b], axis=1).astype(_BF)
    ones_lane = (jnp.arange(_LANES) == 2 * GLA_GATE_RANK)[None]
    gfb_aug = jnp.where(ones_lane, 1.0, sv['gfb']).astype(_BF)
    dwg = _mm(gfb_aug, dz, 'tn', "mm_dwg")
    r16 = GLA_GATE_RANK
    gs['gla_wg2_f'] = dwg[:r16, :dk]
    gs['gla_wg2_b'] = dwg[r16:2 * r16, dk:]
    gs['gla_bg_f'] = dwg[2 * r16, :dk]
    gs['gla_bg_b'] = dwg[2 * r16, dk:]
    dgfb = _mm(dz, sv['wg'], 'nt', "mm_dgfb", out_dtype=_BF)
    dproj = jnp.concatenate([dgq.astype(_BF), dgk.astype(_BF), dgv.astype(_BF), dgr, dnq, dnk, dnv.astype(_BF), dmq],
                            axis=1)
    t = _mm(dgfb, sv['w_gate'], 'nt', "mm_dxn_gate")
    if pending is None:
        dxn = _mm(dproj, sv['w_main'], 'nt', "mm_dxn", res=t)
        done = None
    else:
        dxn, joined = _mm(dproj, sv['w_main'], 'nt', "mm_dxn_join", res=t, rider=_swap_reduced(reduced))
        done = _assemble_shards(joined)
    dw_main = _mm(sv['xn'], dproj, 'tn', "mm_dwmain")
    dw_gate = _mm(sv['xn'], dgfb, 'tn', "mm_dwgate")
    gb['w_in'] = _w_in_to_shards(dw_main, dw_gate)
    dx, dxb, g = _rmsnorm_bwd(sv['x'], sv['r1'], w['attn_norm'][None], dxn, dx1, "rms1_bwd")
    gs['attn_norm'] = g[0]
    return dx, dxb, gb, gs, done


def kernel(x, mem, attn_norm, w_in, gla_wg2_f, gla_bg_f, gla_wg2_b, gla_bg_b, gla_out_norm, na_q_norm, na_k_norm, na_rpb, na_out_norm, mem_norm, mem_wkv, mem_q_norm, mem_k_norm, mem_out_norm, w_out, ffn_norm, ffn_w13, ffn_w2, loss_target, m_attn_norm, m_w_in, m_gla_wg2_f, m_gla_bg_f, m_gla_wg2_b, m_gla_bg_b, m_gla_out_norm, m_na_q_norm, m_na_k_norm, m_na_rpb, m_na_out_norm, m_mem_norm, m_mem_wkv, m_mem_q_norm, m_mem_k_norm, m_mem_out_norm, m_w_out, m_ffn_norm, m_ffn_w13, m_ffn_w2, v_attn_norm, v_w_in, v_gla_wg2_f, v_gla_bg_f, v_gla_wg2_b, v_gla_bg_b, v_gla_out_norm, v_na_q_norm, v_na_k_norm, v_na_rpb, v_na_out_norm, v_mem_norm, v_mem_wkv, v_mem_q_norm, v_mem_k_norm, v_mem_out_norm, v_w_out, v_ffn_norm, v_ffn_w13, v_ffn_w2):
    args = locals()
    W = {k: args[k] for k in WEIGHTS}
    M = {k: args['m_' + k] for k in WEIGHTS}
    V = {k: args['v_' + k] for k in WEIGHTS}
    depth = attn_norm.shape[0]
    xs, mems, tgt = x[0], mem[0], loss_target[0]
    cidx = lax.axis_index("c").astype(jnp.int32).reshape(1)

    gate_rows, gate_spans = _pack_rows([W[k] for k in SMALL_SHARDED])
    gate_all = _all_devices(gate_rows, False, "gate_weights_gather")
    gate_full = {}
    for i, k in enumerate(SMALL_SHARDED):
        per_chip = [_unpack_rows(gate_all[2 * j], gate_spans)[i] for j in range(4)]
        gate_full[k] = jnp.concatenate(per_chip, axis=-1)

    small = [k for k in WEIGHTS if k not in BIG]

    def layer_small(l):
        d = {k: W[k][l] for k in small if k not in SMALL_SHARDED}
        d.update({k: gate_full[k][l] for k in SMALL_SHARDED})
        return d

    saved, fulls, tabs = [], [], []
    cur = xs
    grid_rows = xs.shape[0] // GRID_W
    all_tabs = jax.vmap(lambda r: _na_bias_tables(r, grid_rows))(W['na_rpb'])
    halves = [_weight_halves({k: W[k][l].astype(_BF) for k in BIG}) for l in range(depth)]
    full = _gather_layer_weights(halves[0], "gather_weights")
    for l in range(depth):
        tab = all_tabs[l]
        nxt = halves[l + 1] if l + 1 < depth else None
        cur, sv, got = _layer_fwd(l, cur, mems, layer_small(l), full, tab, nxt)
        saved.append(sv)
        fulls.append(full)
        tabs.append(tab)
        if nxt is not None:
            full = _assemble_weights(got, nxt)
    loss_tile, dy, dyb = _loss_head(cur, tgt, "loss_head")
    loss = lax.psum(loss_tile[0, 0], ("x", "y", "c"))

    big_grads = [None] * depth
    small_grads = [None] * depth
    pending = None
    for l in range(depth - 1, -1, -1):
        dy, dyb, gb, gs, done = _layer_bwd(l, dy, dyb, mems, layer_small(l), fulls[l], tabs[l], saved[l], pending, cidx)
        if pending is not None:
            big_grads[l + 1] = done
        pending = _grads_as_shards(gb)
        small_grads[l] = gs
    big_grads[0] = _reduce_layer_grads(pending, cidx, "reduce_grads")
    grad_x = dy[None]

    small_stack = {k: jnp.stack([small_grads[l][k] for l in range(depth)]) for k in small}
    small_stack['na_rpb'] = jax.vmap(lambda d: _na_bias_grad(d, grid_rows))(small_stack['na_rpb'])
    packed, spans = _pack_rows([small_stack[k] for k in small])
    summed = _unpack_rows(_all_devices(packed, True, "small_grads_sum"), spans)
    G = dict(zip(small, summed))
    chip = 2 * lax.axis_index("x") + lax.axis_index("y")
    for k in SMALL_SHARDED:
        wdt = W[k].shape[-1]
        G[k] = lax.dynamic_slice_in_dim(G[k], chip * wdt, wdt, axis=2)
    for k in BIG:
        G[k] = jnp.stack([big_grads[l][k] for l in range(depth)])

    delta, new_m, new_v = {}, {}, {}
    for k in WEIGHTS:
        delta[k], new_m[k], new_v[k] = _adamw(W[k], G[k], M[k], V[k], "adamw_" + k)
    return (loss, grad_x, *[G[k] for k in WEIGHTS], *[delta[k] for k in WEIGHTS], *[new_m[k] for k in WEIGHTS],
            *[new_v[k] for k in WEIGHTS])
```

```python
import functools

import numpy as np
import jax
import jax.numpy as jnp
from jax import lax
from jax.experimental import pallas as pl
from jax.experimental.pallas import tpu as pltpu

_F32 = jnp.float32
_BF = jnp.bfloat16
_MESH = pl.DeviceIdType.MESH

_VMEM_LIMIT_BYTES = 56 * 1024 * 1024
_LANES = 128

RMS_EPS = 1e-6
GLA_HEADS = 4
GLA_GATE_RANK = 16
GLA_TAU = 16.0
GLA_CHUNK = 64
GRID_W = 64
NA_HD = 64
NA_ROWS = 8
NA_COLS = 16
MEM_HEADS = 4
ADAM_LR = 0.001
ADAM_B1 = 0.9
ADAM_B2 = 0.999
ADAM_EPS = 1e-08
ADAM_WD = 0.01
ADAM_STEP = 10

WEIGHTS = ['attn_norm', 'w_in', 'gla_wg2_f', 'gla_bg_f', 'gla_wg2_b', 'gla_bg_b', 'gla_out_norm', 'na_q_norm',
           'na_k_norm', 'na_rpb', 'na_out_norm', 'mem_norm', 'mem_wkv', 'mem_q_norm', 'mem_k_norm', 'mem_out_norm',
           'w_out', 'ffn_norm', 'ffn_w13', 'ffn_w2']
BIG = ['w_in', 'mem_wkv', 'w_out', 'ffn_w13', 'ffn_w2']
SMALL_SHARDED = ['gla_wg2_f', 'gla_wg2_b']


def _pick(n, cands):
    for c in cands:
        if n % c == 0:
            return c
    return n


def _row_block(rows, row_bytes, cap_bytes):
    best = 8
    for rb in range(8, rows + 1, 8):
        if rows % rb == 0 and rb * row_bytes <= cap_bytes:
            best = rb
    return best


def _params(sem=None):
    return pltpu.CompilerParams(dimension_semantics=sem, vmem_limit_bytes=_VMEM_LIMIT_BYTES)


def _dg(a, b, ca, cb):
    return lax.dot_general(a.astype(_BF), b.astype(_BF), (((ca,), (cb,)), ((), ())), preferred_element_type=_F32)


def _split(a):
    hi = a.astype(_BF)
    return hi, (a - hi.astype(_F32)).astype(_BF)


def _dg_hl(a, b, ca, cb):
    hi, lo = _split(a)
    return _dg(hi, b, ca, cb) + _dg(lo, b, ca, cb)


def _dg_lh(a, b, ca, cb):
    hi, lo = _split(b)
    return _dg(a, hi, ca, cb) + _dg(a, lo, ca, cb)


def _sigmoid(x):
    return 1.0 / (1.0 + jnp.exp(-x))


def _log_sigmoid(z):
    return jnp.minimum(z, 0.0) - jnp.log(1.0 + jnp.exp(-jnp.abs(z)))


def _block_diag(width, hd):
    i = np.arange(width) // hd
    return jnp.asarray((i[:, None] == i[None, :]).astype(np.float32), dtype=_BF)


class _Rider:
    def __init__(self, ins, outs, sems, start, finish, aliases=()):
        self.ins, self.outs, self.sems = list(ins), list(outs), list(sems)
        self.start, self.finish, self.aliases = start, finish, tuple(aliases)


def _call(core, name, grid, in_specs, out_specs, out_shape, scratch, sem, args, rider=None):
    out_specs, out_shape = tuple(out_specs), tuple(out_shape)
    if rider is None:
        return pl.pallas_call(core, name=name, out_shape=out_shape, grid=grid, in_specs=list(in_specs),
                              out_specs=out_specs, scratch_shapes=list(scratch), compiler_params=_params(sem))(*args)
    ni, no, ns = len(in_specs), len(out_specs), len(scratch)
    ri, ro = len(rider.ins), len(rider.outs)

    def body(*refs):
        ins, rins = refs[:ni], refs[ni:ni + ri]
        outs, routs = refs[ni + ri:ni + ri + no], refs[ni + ri + no:ni + ri + no + ro]
        scr, rsem = refs[ni + ri + no + ro:ni + ri + no + ro + ns], refs[ni + ri + no + ro + ns:]
        first = pl.program_id(0) == 0
        last = pl.program_id(0) == grid[0] - 1
        for ax in range(1, len(grid)):
            first = first & (pl.program_id(ax) == 0)
            last = last & (pl.program_id(ax) == grid[ax] - 1)

        @pl.when(first)
        def _():
            rider.start(rins, routs, rsem)

        core(*ins, *outs, *scr)

        @pl.when(last)
        def _():
            rider.finish(rins, routs, rsem)

    res = pl.pallas_call(
        body, name=name, out_shape=out_shape + tuple(rider.outs), grid=grid,
        in_specs=list(in_specs) + [_ANY] * ri, out_specs=out_specs + (_ANY,) * ro,
        scratch_shapes=list(scratch) + rider.sems,
        input_output_aliases={ni + i: no + o for i, o in rider.aliases},
        compiler_params=_params(("arbitrary",) * len(grid)))(*args, *rider.ins)
    return tuple(res[:no]), tuple(res[no:])


_ANY = pl.BlockSpec(memory_space=pl.ANY)


def _mm(a, b, mode, name, out_dtype=_F32, res=None, rider=None):
    def finish(r):
        return r[0] if rider is None else (r[0][0], r[1])

    if mode == 'nn':
        (M, K), N = a.shape, b.shape[1]
    elif mode == 'nt':
        (M, K), N = a.shape, b.shape[0]
    else:
        (K, M), N = a.shape, b.shape[1]
    ca, cb = {'nn': (1, 0), 'nt': (1, 1), 'tn': (0, 0)}[mode]
    has_res = res is not None
    args = (a, b) + ((res,) if has_res else ())
    out_shape = jax.ShapeDtypeStruct((M, N), out_dtype)
    if K <= _K_RESIDENT:
        tm = _pick(M, (1024, 512, 256, 128))
        tn = _pick(N, (1024, 512, 256, 128))
        a_spec = (pl.BlockSpec((K, tm), lambda i, j: (0, i)) if mode == 'tn'
                  else pl.BlockSpec((tm, K), lambda i, j: (i, 0)))
        b_spec = (pl.BlockSpec((tn, K), lambda i, j: (j, 0)) if mode == 'nt'
                  else pl.BlockSpec((K, tn), lambda i, j: (0, j)))
        o_spec = pl.BlockSpec((tm, tn), lambda i, j: (i, j))

        def body1(*refs):
            a_ref, b_ref = refs[:2]
            o_ref = refs[-1]
            out = _dg(a_ref[...], b_ref[...], ca, cb)
            if has_res:
                out = out + refs[2][...]
            o_ref[...] = out.astype(o_ref.dtype)

        return finish(_call(body1, name, (M // tm, N // tn), [a_spec, b_spec] + ([o_spec] if has_res else []),
                            [o_spec], [out_shape], [], ("parallel", "parallel"), args, rider))

    tm, tn, tk = _loop_tiles(M, N, K, has_res, out_dtype)
    nk = K // tk
    if mode == 'tn':
        a_spec = pl.BlockSpec((tk, tm), lambda i, j, k: (k, i))
    else:
        a_spec = pl.BlockSpec((tm, tk), lambda i, j, k: (i, k))
    if mode == 'nt':
        b_spec = pl.BlockSpec((tn, tk), lambda i, j, k: (j, k))
    else:
        b_spec = pl.BlockSpec((tk, tn), lambda i, j, k: (k, j))
    o_spec = pl.BlockSpec((tm, tn), lambda i, j, k: (i, j))
    in_out = out_dtype == _F32

    def body(*refs):
        a_ref, b_ref = refs[:2]
        o_ref = refs[-1] if in_out else refs[-2]
        acc = o_ref if in_out else refs[-1]
        k = pl.program_id(2)

        @pl.when(k == 0)
        def _():
            part = _dg(a_ref[...], b_ref[...], ca, cb)
            acc[...] = part + refs[2][...] if has_res else part

        @pl.when(k > 0)
        def _():
            acc[...] += _dg(a_ref[...], b_ref[...], ca, cb)

        if not in_out:
            @pl.when(k == nk - 1)
            def _():
                o_ref[...] = acc[...].astype(o_ref.dtype)

    return finish(_call(body, name, (M // tm, N // tn, nk), [a_spec, b_spec] + ([o_spec] if has_res else []),
                        [o_spec], [out_shape], [] if in_out else [pltpu.VMEM((tm, tn), _F32)],
                        ("parallel", "parallel", "arbitrary"), args, rider))


_K_RESIDENT = 2048
_LOOP_TILE_BYTES = 40 * 1024 * 1024
_LOOP_TILE_AREA = 1024 * 1024


def _loop_tiles(M, N, K, has_res, out_dtype):
    best, best_score = None, None
    obytes = 4 if out_dtype == _F32 else 2
    for tk in (2816, 2560, 2048, 1408, 1024, 512, 256, 128):
        if K % tk:
            continue
        for tm in (2048, 1408, 1024, 512, 256, 128):
            for tn in (2048, 1024, 512, 256, 128):
                if M % tm or N % tn:
                    continue
                need = 2 * 2 * tk * (tm + tn) + tm * tn * (2 * obytes + (8 if has_res else 0) + (0 if obytes == 4 else 4))
                score = (min(tm * tn, _LOOP_TILE_AREA), tk, tm * tn)
                if need <= _LOOP_TILE_BYTES and (best is None or score > best_score):
                    best, best_score = (tm, tn, tk), score
    return best


def _cols_to_shards(w):
    r, c = w.shape[0] // 2, w.shape[1] // 4
    return jnp.transpose(w.reshape(2, r, 4, c), (2, 0, 1, 3))


def _shards_to_cols(g):
    _, _, r, c = g.shape
    return jnp.transpose(g, (1, 2, 0, 3)).reshape(2 * r, 4 * c)


_FFN_SUB = 256


def _ffn_up_fwd(h, g13, name, rider=None):
    S = h.shape[0]
    _, _, R, C = g13.shape
    tm = _pick(S, (512, 256, 128))
    cw, nc = _ffn_cols(C)
    subs = [(s, min(s + _FFN_SUB, cw)) for s in range(0, cw, _FFN_SUB)]

    def body(h_ref, wg_ref, wu_ref, a_ref, g_ref, u_ref):
        h0, h1 = h_ref[:, :R], h_ref[:, R:]
        for lo, hi in subs:
            sl = slice(lo, hi)
            gv = _dg(h0, wg_ref[0, 0, :, sl], 1, 0) + _dg(h1, wg_ref[0, 1, :, sl], 1, 0)
            uv = _dg(h0, wu_ref[0, 0, :, sl], 1, 0) + _dg(h1, wu_ref[0, 1, :, sl], 1, 0)
            sg = _sigmoid(gv)
            silu = gv * sg
            a_ref[:, sl] = (silu * uv).astype(a_ref.dtype)
            g_ref[:, sl] = (uv * (sg * (1.0 + gv * (1.0 - sg)))).astype(g_ref.dtype)
            u_ref[:, sl] = silu.astype(u_ref.dtype)

    out = jax.ShapeDtypeStruct((S, 2 * C), _BF)
    ospec = pl.BlockSpec((tm, cw), lambda j, i: (i, j))
    return _call(
        body, name, (2 * nc, S // tm),
        [pl.BlockSpec((tm, 2 * R), lambda j, i: (i, 0)),
         pl.BlockSpec((1, 2, R, cw), lambda j, i: (lax.div(j, nc), 0, 0, lax.rem(j, nc))),
         pl.BlockSpec((1, 2, R, cw), lambda j, i: (2 + lax.div(j, nc), 0, 0, lax.rem(j, nc)))],
        (ospec, ospec, ospec), (out, out, out), [], ("parallel", "parallel"), (h, g13, g13), rider)


def _ffn_cols(C):
    if C % (2 * _LANES) == 0:
        return C // 2, 2
    return C, 1


def _ffn_up_bwd(dxb, w2, gate, up, name, rider=None):
    S, D = dxb.shape
    F = w2.shape[0]
    tf = 2 * _FFN_SUB
    tm = _pick(S, (1024, 512, 256, 128))

    def body(d_ref, w_ref, g_ref, u_ref, dg_ref, du_ref):
        dv = d_ref[...]
        for s in range(tf // _FFN_SUB):
            sl = slice(s * _FFN_SUB, (s + 1) * _FFN_SUB)
            da = _dg(dv, w_ref[sl, :], 1, 1)
            dg_ref[:, sl] = (da * g_ref[:, sl].astype(_F32)).astype(dg_ref.dtype)
            du_ref[:, sl] = (da * u_ref[:, sl].astype(_F32)).astype(du_ref.dtype)

    out = jax.ShapeDtypeStruct((S, F), _BF)
    tile = pl.BlockSpec((tm, tf), lambda i, j: (i, j))
    return _call(
        body, name, (S // tm, F // tf),
        [pl.BlockSpec((tm, D), lambda i, j: (i, 0)), pl.BlockSpec((tf, D), lambda i, j: (j, 0)), tile, tile],
        (tile, tile), (out, out), [], ("parallel", "parallel"), (dxb, w2, gate, up), rider)


def _ffn_up_dh(dgate, dup, g13, name, rider=None):
    S = dgate.shape[0]
    _, _, R, C = g13.shape
    tm = _pick(S, (512, 256, 128))

    def body(dg_ref, du_ref, wg_ref, wu_ref, o_ref):
        j = pl.program_id(1)

        def part(hh):
            return _dg(dg_ref[...], wg_ref[0, hh], 1, 1) + _dg(du_ref[...], wu_ref[0, hh], 1, 1)

        @pl.when(j == 0)
        def _():
            for hh in range(2):
                o_ref[:, hh * R:(hh + 1) * R] = part(hh)

        @pl.when(j > 0)
        def _():
            for hh in range(2):
                o_ref[:, hh * R:(hh + 1) * R] += part(hh)

    cw, nc = _ffn_cols(C)
    tile = pl.BlockSpec((tm, cw), lambda i, j: (i, j))
    r = _call(
        body, name, (S // tm, 2 * nc),
        [tile, tile, pl.BlockSpec((1, 2, R, cw), lambda i, j: (lax.div(j, nc), 0, 0, lax.rem(j, nc))),
         pl.BlockSpec((1, 2, R, cw), lambda i, j: (2 + lax.div(j, nc), 0, 0, lax.rem(j, nc)))],
        [pl.BlockSpec((tm, 2 * R), lambda i, j: (i, 0))], [jax.ShapeDtypeStruct((S, 2 * R), _F32)], [],
        ("parallel", "arbitrary"), (dgate, dup, g13, g13), rider)
    return r[0] if rider is None else (r[0][0], r[1])


def _ffn_up_dw(hact, dgate, dup, name, rider=None):
    S, R2 = hact.shape
    R = R2 // 2
    C = dgate.shape[1] // 2
    tk = _pick(S, (1024, 512, 256, 128))
    nk = S // tk
    cw, nc = _ffn_cols(C)

    def body(h_ref, dg_ref, du_ref, o_ref):
        chip = pl.program_id(0)
        k = pl.program_id(3)

        def accumulate(d_ref):
            @pl.when(k == 0)
            def _():
                o_ref[0, 0] = _dg(h_ref[...], d_ref[...], 0, 0)

            @pl.when(k > 0)
            def _():
                o_ref[0, 0] += _dg(h_ref[...], d_ref[...], 0, 0)

        @pl.when(chip < 2)
        def _():
            accumulate(dg_ref)

        @pl.when(chip >= 2)
        def _():
            accumulate(du_ref)

    r = _call(
        body, name, (4, nc, 2, nk),
        [pl.BlockSpec((tk, R), lambda c, b, h, k: (k, h)),
         pl.BlockSpec((tk, cw), lambda c, b, h, k: (jnp.where(c < 2, k, 0), jnp.where(c < 2, c * nc + b, 0))),
         pl.BlockSpec((tk, cw), lambda c, b, h, k: (jnp.where(c >= 2, k, 0), jnp.where(c >= 2, (c - 2) * nc + b, 0)))],
        [pl.BlockSpec((1, 1, R, cw), lambda c, b, h, k: (c, h, 0, b))], [jax.ShapeDtypeStruct((4, 2, R, C), _F32)], [],
        ("parallel", "parallel", "parallel", "arbitrary"), (hact, dgate, dup), rider)
    return r[0] if rider is None else (r[0][0], r[1])


def _rmsnorm_fwd(x, g, name):
    S, D = x.shape
    ts = _pick(S, (256,))

    def body(x_ref, g_ref, o_ref, r_ref):
        xv = x_ref[...]
        r = lax.rsqrt(jnp.mean(xv * xv, axis=-1, keepdims=True) + RMS_EPS)
        o_ref[...] = (xv * r * g_ref[...]).astype(o_ref.dtype)
        r_ref[...] = r

    return pl.pallas_call(
        body, name=name,
        out_shape=(jax.ShapeDtypeStruct((S, D), _BF), jax.ShapeDtypeStruct((S, 1), _F32)),
        grid=(S // ts,),
        in_specs=[pl.BlockSpec((ts, D), lambda i: (i, 0)), pl.BlockSpec((1, D), lambda i: (0, 0))],
        out_specs=(pl.BlockSpec((ts, D), lambda i: (i, 0)), pl.BlockSpec((ts, 1), lambda i: (i, 0))),
        compiler_params=_params(("parallel",)))(x, g)


def _rmsnorm_bwd(x, r, g, dy, dres, name):
    S, D = x.shape
    ts = _pick(S, (256,))
    has_res = dres is not None

    def body(*refs):
        if has_res:
            x_ref, r_ref, g_ref, dy_ref, dr_ref, dx_ref, dxb_ref, dg_ref = refs
        else:
            x_ref, r_ref, g_ref, dy_ref, dx_ref, dxb_ref, dg_ref = refs
        rv = r_ref[...]
        n = x_ref[...] * rv
        dyv = dy_ref[...]
        dn = dyv * g_ref[...]
        c = jnp.mean(dn * n, axis=-1, keepdims=True)
        dx = rv * (dn - n * c)
        if has_res:
            dx = dx + dr_ref[...]
        dx_ref[...] = dx
        dxb_ref[...] = dx.astype(dxb_ref.dtype)

        @pl.when(pl.program_id(0) == 0)
        def _():
            dg_ref[...] = jnp.zeros_like(dg_ref)

        dg_ref[...] += jnp.sum(dyv * n, axis=0, keepdims=True)

    row = pl.BlockSpec((ts, D), lambda i: (i, 0))
    vec = pl.BlockSpec((1, D), lambda i: (0, 0))
    in_specs = [row, pl.BlockSpec((ts, 1), lambda i: (i, 0)), vec, row] + ([row] if has_res else [])
    args = (x, r, g, dy) + ((dres,) if has_res else ())
    return pl.pallas_call(
        body, name=name,
        out_shape=(jax.ShapeDtypeStruct((S, D), _F32), jax.ShapeDtypeStruct((S, D), _BF),
                   jax.ShapeDtypeStruct((1, D), _F32)),
        grid=(S // ts,), in_specs=in_specs, out_specs=(row, row, vec),
        compiler_params=_params(("arbitrary",)))(*args)


def _headnorm_fwd(t, cb, W, hd, g, name):
    S = t.shape[0]
    ts = _pick(S, (512, 256))
    bd = _block_diag(W, hd)

    def body(x_ref, g_ref, bd_ref, o_ref):
        xv = x_ref[...].astype(_F32)
        ms = _dg_hl(xv * xv, bd_ref[...], 1, 0) * (1.0 / hd)
        o_ref[...] = (xv * lax.rsqrt(ms + RMS_EPS) * g_ref[...]).astype(o_ref.dtype)

    return pl.pallas_call(
        body, name=name, out_shape=jax.ShapeDtypeStruct((S, W), _BF), grid=(S // ts,),
        in_specs=[pl.BlockSpec((ts, W), lambda i: (i, cb)), pl.BlockSpec((1, W), lambda i: (0, 0)),
                  pl.BlockSpec((W, W), lambda i: (0, 0))],
        out_specs=pl.BlockSpec((ts, W), lambda i: (i, 0)),
        compiler_params=_params(("parallel",)))(t, g, bd)


def _headnorm_bwd(t, cb, W, hd, g, dy, dcb, name, out_dtype=_F32):
    S = t.shape[0]
    ts = _pick(S, (512, 256))
    bd = _block_diag(W, hd)

    def body(x_ref, g_ref, bd_ref, dy_ref, dx_ref, dg_ref):
        xv = x_ref[...].astype(_F32)
        bdv = bd_ref[...]
        ms = _dg_hl(xv * xv, bdv, 1, 0) * (1.0 / hd)
        rv = lax.rsqrt(ms + RMS_EPS)
        n = xv * rv
        dyv = dy_ref[...].astype(_F32)
        dn = dyv * g_ref[...]
        c = _dg_hl(dn * n, bdv, 1, 0) * (1.0 / hd)
        dx_ref[...] = (rv * (dn - n * c)).astype(dx_ref.dtype)

        @pl.when(pl.program_id(0) == 0)
        def _():
            dg_ref[...] = jnp.zeros_like(dg_ref)

        dg_ref[...] += jnp.sum(dyv * n, axis=0, keepdims=True)

    return pl.pallas_call(
        body, name=name,
        out_shape=(jax.ShapeDtypeStruct((S, W), out_dtype), jax.ShapeDtypeStruct((1, W), _F32)),
        grid=(S // ts,),
        in_specs=[pl.BlockSpec((ts, W), lambda i: (i, cb)), pl.BlockSpec((1, W), lambda i: (0, 0)),
                  pl.BlockSpec((W, W), lambda i: (0, 0)), pl.BlockSpec((ts, W), lambda i: (i, dcb))],
        out_specs=(pl.BlockSpec((ts, W), lambda i: (i, 0)), pl.BlockSpec((1, W), lambda i: (0, 0))),
        compiler_params=_params(("arbitrary",)))(t, g, bd, dy)


def _gla_post_fwd(o_f, o_b, proj, r_cb, g, name):
    S, W = o_f.shape
    hd = W // GLA_HEADS
    ts = _pick(S, (256,))
    bd = _block_diag(W, hd)

    def body(of_ref, ob_ref, r_ref, g_ref, bd_ref, y_ref):
        o = of_ref[...] + ob_ref[...]
        ms = _dg_hl(o * o, bd_ref[...], 1, 0) * (1.0 / hd)
        u = o * lax.rsqrt(ms + RMS_EPS) * g_ref[...]
        rr = r_ref[...].astype(_F32)
        y_ref[...] = (u * (rr * _sigmoid(rr))).astype(y_ref.dtype)

    row = pl.BlockSpec((ts, W), lambda i: (i, 0))
    return pl.pallas_call(
        body, name=name, out_shape=jax.ShapeDtypeStruct((S, W), _BF), grid=(S // ts,),
        in_specs=[row, row, pl.BlockSpec((ts, W), lambda i: (i, r_cb)), pl.BlockSpec((1, W), lambda i: (0, 0)),
                  pl.BlockSpec((W, W), lambda i: (0, 0))],
        out_specs=row, compiler_params=_params(("parallel",)))(o_f, o_b, proj, g, bd)


def _gla_post_bwd(o_f, o_b, proj, r_cb, g, dy, name):
    S, W = o_f.shape
    hd = W // GLA_HEADS
    ts = _pick(S, (256,))
    bd = _block_diag(W, hd)

    def body(of_ref, ob_ref, r_ref, g_ref, bd_ref, dy_ref, do_ref, dr_ref, dg_ref):
        o = of_ref[...] + ob_ref[...]
        bdv = bd_ref[...]
        ms = _dg_hl(o * o, bdv, 1, 0) * (1.0 / hd)
        rv = lax.rsqrt(ms + RMS_EPS)
        n = o * rv
        gv = g_ref[...]
        rr = r_ref[...].astype(_F32)
        sg = _sigmoid(rr)
        dyv = dy_ref[...].astype(_F32)
        dr_ref[...] = (dyv * (n * gv) * (sg * (1.0 + rr * (1.0 - sg)))).astype(dr_ref.dtype)
        du = dyv * (rr * sg)
        dn = du * gv
        c = _dg_hl(dn * n, bdv, 1, 0) * (1.0 / hd)
        do_ref[...] = rv * (dn - n * c)

        @pl.when(pl.program_id(0) == 0)
        def _():
            dg_ref[...] = jnp.zeros_like(dg_ref)

        dg_ref[...] += jnp.sum(du * n, axis=0, keepdims=True)

    row = pl.BlockSpec((ts, W), lambda i: (i, 0))
    vec = pl.BlockSpec((1, W), lambda i: (0, 0))
    return pl.pallas_call(
        body, name=name,
        out_shape=(jax.ShapeDtypeStruct((S, W), _F32), jax.ShapeDtypeStruct((S, W), _BF),
                   jax.ShapeDtypeStruct((1, W), _F32)),
        grid=(S // ts,),
        in_specs=[row, row, pl.BlockSpec((ts, W), lambda i: (i, r_cb)), vec, pl.BlockSpec((W, W), lambda i: (0, 0)),
                  row],
        out_specs=(row, row, vec), compiler_params=_params(("arbitrary",)))(o_f, o_b, proj, g, bd, dy)


def _loss_head(y, tgt, name):
    S, D = y.shape
    ts = _pick(S, (256,))

    def body(y_ref, t_ref, l_ref, d_ref, db_ref):
        err = y_ref[...] - t_ref[...]
        d = err * (1.0 / D)
        d_ref[...] = d
        db_ref[...] = d.astype(db_ref.dtype)

        @pl.when(pl.program_id(0) == 0)
        def _():
            l_ref[...] = jnp.zeros_like(l_ref)

        l_ref[...] += 0.5 * jnp.sum(jnp.mean(err * err, axis=-1, keepdims=True))

    row = pl.BlockSpec((ts, D), lambda i: (i, 0))
    return pl.pallas_call(
        body, name=name,
        out_shape=(jax.ShapeDtypeStruct((8, _LANES), _F32), jax.ShapeDtypeStruct((S, D), _F32),
                   jax.ShapeDtypeStruct((S, D), _BF)),
        grid=(S // ts,), in_specs=[row, row],
        out_specs=(pl.BlockSpec((8, _LANES), lambda i: (0, 0)), row, row),
        compiler_params=_params(("arbitrary",)))(y, tgt)


def _adamw(w, g, m, v, name):
    shape = w.shape
    if w.ndim == 3 and shape[1] * shape[2] > 256 * 1024:
        rb = _row_block(shape[1], shape[2] * 4, 1536 * 1024)
        grid = (shape[0], shape[1] // rb)
        spec = pl.BlockSpec((1, rb, shape[2]), lambda l, i: (l, i, 0))
        sem = ("parallel", "parallel")
    else:
        grid = ()
        spec = pl.BlockSpec(memory_space=pltpu.VMEM)
        sem = None

    def body(w_ref, g_ref, m_ref, v_ref, d_ref, nm_ref, nv_ref):
        gv = g_ref[...]
        mn = ADAM_B1 * m_ref[...] + (1.0 - ADAM_B1) * gv
        vn = ADAM_B2 * v_ref[...] + (1.0 - ADAM_B2) * (gv * gv)
        m_hat = mn / (1.0 - ADAM_B1 ** ADAM_STEP)
        v_hat = vn / (1.0 - ADAM_B2 ** ADAM_STEP)
        d_ref[...] = -ADAM_LR * (m_hat / (jnp.sqrt(v_hat) + ADAM_EPS) + ADAM_WD * w_ref[...])
        nm_ref[...] = mn
        nv_ref[...] = vn

    out = jax.ShapeDtypeStruct(shape, _F32)
    return pl.pallas_call(
        body, name=name, out_shape=(out, out, out), grid=grid, in_specs=[spec] * 4, out_specs=(spec,) * 3,
        compiler_params=_params(sem))(w, g, m, v)


def _add_pair(t, la, cidx, name):
    _, _, r, c = t.shape
    rb = _pick(r, (256, 352, 176, 88, 8))

    def body(c_ref, t_ref, l_ref, o_ref):
        o_ref[...] = (t_ref[...] + l_ref[...]).astype(o_ref.dtype)

    return pl.pallas_call(
        body, name=name, out_shape=jax.ShapeDtypeStruct((4, r, c), _BF),
        grid_spec=pltpu.PrefetchScalarGridSpec(
            num_scalar_prefetch=1, grid=(4, r // rb),
            in_specs=[pl.BlockSpec((1, None, rb, c), lambda j, i, cr: (j, cr[0], i, 0)),
                      pl.BlockSpec((1, rb, c), lambda j, i, cr: (j, i, 0))],
            out_specs=pl.BlockSpec((1, rb, c), lambda j, i, cr: (j, i, 0))),
        compiler_params=_params(("parallel", "parallel")))(cidx, t, la)


def _add_four(p, land, chip, core, name):
    _, r, c = p.shape
    rb = _pick(r, (256, 352, 176, 88, 16))

    def body(chip_ref, core_ref, p_ref, l1_ref, l2_ref, l3_ref, o_ref):
        o_ref[...] = ((p_ref[...].astype(_F32) + l1_ref[...].astype(_F32)) + l2_ref[...].astype(_F32)
                      ) + l3_ref[...].astype(_F32)

    def slot(flip):
        return pl.BlockSpec((None, rb, c), lambda i, ch, co: (jnp.bitwise_xor(ch[0], flip), i, 0))

    return pl.pallas_call(
        body, name=name, out_shape=jax.ShapeDtypeStruct((2, r, c), _F32),
        grid_spec=pltpu.PrefetchScalarGridSpec(
            num_scalar_prefetch=2, grid=(r // rb,),
            in_specs=[slot(0), slot(1), slot(2), slot(3)],
            out_specs=pl.BlockSpec((None, rb, c), lambda i, ch, co: (co[0], i, 0))),
        compiler_params=_params(("parallel",)))(chip, core, p, land, land, land)


def _gla_masks(tb, rev):
    i = np.arange(tb)
    same = (i[:, None] // GLA_CHUNK) == (i[None, :] // GLA_CHUNK)
    tri = same & (i[None, :] <= i[:, None])
    keep = (same & ~tri) if rev else tri
    return tuple(jnp.asarray(m.astype(np.float32), dtype=_BF) for m in (tri, same, keep))


def _gla_block(q, k, gfb, wg, bg, tri, same, keepm, rev, scale):
    z = _dg(gfb, wg, 1, 0) + bg
    la = _log_sigmoid(z) * (1.0 / GLA_TAU)
    cum = _dg_lh(tri, la, 0 if rev else 1, 0)
    tot = _dg_lh(same, la, 1, 0)
    e_a = jnp.exp(cum)
    e_na = jnp.exp(-cum)
    e_la = jnp.exp(tot - cum)
    qe = q * scale * e_a
    ke = k * e_na
    kend = k * e_la
    keep = keepm > 0
    p = jnp.where(keep, _dg(qe, ke, 1, 1), 0.0)
    return dict(z=z, tot=tot, e_a=e_a, e_na=e_na, e_la=e_la, qe=qe, ke=ke, kend=kend, keep=keep, p=p)


def _gla_fwd(proj, gfb, wg, bg, rev, name):
    S = proj.shape[0]
    H = GLA_HEADS
    dk = wg.shape[1] // (2 * H)
    dv = 2 * dk
    C = GLA_CHUNK
    cb_n = _pick(S // C, (8, 4, 2, 1))
    tb = cb_n * C
    nb = S // tb
    scale = float(dk) ** -0.5
    masks = _gla_masks(tb, rev)
    wcol = H if rev else 0

    def bmap(b):
        return nb - 1 - b if rev else b

    def body(q_ref, k_ref, v_ref, g_ref, wg_ref, bg_ref, tri_ref, same_ref, keep_ref, o_ref, st_ref, state):
        h = pl.program_id(1)

        @pl.when(pl.program_id(0) == 0)
        def _():
            state[h] = jnp.zeros((dv, dk), _F32)

        t = _gla_block(q_ref[...].astype(_F32), k_ref[...].astype(_F32), g_ref[...], wg_ref[...], bg_ref[...],
                       tri_ref[...], same_ref[...],
                       keep_ref[...], rev, scale)
        vv = v_ref[...]
        o_ref[...] = _dg(t['p'], vv, 1, 0)
        order = range(cb_n - 1, -1, -1) if rev else range(cb_n)
        for ci in order:
            sl = slice(ci * C, (ci + 1) * C)
            st = state[h]
            o_ref[sl, :] += _dg(t['qe'][sl], st, 1, 1)
            st_ref[0, ci] = st
            state[h] = st * jnp.exp(t['tot'][ci * C:ci * C + 1]) + _dg(vv[sl], t['kend'][sl], 0, 0)

    nq = (H * dk) // dk
    msk = pl.BlockSpec((tb, tb), lambda b, h: (0, 0))
    return pl.pallas_call(
        body, name=name,
        out_shape=(jax.ShapeDtypeStruct((S, H * dv), _F32), jax.ShapeDtypeStruct((H, S // C, dv, dk), _F32)),
        grid=(nb, H),
        in_specs=[pl.BlockSpec((tb, dk), lambda b, h: (bmap(b), h)),
                  pl.BlockSpec((tb, dk), lambda b, h: (bmap(b), nq + h)),
                  pl.BlockSpec((tb, dv), lambda b, h: (bmap(b), (2 * H * dk) // dv + h)),
                  pl.BlockSpec((tb, _LANES), lambda b, h: (bmap(b), 0)),
                  pl.BlockSpec((_LANES, dk), lambda b, h: (0, wcol + h)),
                  pl.BlockSpec((1, dk), lambda b, h: (0, wcol + h)),
                  msk, msk, msk],
        out_specs=(pl.BlockSpec((tb, dv), lambda b, h: (bmap(b), h)),
                   pl.BlockSpec((1, cb_n, dv, dk), lambda b, h: (h, bmap(b), 0, 0))),
        scratch_shapes=[pltpu.VMEM((H, dv, dk), _F32)],
        compiler_params=_params(("arbitrary", "arbitrary")))(proj, proj, proj, gfb, wg, bg, *masks)


def _gla_bwd(proj, gfb, wg, bg, st, do, rev, prev, name):
    S = proj.shape[0]
    H = GLA_HEADS
    dk = wg.shape[1] // (2 * H)
    dv = 2 * dk
    C = GLA_CHUNK
    cb_n = _pick(S // C, (8, 4, 2, 1))
    tb = cb_n * C
    nb = S // tb
    scale = float(dk) ** -0.5
    masks = _gla_masks(tb, rev)
    wcol = H if rev else 0
    has_prev = prev is not None

    def bmap(b):
        return b if rev else nb - 1 - b

    def body(*refs):
        if has_prev:
            (q_ref, k_ref, v_ref, g_ref, wg_ref, bg_ref, tri_ref, same_ref, keep_ref, st_ref, do_ref,
             pq_ref, pk_ref, pv_ref, dq_ref, dk_ref, dv_ref, dz_ref, dstate) = refs
        else:
            (q_ref, k_ref, v_ref, g_ref, wg_ref, bg_ref, tri_ref, same_ref, keep_ref, st_ref, do_ref,
             dq_ref, dk_ref, dv_ref, dz_ref, dstate) = refs
        h = pl.program_id(1)

        @pl.when(pl.program_id(0) == 0)
        def _():
            dstate[h] = jnp.zeros((dv, dk), _F32)

        t = _gla_block(q_ref[...].astype(_F32), k_ref[...].astype(_F32), g_ref[...], wg_ref[...], bg_ref[...],
                       tri_ref[...], same_ref[...],
                       keep_ref[...], rev, scale)
        vv = v_ref[...]
        dov = do_ref[...]
        order = range(cb_n) if rev else range(cb_n - 1, -1, -1)
        for ci in order:
            sl = slice(ci * C, (ci + 1) * C)
            stp = st_ref[0, ci]
            dst = dstate[h]
            e_l = jnp.exp(t['tot'][ci * C:ci * C + 1])
            kend_c = t['kend'][sl]
            dkend_c = _dg(vv[sl], dst, 1, 0)
            dq_ref[sl, :] = _dg(dov[sl], stp, 1, 0)
            dk_ref[sl, :] = dkend_c
            dv_ref[sl, :] = _dg(kend_c, dst, 1, 1)
            dtot = (e_l * jnp.sum(dst * stp, axis=0, keepdims=True)
                    + jnp.sum(dkend_c * kend_c, axis=0, keepdims=True))
            dz_ref[sl, :] = jnp.broadcast_to(dtot, (C, dk))
            dstate[h] = dst * e_l + _dg(dov[sl], t['qe'][sl], 0, 0)
        dp = jnp.where(t['keep'], _dg(dov, vv, 1, 1), 0.0)
        dqe = _dg(dp, t['ke'], 1, 0) + dq_ref[...]
        dke = _dg(dp, t['qe'], 0, 0)
        dkend = dk_ref[...]
        dvv = _dg(t['p'], dov, 0, 0) + dv_ref[...]
        dqv = dqe * t['e_a'] * scale
        dkv = dke * t['e_na'] + dkend * t['e_la']
        d_a = dqe * t['qe'] - dke * t['ke'] - dkend * t['kend']
        dla = _dg_lh(tri_ref[...], d_a, 1 if rev else 0, 0) + dz_ref[...]
        dz_ref[...] = dla * (1.0 / GLA_TAU) * _sigmoid(-t['z'])
        if has_prev:
            dqv = dqv + pq_ref[...]
            dkv = dkv + pk_ref[...]
            dvv = dvv + pv_ref[...]
        dq_ref[...] = dqv
        dk_ref[...] = dkv
        dv_ref[...] = dvv

    nq = (H * dk) // dk
    msk = pl.BlockSpec((tb, tb), lambda b, h: (0, 0))
    kblk = pl.BlockSpec((tb, dk), lambda b, h: (bmap(b), h))
    vblk = pl.BlockSpec((tb, dv), lambda b, h: (bmap(b), h))
    in_specs = [kblk,
                pl.BlockSpec((tb, dk), lambda b, h: (bmap(b), nq + h)),
                pl.BlockSpec((tb, dv), lambda b, h: (bmap(b), (2 * H * dk) // dv + h)),
                pl.BlockSpec((tb, _LANES), lambda b, h: (bmap(b), 0)),
                pl.BlockSpec((_LANES, dk), lambda b, h: (0, wcol + h)),
                pl.BlockSpec((1, dk), lambda b, h: (0, wcol + h)),
                msk, msk, msk,
                pl.BlockSpec((1, cb_n, dv, dk), lambda b, h: (h, bmap(b), 0, 0)),
                vblk]
    args = [proj, proj, proj, gfb, wg, bg, *masks, st, do]
    if has_prev:
        in_specs += [kblk, kblk, vblk]
        args += list(prev)
    return pl.pallas_call(
        body, name=name,
        out_shape=(jax.ShapeDtypeStruct((S, H * dk), _F32), jax.ShapeDtypeStruct((S, H * dk), _F32),
                   jax.ShapeDtypeStruct((S, H * dv), _F32), jax.ShapeDtypeStruct((S, H * dk), _F32)),
        grid=(nb, H), in_specs=in_specs, out_specs=(kblk, kblk, vblk, kblk),
        scratch_shapes=[pltpu.VMEM((H, dv, dk), _F32)],
        compiler_params=_params(("arbitrary", "arbitrary")))(*args)


NA_GROUP = 4
NA_WIN_ROWS = NA_GROUP + NA_ROWS


def _na_geometry(S):
    rows = S // GRID_W
    assert rows % NA_GROUP == 0 and rows >= NA_WIN_ROWS + NA_GROUP
    return rows, rows // NA_GROUP


def _na_win_start(g, rows):
    return jnp.clip(NA_GROUP * g - NA_ROWS // 2, 0, rows - NA_WIN_ROWS)


def _na_class(g, groups):
    return jnp.where(g == 0, 0, jnp.where(g == groups - 1, 2, 1))


def _na_onehots(rows):
    groups = rows // NA_GROUP
    by_row = np.zeros((3, NA_GROUP, NA_WIN_ROWS, 2 * NA_ROWS - 1), np.float32)
    for cls, g in enumerate((0, 1, groups - 1)):
        ws = int(np.clip(NA_GROUP * g - NA_ROWS // 2, 0, rows - NA_WIN_ROWS))
        for qr in range(NA_GROUP):
            r = NA_GROUP * g + qr
            rs = int(np.clip(r - NA_ROWS // 2, 0, rows - NA_ROWS))
            for kr in range(NA_WIN_ROWS):
                if rs <= ws + kr < rs + NA_ROWS:
                    by_row[cls, qr, kr, ws + kr - r + NA_ROWS - 1] = 1.0
    c = np.arange(GRID_W)
    cs = np.clip(c - NA_COLS // 2, 0, GRID_W - NA_COLS)
    kc = np.arange(GRID_W)
    win = (kc[None, :] >= cs[:, None]) & (kc[None, :] < cs[:, None] + NA_COLS)
    idx = np.clip(kc[None, :] - c[:, None], -(NA_COLS - 1), NA_COLS - 1) + (NA_COLS - 1)
    by_col = ((idx[:, :, None] == np.arange(2 * NA_COLS - 1)[None, None, :]) & win[:, :, None]).astype(np.float32)
    return by_row, by_col


def _na_bias_tables(rpb, rows):
    by_row, by_col = _na_onehots(rows)
    H = rpb.shape[0]
    e1 = jnp.einsum('hij,ckj->hick', rpb, by_col, precision=lax.Precision.HIGHEST)
    e1 = jnp.where((by_col.sum(-1) > 0)[None, None], e1, -jnp.inf)
    none = jnp.full((H, GRID_W, GRID_W), -jnp.inf, _F32)
    pick, valid = by_row.argmax(-1), by_row.sum(-1) > 0
    tabs = []
    for z in range(3):
        bands = [jnp.concatenate([e1[:, pick[z, q, r]] if valid[z, q, r] else none for r in range(NA_WIN_ROWS)],
                                 axis=-1) for q in range(NA_GROUP)]
        tabs.append(jnp.stack(bands, axis=1).reshape(H, NA_GROUP * GRID_W, NA_WIN_ROWS * GRID_W))
    return jnp.stack(tabs)


def _na_bias_grad(dtab, rows):
    by_row, by_col = _na_onehots(rows)
    H = dtab.shape[1]
    pick, valid = by_row.argmax(-1), by_row.sum(-1) > 0
    d6 = dtab.reshape(3, H, NA_GROUP, GRID_W, NA_WIN_ROWS, GRID_W)
    slabs = [[] for _ in range(2 * NA_ROWS - 1)]
    for z in range(3):
        for q in range(NA_GROUP):
            for r in range(NA_WIN_ROWS):
                if valid[z, q, r]:
                    slabs[pick[z, q, r]].append(d6[z, :, q, :, r, :])
    zero = jnp.zeros((H, GRID_W, GRID_W), _F32)
    de1 = jnp.stack([sum(s[1:], s[0]) if s else zero for s in slabs], axis=1)
    return jnp.einsum('hick,ckj->hij', de1, by_col, precision=lax.Precision.HIGHEST)


def _na_fwd(qn, kn, vb, tab, name):
    S, W = qn.shape
    rows, groups = _na_geometry(S)
    npair = W // _LANES
    nq = NA_GROUP * GRID_W
    nk = NA_WIN_ROWS * GRID_W
    sc = float(NA_HD) ** -0.5

    def body(q_ref, k_ref, v_ref, be_ref, bo_ref, o_ref):
        i = pl.program_id(1)
        lane = lax.broadcasted_iota(jnp.int32, (nq, _LANES), 1)
        for gi, b_ref in enumerate((be_ref, bo_ref)):
            k0 = pl.multiple_of(_na_win_start(2 * i + gi, rows) * GRID_W, GRID_W)
            kw = k_ref[pl.ds(k0, nk), :]
            vw = v_ref[pl.ds(k0, nk), :]
            qv = q_ref[gi * nq:(gi + 1) * nq, :]
            outs = []
            for hh in range(2):
                mine = (lane >= hh * NA_HD) & (lane < (hh + 1) * NA_HD)
                s = _dg(jnp.where(mine, qv, jnp.zeros_like(qv)), kw, 1, 1) * sc + b_ref[0, hh]
                e = jnp.exp(s - jnp.max(s, axis=-1, keepdims=True))
                p = e * (1.0 / jnp.sum(e, axis=-1, keepdims=True))
                outs.append(_dg(p, vw, 1, 0))
            o_ref[gi * nq:(gi + 1) * nq, :] = jnp.where(lane < NA_HD, outs[0], outs[1])

    def tspec(gi):
        return pl.BlockSpec((1, 2, nq, nk), lambda p, i: (_na_class(2 * i + gi, groups), p, 0, 0))

    return pl.pallas_call(
        body, name=name, out_shape=jax.ShapeDtypeStruct((S, W), _F32), grid=(npair, groups // 2),
        in_specs=[pl.BlockSpec((2 * nq, _LANES), lambda p, i: (i, p)),
                  pl.BlockSpec((S, _LANES), lambda p, i: (0, p)),
                  pl.BlockSpec((S, _LANES), lambda p, i: (0, p)),
                  tspec(0), tspec(1)],
        out_specs=pl.BlockSpec((2 * nq, _LANES), lambda p, i: (i, p)),
        compiler_params=_params(("parallel", "arbitrary")))(qn, kn, vb, tab, tab)


def _na_bwd(qn, kn, vb, tab, do, name):
    S, W = qn.shape
    rows, groups = _na_geometry(S)
    npair = W // _LANES
    nq = NA_GROUP * GRID_W
    nk = NA_WIN_ROWS * GRID_W
    sc = float(NA_HD) ** -0.5

    half = groups // 2

    def body(q_ref, k_ref, v_ref, be_ref, bo_ref, do_ref, dq_ref, dk_ref, dv_ref, dbe_ref, dbo_ref):
        i = pl.program_id(1)

        @pl.when(i == 0)
        def _():
            dk_ref[...] = jnp.zeros_like(dk_ref)
            dv_ref[...] = jnp.zeros_like(dv_ref)

        @pl.when(i <= 1)
        def _():
            dbe_ref[...] = jnp.zeros_like(dbe_ref)

        @pl.when((i == 0) | (i == half - 1))
        def _():
            dbo_ref[...] = jnp.zeros_like(dbo_ref)

        lane = lax.broadcasted_iota(jnp.int32, (nq, _LANES), 1)
        for gi, (b_ref, db_ref) in enumerate(((be_ref, dbe_ref), (bo_ref, dbo_ref))):
            k0 = pl.multiple_of(_na_win_start(2 * i + gi, rows) * GRID_W, GRID_W)
            kw = k_ref[pl.ds(k0, nk), :]
            vw = v_ref[pl.ds(k0, nk), :]
            qv = q_ref[gi * nq:(gi + 1) * nq, :]
            dov = do_ref[gi * nq:(gi + 1) * nq, :]
            dqs = []
            dkw = jnp.zeros((nk, _LANES), _F32)
            dvw = jnp.zeros((nk, _LANES), _F32)
            for hh in range(2):
                mine = (lane >= hh * NA_HD) & (lane < (hh + 1) * NA_HD)
                qm = jnp.where(mine, qv, jnp.zeros_like(qv))
                dom = jnp.where(mine, dov, 0.0)
                s = _dg(qm, kw, 1, 1) * sc + b_ref[0, hh]
                e = jnp.exp(s - jnp.max(s, axis=-1, keepdims=True))
                p = e * (1.0 / jnp.sum(e, axis=-1, keepdims=True))
                dp = _dg(dom, vw, 1, 1)
                ds = p * (dp - jnp.sum(p * dp, axis=-1, keepdims=True))
                db_ref[0, hh] += ds
                dqs.append(_dg(ds, kw, 1, 0) * sc)
                dkw = dkw + _dg(ds, qm, 0, 0) * sc
                dvw = dvw + _dg(p, dom, 0, 0)
            dq_ref[gi * nq:(gi + 1) * nq, :] = jnp.where(lane < NA_HD, dqs[0], dqs[1])
            dk_ref[pl.ds(k0, nk), :] += dkw
            dv_ref[pl.ds(k0, nk), :] += dvw

    def tspec(gi):
        return pl.BlockSpec((1, 2, nq, nk), lambda p, i: (_na_class(2 * i + gi, groups), p, 0, 0))

    blk = pl.BlockSpec((2 * nq, _LANES), lambda p, i: (i, p))
    full = pl.BlockSpec((S, _LANES), lambda p, i: (0, p))
    tshape = jax.ShapeDtypeStruct(tab.shape, _F32)
    dq, dk, dv, dbe, dbo = pl.pallas_call(
        body, name=name,
        out_shape=(jax.ShapeDtypeStruct((S, W), _F32), jax.ShapeDtypeStruct((S, W), _F32),
                   jax.ShapeDtypeStruct((S, W), _F32), tshape, tshape),
        grid=(npair, half), in_specs=[blk, full, full, tspec(0), tspec(1), blk],
        out_specs=(blk, full, full, tspec(0), tspec(1)),
        compiler_params=_params(("arbitrary", "arbitrary")))(qn, kn, vb, tab, tab, do)
    return dq, dk, dv, jnp.stack([dbe[0], dbe[1] + dbo[1], dbo[2]])


def _mem_fwd(qn, km, vm, name):
    S, W = qn.shape
    hd = W // MEM_HEADS
    tq = _pick(S, (512, 256))
    sc = float(hd) ** -0.5

    def body(q_ref, k_ref, v_ref, o_ref):
        for h in range(MEM_HEADS):
            cs = slice(h * hd, (h + 1) * hd)
            s = _dg(q_ref[:, cs], k_ref[:, cs], 1, 1) * sc
            e = jnp.exp(s - jnp.max(s, axis=-1, keepdims=True))
            p = e * (1.0 / jnp.sum(e, axis=-1, keepdims=True))
            o_ref[:, cs] = _dg(p, v_ref[:, cs], 1, 0)

    full = pl.BlockSpec(km.shape, lambda i: (0, 0))
    return pl.pallas_call(
        body, name=name, out_shape=jax.ShapeDtypeStruct((S, W), _F32), grid=(S // tq,),
        in_specs=[pl.BlockSpec((tq, W), lambda i: (i, 0)), full, full],
        out_specs=pl.BlockSpec((tq, W), lambda i: (i, 0)),
        compiler_params=_params(("parallel",)))(qn, km, vm)


def _mem_bwd(qn, km, vm, do, name):
    S, W = qn.shape
    hd = W // MEM_HEADS
    tq = _pick(S, (512, 256))
    sc = float(hd) ** -0.5

    def body(q_ref, k_ref, v_ref, do_ref, dq_ref, dk_ref, dv_ref):
        @pl.when(pl.program_id(0) == 0)
        def _():
            dk_ref[...] = jnp.zeros_like(dk_ref)
            dv_ref[...] = jnp.zeros_like(dv_ref)

        for h in range(MEM_HEADS):
            cs = slice(h * hd, (h + 1) * hd)
            qh = q_ref[:, cs]
            kh = k_ref[:, cs]
            doh = do_ref[:, cs]
            s = _dg(qh, kh, 1, 1) * sc
            e = jnp.exp(s - jnp.max(s, axis=-1, keepdims=True))
            p = e * (1.0 / jnp.sum(e, axis=-1, keepdims=True))
            dp = _dg(doh, v_ref[:, cs], 1, 1)
            ds = p * (dp - jnp.sum(p * dp, axis=-1, keepdims=True))
            dq_ref[:, cs] = _dg(ds, kh, 1, 0) * sc
            dk_ref[:, cs] += _dg(ds, qh, 0, 0) * sc
            dv_ref[:, cs] += _dg(p, doh, 0, 0)

    full = pl.BlockSpec(km.shape, lambda i: (0, 0))
    row = pl.BlockSpec((tq, W), lambda i: (i, 0))
    return pl.pallas_call(
        body, name=name,
        out_shape=(jax.ShapeDtypeStruct((S, W), _F32), jax.ShapeDtypeStruct(km.shape, _F32),
                   jax.ShapeDtypeStruct(km.shape, _F32)),
        grid=(S // tq,), in_specs=[row, full, full, row], out_specs=(row, full, full),
        compiler_params=_params(("arbitrary",)))(qn, km, vm, do)


def _place():
    x, y, c = lax.axis_index("x"), lax.axis_index("y"), lax.axis_index("c")
    chips = [(1 - x, y), (x, 1 - y), (1 - x, 1 - y)]
    return x, y, c, chips


def _exchange(rider, name):
    ri = len(rider.ins)
    ro = len(rider.outs)

    def body(*refs):
        ins, outs, sems = refs[:ri], refs[ri:ri + ro], refs[ri + ro:]
        rider.start(ins, outs, sems)
        rider.finish(ins, outs, sems)

    return pl.pallas_call(
        body, name=name, out_shape=tuple(rider.outs), in_specs=[_ANY] * ri, out_specs=(_ANY,) * ro,
        scratch_shapes=rider.sems, input_output_aliases={i: o for i, o in rider.aliases},
        compiler_params=_params())(*rider.ins)


def _pair_sems(*shape):
    return [pltpu.SemaphoreType.DMA(shape), pltpu.SemaphoreType.DMA(shape)]


def _gather_ici(shards):
    n = len(shards)

    def copies(w, g, sems):
        send, recv = sems
        x, y, c, chips = _place()
        me = 2 * x + y
        out, back = [], []
        for t in range(n):
            for j, (cx, cy) in enumerate(chips):
                out.append(pltpu.make_async_remote_copy(
                    src_ref=w[t].at[c], dst_ref=g[t].at[me, c], send_sem=send.at[t, j], recv_sem=recv.at[t, j],
                    device_id=(cx, cy, c), device_id_type=_MESH))
                back.append(functools.partial(
                    pltpu.make_async_remote_copy,
                    src_ref=w[t].at[c], dst_ref=g[t].at[2 * cx + cy, c], send_sem=send.at[t, j],
                    recv_sem=recv.at[t, j], device_id=(cx, cy, c), device_id_type=_MESH))
        return out, back

    def start(w, g, sems):
        for cp in copies(w, g, sems)[0]:
            cp.start()

    def finish(w, g, sems):
        out, back = copies(w, g, sems)
        for make in back:
            make().wait_recv()
        for cp in out:
            cp.wait_send()

    return _Rider(shards, [jax.ShapeDtypeStruct((4,) + s.shape, s.dtype) for s in shards], _pair_sems(n, 3),
                  start, finish)


def _gather_d2d(gs):
    n = len(gs)

    def copies(g, sems):
        send, recv = sems
        x, y, c, chips = _place()
        out, back = [], []
        for t in range(n):
            for j, (cx, cy) in enumerate(chips):
                mine, theirs = g[t].at[2 * cx + cy, c], g[t].at[2 * cx + cy, 1 - c]
                out.append(pltpu.make_async_remote_copy(
                    src_ref=mine, dst_ref=mine, send_sem=send.at[t, j], recv_sem=recv.at[t, j],
                    device_id=(x, y, 1 - c), device_id_type=_MESH))
                back.append(functools.partial(
                    pltpu.make_async_remote_copy,
                    src_ref=mine, dst_ref=theirs, send_sem=send.at[t, j], recv_sem=recv.at[t, j],
                    device_id=(x, y, 1 - c), device_id_type=_MESH))
        return out, back

    def start(_, g, sems):
        for cp in copies(g, sems)[0]:
            cp.start()

    def finish(_, g, sems):
        out, back = copies(g, sems)
        for make in back:
            make().wait_recv()
        for cp in out:
            cp.wait_send()

    return _Rider(gs, [jax.ShapeDtypeStruct(g.shape, g.dtype) for g in gs], _pair_sems(n, 3), start, finish,
                  aliases=[(t, t) for t in range(n)])


def _swap_halves(ts):
    n = len(ts)

    def copies(t_in, land, sems):
        send, recv = sems
        x, y, c, _ = _place()
        return [pltpu.make_async_remote_copy(
            src_ref=t_in[t].at[:, 1 - c], dst_ref=land[t], send_sem=send.at[t], recv_sem=recv.at[t],
            device_id=(x, y, 1 - c), device_id_type=_MESH) for t in range(n)]

    def start(t_in, land, sems):
        for cp in copies(t_in, land, sems):
            cp.start()

    def finish(t_in, land, sems):
        for cp in copies(t_in, land, sems):
            cp.wait()

    return _Rider(ts, [jax.ShapeDtypeStruct((4,) + t.shape[2:], t.dtype) for t in ts], _pair_sems(n), start, finish)


def _scatter_chips(ps):
    n = len(ps)

    def copies(p, land, sems):
        send, recv = sems
        x, y, c, chips = _place()
        me = 2 * x + y
        out, back = [], []
        for t in range(n):
            for j, (cx, cy) in enumerate(chips):
                out.append(pltpu.make_async_remote_copy(
                    src_ref=p[t].at[2 * cx + cy], dst_ref=land[t].at[me], send_sem=send.at[t, j],
                    recv_sem=recv.at[t, j], device_id=(cx, cy, c), device_id_type=_MESH))
                back.append(functools.partial(
                    pltpu.make_async_remote_copy,
                    src_ref=p[t].at[me], dst_ref=land[t].at[2 * cx + cy], send_sem=send.at[t, j],
                    recv_sem=recv.at[t, j], device_id=(cx, cy, c), device_id_type=_MESH))
        return out, back

    def start(p, land, sems):
        for cp in copies(p, land, sems)[0]:
            cp.start()

    def finish(p, land, sems):
        out, back = copies(p, land, sems)
        for make in back:
            make().wait_recv()
        for cp in out:
            cp.wait_send()

    return _Rider(ps, [jax.ShapeDtypeStruct(t.shape, t.dtype) for t in ps], _pair_sems(n, 3), start, finish)


def _swap_reduced(rs):
    n = len(rs)

    def copies(out, sems):
        send, recv = sems
        x, y, c, _ = _place()
        return [pltpu.make_async_remote_copy(
            src_ref=out[t].at[c], dst_ref=out[t].at[c], send_sem=send.at[t], recv_sem=recv.at[t],
            device_id=(x, y, 1 - c), device_id_type=_MESH) for t in range(n)]

    def start(_, out, sems):
        for cp in copies(out, sems):
            cp.start()

    def finish(_, out, sems):
        for cp in copies(out, sems):
            cp.wait()

    return _Rider(rs, [jax.ShapeDtypeStruct(t.shape, t.dtype) for t in rs], _pair_sems(n), start, finish,
                  aliases=[(t, t) for t in range(n)])


def _all_devices(v, reduce, name):
    rows = v.shape[0]

    def body(v_ref, o_ref, *rest):
        if reduce:
            all_ref, send, recv = rest
        else:
            send, recv = rest
            all_ref = o_ref
        x, y, c, _ = _place()
        me = 4 * x + 2 * y + c
        all_ref[me] = v_ref[...]
        cps = []
        for k in range(1, 8):
            fx, fy, fc = (k >> 2) & 1, (k >> 1) & 1, k & 1
            to = (x ^ fx, y ^ fy, c ^ fc)
            cps.append(pltpu.make_async_remote_copy(
                src_ref=v_ref, dst_ref=all_ref.at[me], send_sem=send.at[k - 1], recv_sem=recv.at[k - 1],
                device_id=to, device_id_type=_MESH))
        for cp in cps:
            cp.start()
        for k in range(1, 8):
            fx, fy, fc = (k >> 2) & 1, (k >> 1) & 1, k & 1
            frm = 4 * (x ^ fx) + 2 * (y ^ fy) + (c ^ fc)
            pltpu.make_async_remote_copy(
                src_ref=v_ref, dst_ref=all_ref.at[frm], send_sem=send.at[k - 1], recv_sem=recv.at[k - 1],
                device_id=(x, y, c), device_id_type=_MESH).wait_recv()
        for cp in cps:
            cp.wait_send()
        if reduce:
            acc = all_ref[0]
            for d in range(1, 8):
                acc = acc + all_ref[d]
            o_ref[...] = acc

    vm = pl.BlockSpec(memory_space=pltpu.VMEM)
    if reduce:
        out_shape = jax.ShapeDtypeStruct((rows, _LANES), _F32)
        scratch = [pltpu.VMEM((8, rows, _LANES), _F32)]
    else:
        out_shape = jax.ShapeDtypeStruct((8, rows, _LANES), _F32)
        scratch = []
    return pl.pallas_call(
        body, name=name, out_shape=out_shape, in_specs=[vm], out_specs=vm,
        scratch_shapes=scratch + [pltpu.SemaphoreType.DMA((7,)), pltpu.SemaphoreType.DMA((7,))],
        compiler_params=_params())(v)


def _weight_halves(shards):
    return [shards[k].reshape((2, shards[k].shape[0] // 2) + shards[k].shape[1:]) for k in BIG]


def _gather_layer_weights(halves, name):
    g1 = _exchange(_gather_ici(halves), name + "_ici")
    g2 = _exchange(_gather_d2d(g1), name + "_d2d")
    return _assemble_weights(g2, halves)


def _assemble_weights(got, halves):
    chip = 2 * lax.axis_index("x") + lax.axis_index("y")
    full = {}
    for k, g, own in zip(BIG, got, halves):
        g = lax.dynamic_update_slice(g, own[None], (chip, 0, 0, 0))
        _, _, r, c = g.shape
        if k in ('w_in', 'ffn_w13'):
            full[k] = g
        else:
            full[k] = g.reshape(8 * r, c)
    return full


def _grads_as_shards(grads):
    ts = []
    for k in BIG:
        g = grads[k]
        if k in ('w_in', 'ffn_w13'):
            ts.append(g)
        else:
            r, c = g.shape[0] // 8, g.shape[1]
            ts.append(g.reshape(4, 2, r, c))
    return ts


def _pair_sums(ts, landed, cidx):
    return [_add_pair(t, la, cidx, "reduce_add2_" + k) for k, t, la in zip(BIG, ts, landed)]


def _chip_sums(partial, slots, cidx):
    chip = (2 * lax.axis_index("x") + lax.axis_index("y")).astype(jnp.int32).reshape(1)
    return [_add_four(p, s, chip, cidx, "reduce_add4_" + k) for k, p, s in zip(BIG, partial, slots)]


def _assemble_shards(joined):
    return {k: j.reshape((2 * j.shape[1], j.shape[2])) for k, j in zip(BIG, joined)}


def _reduce_layer_grads(ts, cidx, name):
    landed = _exchange(_swap_halves(ts), name + "_swap")
    partial = _pair_sums(ts, landed, cidx)
    slots = _exchange(_scatter_chips(partial), name + "_scatter")
    reduced = _chip_sums(partial, slots, cidx)
    return _assemble_shards(_exchange(_swap_reduced(reduced), name + "_join"))


def _pack_rows(arrs):
    parts, spans = [], []
    off = 0
    for a in arrs:
        n = int(np.prod(a.shape))
        pad = (-n) % (8 * _LANES)
        parts.append(jnp.pad(a.reshape(-1), (0, pad)))
        spans.append((off, n, a.shape))
        off += n + pad
    return jnp.concatenate(parts).reshape(-1, _LANES), spans


def _unpack_rows(packed, spans):
    flat = packed.reshape(-1)
    return [flat[o:o + n].reshape(shape) for o, n, shape in spans]


def _in_split(d):
    dk, dv, w = d // 4, d // 2, d // 4
    names = [('g_q', dk), ('g_k', dk), ('g_v', dv), ('g_r', dv), ('g_f', GLA_GATE_RANK), ('g_b', GLA_GATE_RANK),
             ('n_q', w), ('n_k', w), ('n_v', w), ('m_q', w)]
    out, off = {}, 0
    for nme, wd in names:
        out[nme] = (off, wd)
        off += wd
    return out


def _w_in_from_shards(g):
    _, _, R, C = g.shape
    sp = _in_split(2 * R)
    f0, n0 = sp['g_f'][0], sp['n_q'][0]

    def cols(lo, hi, h):
        out = []
        for j in range(4):
            a, b = max(lo, j * C), min(hi, (j + 1) * C)
            if a < b:
                out.append(g[j, h, :, a - j * C:b - j * C])
        return out

    main = jnp.concatenate([jnp.concatenate(cols(0, f0, h) + cols(n0, 4 * C, h), axis=1) for h in range(2)], axis=0)
    gate = jnp.concatenate([jnp.concatenate(cols(f0, n0, h), axis=1) for h in range(2)], axis=0)
    return main, jnp.pad(gate, ((0, 0), (0, _LANES - (n0 - f0))))


def _w_in_to_shards(main, gate):
    R = main.shape[0] // 2
    sp = _in_split(2 * R)
    f0, n0 = sp['g_f'][0], sp['n_q'][0]
    C = (main.shape[1] + n0 - f0) // 4

    def cols(lo, hi, h):
        rows = slice(h * R, (h + 1) * R)
        out = []
        for src, a, b, shift in ((main, 0, f0, 0), (gate, f0, n0, -f0), (main, n0, 4 * C, f0 - n0)):
            lo2, hi2 = max(lo, a), min(hi, b)
            if lo2 < hi2:
                out.append(src[rows, lo2 + shift:hi2 + shift])
        return out

    return jnp.stack([jnp.stack([jnp.concatenate(cols(j * C, (j + 1) * C, h), axis=1) for h in range(2)])
                      for j in range(4)])


def _gate_weight(wg2_f, wg2_b):
    r, dk = wg2_f.shape
    top = jnp.concatenate([wg2_f, jnp.zeros_like(wg2_f)], axis=1)
    mid = jnp.concatenate([jnp.zeros_like(wg2_b), wg2_b], axis=1)
    return jnp.concatenate([top, mid, jnp.zeros((_LANES - 2 * r, 2 * dk), wg2_f.dtype)], axis=0)


def _layer_fwd(l, x, mem, w, full, tab, next_halves=None):
    S, D = x.shape
    dk, dv, nw = D // 4, D // 2, D // 4
    sv = {}
    w_main, w_gate = _w_in_from_shards(full['w_in'])
    sv['w_main'], sv['w_gate'] = w_main, w_gate
    xn, r1 = _rmsnorm_fwd(x, w['attn_norm'][None], f"rms1_fwd")
    if next_halves is None:
        proj = _mm(xn, w_main, 'nn', "mm_proj", out_dtype=_BF)
    else:
        proj, g_a = _mm(xn, w_main, 'nn', "mm_proj_gather", out_dtype=_BF, rider=_gather_ici(next_halves[:3]))
    gfb = _mm(xn, w_gate, 'nn', "mm_gate")
    sv.update(x=x, xn=xn, r1=r1, proj=proj, gfb=gfb)
    wg = _gate_weight(w['gla_wg2_f'], w['gla_wg2_b']).astype(_BF)
    bg = jnp.concatenate([w['gla_bg_f'], w['gla_bg_b']])[None]
    o_f, st_f = _gla_fwd(proj, gfb, wg, bg, False, "gla_fwd_f")
    o_b, st_b = _gla_fwd(proj, gfb, wg, bg, True, "gla_fwd_b")
    y_gla = _gla_post_fwd(o_f, o_b, proj, (2 * dk) // dv + 1, w['gla_out_norm'][None], "gla_post_fwd")
    sv.update(wg=wg, bg=bg, o_f=o_f, o_b=o_b, st_f=st_f, st_b=st_b)
    c0 = (2 * dk + 2 * dv) // nw
    qn = _headnorm_fwd(proj, c0, nw, NA_HD, jnp.tile(w['na_q_norm'], nw // NA_HD)[None], "na_qnorm_fwd")
    kn = _headnorm_fwd(proj, c0 + 1, nw, NA_HD, jnp.tile(w['na_k_norm'], nw // NA_HD)[None], "na_knorm_fwd")
    vb = proj[:, (c0 + 2) * nw:(c0 + 3) * nw].astype(_BF)
    o_na = _na_fwd(qn, kn, vb, tab, "na_fwd")
    y_na = _headnorm_fwd(o_na, 0, nw, NA_HD, w['na_out_norm'][None], "na_onorm_fwd")
    sv.update(qn=qn, kn=kn, vb=vb, o_na=o_na)
    mhd = nw // MEM_HEADS
    mqn = _headnorm_fwd(proj, c0 + 3, nw, mhd, jnp.tile(w['mem_q_norm'], MEM_HEADS)[None], "mem_qnorm_fwd")
    memn, rm = _rmsnorm_fwd(mem, w['mem_norm'][None], "mem_rms_fwd")
    kv = _mm(memn, full['mem_wkv'], 'nn', "mm_memkv")
    km = _headnorm_fwd(kv, 0, nw, mhd, jnp.tile(w['mem_k_norm'], MEM_HEADS)[None], "mem_knorm_fwd")
    vm = kv[:, nw:].astype(_BF)
    o_mem = _mem_fwd(mqn, km, vm, "mem_fwd")
    y_mem = _headnorm_fwd(o_mem, 0, nw, mhd, w['mem_out_norm'][None], "mem_onorm_fwd")
    sv.update(mqn=mqn, memn=memn, rm=rm, kv=kv, km=km, vm=vm, o_mem=o_mem)
    y = jnp.concatenate([y_gla, y_na, y_mem], axis=1)
    x1 = _mm(y, full['w_out'], 'nn', "mm_out", res=x)
    h, r2 = _rmsnorm_fwd(x1, w['ffn_norm'][None], "rms2_fwd")
    if next_halves is None:
        a, gate, up = _ffn_up_fwd(h, full['ffn_w13'], "ffn_up_fwd")
        x2 = _mm(a, full['ffn_w2'], 'nn', "mm_w2", res=x1)
        got = None
    else:
        (a, gate, up), g_b = _ffn_up_fwd(h, full['ffn_w13'], "ffn_up_fwd_gather", _gather_ici(next_halves[3:]))
        x2, got = _mm(a, full['ffn_w2'], 'nn', "mm_w2_gather", res=x1, rider=_gather_d2d(list(g_a) + list(g_b)))
    sv.update(y=y, x1=x1, h=h, r2=r2, gate=gate, up=up, a=a)
    return x2, sv, got


def _layer_bwd(l, dx2, dx2b, mem, w, full, tab, sv, pending=None, cidx=None):
    S, D = dx2.shape
    dk, dv, nw = D // 4, D // 2, D // 4
    gb, gs = {}, {}
    if pending is None:
        dgate, dup = _ffn_up_bwd(dx2b, full['ffn_w2'], sv['gate'], sv['up'], "ffn_up_bwd")
    else:
        (dgate, dup), landed = _ffn_up_bwd(dx2b, full['ffn_w2'], sv['gate'], sv['up'], "ffn_up_bwd_swap",
                                           _swap_halves(pending))
        partial = _pair_sums(pending, landed, cidx)
    gb['ffn_w2'] = _mm(sv['a'], dx2b, 'tn', "mm_dw2")
    if pending is None:
        dh = _ffn_up_dh(dgate, dup, full['ffn_w13'], "ffn_up_dh")
        gb['ffn_w13'] = _ffn_up_dw(sv['h'], dgate, dup, "ffn_up_dw")
    else:
        dh, slots_b = _ffn_up_dh(dgate, dup, full['ffn_w13'], "ffn_up_dh_scatter", _scatter_chips(partial[3:]))
        gb['ffn_w13'], slots_a = _ffn_up_dw(sv['h'], dgate, dup, "ffn_up_dw_scatter", _scatter_chips(partial[:3]))
        reduced = _chip_sums(partial, list(slots_a) + list(slots_b), cidx)
    dx1, dx1b, g = _rmsnorm_bwd(sv['x1'], sv['r2'], w['ffn_norm'][None], dh, dx2, "rms2_bwd")
    gs['ffn_norm'] = g[0]
    dy = _mm(dx1b, full['w_out'], 'nt', "mm_dy", out_dtype=_BF)
    gb['w_out'] = _mm(sv['y'], dx1b, 'tn', "mm_dwout")
    c0 = (2 * dk + 2 * dv) // nw
    mhd = nw // MEM_HEADS
    do_mem, g = _headnorm_bwd(sv['o_mem'], 0, nw, mhd, w['mem_out_norm'][None], dy, (dv + nw) // nw, "mem_onorm_bwd")
    gs['mem_out_norm'] = g[0]
    dmqn, dkm, dvm = _mem_bwd(sv['mqn'], sv['km'], sv['vm'], do_mem, "mem_bwd")
    dmq, g = _headnorm_bwd(sv['proj'], c0 + 3, nw, mhd, jnp.tile(w['mem_q_norm'], MEM_HEADS)[None], dmqn, 0,
                           "mem_qnorm_bwd", _BF)
    gs['mem_q_norm'] = g[0].reshape(MEM_HEADS, mhd).sum(0)
    dkvk, g = _headnorm_bwd(sv['kv'], 0, nw, mhd, jnp.tile(w['mem_k_norm'], MEM_HEADS)[None], dkm, 0,
                            "mem_knorm_bwd")
    gs['mem_k_norm'] = g[0].reshape(MEM_HEADS, mhd).sum(0)
    dkv = jnp.concatenate([dkvk, dvm], axis=1).astype(_BF)
    gb['mem_wkv'] = _mm(sv['memn'], dkv, 'tn', "mm_dwkv")
    dmemn = _mm(dkv, full['mem_wkv'], 'nt', "mm_dmemn")
    _, _, g = _rmsnorm_bwd(mem, sv['rm'], w['mem_norm'][None], dmemn, None, "mem_rms_bwd")
    gs['mem_norm'] = g[0]
    do_na, g = _headnorm_bwd(sv['o_na'], 0, nw, NA_HD, w['na_out_norm'][None], dy, dv // nw, "na_onorm_bwd")
    gs['na_out_norm'] = g[0]
    dqn, dkn, dnv, dtab = _na_bwd(sv['qn'], sv['kn'], sv['vb'], tab, do_na, "na_bwd")
    gs['na_rpb'] = dtab
    dnq, g = _headnorm_bwd(sv['proj'], c0, nw, NA_HD, jnp.tile(w['na_q_norm'], nw // NA_HD)[None], dqn, 0,
                           "na_qnorm_bwd", _BF)
    gs['na_q_norm'] = g[0].reshape(nw // NA_HD, NA_HD).sum(0)
    dnk, g = _headnorm_bwd(sv['proj'], c0 + 1, nw, NA_HD, jnp.tile(w['na_k_norm'], nw // NA_HD)[None], dkn, 0,
                           "na_knorm_bwd", _BF)
    gs['na_k_norm'] = g[0].reshape(nw // NA_HD, NA_HD).sum(0)
    do_gla, dgr, g = _gla_post_bwd(sv['o_f'], sv['o_b'], sv['proj'], (2 * dk) // dv + 1, w['gla_out_norm'][None], dy,
                                   "gla_post_bwd")
    gs['gla_out_norm'] = g[0]
    dq1, dk1, dv1, dz_f = _gla_bwd(sv['proj'], sv['gfb'], sv['wg'], sv['bg'], sv['st_f'], do_gla, False, None,
                                   "gla_bwd_f")
    dgq, dgk, dgv, dz_b = _gla_bwd(sv['proj'], sv['gfb'], sv['wg'], sv['bg'], sv['st_b'], do_gla, True,
                                   (dq1, dk1, dv1), "gla_bwd_b")
    dz = jnp.concatenate([dz_f, dz_b], axis=1).astype(_BF)
    ones_lane = (jnp.arange(_LANES) == 2 * GLA_GATE_RANK)[None]
    gfb_aug = jnp.where(ones_lane, 1.0, sv['gfb']).astype(_BF)
    dwg = _mm(gfb_aug, dz, 'tn', "mm_dwg")
    r16 = GLA_GATE_RANK
    gs['gla_wg2_f'] = dwg[:r16, :dk]
    gs['gla_wg2_b'] = dwg[r16:2 * r16, dk:]
    gs['gla_bg_f'] = dwg[2 * r16, :dk]
    gs['gla_bg_b'] = dwg[2 * r16, dk:]
    dgfb = _mm(dz, sv['wg'], 'nt', "mm_dgfb", out_dtype=_BF)
    dproj = jnp.concatenate([dgq.astype(_BF), dgk.astype(_BF), dgv.astype(_BF), dgr, dnq, dnk, dnv.astype(_BF), dmq],
                            axis=1)
    t = _mm(dgfb, sv['w_gate'], 'nt', "mm_dxn_gate")
    if pending is None:
        dxn = _mm(dproj, sv['w_main'], 'nt', "mm_dxn", res=t)
        done = None
    else:
        dxn, joined = _mm(dproj, sv['w_main'], 'nt', "mm_dxn_join", res=t, rider=_swap_reduced(reduced))
        done = _assemble_shards(joined)
    dw_main = _mm(sv['xn'], dproj, 'tn', "mm_dwmain")
    dw_gate = _mm(sv['xn'], dgfb, 'tn', "mm_dwgate")
    gb['w_in'] = _w_in_to_shards(dw_main, dw_gate)
    dx, dxb, g = _rmsnorm_bwd(sv['x'], sv['r1'], w['attn_norm'][None], dxn, dx1, "rms1_bwd")
    gs['attn_norm'] = g[0]
    return dx, dxb, gb, gs, done


def kernel(x, mem, attn_norm, w_in, gla_wg2_f, gla_bg_f, gla_wg2_b, gla_bg_b, gla_out_norm, na_q_norm, na_k_norm, na_rpb, na_out_norm, mem_norm, mem_wkv, mem_q_norm, mem_k_norm, mem_out_norm, w_out, ffn_norm, ffn_w13, ffn_w2, loss_target, m_attn_norm, m_w_in, m_gla_wg2_f, m_gla_bg_f, m_gla_wg2_b, m_gla_bg_b, m_gla_out_norm, m_na_q_norm, m_na_k_norm, m_na_rpb, m_na_out_norm, m_mem_norm, m_mem_wkv, m_mem_q_norm, m_mem_k_norm, m_mem_out_norm, m_w_out, m_ffn_norm, m_ffn_w13, m_ffn_w2, v_attn_norm, v_w_in, v_gla_wg2_f, v_gla_bg_f, v_gla_wg2_b, v_gla_bg_b, v_gla_out_norm, v_na_q_norm, v_na_k_norm, v_na_rpb, v_na_out_norm, v_mem_norm, v_mem_wkv, v_mem_q_norm, v_mem_k_norm, v_mem_out_norm, v_w_out, v_ffn_norm, v_ffn_w13, v_ffn_w2):
    args = locals()
    W = {k: args[k] for k in WEIGHTS}
    M = {k: args['m_' + k] for k in WEIGHTS}
    V = {k: args['v_' + k] for k in WEIGHTS}
    depth = attn_norm.shape[0]
    xs, mems, tgt = x[0], mem[0], loss_target[0]
    cidx = lax.axis_index("c").astype(jnp.int32).reshape(1)

    gate_rows, gate_spans = _pack_rows([W[k] for k in SMALL_SHARDED])
    gate_all = _all_devices(gate_rows, False, "gate_weights_gather")
    gate_full = {}
    for i, k in enumerate(SMALL_SHARDED):
        per_chip = [_unpack_rows(gate_all[2 * j], gate_spans)[i] for j in range(4)]
        gate_full[k] = jnp.concatenate(per_chip, axis=-1)

    small = [k for k in WEIGHTS if k not in BIG]

    def layer_small(l):
        d = {k: W[k][l] for k in small if k not in SMALL_SHARDED}
        d.update({k: gate_full[k][l] for k in SMALL_SHARDED})
        return d

    saved, fulls, tabs = [], [], []
    cur = xs
    grid_rows = xs.shape[0] // GRID_W
    all_tabs = jax.vmap(lambda r: _na_bias_tables(r, grid_rows))(W['na_rpb'])
    halves = [_weight_halves({k: W[k][l].astype(_BF) for k in BIG}) for l in range(depth)]
    full = _gather_layer_weights(halves[0], "gather_weights")
    for l in range(depth):
        tab = all_tabs[l]
        nxt = halves[l + 1] if l + 1 < depth else None
        cur, sv, got = _layer_fwd(l, cur, mems, layer_small(l), full, tab, nxt)
        saved.append(sv)
        fulls.append(full)
        tabs.append(tab)
        if nxt is not None:
            full = _assemble_weights(got, nxt)
    loss_tile, dy, dyb = _loss_head(cur, tgt, "loss_head")
    loss = lax.psum(loss_tile[0, 0], ("x", "y", "c"))

    big_grads = [None] * depth
    small_grads = [None] * depth
    pending = None
    for l in range(depth - 1, -1, -1):
        dy, dyb, gb, gs, done = _layer_bwd(l, dy, dyb, mems, layer_small(l), fulls[l], tabs[l], saved[l], pending, cidx)
        if pending is not None:
            big_grads[l + 1] = done
        pending = _grads_as_shards(gb)
        small_grads[l] = gs
    big_grads[0] = _reduce_layer_grads(pending, cidx, "reduce_grads")
    grad_x = dy[None]

    small_stack = {k: jnp.stack([small_grads[l][k] for l in range(depth)]) for k in small}
    small_stack['na_rpb'] = jax.vmap(lambda d: _na_bias_grad(d, grid_rows))(small_stack['na_rpb'])
    packed, spans = _pack_rows([small_stack[k] for k in small])
    summed = _unpack_rows(_all_devices(packed, True, "small_grads_sum"), spans)
    G = dict(zip(small, summed))
    chip = 2 * lax.axis_index("x") + lax.axis_index("y")
    for k in SMALL_SHARDED:
        wdt = W[k].shape[-1]
        G[k] = lax.dynamic_slice_in_dim(G[k], chip * wdt, wdt, axis=2)
    for k in BIG:
        G[k] = jnp.stack([big_grads[l][k] for l in range(depth)])

    delta, new_m, new_v = {}, {}, {}
    for k in WEIGHTS:
        delta[k], new_m[k], new_v[k] = _adamw(W[k], G[k], M[k], V[k], "adamw_" + k)
    return (loss, grad_x, *[G[k] for k in WEIGHTS], *[delta[k] for k in WEIGHTS], *[new_m[k] for k in WEIGHTS],
            *[new_v[k] for k in WEIGHTS])
```
